```python
import math, functools
import jax, jax.numpy as jnp
from jax import lax
import numpy as np

D_MODEL = 1024
BATCH = 4
SEQ = 8192
DEPTH = 2
DEC_BATCH = 128
DEC_SEQ = 4
PAST_LEN = 16384
PAGE_SIZE = 128

HEAD_DIM = 64
H_A = 4
H_B = 4
KV_B = 2
H_C = 4
H_D = 4
W_A = H_A * HEAD_DIM
W_B = H_B * HEAD_DIM
W_C = H_C * HEAD_DIM
W_D = H_D * HEAD_DIM
MIX_WIDTH = W_A + W_B + W_C + W_D
CONV_A = 4
CHUNK_A = 64
CHUNK_C = 64
WIN_B = 128
DILATED = ((128, 1), (512, 4), (2048, 16))
WIN_D = 2048
BAND = 128
D_FF = 2816
N_EXPERTS = 8
TOP_K = 2
D_FF_E = 3584
MOE_BLOCK = 128
N_DENSE = (DEPTH + 1) // 2
N_MOE = DEPTH // 2
EPS = 1e-6
PROJ_SIZES = (3 * W_A, W_A, H_A, H_A,
              W_B, KV_B * HEAD_DIM, KV_B * HEAD_DIM,
              W_C, W_C, W_C, W_C,
              W_D, W_D, W_D)
PROJ_COLS = sum(PROJ_SIZES)
SPLIT_IDX = tuple(int(i) for i in np.cumsum(PROJ_SIZES)[:-1])

kernel_name = 'hybrid_parallel_heads_decode_step'

F32 = jnp.float32


def rmsnorm(x, gain):
    x32 = x.astype(F32)
    y = x32 * lax.rsqrt(jnp.mean(x32 * x32, axis=-1, keepdims=True) + EPS) * gain.astype(F32)
    return y.astype(x.dtype)


def l2norm(x):
    return x * lax.rsqrt(jnp.sum(x * x, axis=-1, keepdims=True) + EPS)


def gated_rmsnorm(o, gate, gain):
    B, T = o.shape[:2]
    o = o * lax.rsqrt(jnp.mean(o * o, axis=-1, keepdims=True) + EPS) * gain.astype(F32)
    return o.reshape(B, T, -1) * jax.nn.silu(gate.astype(F32))


def masked_softmax(s, mask, sink=None):
    s = jnp.where(mask, s, -jnp.inf)
    m = jnp.max(s, axis=-1)
    if sink is not None:
        m = jnp.maximum(m, sink)
    p = jnp.exp(s - m[..., None])
    l = jnp.sum(p, axis=-1)
    if sink is not None:
        l = l + jnp.exp(sink - m)
    return p / l[..., None], m + jnp.log(l)


def short_conv(x, buf, w):
    xp = jnp.concatenate([buf.astype(x.dtype), x], axis=1)
    T = x.shape[1]
    y = sum(w[j] * xp[:, j:j + T] for j in range(CONV_A))
    return y, xp[:, -(CONV_A - 1):]


def banded_attn(q, k, v, window, sink=None):
    B, L, H, dh = q.shape
    KV = k.shape[2]
    G = H // KV
    nb = -(-L // BAND)
    Lp = nb * BAND
    qb = jnp.pad(q, ((0, 0), (0, Lp - L), (0, 0), (0, 0))).reshape(B, nb, BAND, KV, G, dh)
    def band(t):
        tp = jnp.pad(t, ((0, 0), (BAND, Lp - L), (0, 0), (0, 0)))
        prev = tp[:, :Lp].reshape(B, nb, BAND, KV, dh)
        cur = tp[:, BAND:].reshape(B, nb, BAND, KV, dh)
        return jnp.concatenate([prev, cur], axis=2)
    kb, vb = band(k), band(v)
    s = jnp.einsum('bnqkgd,bnskd->bnkgqs', qb, kb, preferred_element_type=F32) * HEAD_DIM ** -0.5
    dist = np.arange(BAND)[:, None] + BAND - np.arange(2 * BAND)[None, :]
    inband = (dist >= 0) & (dist <= window)
    has_prev = np.arange(2 * BAND)[None, :] >= BAND
    mask = np.where(np.arange(nb)[:, None, None] == 0, inband & has_prev, inband)
    mask = mask[None, :, None, None]
    sink_b = None if sink is None else sink.astype(F32).reshape(KV, G)[None, None, :, :, None]
    p, lse = masked_softmax(s, mask, sink_b)
    o = jnp.einsum('bnkgqs,bnskd->bnqkgd', p, vb.astype(F32)).reshape(B, Lp, H, dh)[:, :L]
    lse = jnp.transpose(lse, (0, 1, 4, 2, 3)).reshape(B, Lp, H)[:, :L]
    return o, lse


def merge_branches(outs, lses):
    wts = jax.nn.softmax(jnp.stack(lses, 0), axis=0)
    return jnp.sum(wts[..., None] * jnp.stack(outs, 0), axis=0)


def dilated_prompt(q, k, v):
    B, S, H, dh = q.shape
    outs, lses = [], []
    for w, r in DILATED:
        L = S // r
        def strided(t):
            return jnp.swapaxes(t.reshape(B, L, r, H, dh), 1, 2).reshape(B * r, L, H, dh)
        o, lse = banded_attn(strided(q), strided(k), strided(v), w // r)
        outs.append(jnp.swapaxes(o.reshape(B, r, L, H, dh), 1, 2).reshape(B, S, H, dh))
        lses.append(jnp.swapaxes(lse.reshape(B, r, L, H), 1, 2).reshape(B, S, H))
    return merge_branches(outs, lses)


def dilated_sample(q, k_all, v_all, n_past):
    T = q.shape[1]
    outs, lses = [], []
    for w, r in DILATED:
        idx = n_past + np.arange(T)[:, None] - r * np.arange(w // r + 1)[None, :]
        safe = np.maximum(idx, 0)
        kg, vg = k_all[:, safe], v_all[:, safe]
        s = jnp.einsum('bthd,btjhd->bhtj', q, kg, preferred_element_type=F32) * HEAD_DIM ** -0.5
        p, lse = masked_softmax(s, (idx >= 0)[None, None])
        outs.append(jnp.einsum('bhtj,btjhd->bthd', p, vg.astype(F32)))
        lses.append(jnp.swapaxes(lse, 1, 2))
    return merge_branches(outs, lses)


def window_sample(q, k_all, v_all, n_past, sink):
    B, T, H, dh = q.shape
    KV = k_all.shape[2]
    G = H // KV
    s = jnp.einsum('btkgd,bskd->bkgts', q.reshape(B, T, KV, G, dh), k_all,
                   preferred_element_type=F32) * HEAD_DIM ** -0.5
    dist = n_past + np.arange(T)[:, None] - np.arange(n_past + T)[None, :]
    p, _ = masked_softmax(s, (dist >= 0) & (dist <= WIN_B),
                          sink.astype(F32).reshape(KV, G)[None, :, :, None])
    return jnp.einsum('bkgts,bskd->btkgd', p, v_all.astype(F32)).reshape(B, T, H, dh)


def gated_delta_chunked(q, k, v, beta, g):
    B, S, H, dk = q.shape
    dv = v.shape[-1]
    C = CHUNK_A
    n = S // C
    def chunks(t):
        return jnp.swapaxes(t.reshape((B, n, C, H) + t.shape[3:]), 2, 3)
    q, k, v, beta, g = (chunks(t) for t in (q, k, v, beta, g))
    gc = jnp.cumsum(g, axis=-1)
    causal = np.tril(np.ones((C, C), bool))
    decay = jnp.exp(jnp.where(causal, gc[..., :, None] - gc[..., None, :], -jnp.inf))
    lower = beta[..., :, None] * jnp.einsum('bnhtd,bnhsd->bnhts', k, k) * decay * np.tri(C, k=-1, dtype=np.float32)
    rhs = jnp.concatenate([v * beta[..., None], k * (beta * jnp.exp(gc))[..., None]], axis=-1)
    sol = lax.linalg.triangular_solve(lower + np.eye(C, dtype=np.float32), rhs,
                                      left_side=True, lower=True, unit_diagonal=True)
    u_part, w_part = sol[..., :dv], sol[..., dv:]
    intra = jnp.einsum('bnhtd,bnhsd->bnhts', q, k) * decay
    q_dec = q * jnp.exp(gc)[..., None]
    k_dec = k * jnp.exp(gc[..., -1:] - gc)[..., None]
    g_tot = jnp.exp(gc[..., -1])
    def step(state, xs):
        u_c, w_c, a_c, q_c, k_c, g_c = xs
        u = u_c - jnp.einsum('bhtk,bhkv->bhtv', w_c, state)
        o = jnp.einsum('bhtk,bhkv->bhtv', q_c, state) + jnp.einsum('bhts,bhsv->bhtv', a_c, u)
        state = g_c[..., None, None] * state + jnp.einsum('bhsk,bhsv->bhkv', k_c, u)
        return state, o
    xs = tuple(jnp.moveaxis(t, 1, 0) for t in (u_part, w_part, intra, q_dec, k_dec, g_tot))
    state, o = lax.scan(step, jnp.zeros((B, H, dk, dv), F32), xs)
    return jnp.transpose(o, (1, 0, 3, 2, 4)).reshape(B, S, H, dv), state


def hgrn2_chunked(q, k, v, g):
    B, S, H, dk = q.shape
    dv = v.shape[-1]
    C = CHUNK_C
    n = S // C
    def chunks(t):
        return jnp.moveaxis(jnp.swapaxes(t.reshape(B, n, C, H, t.shape[-1]), 2, 3), 1, 0)
    causal = np.tril(np.ones((C, C), bool))[:, :, None]
    def step(state, xs):
        q_c, k_c, v_c, g_c = xs
        gc = jnp.cumsum(g_c, axis=-2)
        decay = jnp.exp(jnp.where(causal, gc[..., :, None, :] - gc[..., None, :, :], -jnp.inf))
        intra = jnp.einsum('bhtk,bhsk,bhtsk->bhts', q_c, k_c, decay)
        o = jnp.einsum('bhtk,bhkv->bhtv', q_c * jnp.exp(gc), state) + jnp.einsum('bhts,bhsv->bhtv', intra, v_c)
        state = (jnp.exp(gc[..., -1, :])[..., None] * state
                 + jnp.einsum('bhsk,bhsv->bhkv', k_c * jnp.exp(gc[..., -1:, :] - gc), v_c))
        return state, o
    state, o = lax.scan(step, jnp.zeros((B, H, dk, dv), F32), tuple(chunks(t) for t in (q, k, v, g)))
    return jnp.transpose(o, (1, 0, 3, 2, 4)).reshape(B, S, H, dv), state


def deltanet_step(S, xs):
    q, k, v, beta, g = xs
    S = jnp.exp(g)[..., None, None] * S
    err = v - jnp.einsum('bhk,bhkv->bhv', k, S)
    S = S + jnp.einsum('bhk,bhv->bhkv', k, beta[..., None] * err)
    return S, jnp.einsum('bhk,bhkv->bhv', q, S)


def hgrn2_step(S, xs):
    q, k, v, g = xs
    S = jnp.exp(g)[..., None] * S + k[..., None] * v[..., None, :]
    return S, jnp.einsum('bhk,bhkv->bhv', q, S)


def run_steps(step, S0, xs):
    S, o = lax.scan(step, S0, tuple(jnp.moveaxis(t, 1, 0) for t in xs))
    return jnp.moveaxis(o, 0, 1), S


def deltanet_inputs(qkv, beta_in, a_in, conv_buf, conv_w, a_log, dt_bias):
    y, new_buf = short_conv(qkv, conv_buf, conv_w)
    y = jax.nn.silu(y.astype(F32))
    B, T, _ = y.shape
    q, k, v = (t.reshape(B, T, H_A, HEAD_DIM) for t in jnp.split(y, 3, axis=-1))
    q = l2norm(q) * HEAD_DIM ** -0.5
    k = l2norm(k)
    beta = jax.nn.sigmoid(beta_in.astype(F32))
    g = -jnp.exp(a_log.astype(F32)) * jax.nn.softplus(a_in.astype(F32) + dt_bias.astype(F32))
    return q, k, v, beta, g, new_buf


def hgrn2_inputs(q_in, f_in, i_in, lb):
    B, T, _ = q_in.shape
    f = lb + (1.0 - lb) * jax.nn.sigmoid(f_in.astype(F32))
    heads = lambda t: t.reshape(B, T, H_C, HEAD_DIM)
    return heads(q_in.astype(F32)), heads(1.0 - f), heads(i_in.astype(F32)), heads(jnp.log(f))


def token_mixers(proj, conv_w, a_log, dt_bias, norm_a, sinks, lb, norm_c, st):
    B, T, _ = proj.shape
    (qkv_a, gate_a, beta_a, alpha_a, q_b, k_b, v_b, q_c, f_c, i_c, gate_c,
     q_d, k_d, v_d) = jnp.split(proj, SPLIT_IDX, axis=-1)
    heads = lambda t, h: t.reshape(B, T, h, HEAD_DIM)
    q_b, k_b, v_b = heads(q_b, H_B), heads(k_b, KV_B), heads(v_b, KV_B)
    q_d, k_d, v_d = heads(q_d, H_D), heads(k_d, H_D), heads(v_d, H_D)
    if st is None:
        conv_buf = jnp.zeros((B, CONV_A - 1, 3 * W_A), proj.dtype)
    else:
        bk, bv, dk, dv, conv_buf, s_a, s_c = st
    qa, ka, va, beta, g, new_conv = deltanet_inputs(qkv_a, beta_a, alpha_a, conv_buf, conv_w, a_log, dt_bias)
    qc, kc, vc, gc = hgrn2_inputs(q_c, f_c, i_c, lb)
    if st is None:
        o_a, s_a_new = gated_delta_chunked(qa, ka, va, beta, g)
        o_c, s_c_new = hgrn2_chunked(qc, kc, vc, gc)
        o_b, _ = banded_attn(q_b, k_b, v_b, WIN_B, sinks)
        o_d = dilated_prompt(q_d, k_d, v_d)
        lb_len, ld_len = min(WIN_B, T), min(WIN_D, T)
        new_st = (k_b[:, -lb_len:], v_b[:, -lb_len:], k_d[:, -ld_len:], v_d[:, -ld_len:],
                  new_conv, s_a_new, s_c_new)
    else:
        o_a, s_a_new = run_steps(deltanet_step, s_a.astype(F32), (qa, ka, va, beta, g))
        o_c, s_c_new = run_steps(hgrn2_step, s_c.astype(F32), (qc, kc, vc, gc))
        kb_all = jnp.concatenate([bk.astype(proj.dtype), k_b], axis=1)
        vb_all = jnp.concatenate([bv.astype(proj.dtype), v_b], axis=1)
        kd_all = jnp.concatenate([dk.astype(proj.dtype), k_d], axis=1)
        vd_all = jnp.concatenate([dv.astype(proj.dtype), v_d], axis=1)
        o_b = window_sample(q_b, kb_all, vb_all, bk.shape[1], sinks)
        o_d = dilated_sample(q_d, kd_all, vd_all, dk.shape[1])
        new_st = (kb_all[:, -bk.shape[1]:].astype(bk.dtype), vb_all[:, -bv.shape[1]:].astype(bv.dtype),
                  kd_all[:, -dk.shape[1]:].astype(dk.dtype), vd_all[:, -dv.shape[1]:].astype(dv.dtype),
                  new_conv.astype(conv_buf.dtype), s_a_new.astype(s_a.dtype), s_c_new.astype(s_c.dtype))
    o = jnp.concatenate([gated_rmsnorm(o_a, gate_a, norm_a), o_b.reshape(B, T, W_B),
                         gated_rmsnorm(o_c, gate_c, norm_c), o_d.reshape(B, T, W_D)], axis=-1)
    return o.astype(proj.dtype), new_st


def swiglu(h, w_gu, w_down):
    gate, up = jnp.split(h @ w_gu, 2, axis=-1)
    return (jax.nn.silu(gate) * up) @ w_down


def moe_swiglu(h, w_router, b_router, w_gu, w_down):
    shp = h.shape
    x = h.reshape(-1, shp[-1])
    N, D = x.shape
    logits = jnp.matmul(x, w_router, preferred_element_type=F32) + b_router.astype(F32)
    top_logits, top_e = lax.top_k(logits, TOP_K)
    gates = jax.nn.softmax(top_logits, axis=-1)
    n_assign = N * TOP_K
    e_flat = top_e.reshape(-1)
    tok_flat = jnp.repeat(jnp.arange(N, dtype=jnp.int32), TOP_K)
    counts = jnp.zeros((N_EXPERTS,), jnp.int32).at[e_flat].add(1)
    padded = (counts + MOE_BLOCK - 1) // MOE_BLOCK * MOE_BLOCK
    pad_end = jnp.cumsum(padded)
    pad_start = pad_end - padded
    start = jnp.cumsum(counts) - counts
    order = jnp.argsort(e_flat)
    e_sorted = e_flat[order]
    dest = pad_start[e_sorted] + jnp.arange(n_assign, dtype=jnp.int32) - start[e_sorted]
    n_blocks = -(-(n_assign + N_EXPERTS * (MOE_BLOCK - 1)) // MOE_BLOCK)
    n_rows = n_blocks * MOE_BLOCK
    row_tok = jnp.full((n_rows,), N, jnp.int32).at[dest].set(tok_flat[order])
    row_gate = jnp.zeros((n_rows,), F32).at[dest].set(gates.reshape(-1)[order])
    blk_start = jnp.arange(n_blocks, dtype=jnp.int32) * MOE_BLOCK
    blk_e = jnp.minimum(jnp.sum(blk_start[:, None] >= pad_end[None, :], axis=1), N_EXPERTS - 1)
    x_rows = jnp.concatenate([x, jnp.zeros((1, D), x.dtype)], axis=0)[row_tok]
    def expert_block(args):
        xb, e = args
        return swiglu(xb, w_gu[e], w_down[e])
    y_rows = lax.map(expert_block, (x_rows.reshape(n_blocks, MOE_BLOCK, D), blk_e)).reshape(n_rows, D)
    y = jnp.zeros((N + 1, D), F32).at[row_tok].add(y_rows.astype(F32) * row_gate[:, None])
    return y[:N].astype(h.dtype).reshape(shp)


def decoder_layer(x, st, norm_mix, w_in, w_out, norm_ffn, mix_params, ffn):
    o, new_st = token_mixers(rmsnorm(x, norm_mix) @ w_in, *mix_params, st)
    x = x + o @ w_out
    return x + ffn(rmsnorm(x, norm_ffn)), new_st


def setup_inputs(seed: int = 0) -> dict:
    key = jax.random.key(seed)
    ks = iter(jax.random.split(key, 40))
    def nrm(shape, scale):
        return jax.random.normal(next(ks), shape, F32) * scale
    def gain(shape):
        return 1.0 + nrm(shape, 0.1)
    lb_len = min(WIN_B, PAST_LEN)
    ld_len = min(WIN_D, PAST_LEN)
    dt = jnp.exp(jax.random.uniform(next(ks), (DEPTH, H_A), F32, math.log(1e-3), math.log(1e-1)))
    return {
        'x_prompt': nrm((BATCH, SEQ, D_MODEL), 1.0),
        'x_sample': nrm((DEC_BATCH, DEC_SEQ, D_MODEL), 1.0),
        'cache_b_k': nrm((DEPTH, DEC_BATCH, lb_len, KV_B, HEAD_DIM), 1.0),
        'cache_b_v': nrm((DEPTH, DEC_BATCH, lb_len, KV_B, HEAD_DIM), 1.0),
        'cache_d_k': nrm((DEPTH, DEC_BATCH, ld_len, H_D, HEAD_DIM), 1.0),
        'cache_d_v': nrm((DEPTH, DEC_BATCH, ld_len, H_D, HEAD_DIM), 1.0),
        'state_a_conv': nrm((DEPTH, DEC_BATCH, CONV_A - 1, 3 * W_A), 1.0),
        'state_a_s': nrm((DEPTH, DEC_BATCH, H_A, HEAD_DIM, HEAD_DIM), 0.3),
        'state_c_s': nrm((DEPTH, DEC_BATCH, H_C, HEAD_DIM, HEAD_DIM), 1.0),
        'norm_mix': gain((DEPTH, D_MODEL)),
        'w_in': nrm((DEPTH, D_MODEL, PROJ_COLS), D_MODEL ** -0.5),
        'conv_a': nrm((DEPTH, CONV_A, 3 * W_A), CONV_A ** -0.5),
        'a_log': jnp.log(jax.random.uniform(next(ks), (DEPTH, H_A), F32, 1.0, 16.0)),
        'dt_bias': dt + jnp.log(-jnp.expm1(-dt)),
        'norm_a': gain((DEPTH, HEAD_DIM)),
        'sinks_b': nrm((DEPTH, H_B), 0.5),
        'lb_logits': nrm((DEPTH, W_C), 0.5),
        'norm_c': gain((DEPTH, HEAD_DIM)),
        'w_out': nrm((DEPTH, MIX_WIDTH, D_MODEL), MIX_WIDTH ** -0.5),
        'norm_ffn': gain((DEPTH, D_MODEL)),
        'w_ffn_gu': nrm((N_DENSE, D_MODEL, 2 * D_FF), D_MODEL ** -0.5),
        'w_ffn_down': nrm((N_DENSE, D_FF, D_MODEL), D_FF ** -0.5),
        'w_router': nrm((N_MOE, D_MODEL, N_EXPERTS), D_MODEL ** -0.5),
        'b_router': nrm((N_MOE, N_EXPERTS), 0.01),
        'w_moe_gu': nrm((N_MOE, N_EXPERTS, D_MODEL, 2 * D_FF_E), D_MODEL ** -0.5),
        'w_moe_down': nrm((N_MOE, N_EXPERTS, D_FF_E, D_MODEL), D_FF_E ** -0.5),
        'norm_final': gain((D_MODEL,)),
    }


def reference(x_prompt, x_sample, cache_b_k, cache_b_v, cache_d_k, cache_d_v, state_a_conv, state_a_s,
              state_c_s, norm_mix, w_in, conv_a, a_log, dt_bias, norm_a, sinks_b, lb_logits, norm_c, w_out,
              norm_ffn, w_ffn_gu, w_ffn_down, w_router, b_router, w_moe_gu, w_moe_down, norm_final):
    lb_p = jax.nn.softmax(lb_logits.astype(F32), axis=0)
    lower_bounds = jnp.cumsum(lb_p, axis=0) - lb_p[0]
    xp, xs = x_prompt, x_sample
    prompt_states, sample_states = [], []
    for l in range(DEPTH):
        mix = (conv_a[l], a_log[l], dt_bias[l], norm_a[l], sinks_b[l], lower_bounds[l], norm_c[l])
        if l % 2 == 0:
            ffn = functools.partial(swiglu, w_gu=w_ffn_gu[l // 2], w_down=w_ffn_down[l // 2])
        else:
            ffn = functools.partial(moe_swiglu, w_router=w_router[l // 2], b_router=b_router[l // 2],
                                    w_gu=w_moe_gu[l // 2], w_down=w_moe_down[l // 2])
        layer = functools.partial(decoder_layer, norm_mix=norm_mix[l], w_in=w_in[l], w_out=w_out[l],
                                  norm_ffn=norm_ffn[l], mix_params=mix, ffn=ffn)
        xp, st_p = layer(xp, None)
        prompt_states.append(st_p)
        xs, st_s = layer(xs, (cache_b_k[l], cache_b_v[l], cache_d_k[l], cache_d_v[l],
                              state_a_conv[l], state_a_s[l], state_c_s[l]))
        sample_states.append(st_s)
    p_b_k, p_b_v, p_d_k, p_d_v, p_a_conv, p_a_s, p_c_s = [jnp.stack(t, 0) for t in zip(*prompt_states)]
    s_b_k, s_b_v, s_d_k, s_d_v, s_a_conv, s_a_s, s_c_s = [jnp.stack(t, 0) for t in zip(*sample_states)]
    y_prompt = rmsnorm(xp, norm_final)
    y_sample = rmsnorm(xs, norm_final)
    return (y_prompt, y_sample, p_b_k, p_b_v, p_d_k, p_d_v, p_a_conv, p_a_s, p_c_s,
            s_b_k, s_b_v, s_d_k, s_d_v, s_a_conv, s_a_s, s_c_s)
```

```python
import functools
import math

import numpy as np
import jax
import jax.numpy as jnp
from jax import lax
from jax.experimental import pallas as pl
from jax.experimental.pallas import tpu as pltpu

F32 = jnp.float32
BF16 = jnp.bfloat16
HIGHEST = lax.Precision.HIGHEST

D_MODEL = 1024
HEAD_DIM = 64
N_HEADS = 4
KV_B = 2
W_MIX = N_HEADS * HEAD_DIM
CONV_A = 4
WIN = 128
DILATIONS = (1, 4, 16)
CACHE_D = 2048
CACHE_B = 128
D_FF = 2816
N_EXPERTS = 8
D_FF_E = 3584
EPS = 1e-6
SCALE = HEAD_DIM ** -0.5
NEG = -1e30

COLS_A = 1024
COLS_B = 512
COLS_C = 1024
COLS_D = 768
COLS_BA = 128
COLS_ALL = COLS_A + COLS_B + COLS_C + COLS_D + COLS_BA

CHUNK = 64
VMEM_LIMIT = 56 * 1024 * 1024


def _cparams(*sem):
    return pltpu.CompilerParams(dimension_semantics=sem, vmem_limit_bytes=VMEM_LIMIT)


def _mm(a, b):
    return jnp.dot(a.astype(BF16), b.astype(BF16), preferred_element_type=F32)


def _mm_nt(a, b):
    return lax.dot_general(a.astype(BF16), b.astype(BF16), (((1,), (1,)), ((), ())),
                           preferred_element_type=F32)


def _mm_tn(a, b):
    return lax.dot_general(a.astype(BF16), b.astype(BF16), (((0,), (0,)), ((), ())),
                           preferred_element_type=F32)


def _mm_f32(a, b):
    return jnp.dot(a, b, precision=HIGHEST, preferred_element_type=F32)


def _sigmoid(x):
    return 1.0 / (1.0 + jnp.exp(-x))


def _silu(x):
    return x * _sigmoid(x)


def _softplus(x):
    return jnp.maximum(x, 0.0) + jnp.log(1.0 + jnp.exp(-jnp.abs(x)))


def _rms_rows(x, gain):
    return x * lax.rsqrt(jnp.mean(x * x, axis=-1, keepdims=True) + EPS) * gain


def _inproj_kernel(x_ref, g_ref, w_ref, oa_ref, ob_ref, oc_ref, od_ref, oba_ref):
    h = _rms_rows(x_ref[...], g_ref[...]).astype(BF16)
    c = 0
    for o_ref in (oa_ref, ob_ref, oc_ref, od_ref, oba_ref):
        n = o_ref.shape[1]
        o_ref[...] = jnp.dot(h, w_ref[:, c:c + n], preferred_element_type=F32)
        c += n


def _inproj(x2, gain, w_perm, tm):
    n = x2.shape[0]
    widths = (COLS_A, COLS_B, COLS_C, COLS_D, COLS_BA)
    return pl.pallas_call(
        _inproj_kernel,
        grid=(n // tm,),
        in_specs=[pl.BlockSpec((tm, D_MODEL), lambda i: (i, 0)),
                  pl.BlockSpec((1, D_MODEL), lambda i: (0, 0)),
                  pl.BlockSpec((D_MODEL, COLS_ALL), lambda i: (0, 0))],
        out_specs=[pl.BlockSpec((tm, w), lambda i: (i, 0)) for w in widths],
        out_shape=[jax.ShapeDtypeStruct((n, w), F32) for w in widths],
        compiler_params=_cparams("parallel"),
        name="inproj",
    )(x2, gain, w_perm)


def _permute_w_in(w):
    ba = jnp.pad(w[:, 1024:1032], ((0, 0), (0, COLS_BA - 8)))
    return jnp.concatenate([w[:, 0:1024], w[:, 1032:3336], ba], axis=1).astype(BF16)


def _band_kernel(q_ref, kp_ref, kc_ref, vp_ref, vc_ref, sink_ref, *out_refs, kv, with_sink, with_lse):
    o_ref = out_refs[0]
    n = pl.program_id(2)
    q = q_ref[0] * SCALE
    kcat = jnp.concatenate([kp_ref[0], kc_ref[0]], axis=0)
    vcat = jnp.concatenate([vp_ref[0], vc_ref[0]], axis=0)
    row = lax.broadcasted_iota(jnp.int32, (WIN, 2 * WIN), 0)
    col = lax.broadcasted_iota(jnp.int32, (WIN, 2 * WIN), 1)
    dist = row + WIN - col
    valid = (dist >= 0) & (dist <= WIN) & ((col >= WIN) | (n > 0))
    group = N_HEADS // kv
    outs, lses = [], []
    for h in range(N_HEADS):
        g = h // group
        qh = q[:, h * HEAD_DIM:(h + 1) * HEAD_DIM]
        kh = kcat[:, g * HEAD_DIM:(g + 1) * HEAD_DIM]
        vh = vcat[:, g * HEAD_DIM:(g + 1) * HEAD_DIM]
        s = jnp.where(valid, _mm_nt(qh, kh), NEG)
        m = jnp.max(s, axis=-1, keepdims=True)
        if with_sink:
            sink = sink_ref[:, h:h + 1]
            m = jnp.maximum(m, sink)
        p = jnp.exp(s - m)
        l = jnp.sum(p, axis=-1, keepdims=True)
        if with_sink:
            l = l + jnp.exp(sink - m)
        outs.append(_mm(p, vh) / l)
        if with_lse:
            lses.append(jnp.broadcast_to(m + jnp.log(l), (WIN, HEAD_DIM)))
    o_ref[0] = jnp.concatenate(outs, axis=-1)
    if with_lse:
        out_refs[1][0] = jnp.concatenate(lses, axis=-1)


def _band_attention(p3, batch, seq, r, q_col, k_col, v_col, kv, sinks, with_lse):
    width = p3.shape[-1]
    ln = seq // r
    pv = p3.reshape(batch, ln, r * width)
    wq, wk = W_MIX, kv * HEAD_DIM
    qb, kb, vb = q_col // wq, k_col // wk, v_col // wk
    nq, nk = width // wq, width // wk
    cur = lambda off, per: (lambda b, rho, n: (b, n, rho * per + off))
    prev = lambda off, per: (lambda b, rho, n: (b, jnp.maximum(n - 1, 0), rho * per + off))
    out_spec = pl.BlockSpec((1, WIN, W_MIX), lambda b, rho, n: (b, n, rho))
    out_shape = jax.ShapeDtypeStruct((batch, ln, r * W_MIX), F32)
    n_out = 2 if with_lse else 1
    res = pl.pallas_call(
        functools.partial(_band_kernel, kv=kv, with_sink=sinks is not None, with_lse=with_lse),
        grid=(batch, r, ln // WIN),
        in_specs=[pl.BlockSpec((1, WIN, wq), cur(qb, nq)),
                  pl.BlockSpec((1, WIN, wk), prev(kb, nk)),
                  pl.BlockSpec((1, WIN, wk), cur(kb, nk)),
                  pl.BlockSpec((1, WIN, wk), prev(vb, nk)),
                  pl.BlockSpec((1, WIN, wk), cur(vb, nk)),
                  pl.BlockSpec((1, 128), lambda b, rho, n: (0, 0))],
        out_specs=[out_spec] * n_out,
        out_shape=[out_shape] * n_out,
        compiler_params=_cparams("parallel", "parallel", "arbitrary"),
        name=f"band_r{r}",
    )(pv, pv, pv, pv, pv, sinks if sinks is not None else jnp.zeros((1, 128), F32))
    return [t.reshape(batch * seq, W_MIX) for t in res]


def _gated_norm_rows(o, gate, gain):
    return _rms_rows(o, gain) * _silu(gate)


def _hgrn_constants(c):
    halves = []
    h = c // 2
    while h >= 1:
        halves.append(h)
        h //= 2
    t = np.arange(c)[:, None]
    u = np.arange(c)[None, :]
    mats = [(u <= t).astype(np.float32)]
    level = np.full((c, c), -1, np.int32)
    level[np.arange(c), np.arange(c)] = 0
    for li, h in enumerate(halves, 1):
        mid = (t // (2 * h)) * (2 * h) + h
        second = (t % (2 * h)) >= h
        mats.append(np.where(second, (u > mid) & (u <= t), (u > t) & (u <= mid)).astype(np.float32))
        pair = (t // (2 * h) == u // (2 * h)) & ((t % (2 * h)) >= h) & ((u % (2 * h)) < h)
        level[pair] = li
    return np.concatenate(mats, axis=0), level, len(halves)


def _hgrn_kernel(pc_ref, lb_ref, gain_ref, mat_ref, lvl_ref, o_ref, st_ref, s_scr, *, c, n_levels):
    j = pl.program_id(1)

    @pl.when(j == 0)
    def _():
        s_scr[...] = jnp.zeros_like(s_scr)

    x = pc_ref[0]
    q = x[:, 0:256]
    lb = lb_ref[...]
    f = lb + (1.0 - lb) * _sigmoid(x[:, 256:512])
    k = 1.0 - f
    g = jnp.log(f)
    v = x[:, 512:768]
    gate = x[:, 768:1024]
    sums = _mm_f32(mat_ref[...], -g)
    gc = -sums[0:c]
    g_last = gc[c - 1:c, :]
    q_dec = q * jnp.exp(gc)
    k_dec = k * jnp.exp(g_last - gc)
    lvl = lvl_ref[...]
    damp = [jnp.exp(-sums[li * c:(li + 1) * c]) for li in range(1, n_levels + 1)]
    outs = []
    for h in range(N_HEADS):
        sl = slice(h * HEAD_DIM, (h + 1) * HEAD_DIM)
        qh, kh, vh = q[:, sl], k[:, sl], v[:, sl]
        a = jnp.where(lvl == 0, _mm_nt(qh, kh), 0.0)
        for li in range(1, n_levels + 1):
            d = damp[li - 1][:, sl]
            a = a + jnp.where(lvl == li, _mm_nt(qh * d, kh * d), 0.0)
        st = s_scr[h]
        o = _mm_nt(q_dec[:, sl], st) + _mm(a, vh)
        s_scr[h] = jnp.exp(g_last[:, sl]) * st + _mm_tn(vh, k_dec[:, sl])
        outs.append(_gated_norm_rows(o, gate[:, sl], gain_ref[...]))
    o_ref[0] = jnp.concatenate(outs, axis=-1)

    @pl.when(j == pl.num_programs(1) - 1)
    def _():
        st_ref[0] = s_scr[...]


def _hgrn_prompt(pc3, lb, gain):
    batch, seq, _ = pc3.shape
    c = CHUNK
    mat, level, n_levels = _hgrn_constants(c)
    o, st = pl.pallas_call(
        functools.partial(_hgrn_kernel, c=c, n_levels=n_levels),
        grid=(batch, seq // c),
        in_specs=[pl.BlockSpec((1, c, COLS_C), lambda b, j: (b, j, 0)),
                  pl.BlockSpec((1, W_MIX), lambda b, j: (0, 0)),
                  pl.BlockSpec((1, HEAD_DIM), lambda b, j: (0, 0)),
                  pl.BlockSpec(mat.shape, lambda b, j: (0, 0)),
                  pl.BlockSpec(level.shape, lambda b, j: (0, 0))],
        out_specs=[pl.BlockSpec((1, c, W_MIX), lambda b, j: (b, j, 0)),
                   pl.BlockSpec((1, N_HEADS, HEAD_DIM, HEAD_DIM), lambda b, j: (b, 0, 0, 0))],
        out_shape=[jax.ShapeDtypeStruct((batch, seq, W_MIX), F32),
                   jax.ShapeDtypeStruct((batch, N_HEADS, HEAD_DIM, HEAD_DIM), F32)],
        scratch_shapes=[pltpu.VMEM((N_HEADS, HEAD_DIM, HEAD_DIM), F32)],
        compiler_params=_cparams("parallel", "arbitrary"),
        name="hgrn_prompt",
    )(pc3, lb, gain, jnp.asarray(mat), jnp.asarray(level))
    return o.reshape(batch * seq, W_MIX), jnp.swapaxes(st, -1, -2)


def _unit_lower_solve(low, rhs, c):
    x = rhs - _mm(low, rhs)
    p = low
    span = 2
    while span < c:
        p = _mm(p, p)
        x = x + _mm(p, x)
        span *= 2
    return x


def _gdn_kernel(pa_ref, pba_ref, cw_ref, hp_ref, gain_ref, tri_ref, ones_ref, o_ref, st_ref,
                s_scr, buf_scr, *, c):
    j = pl.program_id(1)
    pad = 8

    @pl.when(j == 0)
    def _():
        s_scr[...] = jnp.zeros_like(s_scr)
        buf_scr[0:pad, :] = jnp.zeros((pad, 3 * W_MIX), F32)

    x = pa_ref[0, :, 0:3 * W_MIX]
    gate = pa_ref[0, :, 3 * W_MIX:4 * W_MIX]
    buf_scr[pad:pad + c, :] = x
    y = cw_ref[CONV_A - 1:CONV_A, :] * x
    for tap in range(CONV_A - 1):
        back = CONV_A - 1 - tap
        y = y + cw_ref[tap:tap + 1, :] * buf_scr[pad - back:pad - back + c, :]
    buf_scr[0:pad, :] = buf_scr[c:c + pad, :]
    y = _silu(y)
    ones = ones_ref[...]
    q = y[:, 0:W_MIX]
    k = y[:, W_MIX:2 * W_MIX]
    v = y[:, 2 * W_MIX:3 * W_MIX]
    q = q * lax.rsqrt(_mm_f32(q * q, ones) + EPS) * SCALE
    k = k * lax.rsqrt(_mm_f32(k * k, ones) + EPS)
    ba = pba_ref[0]
    beta = _sigmoid(ba)
    g = hp_ref[0:1, :] * _softplus(ba + hp_ref[1:2, :])
    gc = _mm_f32(tri_ref[0], g)
    gc_t = lax.dot_general(g, tri_ref[1], (((0,), (0,)), ((), ())), precision=HIGHEST,
                           preferred_element_type=F32)
    row = lax.broadcasted_iota(jnp.int32, (c, c), 0)
    col = lax.broadcasted_iota(jnp.int32, (c, c), 1)
    outs = []
    for h in range(N_HEADS):
        sl = slice(h * HEAD_DIM, (h + 1) * HEAD_DIM)
        qh, kh, vh = q[:, sl], k[:, sl], v[:, sl]
        b_col = beta[:, h:h + 1]
        g_col = gc[:, 4 + h:5 + h]
        g_row = gc_t[4 + h:5 + h, :]
        decay = jnp.where(row >= col, jnp.exp(jnp.minimum(g_col - g_row, 0.0)), 0.0)
        low = jnp.where(row > col, b_col * _mm_nt(kh, kh) * decay, 0.0)
        eg = jnp.exp(g_col)
        rhs = jnp.concatenate([vh * b_col, kh * (b_col * eg)], axis=-1)
        sol = _unit_lower_solve(low, rhs, c)
        u = sol[:, 0:HEAD_DIM]
        w = sol[:, HEAD_DIM:2 * HEAD_DIM]
        intra = _mm_nt(qh, kh) * decay
        g_last = g_col[c - 1:c, :]
        st = s_scr[h]
        u = u - _mm_nt(w, st)
        o = _mm_nt(qh * eg, st) + _mm(intra, u)
        s_scr[h] = jnp.exp(g_last) * st + _mm_tn(u, kh * jnp.exp(g_last - g_col))
        outs.append(_gated_norm_rows(o, gate[:, sl], gain_ref[...]))
    o_ref[0] = jnp.concatenate(outs, axis=-1)

    @pl.when(j == pl.num_programs(1) - 1)
    def _():
        st_ref[0] = s_scr[...]


def _gdn_prompt(pa3, pba3, conv_w, head_params, gain):
    batch, seq, _ = pa3.shape
    c = CHUNK
    lower = np.tril(np.ones((c, c), np.float32))
    tri = np.stack([lower, lower.T], axis=0)
    ones = np.kron(np.eye(N_HEADS, dtype=np.float32), np.ones((HEAD_DIM, HEAD_DIM), np.float32))
    o, st = pl.pallas_call(
        functools.partial(_gdn_kernel, c=c),
        grid=(batch, seq // c),
        in_specs=[pl.BlockSpec((1, c, COLS_A), lambda b, j: (b, j, 0)),
                  pl.BlockSpec((1, c, COLS_BA), lambda b, j: (b, j, 0)),
                  pl.BlockSpec((CONV_A, 3 * W_MIX), lambda b, j: (0, 0)),
                  pl.BlockSpec((2, COLS_BA), lambda b, j: (0, 0)),
                  pl.BlockSpec((1, HEAD_DIM), lambda b, j: (0, 0)),
                  pl.BlockSpec((2, c, c), lambda b, j: (0, 0, 0)),
                  pl.BlockSpec((W_MIX, W_MIX), lambda b, j: (0, 0))],
        out_specs=[pl.BlockSpec((1, c, W_MIX), lambda b, j: (b, j, 0)),
                   pl.BlockSpec((1, N_HEADS, HEAD_DIM, HEAD_DIM), lambda b, j: (b, 0, 0, 0))],
        out_shape=[jax.ShapeDtypeStruct((batch, seq, W_MIX), F32),
                   jax.ShapeDtypeStruct((batch, N_HEADS, HEAD_DIM, HEAD_DIM), F32)],
        scratch_shapes=[pltpu.VMEM((N_HEADS, HEAD_DIM, HEAD_DIM), F32),
                        pltpu.VMEM((c + 8, 3 * W_MIX), F32)],
        compiler_params=_cparams("parallel", "arbitrary"),
        name="gdn_prompt",
    )(pa3, pba3, conv_w, head_params, gain, jnp.asarray(tri), jnp.asarray(ones))
    return o.reshape(batch * seq, W_MIX), jnp.swapaxes(st, -1, -2)


def _gdn_head_params(a_log, dt_bias):
    neg_a = jnp.pad(-jnp.exp(a_log.astype(F32)), (4, COLS_BA - 8))
    dtb = jnp.pad(dt_bias.astype(F32), (4, COLS_BA - 8))
    return jnp.stack([neg_a, dtb], axis=0)


def _outproj_kernel(*refs, n_branch):
    x_ref, oa_ref, ob_ref, oc_ref = refs[0:4]
    d_refs = refs[4:4 + n_branch]
    l_refs = refs[4 + n_branch:4 + 2 * n_branch - (0 if n_branch > 1 else 1)]
    w_ref, o_ref = refs[-2], refs[-1]
    if n_branch == 1:
        od = d_refs[0][...]
    else:
        lses = [r[...] for r in l_refs]
        m = functools.reduce(jnp.maximum, lses)
        es = [jnp.exp(l - m) for l in lses]
        num = sum(e * r[...] for e, r in zip(es, d_refs))
        od = num / sum(es)
    acc = x_ref[...]
    for i, part in enumerate((oa_ref[...], ob_ref[...], oc_ref[...], od)):
        acc = acc + _mm(part, w_ref[i * W_MIX:(i + 1) * W_MIX, :])
    o_ref[...] = acc


def _outproj(x2, oa, ob, oc, ods, lses, w_bf, tm):
    n = x2.shape[0]
    n_branch = len(ods)
    row = lambda w: pl.BlockSpec((tm, w), lambda i: (i, 0))
    args = [x2, oa, ob, oc, *ods, *lses, w_bf]
    specs = [row(D_MODEL)] + [row(W_MIX)] * (3 + n_branch + len(lses))
    specs.append(pl.BlockSpec((4 * W_MIX, D_MODEL), lambda i: (0, 0)))
    return pl.pallas_call(
        functools.partial(_outproj_kernel, n_branch=n_branch),
        grid=(n // tm,),
        in_specs=specs,
        out_specs=row(D_MODEL),
        out_shape=jax.ShapeDtypeStruct((n, D_MODEL), F32),
        compiler_params=_cparams("parallel"),
        name="outproj",
    )(*args)


def _ffn_kernel(x_ref, g_ref, wg_ref, wu_ref, wd_ref, o_ref, h_scr, acc_scr):
    f = pl.program_id(1)

    @pl.when(f == 0)
    def _():
        h_scr[...] = _rms_rows(x_ref[...], g_ref[...]).astype(BF16)
        acc_scr[...] = jnp.zeros_like(acc_scr)

    h = h_scr[...]
    gate = jnp.dot(h, wg_ref[...], preferred_element_type=F32)
    up = jnp.dot(h, wu_ref[...], preferred_element_type=F32)
    acc_scr[...] += _mm(_silu(gate) * up, wd_ref[...])

    @pl.when(f == pl.num_programs(1) - 1)
    def _():
        o_ref[...] = x_ref[...] + acc_scr[...]


def _ffn(x2, gain, w_gu_bf, w_down_bf, tm, tf=256):
    n = x2.shape[0]
    nf = D_FF // tf
    return pl.pallas_call(
        _ffn_kernel,
        grid=(n // tm, nf),
        in_specs=[pl.BlockSpec((tm, D_MODEL), lambda i, f: (i, 0)),
                  pl.BlockSpec((1, D_MODEL), lambda i, f: (0, 0)),
                  pl.BlockSpec((D_MODEL, tf), lambda i, f: (0, f)),
                  pl.BlockSpec((D_MODEL, tf), lambda i, f: (0, nf + f)),
                  pl.BlockSpec((tf, D_MODEL), lambda i, f: (f, 0))],
        out_specs=pl.BlockSpec((tm, D_MODEL), lambda i, f: (i, 0)),
        out_shape=jax.ShapeDtypeStruct((n, D_MODEL), F32),
        scratch_shapes=[pltpu.VMEM((tm, D_MODEL), BF16), pltpu.VMEM((tm, D_MODEL), F32)],
        compiler_params=_cparams("parallel", "arbitrary"),
        name="ffn",
    )(x2, gain, w_gu_bf, w_gu_bf, w_down_bf)


def _moe_kernel(x_ref, g_ref, wr_ref, br_ref, wg_ref, wu_ref, wd_ref, o_ref, h_scr, gate_scr, acc_scr):
    e = pl.program_id(1)
    f = pl.program_id(2)
    tm = x_ref.shape[0]
    lane = lax.broadcasted_iota(jnp.int32, (tm, 128), 1).astype(F32)

    @pl.when((e == 0) & (f == 0))
    def _():
        h = _rms_rows(x_ref[...], g_ref[...])
        h_scr[...] = h.astype(BF16)
        logits = _mm_f32(h, wr_ref[...]) + br_ref[...]
        m1 = jnp.max(logits, axis=-1, keepdims=True)
        i1 = jnp.min(jnp.where(logits == m1, lane, 128.0), axis=-1, keepdims=True)
        rest = jnp.where(lane == i1, NEG, logits)
        m2 = jnp.max(rest, axis=-1, keepdims=True)
        i2 = jnp.min(jnp.where(rest == m2, lane, 128.0), axis=-1, keepdims=True)
        e2 = jnp.exp(m2 - m1)
        gate_scr[...] = jnp.where(lane == i1, 1.0 / (1.0 + e2), 0.0) + jnp.where(lane == i2, e2 / (1.0 + e2), 0.0)
        acc_scr[...] = jnp.zeros_like(acc_scr)

    ge = jnp.sum(jnp.where(lane == e.astype(F32), gate_scr[...], 0.0), axis=-1, keepdims=True)
    h = h_scr[...]
    gate = jnp.dot(h, wg_ref[0], preferred_element_type=F32)
    up = jnp.dot(h, wu_ref[0], preferred_element_type=F32)
    acc_scr[...] += _mm(_silu(gate) * up * ge, wd_ref[0])

    @pl.when((e == N_EXPERTS - 1) & (f == pl.num_programs(2) - 1))
    def _():
        o_ref[...] = x_ref[...] + acc_scr[...]


def _moe(x2, gain, w_router, b_router, w_gu_bf, w_down_bf, tm, tf=256):
    n = x2.shape[0]
    nf = D_FF_E // tf
    wr = jnp.pad(w_router.astype(F32), ((0, 0), (0, 128 - N_EXPERTS)))
    br = jnp.pad(b_router.astype(F32), (0, 128 - N_EXPERTS), constant_values=NEG).reshape(1, 128)
    return pl.pallas_call(
        _moe_kernel,
        grid=(n // tm, N_EXPERTS, nf),
        in_specs=[pl.BlockSpec((tm, D_MODEL), lambda i, e, f: (i, 0)),
                  pl.BlockSpec((1, D_MODEL), lambda i, e, f: (0, 0)),
                  pl.BlockSpec((D_MODEL, 128), lambda i, e, f: (0, 0)),
                  pl.BlockSpec((1, 128), lambda i, e, f: (0, 0)),
                  pl.BlockSpec((1, D_MODEL, tf), lambda i, e, f: (e, 0, f)),
                  pl.BlockSpec((1, D_MODEL, tf), lambda i, e, f: (e, 0, nf + f)),
                  pl.BlockSpec((1, tf, D_MODEL), lambda i, e, f: (e, f, 0))],
        out_specs=pl.BlockSpec((tm, D_MODEL), lambda i, e, f: (i, 0)),
        out_shape=jax.ShapeDtypeStruct((n, D_MODEL), F32),
        scratch_shapes=[pltpu.VMEM((tm, D_MODEL), BF16), pltpu.VMEM((tm, 128), F32),
                        pltpu.VMEM((tm, D_MODEL), F32)],
        compiler_params=_cparams("parallel", "arbitrary", "arbitrary"),
        name="moe",
    )(x2, gain, wr, br, w_gu_bf, w_gu_bf, w_down_bf)


def _norm_kernel(x_ref, g_ref, o_ref):
    o_ref[...] = _rms_rows(x_ref[...], g_ref[...])


def _final_norm(x2, gain, tm):
    n = x2.shape[0]
    return pl.pallas_call(
        _norm_kernel,
        grid=(n // tm,),
        in_specs=[pl.BlockSpec((tm, D_MODEL), lambda i: (i, 0)),
                  pl.BlockSpec((1, D_MODEL), lambda i: (0, 0))],
        out_specs=pl.BlockSpec((tm, D_MODEL), lambda i: (i, 0)),
        out_shape=jax.ShapeDtypeStruct((n, D_MODEL), F32),
        compiler_params=_cparams("parallel"),
        name="final_norm",
    )(x2, gain)


ROWS_T = 8
QROWS = N_HEADS * ROWS_T


def _softmax_two_parts(sc, sn, vc, vn, valid_c, valid_n, sink):
    sc = jnp.where(valid_c, sc, NEG)
    sn = jnp.where(valid_n, sn, NEG)
    m = jnp.maximum(jnp.max(sc, axis=-1, keepdims=True), jnp.max(sn, axis=-1, keepdims=True))
    if sink is not None:
        m = jnp.maximum(m, sink)
    pc = jnp.exp(sc - m)
    pn = jnp.exp(sn - m)
    l = jnp.sum(pc, axis=-1, keepdims=True) + jnp.sum(pn, axis=-1, keepdims=True)
    if sink is not None:
        l = l + jnp.exp(sink - m)
    return (_mm(pc, vc) + _mm(pn, vn)) / l, m + jnp.log(l)


def _sattn_kernel(qd_ref, kdn_ref, vdn_ref, kdc_ref, vdc_ref, qb_ref, kbn_ref, vbn_ref, kbc_ref, vbc_ref,
                  sink_ref, od_ref, ob_ref, *, bb, t_new):
    def dist_maps(n_cache):
        row = lax.broadcasted_iota(jnp.int32, (QROWS, n_cache), 0)
        col = lax.broadcasted_iota(jnp.int32, (QROWS, n_cache), 1)
        d_cache = n_cache + (row & (ROWS_T - 1)) - col
        rown = lax.broadcasted_iota(jnp.int32, (QROWS, 128), 0)
        coln = lax.broadcasted_iota(jnp.int32, (QROWS, 128), 1)
        d_new = (rown & (ROWS_T - 1)) - coln
        return d_cache, d_new, (coln < t_new) & (d_new >= 0)

    dc_d, dn_d, ok_new_d = dist_maps(CACHE_D)
    dc_b, dn_b, ok_new_b = dist_maps(CACHE_B)
    hrow = lax.broadcasted_iota(jnp.int32, (QROWS, W_MIX), 0) >> int(math.log2(ROWS_T))
    hlane = lax.broadcasted_iota(jnp.int32, (QROWS, W_MIX), 1) >> int(math.log2(HEAD_DIM))
    own = hrow == hlane
    zpad_d = jnp.zeros((128 - ROWS_T, W_MIX), F32)
    zpad_b = jnp.zeros((128 - ROWS_T, KV_B * HEAD_DIM), F32)
    zq = jnp.zeros((ROWS_T, HEAD_DIM), F32)
    for i in range(bb):
        q = qd_ref[i] * SCALE
        qblk = jnp.where(own, jnp.concatenate([q] * N_HEADS, axis=0), 0.0)
        kc, vc = kdc_ref[i], vdc_ref[i]
        kn = jnp.concatenate([kdn_ref[i], zpad_d], axis=0)
        vn = jnp.concatenate([vdn_ref[i], zpad_d], axis=0)
        sc = _mm_nt(qblk, kc)
        sn = _mm_nt(qblk, kn)
        outs, lses = [], []
        for r in DILATIONS:
            ok_c = (dc_d <= WIN * r) & ((dc_d & (r - 1)) == 0)
            ok_n = ok_new_d & ((dn_d & (r - 1)) == 0)
            o, lse = _softmax_two_parts(sc, sn, vc, vn, ok_c, ok_n, None)
            outs.append(o)
            lses.append(lse)
        m = functools.reduce(jnp.maximum, lses)
        es = [jnp.exp(l - m) for l in lses]
        o = sum(e * t for e, t in zip(es, outs)) / sum(es)
        o = jnp.where(own, o, 0.0)
        od_ref[i] = sum(o[h * ROWS_T:(h + 1) * ROWS_T] for h in range(N_HEADS))
        qb = qb_ref[i] * SCALE
        blocks = []
        for h in range(N_HEADS):
            qh = qb[:, h * HEAD_DIM:(h + 1) * HEAD_DIM]
            blocks.append(jnp.concatenate([qh, zq] if h < N_HEADS // KV_B else [zq, qh], axis=-1))
        qblk = jnp.concatenate(blocks, axis=0)
        kn = jnp.concatenate([kbn_ref[i], zpad_b], axis=0)
        vn = jnp.concatenate([vbn_ref[i], zpad_b], axis=0)
        o, _ = _softmax_two_parts(_mm_nt(qblk, kbc_ref[i]), _mm_nt(qblk, kn), vbc_ref[i], vn,
                                  dc_b <= WIN, ok_new_b, sink_ref[...])
        pieces = []
        for h in range(N_HEADS):
            g = h // (N_HEADS // KV_B)
            pieces.append(o[h * ROWS_T:(h + 1) * ROWS_T, g * HEAD_DIM:(g + 1) * HEAD_DIM])
        ob_ref[i] = jnp.concatenate(pieces, axis=-1)


def _pad_rows(t, rows):
    return jnp.pad(t, ((0, 0), (0, rows - t.shape[1]), (0, 0)))


def _sample_attention(pb_s, pd_s, cbk, cbv, cdk, cdv, sinks, bb=2):
    bs, t_new, _ = pb_s.shape
    pb8, pd8 = _pad_rows(pb_s, ROWS_T), _pad_rows(pd_s, ROWS_T)
    sink_col = jnp.repeat(sinks.astype(F32), ROWS_T).reshape(QROWS, 1)
    wb = KV_B * HEAD_DIM
    blk = lambda rows, w, c: pl.BlockSpec((bb, rows, w), lambda i: (i, 0, c))
    od, ob = pl.pallas_call(
        functools.partial(_sattn_kernel, bb=bb, t_new=t_new),
        grid=(bs // bb,),
        in_specs=[blk(ROWS_T, W_MIX, 0), blk(ROWS_T, W_MIX, 1), blk(ROWS_T, W_MIX, 2),
                  blk(CACHE_D, W_MIX, 0), blk(CACHE_D, W_MIX, 0),
                  blk(ROWS_T, W_MIX, 0), blk(ROWS_T, wb, 2), blk(ROWS_T, wb, 3),
                  blk(CACHE_B, wb, 0), blk(CACHE_B, wb, 0),
                  pl.BlockSpec((QROWS, 1), lambda i: (0, 0))],
        out_specs=[blk(ROWS_T, W_MIX, 0), blk(ROWS_T, W_MIX, 0)],
        out_shape=[jax.ShapeDtypeStruct((bs, ROWS_T, W_MIX), F32)] * 2,
        compiler_params=_cparams("parallel"),
        name="sample_attn",
    )(pd8, pd8, pd8, cdk, cdv, pb8, pb8, pb8, cbk, cbv, sink_col)
    return (ob[:, :t_new].reshape(bs * t_new, W_MIX), od[:, :t_new].reshape(bs * t_new, W_MIX))


def _srec_kernel(xq_ref, xk_ref, xv_ref, bq_ref, bk_ref, bv_ref, cwq_ref, cwk_ref, cwv_ref, ga_ref,
                 ba_ref, hp_ref, gna_ref, sa_ref, qc_ref, fc_ref, ic_ref, gcg_ref, lb_ref, gnc_ref, sc_ref,
                 oa_ref, sa_out, oc_ref, sc_out, q_scr, k_scr, d_scr, *, t_new):
    nb = sa_ref.shape[-1]
    zero = jnp.zeros((HEAD_DIM, nb), F32)

    def conv(x_ref, b_ref, cw_ref, t):
        y = None
        for tap in range(CONV_A):
            pos = t + tap
            src = b_ref[pos] if pos < CONV_A - 1 else x_ref[pos - (CONV_A - 1)]
            term = cw_ref[tap] * src
            y = term if y is None else y + term
        return _silu(y)

    def l2(x):
        return x * lax.rsqrt(jnp.sum(x * x, axis=0, keepdims=True) + EPS)

    def gated_norm(o, gate, gain):
        return o * lax.rsqrt(jnp.mean(o * o, axis=0, keepdims=True) + EPS) * gain * _silu(gate)

    sa_out[0] = sa_ref[0]
    for t in range(t_new):
        q_scr[...] = l2(conv(xq_ref, bq_ref, cwq_ref, t)) * SCALE
        k_scr[...] = l2(conv(xk_ref, bk_ref, cwk_ref, t))
        v = conv(xv_ref, bv_ref, cwv_ref, t)
        beta = _sigmoid(ba_ref[0, t:t + 1, :])
        dec = jnp.exp(hp_ref[0, 0:1, :] * _softplus(ba_ref[0, t_new + t:t_new + t + 1, :] + hp_ref[0, 1:2, :]))

        def decay_and_read(kk, acc):
            s = sa_out[0, kk] * dec
            sa_out[0, kk] = s
            return acc + k_scr[pl.ds(kk, 1), :] * s

        err = (v - lax.fori_loop(0, HEAD_DIM, decay_and_read, zero)) * beta

        def write_and_query(kk, acc):
            s = sa_out[0, kk] + k_scr[pl.ds(kk, 1), :] * err
            sa_out[0, kk] = s
            return acc + q_scr[pl.ds(kk, 1), :] * s

        o = lax.fori_loop(0, HEAD_DIM, write_and_query, zero)
        oa_ref[t] = gated_norm(o, ga_ref[t], gna_ref[...])

    sc_out[0] = sc_ref[0]
    for t in range(t_new):
        lb = lb_ref[...]
        f = lb + (1.0 - lb) * _sigmoid(fc_ref[t])
        q_scr[...] = qc_ref[t]
        k_scr[...] = 1.0 - f
        d_scr[...] = jnp.exp(jnp.log(f))
        v = ic_ref[t]

        def update(kk, acc):
            s = sc_out[0, kk] * d_scr[pl.ds(kk, 1), :] + k_scr[pl.ds(kk, 1), :] * v
            sc_out[0, kk] = s
            return acc + q_scr[pl.ds(kk, 1), :] * s

        o = lax.fori_loop(0, HEAD_DIM, update, zero)
        oc_ref[t] = gated_norm(o, gcg_ref[t], gnc_ref[...])


def _sample_recurrences(pa_s, pba_s, pc_s, conv_buf, s_a, s_c, conv_w, a_log, dt_bias, norm_a, lb, norm_c):
    bs, t_new, _ = pa_s.shape
    lanes_last = lambda t: jnp.transpose(t, (1, 2, 0))
    pa_t = lanes_last(pa_s)
    pc_t = lanes_last(pc_s)
    buf_t = lanes_last(conv_buf.astype(F32))
    ba = jnp.transpose(pba_s[:, :, 0:8], (2, 1, 0))
    ba = jnp.concatenate([ba[0:N_HEADS], ba[N_HEADS:2 * N_HEADS]], axis=1)
    hp = jnp.stack([-jnp.exp(a_log.astype(F32)), dt_bias.astype(F32)], axis=1)
    hp = jnp.broadcast_to(hp[:, :, None], (N_HEADS, 2, bs))
    cw = conv_w.astype(F32)[:, :, None]
    sa_t = jnp.transpose(s_a.astype(F32), (1, 2, 3, 0))
    sc_t = jnp.transpose(s_c.astype(F32), (1, 2, 3, 0))
    col = lambda v: v.astype(F32).reshape(-1, 1)
    hd = HEAD_DIM
    feat = lambda rows, off: pl.BlockSpec((rows, hd, bs), lambda h: (0, off + h, 0))
    cwspec = lambda off: pl.BlockSpec((CONV_A, hd, 1), lambda h: (0, off + h, 0))
    per_head = lambda rows: pl.BlockSpec((1, rows, bs), lambda h: (h, 0, 0))
    state = pl.BlockSpec((1, hd, hd, bs), lambda h: (h, 0, 0, 0))
    vec = pl.BlockSpec((hd, 1), lambda h: (0, 0))
    nh = N_HEADS
    oa, sa_n, oc, sc_n = pl.pallas_call(
        functools.partial(_srec_kernel, t_new=t_new),
        grid=(N_HEADS,),
        in_specs=[feat(t_new, 0), feat(t_new, nh), feat(t_new, 2 * nh),
                  feat(CONV_A - 1, 0), feat(CONV_A - 1, nh), feat(CONV_A - 1, 2 * nh),
                  cwspec(0), cwspec(nh), cwspec(2 * nh),
                  feat(t_new, 3 * nh), per_head(2 * t_new), per_head(2), vec, state,
                  feat(t_new, 0), feat(t_new, nh), feat(t_new, 2 * nh), feat(t_new, 3 * nh),
                  pl.BlockSpec((hd, 1), lambda h: (h, 0)), vec, state],
        out_specs=[feat(t_new, 0), state, feat(t_new, 0), state],
        out_shape=[jax.ShapeDtypeStruct((t_new, W_MIX, bs), F32),
                   jax.ShapeDtypeStruct((N_HEADS, hd, hd, bs), F32),
                   jax.ShapeDtypeStruct((t_new, W_MIX, bs), F32),
                   jax.ShapeDtypeStruct((N_HEADS, hd, hd, bs), F32)],
        scratch_shapes=[pltpu.VMEM((hd, bs), F32)] * 3,
        compiler_params=_cparams("parallel"),
        name="sample_recurrences",
    )(pa_t, pa_t, pa_t, buf_t, buf_t, buf_t, cw, cw, cw, pa_t, ba, hp, col(norm_a), sa_t,
      pc_t, pc_t, pc_t, pc_t, col(lb), col(norm_c), sc_t)
    rows_first = lambda t: jnp.transpose(t, (2, 0, 1)).reshape(bs * t_new, W_MIX)
    back = lambda t: jnp.transpose(t, (3, 0, 1, 2))
    return rows_first(oa), back(sa_n), rows_first(oc), back(sc_n)


def _heads(t, n):
    return t.reshape(t.shape[0], t.shape[1], n, HEAD_DIM)


def _prompt_mixers(pa, pb, pc, pd, pba, batch, seq, conv_w, head_params, norm_a, sinks, lb, norm_c):
    pa3 = pa.reshape(batch, seq, COLS_A)
    pb3 = pb.reshape(batch, seq, COLS_B)
    pc3 = pc.reshape(batch, seq, COLS_C)
    pd3 = pd.reshape(batch, seq, COLS_D)
    oa, s_a = _gdn_prompt(pa3, pba.reshape(batch, seq, COLS_BA), conv_w, head_params, norm_a)
    oc, s_c = _hgrn_prompt(pc3, lb, norm_c)
    sink_row = jnp.pad(sinks.astype(F32), (0, 128 - N_HEADS)).reshape(1, 128)
    (ob,) = _band_attention(pb3, batch, seq, 1, 0, 256, 384, KV_B, sink_row, False)
    ods, lses = [], []
    for r in DILATIONS:
        o, lse = _band_attention(pd3, batch, seq, r, 0, 256, 512, N_HEADS, None, True)
        ods.append(o)
        lses.append(lse)
    nb, nd = min(CACHE_B, seq), min(CACHE_D, seq)
    state = (_heads(pb3[:, seq - nb:, 256:384], KV_B), _heads(pb3[:, seq - nb:, 384:512], KV_B),
             _heads(pd3[:, seq - nd:, 256:512], N_HEADS), _heads(pd3[:, seq - nd:, 512:768], N_HEADS),
             pa3[:, seq - (CONV_A - 1):, 0:3 * W_MIX], s_a, s_c)
    return oa, ob, oc, ods, lses, state


def _sample_mixers(pa, pb, pc, pd, pba, bs, t_new, caches, conv_w, a_log, dt_bias, norm_a, sinks, lb, norm_c):
    cbk, cbv, cdk, cdv, conv_buf, s_a, s_c = caches
    pa3 = pa.reshape(bs, t_new, COLS_A)
    pb3 = pb.reshape(bs, t_new, COLS_B)
    pc3 = pc.reshape(bs, t_new, COLS_C)
    pd3 = pd.reshape(bs, t_new, COLS_D)
    flat = lambda c: c.reshape(c.shape[0], c.shape[1], -1)
    ob, od = _sample_attention(pb3, pd3, flat(cbk), flat(cbv), flat(cdk), flat(cdv), sinks)
    oa, s_a_new, oc, s_c_new = _sample_recurrences(
        pa3, pba.reshape(bs, t_new, COLS_BA), pc3, conv_buf, s_a, s_c, conv_w, a_log, dt_bias, norm_a, lb, norm_c)
    roll = lambda cache, new, n: jnp.concatenate([cache[:, t_new:], _heads(new, n).astype(cache.dtype)], axis=1)
    conv_all = jnp.concatenate([conv_buf.astype(F32), pa3[:, :, 0:3 * W_MIX]], axis=1)
    state = (roll(cbk, pb3[:, :, 256:384], KV_B), roll(cbv, pb3[:, :, 384:512], KV_B),
             roll(cdk, pd3[:, :, 256:512], N_HEADS), roll(cdv, pd3[:, :, 512:768], N_HEADS),
             conv_all[:, -(CONV_A - 1):].astype(conv_buf.dtype), s_a_new.astype(s_a.dtype),
             s_c_new.astype(s_c.dtype))
    return oa, ob, oc, od, state


def kernel(x_prompt, x_sample, cache_b_k, cache_b_v, cache_d_k, cache_d_v, state_a_conv, state_a_s, state_c_s, norm_mix, w_in, conv_a, a_log, dt_bias, norm_a, sinks_b, lb_logits, norm_c, w_out, norm_ffn, w_ffn_gu, w_ffn_down, w_router, b_router, w_moe_gu, w_moe_down, norm_final):
    depth = w_in.shape[0]
    batch, seq, _ = x_prompt.shape
    bs, t_new, _ = x_sample.shape
    lb_p = jax.nn.softmax(lb_logits.astype(F32), axis=0)
    lower_bounds = jnp.cumsum(lb_p, axis=0) - lb_p[0]
    xp = x_prompt.reshape(batch * seq, D_MODEL)
    xs = x_sample.reshape(bs * t_new, D_MODEL)
    tm_p, tm_s = 512, 256
    row = lambda v: v.astype(F32).reshape(1, -1)
    prompt_states, sample_states = [], []
    for l in range(depth):
        w_in_l = _permute_w_in(w_in[l])
        w_out_l = w_out[l].astype(BF16)
        conv_w = conv_a[l].astype(F32)
        head_params = _gdn_head_params(a_log[l], dt_bias[l])
        lb = lower_bounds[l]
        mix = (row(norm_a[l]), sinks_b[l], row(lb), row(norm_c[l]))

        projs = _inproj(xp, row(norm_mix[l]), w_in_l, tm_p)
        oa, ob, oc, ods, lses, st_p = _prompt_mixers(*projs, batch, seq, conv_w, head_params, *mix)
        xp = _outproj(xp, oa, ob, oc, ods, lses, w_out_l, tm_p)
        prompt_states.append(st_p)

        projs = _inproj(xs, row(norm_mix[l]), w_in_l, tm_s)
        caches = (cache_b_k[l], cache_b_v[l], cache_d_k[l], cache_d_v[l], state_a_conv[l], state_a_s[l],
                  state_c_s[l])
        oa, ob, oc, od, st_s = _sample_mixers(*projs, bs, t_new, caches, conv_w, a_log[l], dt_bias[l],
                                              norm_a[l], sinks_b[l], lb, norm_c[l])
        xs = _outproj(xs, oa, ob, oc, [od], [], w_out_l, tm_s)
        sample_states.append(st_s)

        if l % 2 == 0:
            w_gu = w_ffn_gu[l // 2].astype(BF16)
            w_dn = w_ffn_down[l // 2].astype(BF16)
            xp = _ffn(xp, row(norm_ffn[l]), w_gu, w_dn, 1024)
            xs = _ffn(xs, row(norm_ffn[l]), w_gu, w_dn, 512)
        else:
            w_gu = w_moe_gu[l // 2].astype(BF16)
            w_dn = w_moe_down[l // 2].astype(BF16)
            xp = _moe(xp, row(norm_ffn[l]), w_router[l // 2], b_router[l // 2], w_gu, w_dn, 1024)
            xs = _moe(xs, row(norm_ffn[l]), w_router[l // 2], b_router[l // 2], w_gu, w_dn, 512)
    y_prompt = _final_norm(xp, row(norm_final), 1024).reshape(batch, seq, D_MODEL)
    y_sample = _final_norm(xs, row(norm_final), 512).reshape(bs, t_new, D_MODEL)
    stack = lambda states: [jnp.stack(t, 0) for t in zip(*states)]
    return (y_prompt, y_sample, *stack(prompt_states), *stack(sample_states))
```

```python
import functools
import math

import numpy as np
import jax
import jax.numpy as jnp
from jax import lax
from jax.experimental import pallas as pl
from jax.experimental.pallas import tpu as pltpu

F32 = jnp.float32
BF16 = jnp.bfloat16
HIGHEST = lax.Precision.HIGHEST

D_MODEL = 1024
HEAD_DIM = 64
N_HEADS = 4
KV_B = 2
W_MIX = N_HEADS * HEAD_DIM
CONV_A = 4
WIN = 128
DILATIONS = (1, 4, 16)
CACHE_D = 2048
CACHE_B = 128
D_FF = 2816
N_EXPERTS = 8
D_FF_E = 3584
EPS = 1e-6
SCALE = HEAD_DIM ** -0.5
NEG = -1e30

COLS_A = 1024
COLS_B = 512
COLS_C = 1024
COLS_D = 768
COLS_BA = 128
COLS_ALL = COLS_A + COLS_B + COLS_C + COLS_D + COLS_BA

CHUNK = 64
BATCH_BLOCK = 4
VMEM_LIMIT = 56 * 1024 * 1024


def _cparams(*sem):
    return pltpu.CompilerParams(dimension_semantics=sem, vmem_limit_bytes=VMEM_LIMIT)


def _mm(a, b):
    return jnp.dot(a.astype(BF16), b.astype(BF16), preferred_element_type=F32)


def _mm_nt(a, b):
    return lax.dot_general(a.astype(BF16), b.astype(BF16), (((1,), (1,)), ((), ())),
                           preferred_element_type=F32)


def _mm_tn(a, b):
    return lax.dot_general(a.astype(BF16), b.astype(BF16), (((0,), (0,)), ((), ())),
                           preferred_element_type=F32)


def _split_bf16(a):
    hi = a.astype(BF16)
    return hi, (a - hi.astype(F32)).astype(BF16)


def _mm_3pass(a, b):
    ah, al = _split_bf16(a)
    bh, bl = _split_bf16(b)
    dot = functools.partial(jnp.dot, preferred_element_type=F32)
    return dot(ah, bh) + (dot(al, bh) + dot(ah, bl))


def _mm_f32(a, b):
    return jnp.dot(a, b, precision=HIGHEST, preferred_element_type=F32)


def _sigmoid(x):
    return 1.0 / (1.0 + jnp.exp(-x))


def _silu(x):
    return x * _sigmoid(x)


def _softplus(x):
    return jnp.maximum(x, 0.0) + jnp.log(1.0 + jnp.exp(-jnp.abs(x)))


def _rms_rows(x, gain):
    return x * lax.rsqrt(jnp.mean(x * x, axis=-1, keepdims=True) + EPS) * gain


def _inproj_kernel(x_ref, g_ref, w_ref, oa_ref, ob_ref, oc_ref, od_ref, oba_ref):
    h = _rms_rows(x_ref[...], g_ref[...]).astype(BF16)
    c = 0
    for o_ref in (oa_ref, ob_ref, oc_ref, od_ref, oba_ref):
        n = o_ref.shape[1]
        o_ref[...] = jnp.dot(h, w_ref[:, c:c + n], preferred_element_type=F32)
        c += n


def _inproj(x2, gain, w_perm, tm):
    n = x2.shape[0]
    widths = (COLS_A, COLS_B, COLS_C, COLS_D, COLS_BA)
    return pl.pallas_call(
        _inproj_kernel,
        grid=(n // tm,),
        in_specs=[pl.BlockSpec((tm, D_MODEL), lambda i: (i, 0)),
                  pl.BlockSpec((1, D_MODEL), lambda i: (0, 0)),
                  pl.BlockSpec((D_MODEL, COLS_ALL), lambda i: (0, 0))],
        out_specs=[pl.BlockSpec((tm, w), lambda i: (i, 0)) for w in widths],
        out_shape=[jax.ShapeDtypeStruct((n, w), F32) for w in widths],
        compiler_params=_cparams("parallel"),
        name="inproj",
    )(x2, gain, w_perm)


def _permute_w_in(w):
    ba = jnp.pad(w[:, 1024:1032], ((0, 0), (0, COLS_BA - 8)))
    return jnp.concatenate([w[:, 0:1024], w[:, 1032:3336], ba], axis=1).astype(BF16)


def _band_kernel(q_ref, kp_ref, kc_ref, vp_ref, vc_ref, sink_ref, *out_refs, kv, with_sink, with_lse):
    o_ref = out_refs[0]
    n = pl.program_id(2)
    nb = q_ref.shape[0]
    row = lax.broadcasted_iota(jnp.int32, (WIN, 2 * WIN), 0)
    col = lax.broadcasted_iota(jnp.int32, (WIN, 2 * WIN), 1)
    dist = row + WIN - col
    valid = (dist >= 0) & (dist <= WIN) & ((col >= WIN) | (n > 0))
    group = N_HEADS // kv
    qs = [q_ref[b] * SCALE for b in range(nb)]
    kcat = [jnp.concatenate([kp_ref[b], kc_ref[b]], axis=0) for b in range(nb)]
    vcat = [jnp.concatenate([vp_ref[b], vc_ref[b]], axis=0) for b in range(nb)]
    units = [(b, h) for b in range(nb) for h in range(N_HEADS)]
    sl = lambda h: slice(h * HEAD_DIM, (h + 1) * HEAD_DIM)
    s = [jnp.where(valid, _mm_nt(qs[b][:, sl(h)], kcat[b][:, sl(h // group)]), NEG) for b, h in units]
    m = [jnp.max(t, axis=-1, keepdims=True) for t in s]
    if with_sink:
        sink = [sink_ref[:, h:h + 1] for _, h in units]
        m = [jnp.maximum(a, b) for a, b in zip(m, sink)]
    p = [jnp.exp(t - a) for t, a in zip(s, m)]
    l = [jnp.sum(t, axis=-1, keepdims=True) for t in p]
    if with_sink:
        l = [a + jnp.exp(b - c) for a, b, c in zip(l, sink, m)]
    o = [_mm(t, vcat[b][:, sl(h // group)]) / a for t, a, (b, h) in zip(p, l, units)]
    for b in range(nb):
        o_ref[b] = jnp.concatenate(o[b * N_HEADS:(b + 1) * N_HEADS], axis=-1)
        if with_lse:
            lse = [jnp.broadcast_to(m[i] + jnp.log(l[i]), (WIN, HEAD_DIM)) for i in range(b * N_HEADS, (b + 1) * N_HEADS)]
            out_refs[1][b] = jnp.concatenate(lse, axis=-1)


def _band_attention(p3, batch, seq, r, q_col, k_col, v_col, kv, sinks, with_lse):
    width = p3.shape[-1]
    ln = seq // r
    bb = BATCH_BLOCK if batch % BATCH_BLOCK == 0 else 1
    pv = p3.reshape(batch, ln, r * width)
    wq, wk = W_MIX, kv * HEAD_DIM
    qb, kb, vb = q_col // wq, k_col // wk, v_col // wk
    nq, nk = width // wq, width // wk
    cur = lambda off, per: (lambda b, rho, n: (b, n, rho * per + off))
    prev = lambda off, per: (lambda b, rho, n: (b, jnp.maximum(n - 1, 0), rho * per + off))
    out_spec = pl.BlockSpec((bb, WIN, W_MIX), lambda b, rho, n: (b, n, rho))
    out_shape = jax.ShapeDtypeStruct((batch, ln, r * W_MIX), F32)
    n_out = 2 if with_lse else 1
    res = pl.pallas_call(
        functools.partial(_band_kernel, kv=kv, with_sink=sinks is not None, with_lse=with_lse),
        grid=(batch // bb, r, ln // WIN),
        in_specs=[pl.BlockSpec((bb, WIN, wq), cur(qb, nq)),
                  pl.BlockSpec((bb, WIN, wk), prev(kb, nk)),
                  pl.BlockSpec((bb, WIN, wk), cur(kb, nk)),
                  pl.BlockSpec((bb, WIN, wk), prev(vb, nk)),
                  pl.BlockSpec((bb, WIN, wk), cur(vb, nk)),
                  pl.BlockSpec((1, 128), lambda b, rho, n: (0, 0))],
        out_specs=[out_spec] * n_out,
        out_shape=[out_shape] * n_out,
        compiler_params=_cparams("parallel", "parallel", "arbitrary"),
        name=f"band_r{r}",
    )(pv, pv, pv, pv, pv, sinks if sinks is not None else jnp.zeros((1, 128), F32))
    return [t.reshape(batch * seq, W_MIX) for t in res]


def _gated_norm_rows(o, gate, gain):
    return _rms_rows(o, gain) * _silu(gate)


def _hgrn_constants(c):
    halves = []
    h = c // 2
    while h >= 1:
        halves.append(h)
        h //= 2
    t = np.arange(c)[:, None]
    u = np.arange(c)[None, :]
    mats = [(u <= t).astype(np.float32)]
    level = np.full((c, c), -1, np.int32)
    level[np.arange(c), np.arange(c)] = 0
    for li, h in enumerate(halves, 1):
        mid = (t // (2 * h)) * (2 * h) + h
        second = (t % (2 * h)) >= h
        mats.append(np.where(second, (u > mid) & (u <= t), (u > t) & (u <= mid)).astype(np.float32))
        pair = (t // (2 * h) == u // (2 * h)) & ((t % (2 * h)) >= h) & ((u % (2 * h)) < h)
        level[pair] = li
    return np.concatenate(mats, axis=0), level, len(halves)


def _hgrn_kernel(pc_ref, lb_ref, gain_ref, mat_ref, lvl_ref, o_ref, st_ref, s_scr, *, c, n_levels):
    j = pl.program_id(1)
    nb = pc_ref.shape[0]

    @pl.when(j == 0)
    def _():
        s_scr[...] = jnp.zeros_like(s_scr)

    lb = lb_ref[...]
    lvl = lvl_ref[...]
    xs = [pc_ref[b] for b in range(nb)]
    fs = [lb + (1.0 - lb) * _sigmoid(x[:, 256:512]) for x in xs]
    sums = [_mm_f32(mat_ref[...], -jnp.log(f)) for f in fs]
    gcs = [-s[0:c] for s in sums]
    g_last = [gc[c - 1:c, :] for gc in gcs]
    q_dec = [x[:, 0:256] * jnp.exp(gc) for x, gc in zip(xs, gcs)]
    k_dec = [(1.0 - f) * jnp.exp(gl - gc) for f, gl, gc in zip(fs, g_last, gcs)]
    units = [(b, h) for b in range(nb) for h in range(N_HEADS)]
    sl = lambda h: slice(h * HEAD_DIM, (h + 1) * HEAD_DIM)
    qh = [xs[b][:, sl(h)] for b, h in units]
    kh = [1.0 - fs[b][:, sl(h)] for b, h in units]
    vh = [xs[b][:, 512 + h * HEAD_DIM:512 + (h + 1) * HEAD_DIM] for b, h in units]
    a = [jnp.where(lvl == 0, _mm_nt(q, k), 0.0) for q, k in zip(qh, kh)]
    for li in range(1, n_levels + 1):
        damp = [jnp.exp(-sums[b][li * c:(li + 1) * c, sl(h)]) for b, h in units]
        part = [_mm_nt(q * d, k * d) for q, k, d in zip(qh, kh, damp)]
        a = [acc + jnp.where(lvl == li, p, 0.0) for acc, p in zip(a, part)]
    st = [s_scr[b, h] for b, h in units]
    o_inter = [_mm_nt(q_dec[b][:, sl(h)], t) for (b, h), t in zip(units, st)]
    o = [oi + _mm(aa, v) for oi, aa, v in zip(o_inter, a, vh)]
    upd = [_mm_tn(v, k_dec[b][:, sl(h)]) for (b, h), v in zip(units, vh)]
    for i, (b, h) in enumerate(units):
        s_scr[b, h] = jnp.exp(g_last[b][:, sl(h)]) * st[i] + upd[i]
    for b in range(nb):
        outs = [_gated_norm_rows(o[b * N_HEADS + h], xs[b][:, 768 + h * HEAD_DIM:768 + (h + 1) * HEAD_DIM],
                                 gain_ref[...]) for h in range(N_HEADS)]
        o_ref[b] = jnp.concatenate(outs, axis=-1)

    @pl.when(j == pl.num_programs(1) - 1)
    def _():
        st_ref[...] = s_scr[...]


def _hgrn_prompt(pc3, lb, gain):
    batch, seq, _ = pc3.shape
    c = CHUNK
    bb = BATCH_BLOCK if batch % BATCH_BLOCK == 0 else 1
    mat, level, n_levels = _hgrn_constants(c)
    o, st = pl.pallas_call(
        functools.partial(_hgrn_kernel, c=c, n_levels=n_levels),
        grid=(batch // bb, seq // c),
        in_specs=[pl.BlockSpec((bb, c, COLS_C), lambda b, j: (b, j, 0)),
                  pl.BlockSpec((1, W_MIX), lambda b, j: (0, 0)),
                  pl.BlockSpec((1, HEAD_DIM), lambda b, j: (0, 0)),
                  pl.BlockSpec(mat.shape, lambda b, j: (0, 0)),
                  pl.BlockSpec(level.shape, lambda b, j: (0, 0))],
        out_specs=[pl.BlockSpec((bb, c, W_MIX), lambda b, j: (b, j, 0)),
                   pl.BlockSpec((bb, N_HEADS, HEAD_DIM, HEAD_DIM), lambda b, j: (b, 0, 0, 0))],
        out_shape=[jax.ShapeDtypeStruct((batch, seq, W_MIX), F32),
                   jax.ShapeDtypeStruct((batch, N_HEADS, HEAD_DIM, HEAD_DIM), F32)],
        scratch_shapes=[pltpu.VMEM((bb, N_HEADS, HEAD_DIM, HEAD_DIM), F32)],
        compiler_params=_cparams("parallel", "arbitrary"),
        name="hgrn_prompt",
    )(pc3, lb, gain, jnp.asarray(mat), jnp.asarray(level))
    return o.reshape(batch * seq, W_MIX), jnp.swapaxes(st, -1, -2)


def _unit_lower_solve(lows, rhss, c):
    xs = [rhs - _mm_3pass(low, rhs) for low, rhs in zip(lows, rhss)]
    ps = lows
    span = 2
    while span < c:
        mm = _mm_3pass if span == 2 else _mm
        ps = [mm(p, p) for p in ps]
        xs = [x + mm(p, x) for p, x in zip(ps, xs)]
        span *= 2
    return xs


def _gdn_kernel(pa_ref, pba_ref, cw_ref, hp_ref, gain_ref, tri_ref, ones_ref, o_ref, st_ref,
                s_scr, buf_scr, *, c):
    j = pl.program_id(1)
    pad = 8
    nb = pa_ref.shape[0]

    @pl.when(j == 0)
    def _():
        s_scr[...] = jnp.zeros_like(s_scr)
        buf_scr[:, 0:pad, :] = jnp.zeros((nb, pad, 3 * W_MIX), F32)

    ones = ones_ref[...]
    row = lax.broadcasted_iota(jnp.int32, (c, c), 0)
    col = lax.broadcasted_iota(jnp.int32, (c, c), 1)
    qs, ks, vs, gates, betas, gcs, gcts = [], [], [], [], [], [], []
    for b in range(nb):
        x = pa_ref[b, :, 0:3 * W_MIX]
        gates.append(pa_ref[b, :, 3 * W_MIX:4 * W_MIX])
        buf_scr[b, pad:pad + c, :] = x
        y = cw_ref[CONV_A - 1:CONV_A, :] * x
        for tap in range(CONV_A - 1):
            back = CONV_A - 1 - tap
            y = y + cw_ref[tap:tap + 1, :] * buf_scr[b, pad - back:pad - back + c, :]
        buf_scr[b, 0:pad, :] = buf_scr[b, c:c + pad, :]
        y = _silu(y)
        qs.append(y[:, 0:W_MIX])
        ks.append(y[:, W_MIX:2 * W_MIX])
        vs.append(y[:, 2 * W_MIX:3 * W_MIX])
        ba = pba_ref[b]
        betas.append(_sigmoid(ba))
        g = hp_ref[0:1, :] * _softplus(ba + hp_ref[1:2, :])
        gcs.append(_mm_f32(tri_ref[0], g))
        gcts.append(lax.dot_general(g, tri_ref[1], (((0,), (0,)), ((), ())), precision=HIGHEST,
                                    preferred_element_type=F32))
    qs = [q * lax.rsqrt(_mm_f32(q * q, ones) + EPS) * SCALE for q in qs]
    ks = [k * lax.rsqrt(_mm_f32(k * k, ones) + EPS) for k in ks]
    units = [(b, h) for b in range(nb) for h in range(N_HEADS)]
    sl = lambda h: slice(h * HEAD_DIM, (h + 1) * HEAD_DIM)
    qh = [qs[b][:, sl(h)] for b, h in units]
    kh = [ks[b][:, sl(h)] for b, h in units]
    vh = [vs[b][:, sl(h)] for b, h in units]
    b_col = [betas[b][:, h:h + 1] for b, h in units]
    g_col = [gcs[b][:, 4 + h:5 + h] for b, h in units]
    g_row = [gcts[b][4 + h:5 + h, :] for b, h in units]
    decay = [jnp.where(row >= col, jnp.exp(jnp.minimum(gc_ - gr_, 0.0)), 0.0) for gc_, gr_ in zip(g_col, g_row)]
    kk = [_mm_nt(k, k) for k in kh]
    qk = [_mm_nt(q, k) for q, k in zip(qh, kh)]
    low = [jnp.where(row > col, bc * a * d, 0.0) for bc, a, d in zip(b_col, kk, decay)]
    eg = [jnp.exp(gc_) for gc_ in g_col]
    rhs = [jnp.concatenate([v * bc, k * (bc * e)], axis=-1) for v, k, bc, e in zip(vh, kh, b_col, eg)]
    sol = _unit_lower_solve(low, rhs, c)
    intra = [a * d for a, d in zip(qk, decay)]
    g_last = [gc_[c - 1:c, :] for gc_ in g_col]
    st = [s_scr[b, h] for b, h in units]
    u = [s[:, 0:HEAD_DIM] - _mm_nt(s[:, HEAD_DIM:2 * HEAD_DIM], t) for s, t in zip(sol, st)]
    o_inter = [_mm_nt(q * e, t) for q, e, t in zip(qh, eg, st)]
    o = [oi + _mm(a, uu) for oi, a, uu in zip(o_inter, intra, u)]
    k_dec = [k * jnp.exp(gl - gc_) for k, gl, gc_ in zip(kh, g_last, g_col)]
    upd = [_mm_tn(uu, kd) for uu, kd in zip(u, k_dec)]
    for i, (b, h) in enumerate(units):
        s_scr[b, h] = jnp.exp(g_last[i]) * st[i] + upd[i]
    for b in range(nb):
        outs = [_gated_norm_rows(o[b * N_HEADS + h], gates[b][:, sl(h)], gain_ref[...]) for h in range(N_HEADS)]
        o_ref[b] = jnp.concatenate(outs, axis=-1)

    @pl.when(j == pl.num_programs(1) - 1)
    def _():
        st_ref[...] = s_scr[...]


def _gdn_prompt(pa3, pba3, conv_w, head_params, gain):
    batch, seq, _ = pa3.shape
    c = CHUNK
    bb = BATCH_BLOCK if batch % BATCH_BLOCK == 0 else 1
    lower = np.tril(np.ones((c, c), np.float32))
    tri = np.stack([lower, lower.T], axis=0)
    ones = np.kron(np.eye(N_HEADS, dtype=np.float32), np.ones((HEAD_DIM, HEAD_DIM), np.float32))
    o, st = pl.pallas_call(
        functools.partial(_gdn_kernel, c=c),
        grid=(batch // bb, seq // c),
        in_specs=[pl.BlockSpec((bb, c, COLS_A), lambda b, j: (b, j, 0)),
                  pl.BlockSpec((bb, c, COLS_BA), lambda b, j: (b, j, 0)),
                  pl.BlockSpec((CONV_A, 3 * W_MIX), lambda b, j: (0, 0)),
                  pl.BlockSpec((2, COLS_BA), lambda b, j: (0, 0)),
                  pl.BlockSpec((1, HEAD_DIM), lambda b, j: (0, 0)),
                  pl.BlockSpec((2, c, c), lambda b, j: (0, 0, 0)),
                  pl.BlockSpec((W_MIX, W_MIX), lambda b, j: (0, 0))],
        out_specs=[pl.BlockSpec((bb, c, W_MIX), lambda b, j: (b, j, 0)),
                   pl.BlockSpec((bb, N_HEADS, HEAD_DIM, HEAD_DIM), lambda b, j: (b, 0, 0, 0))],
        out_shape=[jax.ShapeDtypeStruct((batch, seq, W_MIX), F32),
                   jax.ShapeDtypeStruct((batch, N_HEADS, HEAD_DIM, HEAD_DIM), F32)],
        scratch_shapes=[pltpu.VMEM((bb, N_HEADS, HEAD_DIM, HEAD_DIM), F32),
                        pltpu.VMEM((bb, c + 8, 3 * W_MIX), F32)],
        compiler_params=_cparams("parallel", "arbitrary"),
        name="gdn_prompt",
    )(pa3, pba3, conv_w, head_params, gain, jnp.asarray(tri), jnp.asarray(ones))
    return o.reshape(batch * seq, W_MIX), jnp.swapaxes(st, -1, -2)


def _gdn_head_params(a_log, dt_bias):
    neg_a = jnp.pad(-jnp.exp(a_log.astype(F32)), (4, COLS_BA - 8))
    dtb = jnp.pad(dt_bias.astype(F32), (4, COLS_BA - 8))
    return jnp.stack([neg_a, dtb], axis=0)


def _outproj_kernel(*refs, n_branch):
    x_ref, oa_ref, ob_ref, oc_ref = refs[0:4]
    d_refs = refs[4:4 + n_branch]
    l_refs = refs[4 + n_branch:4 + 2 * n_branch - (0 if n_branch > 1 else 1)]
    w_ref, o_ref = refs[-2], refs[-1]
    if n_branch == 1:
        od = d_refs[0][...]
    else:
        lses = [r[...] for r in l_refs]
        m = functools.reduce(jnp.maximum, lses)
        es = [jnp.exp(l - m) for l in lses]
        num = sum(e * r[...] for e, r in zip(es, d_refs))
        od = num / sum(es)
    acc = x_ref[...]
    for i, part in enumerate((oa_ref[...], ob_ref[...], oc_ref[...], od)):
        acc = acc + _mm(part, w_ref[i * W_MIX:(i + 1) * W_MIX, :])
    o_ref[...] = acc


def _outproj(x2, oa, ob, oc, ods, lses, w_bf, tm):
    n = x2.shape[0]
    n_branch = len(ods)
    row = lambda w: pl.BlockSpec((tm, w), lambda i: (i, 0))
    args = [x2, oa, ob, oc, *ods, *lses, w_bf]
    specs = [row(D_MODEL)] + [row(W_MIX)] * (3 + n_branch + len(lses))
    specs.append(pl.BlockSpec((4 * W_MIX, D_MODEL), lambda i: (0, 0)))
    return pl.pallas_call(
        functools.partial(_outproj_kernel, n_branch=n_branch),
        grid=(n // tm,),
        in_specs=specs,
        out_specs=row(D_MODEL),
        out_shape=jax.ShapeDtypeStruct((n, D_MODEL), F32),
        compiler_params=_cparams("parallel"),
        name="outproj",
    )(*args)


def _ffn_kernel(x_ref, g_ref, wg_ref, wu_ref, wd_ref, o_ref, h_scr, acc_scr):
    f = pl.program_id(1)

    @pl.when(f == 0)
    def _():
        h_scr[...] = _rms_rows(x_ref[...], g_ref[...]).astype(BF16)
        acc_scr[...] = jnp.zeros_like(acc_scr)

    h = h_scr[...]
    gate = jnp.dot(h, wg_ref[...], preferred_element_type=F32)
    up = jnp.dot(h, wu_ref[...], preferred_element_type=F32)
    acc_scr[...] += _mm(_silu(gate) * up, wd_ref[...])

    @pl.when(f == pl.num_programs(1) - 1)
    def _():
        o_ref[...] = x_ref[...] + acc_scr[...]


def _ffn(x2, gain, w_gu_bf, w_down_bf, tm, tf=256):
    n = x2.shape[0]
    nf = D_FF // tf
    return pl.pallas_call(
        _ffn_kernel,
        grid=(n // tm, nf),
        in_specs=[pl.BlockSpec((tm, D_MODEL), lambda i, f: (i, 0)),
                  pl.BlockSpec((1, D_MODEL), lambda i, f: (0, 0)),
                  pl.BlockSpec((D_MODEL, tf), lambda i, f: (0, f)),
                  pl.BlockSpec((D_MODEL, tf), lambda i, f: (0, nf + f)),
                  pl.BlockSpec((tf, D_MODEL), lambda i, f: (f, 0))],
        out_specs=pl.BlockSpec((tm, D_MODEL), lambda i, f: (i, 0)),
        out_shape=jax.ShapeDtypeStruct((n, D_MODEL), F32),
        scratch_shapes=[pltpu.VMEM((tm, D_MODEL), BF16), pltpu.VMEM((tm, D_MODEL), F32)],
        compiler_params=_cparams("parallel", "arbitrary"),
        name="ffn",
    )(x2, gain, w_gu_bf, w_gu_bf, w_down_bf)


def _moe_kernel(x_ref, g_ref, wr_ref, br_ref, wg_ref, wu_ref, wd_ref, o_ref, h_scr, gate_scr, acc_scr):
    e = pl.program_id(1)
    f = pl.program_id(2)
    tm = x_ref.shape[0]
    lane = lax.broadcasted_iota(jnp.int32, (tm, 128), 1).astype(F32)

    @pl.when((e == 0) & (f == 0))
    def _():
        h = _rms_rows(x_ref[...], g_ref[...])
        h_scr[...] = h.astype(BF16)
        logits = _mm_f32(h, wr_ref[...]) + br_ref[...]
        m1 = jnp.max(logits, axis=-1, keepdims=True)
        i1 = jnp.min(jnp.where(logits == m1, lane, 128.0), axis=-1, keepdims=True)
        rest = jnp.where(lane == i1, NEG, logits)
        m2 = jnp.max(rest, axis=-1, keepdims=True)
        i2 = jnp.min(jnp.where(rest == m2, lane, 128.0), axis=-1, keepdims=True)
        e2 = jnp.exp(m2 - m1)
        gate_scr[...] = jnp.where(lane == i1, 1.0 / (1.0 + e2), 0.0) + jnp.where(lane == i2, e2 / (1.0 + e2), 0.0)
        acc_scr[...] = jnp.zeros_like(acc_scr)

    ge = jnp.sum(jnp.where(lane == e.astype(F32), gate_scr[...], 0.0), axis=-1, keepdims=True)
    h = h_scr[...]
    gate = jnp.dot(h, wg_ref[0], preferred_element_type=F32)
    up = jnp.dot(h, wu_ref[0], preferred_element_type=F32)
    acc_scr[...] += _mm(_silu(gate) * up * ge, wd_ref[0])

    @pl.when((e == N_EXPERTS - 1) & (f == pl.num_programs(2) - 1))
    def _():
        o_ref[...] = x_ref[...] + acc_scr[...]


def _moe(x2, gain, w_router, b_router, w_gu_bf, w_down_bf, tm, tf=256):
    n = x2.shape[0]
    nf = D_FF_E // tf
    wr = jnp.pad(w_router.astype(F32), ((0, 0), (0, 128 - N_EXPERTS)))
    br = jnp.pad(b_router.astype(F32), (0, 128 - N_EXPERTS), constant_values=NEG).reshape(1, 128)
    return pl.pallas_call(
        _moe_kernel,
        grid=(n // tm, N_EXPERTS, nf),
        in_specs=[pl.BlockSpec((tm, D_MODEL), lambda i, e, f: (i, 0)),
                  pl.BlockSpec((1, D_MODEL), lambda i, e, f: (0, 0)),
                  pl.BlockSpec((D_MODEL, 128), lambda i, e, f: (0, 0)),
                  pl.BlockSpec((1, 128), lambda i, e, f: (0, 0)),
                  pl.BlockSpec((1, D_MODEL, tf), lambda i, e, f: (e, 0, f)),
                  pl.BlockSpec((1, D_MODEL, tf), lambda i, e, f: (e, 0, nf + f)),
                  pl.BlockSpec((1, tf, D_MODEL), lambda i, e, f: (e, f, 0))],
        out_specs=pl.BlockSpec((tm, D_MODEL), lambda i, e, f: (i, 0)),
        out_shape=jax.ShapeDtypeStruct((n, D_MODEL), F32),
        scratch_shapes=[pltpu.VMEM((tm, D_MODEL), BF16), pltpu.VMEM((tm, 128), F32),
                        pltpu.VMEM((tm, D_MODEL), F32)],
        compiler_params=_cparams("parallel", "arbitrary", "arbitrary"),
        name="moe",
    )(x2, gain, wr, br, w_gu_bf, w_gu_bf, w_down_bf)


def _norm_kernel(x_ref, g_ref, o_ref):
    o_ref[...] = _rms_rows(x_ref[...], g_ref[...])


def _final_norm(x2, gain, tm):
    n = x2.shape[0]
    return pl.pallas_call(
        _norm_kernel,
        grid=(n // tm,),
        in_specs=[pl.BlockSpec((tm, D_MODEL), lambda i: (i, 0)),
                  pl.BlockSpec((1, D_MODEL), lambda i: (0, 0))],
        out_specs=pl.BlockSpec((tm, D_MODEL), lambda i: (i, 0)),
        out_shape=jax.ShapeDtypeStruct((n, D_MODEL), F32),
        compiler_params=_cparams("parallel"),
        name="final_norm",
    )(x2, gain)


ROWS_T = 8
QROWS = N_HEADS * ROWS_T


def _softmax_two_parts(sc, sn, vc, vn, valid_c, valid_n, sink):
    sc = jnp.where(valid_c, sc, NEG)
    sn = jnp.where(valid_n, sn, NEG)
    m = jnp.maximum(jnp.max(sc, axis=-1, keepdims=True), jnp.max(sn, axis=-1, keepdims=True))
    if sink is not None:
        m = jnp.maximum(m, sink)
    pc = jnp.exp(sc - m)
    pn = jnp.exp(sn - m)
    l = jnp.sum(pc, axis=-1, keepdims=True) + jnp.sum(pn, axis=-1, keepdims=True)
    if sink is not None:
        l = l + jnp.exp(sink - m)
    return (_mm(pc, vc) + _mm(pn, vn)) / l, m + jnp.log(l)


def _sattn_kernel(qd_ref, kdn_ref, vdn_ref, kdc_ref, vdc_ref, qb_ref, kbn_ref, vbn_ref, kbc_ref, vbc_ref,
                  sink_ref, od_ref, ob_ref, *, bb, t_new):
    def dist_maps(n_cache):
        row = lax.broadcasted_iota(jnp.int32, (QROWS, n_cache), 0)
        col = lax.broadcasted_iota(jnp.int32, (QROWS, n_cache), 1)
        d_cache = n_cache + (row & (ROWS_T - 1)) - col
        rown = lax.broadcasted_iota(jnp.int32, (QROWS, 128), 0)
        coln = lax.broadcasted_iota(jnp.int32, (QROWS, 128), 1)
        d_new = (rown & (ROWS_T - 1)) - coln
        return d_cache, d_new, (coln < t_new) & (d_new >= 0)

    dc_d, dn_d, ok_new_d = dist_maps(CACHE_D)
    dc_b, dn_b, ok_new_b = dist_maps(CACHE_B)
    hrow = lax.broadcasted_iota(jnp.int32, (QROWS, W_MIX), 0) >> int(math.log2(ROWS_T))
    hlane = lax.broadcasted_iota(jnp.int32, (QROWS, W_MIX), 1) >> int(math.log2(HEAD_DIM))
    own = hrow == hlane
    zpad_d = jnp.zeros((128 - ROWS_T, W_MIX), F32)
    zpad_b = jnp.zeros((128 - ROWS_T, KV_B * HEAD_DIM), F32)
    zq = jnp.zeros((ROWS_T, HEAD_DIM), F32)
    for i in range(bb):
        q = qd_ref[i] * SCALE
        qblk = jnp.where(own, jnp.concatenate([q] * N_HEADS, axis=0), 0.0)
        kc, vc = kdc_ref[i], vdc_ref[i]
        kn = jnp.concatenate([kdn_ref[i], zpad_d], axis=0)
        vn = jnp.concatenate([vdn_ref[i], zpad_d], axis=0)
        sc = _mm_nt(qblk, kc)
        sn = _mm_nt(qblk, kn)
        outs, lses = [], []
        for r in DILATIONS:
            ok_c = (dc_d <= WIN * r) & ((dc_d & (r - 1)) == 0)
            ok_n = ok_new_d & ((dn_d & (r - 1)) == 0)
            o, lse = _softmax_two_parts(sc, sn, vc, vn, ok_c, ok_n, None)
            outs.append(o)
            lses.append(lse)
        m = functools.reduce(jnp.maximum, lses)
        es = [jnp.exp(l - m) for l in lses]
        o = sum(e * t for e, t in zip(es, outs)) / sum(es)
        o = jnp.where(own, o, 0.0)
        od_ref[i] = sum(o[h * ROWS_T:(h + 1) * ROWS_T] for h in range(N_HEADS))
        qb = qb_ref[i] * SCALE
        blocks = []
        for h in range(N_HEADS):
            qh = qb[:, h * HEAD_DIM:(h + 1) * HEAD_DIM]
            blocks.append(jnp.concatenate([qh, zq] if h < N_HEADS // KV_B else [zq, qh], axis=-1))
        qblk = jnp.concatenate(blocks, axis=0)
        kn = jnp.concatenate([kbn_ref[i], zpad_b], axis=0)
        vn = jnp.concatenate([vbn_ref[i], zpad_b], axis=0)
        o, _ = _softmax_two_parts(_mm_nt(qblk, kbc_ref[i]), _mm_nt(qblk, kn), vbc_ref[i], vn,
                                  dc_b <= WIN, ok_new_b, sink_ref[...])
        pieces = []
        for h in range(N_HEADS):
            g = h // (N_HEADS // KV_B)
            pieces.append(o[h * ROWS_T:(h + 1) * ROWS_T, g * HEAD_DIM:(g + 1) * HEAD_DIM])
        ob_ref[i] = jnp.concatenate(pieces, axis=-1)


def _pad_rows(t, rows):
    return jnp.pad(t, ((0, 0), (0, rows - t.shape[1]), (0, 0)))


def _sample_attention(pb_s, pd_s, cbk, cbv, cdk, cdv, sinks, bb=2):
    bs, t_new, _ = pb_s.shape
    pb8, pd8 = _pad_rows(pb_s, ROWS_T), _pad_rows(pd_s, ROWS_T)
    sink_col = jnp.repeat(sinks.astype(F32), ROWS_T).reshape(QROWS, 1)
    wb = KV_B * HEAD_DIM
    blk = lambda rows, w, c: pl.BlockSpec((bb, rows, w), lambda i: (i, 0, c))
    od, ob = pl.pallas_call(
        functools.partial(_sattn_kernel, bb=bb, t_new=t_new),
        grid=(bs // bb,),
        in_specs=[blk(ROWS_T, W_MIX, 0), blk(ROWS_T, W_MIX, 1), blk(ROWS_T, W_MIX, 2),
                  blk(CACHE_D, W_MIX, 0), blk(CACHE_D, W_MIX, 0),
                  blk(ROWS_T, W_MIX, 0), blk(ROWS_T, wb, 2), blk(ROWS_T, wb, 3),
                  blk(CACHE_B, wb, 0), blk(CACHE_B, wb, 0),
                  pl.BlockSpec((QROWS, 1), lambda i: (0, 0))],
        out_specs=[blk(ROWS_T, W_MIX, 0), blk(ROWS_T, W_MIX, 0)],
        out_shape=[jax.ShapeDtypeStruct((bs, ROWS_T, W_MIX), F32)] * 2,
        compiler_params=_cparams("parallel"),
        name="sample_attn",
    )(pd8, pd8, pd8, cdk, cdv, pb8, pb8, pb8, cbk, cbv, sink_col)
    return (ob[:, :t_new].reshape(bs * t_new, W_MIX), od[:, :t_new].reshape(bs * t_new, W_MIX))


def _srec_kernel(xq_ref, xk_ref, xv_ref, bq_ref, bk_ref, bv_ref, cwq_ref, cwk_ref, cwv_ref, ga_ref,
                 ba_ref, hp_ref, gna_ref, sa_ref, qc_ref, fc_ref, ic_ref, gcg_ref, lb_ref, gnc_ref, sc_ref,
                 oa_ref, sa_out, oc_ref, sc_out, q_scr, k_scr, d_scr, *, t_new):
    nb = sa_ref.shape[-1]
    zero = jnp.zeros((HEAD_DIM, nb), F32)

    def conv(x_ref, b_ref, cw_ref, t):
        y = None
        for tap in range(CONV_A):
            pos = t + tap
            src = b_ref[pos] if pos < CONV_A - 1 else x_ref[pos - (CONV_A - 1)]
            term = cw_ref[tap] * src
            y = term if y is None else y + term
        return _silu(y)

    def l2(x):
        return x * lax.rsqrt(jnp.sum(x * x, axis=0, keepdims=True) + EPS)

    def gated_norm(o, gate, gain):
        return o * lax.rsqrt(jnp.mean(o * o, axis=0, keepdims=True) + EPS) * gain * _silu(gate)

    sa_out[0] = sa_ref[0]
    for t in range(t_new):
        q_scr[...] = l2(conv(xq_ref, bq_ref, cwq_ref, t)) * SCALE
        k_scr[...] = l2(conv(xk_ref, bk_ref, cwk_ref, t))
        v = conv(xv_ref, bv_ref, cwv_ref, t)
        beta = _sigmoid(ba_ref[0, t:t + 1, :])
        dec = jnp.exp(hp_ref[0, 0:1, :] * _softplus(ba_ref[0, t_new + t:t_new + t + 1, :] + hp_ref[0, 1:2, :]))

        def decay_and_read(kk, acc):
            s = sa_out[0, kk] * dec
            sa_out[0, kk] = s
            return acc + k_scr[pl.ds(kk, 1), :] * s

        err = (v - lax.fori_loop(0, HEAD_DIM, decay_and_read, zero)) * beta

        def write_and_query(kk, acc):
            s = sa_out[0, kk] + k_scr[pl.ds(kk, 1), :] * err
            sa_out[0, kk] = s
            return acc + q_scr[pl.ds(kk, 1), :] * s

        o = lax.fori_loop(0, HEAD_DIM, write_and_query, zero)
        oa_ref[t] = gated_norm(o, ga_ref[t], gna_ref[...])

    sc_out[0] = sc_ref[0]
    for t in range(t_new):
        lb = lb_ref[...]
        f = lb + (1.0 - lb) * _sigmoid(fc_ref[t])
        q_scr[...] = qc_ref[t]
        k_scr[...] = 1.0 - f
        d_scr[...] = jnp.exp(jnp.log(f))
        v = ic_ref[t]

        def update(kk, acc):
            s = sc_out[0, kk] * d_scr[pl.ds(kk, 1), :] + k_scr[pl.ds(kk, 1), :] * v
            sc_out[0, kk] = s
            return acc + q_scr[pl.ds(kk, 1), :] * s

        o = lax.fori_loop(0, HEAD_DIM, update, zero)
        oc_ref[t] = gated_norm(o, gcg_ref[t], gnc_ref[...])


def _sample_recurrences(pa_s, pba_s, pc_s, conv_buf, s_a, s_c, conv_w, a_log, dt_bias, norm_a, lb, norm_c):
    bs, t_new, _ = pa_s.shape
    lanes_last = lambda t: jnp.transpose(t, (1, 2, 0))
    pa_t = lanes_last(pa_s)
    pc_t = lanes_last(pc_s)
    buf_t = lanes_last(conv_buf.astype(F32))
    ba = jnp.transpose(pba_s[:, :, 0:8], (2, 1, 0))
    ba = jnp.concatenate([ba[0:N_HEADS], ba[N_HEADS:2 * N_HEADS]], axis=1)
    hp = jnp.stack([-jnp.exp(a_log.astype(F32)), dt_bias.astype(F32)], axis=1)
    hp = jnp.broadcast_to(hp[:, :, None], (N_HEADS, 2, bs))
    cw = conv_w.astype(F32)[:, :, None]
    sa_t = jnp.transpose(s_a.astype(F32), (1, 2, 3, 0))
    sc_t = jnp.transpose(s_c.astype(F32), (1, 2, 3, 0))
    col = lambda v: v.astype(F32).reshape(-1, 1)
    hd = HEAD_DIM
    feat = lambda rows, off: pl.BlockSpec((rows, hd, bs), lambda h: (0, off + h, 0))
    cwspec = lambda off: pl.BlockSpec((CONV_A, hd, 1), lambda h: (0, off + h, 0))
    per_head = lambda rows: pl.BlockSpec((1, rows, bs), lambda h: (h, 0, 0))
    state = pl.BlockSpec((1, hd, hd, bs), lambda h: (h, 0, 0, 0))
    vec = pl.BlockSpec((hd, 1), lambda h: (0, 0))
    nh = N_HEADS
    oa, sa_n, oc, sc_n = pl.pallas_call(
        functools.partial(_srec_kernel, t_new=t_new),
        grid=(N_HEADS,),
        in_specs=[feat(t_new, 0), feat(t_new, nh), feat(t_new, 2 * nh),
                  feat(CONV_A - 1, 0), feat(CONV_A - 1, nh), feat(CONV_A - 1, 2 * nh),
                  cwspec(0), cwspec(nh), cwspec(2 * nh),
                  feat(t_new, 3 * nh), per_head(2 * t_new), per_head(2), vec, state,
                  feat(t_new, 0), feat(t_new, nh), feat(t_new, 2 * nh), feat(t_new, 3 * nh),
                  pl.BlockSpec((hd, 1), lambda h: (h, 0)), vec, state],
        out_specs=[feat(t_new, 0), state, feat(t_new, 0), state],
        out_shape=[jax.ShapeDtypeStruct((t_new, W_MIX, bs), F32),
                   jax.ShapeDtypeStruct((N_HEADS, hd, hd, bs), F32),
                   jax.ShapeDtypeStruct((t_new, W_MIX, bs), F32),
                   jax.ShapeDtypeStruct((N_HEADS, hd, hd, bs), F32)],
        scratch_shapes=[pltpu.VMEM((hd, bs), F32)] * 3,
        compiler_params=_cparams("parallel"),
        name="sample_recurrences",
    )(pa_t, pa_t, pa_t, buf_t, buf_t, buf_t, cw, cw, cw, pa_t, ba, hp, col(norm_a), sa_t,
      pc_t, pc_t, pc_t, pc_t, col(lb), col(norm_c), sc_t)
    rows_first = lambda t: jnp.transpose(t, (2, 0, 1)).reshape(bs * t_new, W_MIX)
    back = lambda t: jnp.transpose(t, (3, 0, 1, 2))
    return rows_first(oa), back(sa_n), rows_first(oc), back(sc_n)


def _heads(t, n):
    return t.reshape(t.shape[0], t.shape[1], n, HEAD_DIM)


def _prompt_mixers(pa, pb, pc, pd, pba, batch, seq, conv_w, head_params, norm_a, sinks, lb, norm_c):
    pa3 = pa.reshape(batch, seq, COLS_A)
    pb3 = pb.reshape(batch, seq, COLS_B)
    pc3 = pc.reshape(batch, seq, COLS_C)
    pd3 = pd.reshape(batch, seq, COLS_D)
    oa, s_a = _gdn_prompt(pa3, pba.reshape(batch, seq, COLS_BA), conv_w, head_params, norm_a)
    oc, s_c = _hgrn_prompt(pc3, lb, norm_c)
    sink_row = jnp.pad(sinks.astype(F32), (0, 128 - N_HEADS)).reshape(1, 128)
    (ob,) = _band_attention(pb3, batch, seq, 1, 0, 256, 384, KV_B, sink_row, False)
    ods, lses = [], []
    for r in DILATIONS:
        o, lse = _band_attention(pd3, batch, seq, r, 0, 256, 512, N_HEADS, None, True)
        ods.append(o)
        lses.append(lse)
    nb, nd = min(CACHE_B, seq), min(CACHE_D, seq)
    state = (_heads(pb3[:, seq - nb:, 256:384], KV_B), _heads(pb3[:, seq - nb:, 384:512], KV_B),
             _heads(pd3[:, seq - nd:, 256:512], N_HEADS), _heads(pd3[:, seq - nd:, 512:768], N_HEADS),
             pa3[:, seq - (CONV_A - 1):, 0:3 * W_MIX], s_a, s_c)
    return oa, ob, oc, ods, lses, state


def _sample_mixers(pa, pb, pc, pd, pba, bs, t_new, caches, conv_w, a_log, dt_bias, norm_a, sinks, lb, norm_c):
    cbk, cbv, cdk, cdv, conv_buf, s_a, s_c = caches
    pa3 = pa.reshape(bs, t_new, COLS_A)
    pb3 = pb.reshape(bs, t_new, COLS_B)
    pc3 = pc.reshape(bs, t_new, COLS_C)
    pd3 = pd.reshape(bs, t_new, COLS_D)
    flat = lambda c: c.reshape(c.shape[0], c.shape[1], -1)
    ob, od = _sample_attention(pb3, pd3, flat(cbk), flat(cbv), flat(cdk), flat(cdv), sinks)
    oa, s_a_new, oc, s_c_new = _sample_recurrences(
        pa3, pba.reshape(bs, t_new, COLS_BA), pc3, conv_buf, s_a, s_c, conv_w, a_log, dt_bias, norm_a, lb, norm_c)
    roll = lambda cache, new, n: jnp.concatenate([cache[:, t_new:], _heads(new, n).astype(cache.dtype)], axis=1)
    conv_all = jnp.concatenate([conv_buf.astype(F32), pa3[:, :, 0:3 * W_MIX]], axis=1)
    state = (roll(cbk, pb3[:, :, 256:384], KV_B), roll(cbv, pb3[:, :, 384:512], KV_B),
             roll(cdk, pd3[:, :, 256:512], N_HEADS), roll(cdv, pd3[:, :, 512:768], N_HEADS),
             conv_all[:, -(CONV_A - 1):].astype(conv_buf.dtype), s_a_new.astype(s_a.dtype),
             s_c_new.astype(s_c.dtype))
    return oa, ob, oc, od, state


def kernel(x_prompt, x_sample, cache_b_k, cache_b_v, cache_d_k, cache_d_v, state_a_conv, state_a_s, state_c_s, norm_mix, w_in, conv_a, a_log, dt_bias, norm_a, sinks_b, lb_logits, norm_c, w_out, norm_ffn, w_ffn_gu, w_ffn_down, w_router, b_router, w_moe_gu, w_moe_down, norm_final):
    depth = w_in.shape[0]
    batch, seq, _ = x_prompt.shape
    bs, t_new, _ = x_sample.shape
    lb_p = jax.nn.softmax(lb_logits.astype(F32), axis=0)
    lower_bounds = jnp.cumsum(lb_p, axis=0) - lb_p[0]
    xp = x_prompt.reshape(batch * seq, D_MODEL)
    xs = x_sample.reshape(bs * t_new, D_MODEL)
    tm_p, tm_s = 512, 256
    row = lambda v: v.astype(F32).reshape(1, -1)
    prompt_states, sample_states = [], []
    for l in range(depth):
        w_in_l = _permute_w_in(w_in[l])
        w_out_l = w_out[l].astype(BF16)
        conv_w = conv_a[l].astype(F32)
        head_params = _gdn_head_params(a_log[l], dt_bias[l])
        lb = lower_bounds[l]
        mix = (row(norm_a[l]), sinks_b[l], row(lb), row(norm_c[l]))

        projs = _inproj(xp, row(norm_mix[l]), w_in_l, tm_p)
        oa, ob, oc, ods, lses, st_p = _prompt_mixers(*projs, batch, seq, conv_w, head_params, *mix)
        xp = _outproj(xp, oa, ob, oc, ods, lses, w_out_l, tm_p)
        prompt_states.append(st_p)

        projs = _inproj(xs, row(norm_mix[l]), w_in_l, tm_s)
        caches = (cache_b_k[l], cache_b_v[l], cache_d_k[l], cache_d_v[l], state_a_conv[l], state_a_s[l],
                  state_c_s[l])
        oa, ob, oc, od, st_s = _sample_mixers(*projs, bs, t_new, caches, conv_w, a_log[l], dt_bias[l],
                                              norm_a[l], sinks_b[l], lb, norm_c[l])
        xs = _outproj(xs, oa, ob, oc, [od], [], w_out_l, tm_s)
        sample_states.append(st_s)

        if l % 2 == 0:
            w_gu = w_ffn_gu[l // 2].astype(BF16)
            w_dn = w_ffn_down[l // 2].astype(BF16)
            xp = _ffn(xp, row(norm_ffn[l]), w_gu, w_dn, 1024)
            xs = _ffn(xs, row(norm_ffn[l]), w_gu, w_dn, 512)
        else:
            w_gu = w_moe_gu[l // 2].astype(BF16)
            w_dn = w_moe_down[l // 2].astype(BF16)
            xp = _moe(xp, row(norm_ffn[l]), w_router[l // 2], b_router[l // 2], w_gu, w_dn, 1024)
            xs = _moe(xs, row(norm_ffn[l]), w_router[l // 2], b_router[l // 2], w_gu, w_dn, 512)
    y_prompt = _final_norm(xp, row(norm_final), 1024).reshape(batch, seq, D_MODEL)
    y_sample = _final_norm(xs, row(norm_final), 512).reshape(bs, t_new, D_MODEL)
    stack = lambda states: [jnp.stack(t, 0) for t in zip(*states)]
    return (y_prompt, y_sample, *stack(prompt_states), *stack(sample_states))
```

```python
import functools
import math

import numpy as np
import jax
import jax.numpy as jnp
from jax import lax
from jax.experimental import pallas as pl
from jax.experimental.pallas import tpu as pltpu

F32 = jnp.float32
BF16 = jnp.bfloat16
HIGHEST = lax.Precision.HIGHEST

D_MODEL = 1024
HEAD_DIM = 64
N_HEADS = 4
KV_B = 2
W_MIX = N_HEADS * HEAD_DIM
CONV_A = 4
WIN = 128
DILATIONS = (1, 4, 16)
CACHE_D = 2048
CACHE_B = 128
D_FF = 2816
N_EXPERTS = 8
TOP_K = 2
D_FF_E = 3584
EPS = 1e-6
SCALE = HEAD_DIM ** -0.5
NEG = -1e30

COLS_A = 1024
COLS_B = 512
COLS_C = 1024
COLS_D = 768
COLS_BA = 128
COLS_ALL = COLS_A + COLS_B + COLS_C + COLS_D + COLS_BA

CHUNK = 64
BATCH_BLOCK = 4
MOE_ROWS = 512
MOE_FF_TILE = 896
VMEM_LIMIT = 56 * 1024 * 1024


def _cparams(*sem):
    return pltpu.CompilerParams(dimension_semantics=sem, vmem_limit_bytes=VMEM_LIMIT)


def _mm(a, b):
    return jnp.dot(a.astype(BF16), b.astype(BF16), preferred_element_type=F32)


def _mm_nt(a, b):
    return lax.dot_general(a.astype(BF16), b.astype(BF16), (((1,), (1,)), ((), ())),
                           preferred_element_type=F32)


def _mm_tn(a, b):
    return lax.dot_general(a.astype(BF16), b.astype(BF16), (((0,), (0,)), ((), ())),
                           preferred_element_type=F32)


def _split_bf16(a):
    hi = a.astype(BF16)
    return hi, (a - hi.astype(F32)).astype(BF16)


def _mm_3pass(a, b):
    ah, al = _split_bf16(a)
    bh, bl = _split_bf16(b)
    dot = functools.partial(jnp.dot, preferred_element_type=F32)
    return dot(ah, bh) + (dot(al, bh) + dot(ah, bl))


def _mm_f32(a, b):
    return jnp.dot(a, b, precision=HIGHEST, preferred_element_type=F32)


def _sigmoid(x):
    return 1.0 / (1.0 + jnp.exp(-x))


def _silu(x):
    return x * _sigmoid(x)


def _softplus(x):
    return jnp.maximum(x, 0.0) + jnp.log(1.0 + jnp.exp(-jnp.abs(x)))


def _rms_rows(x, gain):
    return x * lax.rsqrt(jnp.mean(x * x, axis=-1, keepdims=True) + EPS) * gain


def _inproj_kernel(x_ref, g_ref, w_ref, oa_ref, ob_ref, oc_ref, od_ref, oba_ref):
    h = _rms_rows(x_ref[...], g_ref[...]).astype(BF16)
    c = 0
    for o_ref in (oa_ref, ob_ref, oc_ref, od_ref, oba_ref):
        n = o_ref.shape[1]
        o_ref[...] = jnp.dot(h, w_ref[:, c:c + n], preferred_element_type=F32)
        c += n


def _inproj(x2, gain, w_perm, tm):
    n = x2.shape[0]
    widths = (COLS_A, COLS_B, COLS_C, COLS_D, COLS_BA)
    return pl.pallas_call(
        _inproj_kernel,
        grid=(n // tm,),
        in_specs=[pl.BlockSpec((tm, D_MODEL), lambda i: (i, 0)),
                  pl.BlockSpec((1, D_MODEL), lambda i: (0, 0)),
                  pl.BlockSpec((D_MODEL, COLS_ALL), lambda i: (0, 0))],
        out_specs=[pl.BlockSpec((tm, w), lambda i: (i, 0)) for w in widths],
        out_shape=[jax.ShapeDtypeStruct((n, w), F32) for w in widths],
        compiler_params=_cparams("parallel"),
        name="inproj",
    )(x2, gain, w_perm)


def _permute_w_in(w):
    ba = jnp.pad(w[:, 1024:1032], ((0, 0), (0, COLS_BA - 8)))
    return jnp.concatenate([w[:, 0:1024], w[:, 1032:3336], ba], axis=1).astype(BF16)


def _band_kernel(q_ref, kp_ref, kc_ref, vp_ref, vc_ref, sink_ref, *out_refs, kv, with_sink, with_lse):
    o_ref = out_refs[0]
    n = pl.program_id(2)
    nb = q_ref.shape[0]
    row = lax.broadcasted_iota(jnp.int32, (WIN, 2 * WIN), 0)
    col = lax.broadcasted_iota(jnp.int32, (WIN, 2 * WIN), 1)
    dist = row + WIN - col
    valid = (dist >= 0) & (dist <= WIN) & ((col >= WIN) | (n > 0))
    group = N_HEADS // kv
    qs = [q_ref[b] * SCALE for b in range(nb)]
    kcat = [jnp.concatenate([kp_ref[b], kc_ref[b]], axis=0) for b in range(nb)]
    vcat = [jnp.concatenate([vp_ref[b], vc_ref[b]], axis=0) for b in range(nb)]
    units = [(b, h) for b in range(nb) for h in range(N_HEADS)]
    sl = lambda h: slice(h * HEAD_DIM, (h + 1) * HEAD_DIM)
    s = [jnp.where(valid, _mm_nt(qs[b][:, sl(h)], kcat[b][:, sl(h // group)]), NEG) for b, h in units]
    m = [jnp.max(t, axis=-1, keepdims=True) for t in s]
    if with_sink:
        sink = [sink_ref[:, h:h + 1] for _, h in units]
        m = [jnp.maximum(a, b) for a, b in zip(m, sink)]
    p = [jnp.exp(t - a) for t, a in zip(s, m)]
    l = [jnp.sum(t, axis=-1, keepdims=True) for t in p]
    if with_sink:
        l = [a + jnp.exp(b - c) for a, b, c in zip(l, sink, m)]
    o = [_mm(t, vcat[b][:, sl(h // group)]) / a for t, a, (b, h) in zip(p, l, units)]
    for b in range(nb):
        o_ref[b] = jnp.concatenate(o[b * N_HEADS:(b + 1) * N_HEADS], axis=-1)
        if with_lse:
            lse = [jnp.broadcast_to(m[i] + jnp.log(l[i]), (WIN, HEAD_DIM)) for i in range(b * N_HEADS, (b + 1) * N_HEADS)]
            out_refs[1][b] = jnp.concatenate(lse, axis=-1)


def _band_attention(p3, batch, seq, r, q_col, k_col, v_col, kv, sinks, with_lse):
    width = p3.shape[-1]
    ln = seq // r
    bb = BATCH_BLOCK if batch % BATCH_BLOCK == 0 else 1
    pv = p3.reshape(batch, ln, r * width)
    wq, wk = W_MIX, kv * HEAD_DIM
    qb, kb, vb = q_col // wq, k_col // wk, v_col // wk
    nq, nk = width // wq, width // wk
    cur = lambda off, per: (lambda b, rho, n: (b, n, rho * per + off))
    prev = lambda off, per: (lambda b, rho, n: (b, jnp.maximum(n - 1, 0), rho * per + off))
    out_spec = pl.BlockSpec((bb, WIN, W_MIX), lambda b, rho, n: (b, n, rho))
    out_shape = jax.ShapeDtypeStruct((batch, ln, r * W_MIX), F32)
    n_out = 2 if with_lse else 1
    res = pl.pallas_call(
        functools.partial(_band_kernel, kv=kv, with_sink=sinks is not None, with_lse=with_lse),
        grid=(batch // bb, r, ln // WIN),
        in_specs=[pl.BlockSpec((bb, WIN, wq), cur(qb, nq)),
                  pl.BlockSpec((bb, WIN, wk), prev(kb, nk)),
                  pl.BlockSpec((bb, WIN, wk), cur(kb, nk)),
                  pl.BlockSpec((bb, WIN, wk), prev(vb, nk)),
                  pl.BlockSpec((bb, WIN, wk), cur(vb, nk)),
                  pl.BlockSpec((1, 128), lambda b, rho, n: (0, 0))],
        out_specs=[out_spec] * n_out,
        out_shape=[out_shape] * n_out,
        compiler_params=_cparams("parallel", "parallel", "arbitrary"),
        name=f"band_r{r}",
    )(pv, pv, pv, pv, pv, sinks if sinks is not None else jnp.zeros((1, 128), F32))
    return [t.reshape(batch * seq, W_MIX) for t in res]


def _gated_norm_rows(o, gate, gain):
    return _rms_rows(o, gain) * _silu(gate)


def _hgrn_constants(c):
    halves = []
    h = c // 2
    while h >= 1:
        halves.append(h)
        h //= 2
    t = np.arange(c)[:, None]
    u = np.arange(c)[None, :]
    mats = [(u <= t).astype(np.float32)]
    level = np.full((c, c), -1, np.int32)
    level[np.arange(c), np.arange(c)] = 0
    for li, h in enumerate(halves, 1):
        mid = (t // (2 * h)) * (2 * h) + h
        second = (t % (2 * h)) >= h
        mats.append(np.where(second, (u > mid) & (u <= t), (u > t) & (u <= mid)).astype(np.float32))
        pair = (t // (2 * h) == u // (2 * h)) & ((t % (2 * h)) >= h) & ((u % (2 * h)) < h)
        level[pair] = li
    return np.concatenate(mats, axis=0), level, len(halves)


def _hgrn_kernel(pc_ref, lb_ref, gain_ref, mat_ref, lvl_ref, o_ref, st_ref, s_scr, *, c, n_levels):
    j = pl.program_id(1)
    nb = pc_ref.shape[0]

    @pl.when(j == 0)
    def _():
        s_scr[...] = jnp.zeros_like(s_scr)

    lb = lb_ref[...]
    lvl = lvl_ref[...]
    xs = [pc_ref[b] for b in range(nb)]
    fs = [lb + (1.0 - lb) * _sigmoid(x[:, 256:512]) for x in xs]
    sums = [_mm_f32(mat_ref[...], -jnp.log(f)) for f in fs]
    gcs = [-s[0:c] for s in sums]
    g_last = [gc[c - 1:c, :] for gc in gcs]
    q_dec = [x[:, 0:256] * jnp.exp(gc) for x, gc in zip(xs, gcs)]
    k_dec = [(1.0 - f) * jnp.exp(gl - gc) for f, gl, gc in zip(fs, g_last, gcs)]
    units = [(b, h) for b in range(nb) for h in range(N_HEADS)]
    sl = lambda h: slice(h * HEAD_DIM, (h + 1) * HEAD_DIM)
    qh = [xs[b][:, sl(h)] for b, h in units]
    kh = [1.0 - fs[b][:, sl(h)] for b, h in units]
    vh = [xs[b][:, 512 + h * HEAD_DIM:512 + (h + 1) * HEAD_DIM] for b, h in units]
    a = [jnp.where(lvl == 0, _mm_nt(q, k), 0.0) for q, k in zip(qh, kh)]
    for li in range(1, n_levels + 1):
        damp = [jnp.exp(-sums[b][li * c:(li + 1) * c, sl(h)]) for b, h in units]
        part = [_mm_nt(q * d, k * d) for q, k, d in zip(qh, kh, damp)]
        a = [acc + jnp.where(lvl == li, p, 0.0) for acc, p in zip(a, part)]
    st = [s_scr[b, h] for b, h in units]
    o_inter = [_mm_nt(q_dec[b][:, sl(h)], t) for (b, h), t in zip(units, st)]
    o = [oi + _mm(aa, v) for oi, aa, v in zip(o_inter, a, vh)]
    upd = [_mm_tn(v, k_dec[b][:, sl(h)]) for (b, h), v in zip(units, vh)]
    for i, (b, h) in enumerate(units):
        s_scr[b, h] = jnp.exp(g_last[b][:, sl(h)]) * st[i] + upd[i]
    for b in range(nb):
        outs = [_gated_norm_rows(o[b * N_HEADS + h], xs[b][:, 768 + h * HEAD_DIM:768 + (h + 1) * HEAD_DIM],
                                 gain_ref[...]) for h in range(N_HEADS)]
        o_ref[b] = jnp.concatenate(outs, axis=-1)

    @pl.when(j == pl.num_programs(1) - 1)
    def _():
        st_ref[...] = s_scr[...]


def _hgrn_prompt(pc3, lb, gain):
    batch, seq, _ = pc3.shape
    c = CHUNK
    bb = BATCH_BLOCK if batch % BATCH_BLOCK == 0 else 1
    mat, level, n_levels = _hgrn_constants(c)
    o, st = pl.pallas_call(
        functools.partial(_hgrn_kernel, c=c, n_levels=n_levels),
        grid=(batch // bb, seq // c),
        in_specs=[pl.BlockSpec((bb, c, COLS_C), lambda b, j: (b, j, 0)),
                  pl.BlockSpec((1, W_MIX), lambda b, j: (0, 0)),
                  pl.BlockSpec((1, HEAD_DIM), lambda b, j: (0, 0)),
                  pl.BlockSpec(mat.shape, lambda b, j: (0, 0)),
                  pl.BlockSpec(level.shape, lambda b, j: (0, 0))],
        out_specs=[pl.BlockSpec((bb, c, W_MIX), lambda b, j: (b, j, 0)),
                   pl.BlockSpec((bb, N_HEADS, HEAD_DIM, HEAD_DIM), lambda b, j: (b, 0, 0, 0))],
        out_shape=[jax.ShapeDtypeStruct((batch, seq, W_MIX), F32),
                   jax.ShapeDtypeStruct((batch, N_HEADS, HEAD_DIM, HEAD_DIM), F32)],
        scratch_shapes=[pltpu.VMEM((bb, N_HEADS, HEAD_DIM, HEAD_DIM), F32)],
        compiler_params=_cparams("parallel", "arbitrary"),
        name="hgrn_prompt",
    )(pc3, lb, gain, jnp.asarray(mat), jnp.asarray(level))
    return o.reshape(batch * seq, W_MIX), jnp.swapaxes(st, -1, -2)


def _unit_lower_solve(lows, rhss, c):
    xs = [rhs - _mm_3pass(low, rhs) for low, rhs in zip(lows, rhss)]
    ps = lows
    span = 2
    while span < c:
        mm = _mm_3pass if span == 2 else _mm
        ps = [mm(p, p) for p in ps]
        xs = [x + mm(p, x) for p, x in zip(ps, xs)]
        span *= 2
    return xs


def _gdn_kernel(pa_ref, pba_ref, cw_ref, hp_ref, gain_ref, tri_ref, ones_ref, o_ref, st_ref,
                s_scr, buf_scr, *, c):
    j = pl.program_id(1)
    pad = 8
    nb = pa_ref.shape[0]

    @pl.when(j == 0)
    def _():
        s_scr[...] = jnp.zeros_like(s_scr)
        buf_scr[:, 0:pad, :] = jnp.zeros((nb, pad, 3 * W_MIX), F32)

    ones = ones_ref[...]
    row = lax.broadcasted_iota(jnp.int32, (c, c), 0)
    col = lax.broadcasted_iota(jnp.int32, (c, c), 1)
    qs, ks, vs, gates, betas, gcs, gcts = [], [], [], [], [], [], []
    for b in range(nb):
        x = pa_ref[b, :, 0:3 * W_MIX]
        gates.append(pa_ref[b, :, 3 * W_MIX:4 * W_MIX])
        buf_scr[b, pad:pad + c, :] = x
        y = cw_ref[CONV_A - 1:CONV_A, :] * x
        for tap in range(CONV_A - 1):
            back = CONV_A - 1 - tap
            y = y + cw_ref[tap:tap + 1, :] * buf_scr[b, pad - back:pad - back + c, :]
        buf_scr[b, 0:pad, :] = buf_scr[b, c:c + pad, :]
        y = _silu(y)
        qs.append(y[:, 0:W_MIX])
        ks.append(y[:, W_MIX:2 * W_MIX])
        vs.append(y[:, 2 * W_MIX:3 * W_MIX])
        ba = pba_ref[b]
        betas.append(_sigmoid(ba))
        g = hp_ref[0:1, :] * _softplus(ba + hp_ref[1:2, :])
        gcs.append(_mm_f32(tri_ref[0], g))
        gcts.append(lax.dot_general(g, tri_ref[1], (((0,), (0,)), ((), ())), precision=HIGHEST,
                                    preferred_element_type=F32))
    qs = [q * lax.rsqrt(_mm_f32(q * q, ones) + EPS) * SCALE for q in qs]
    ks = [k * lax.rsqrt(_mm_f32(k * k, ones) + EPS) for k in ks]
    units = [(b, h) for b in range(nb) for h in range(N_HEADS)]
    sl = lambda h: slice(h * HEAD_DIM, (h + 1) * HEAD_DIM)
    qh = [qs[b][:, sl(h)] for b, h in units]
    kh = [ks[b][:, sl(h)] for b, h in units]
    vh = [vs[b][:, sl(h)] for b, h in units]
    b_col = [betas[b][:, h:h + 1] for b, h in units]
    g_col = [gcs[b][:, 4 + h:5 + h] for b, h in units]
    g_row = [gcts[b][4 + h:5 + h, :] for b, h in units]
    decay = [jnp.where(row >= col, jnp.exp(jnp.minimum(gc_ - gr_, 0.0)), 0.0) for gc_, gr_ in zip(g_col, g_row)]
    kk = [_mm_nt(k, k) for k in kh]
    qk = [_mm_nt(q, k) for q, k in zip(qh, kh)]
    low = [jnp.where(row > col, bc * a * d, 0.0) for bc, a, d in zip(b_col, kk, decay)]
    eg = [jnp.exp(gc_) for gc_ in g_col]
    rhs = [jnp.concatenate([v * bc, k * (bc * e)], axis=-1) for v, k, bc, e in zip(vh, kh, b_col, eg)]
    sol = _unit_lower_solve(low, rhs, c)
    intra = [a * d for a, d in zip(qk, decay)]
    g_last = [gc_[c - 1:c, :] for gc_ in g_col]
    st = [s_scr[b, h] for b, h in units]
    u = [s[:, 0:HEAD_DIM] - _mm_nt(s[:, HEAD_DIM:2 * HEAD_DIM], t) for s, t in zip(sol, st)]
    o_inter = [_mm_nt(q * e, t) for q, e, t in zip(qh, eg, st)]
    o = [oi + _mm(a, uu) for oi, a, uu in zip(o_inter, intra, u)]
    k_dec = [k * jnp.exp(gl - gc_) for k, gl, gc_ in zip(kh, g_last, g_col)]
    upd = [_mm_tn(uu, kd) for uu, kd in zip(u, k_dec)]
    for i, (b, h) in enumerate(units):
        s_scr[b, h] = jnp.exp(g_last[i]) * st[i] + upd[i]
    for b in range(nb):
        outs = [_gated_norm_rows(o[b * N_HEADS + h], gates[b][:, sl(h)], gain_ref[...]) for h in range(N_HEADS)]
        o_ref[b] = jnp.concatenate(outs, axis=-1)

    @pl.when(j == pl.num_programs(1) - 1)
    def _():
        st_ref[...] = s_scr[...]


def _gdn_prompt(pa3, pba3, conv_w, head_params, gain):
    batch, seq, _ = pa3.shape
    c = CHUNK
    bb = BATCH_BLOCK if batch % BATCH_BLOCK == 0 else 1
    lower = np.tril(np.ones((c, c), np.float32))
    tri = np.stack([lower, lower.T], axis=0)
    ones = np.kron(np.eye(N_HEADS, dtype=np.float32), np.ones((HEAD_DIM, HEAD_DIM), np.float32))
    o, st = pl.pallas_call(
        functools.partial(_gdn_kernel, c=c),
        grid=(batch // bb, seq // c),
        in_specs=[pl.BlockSpec((bb, c, COLS_A), lambda b, j: (b, j, 0)),
                  pl.BlockSpec((bb, c, COLS_BA), lambda b, j: (b, j, 0)),
                  pl.BlockSpec((CONV_A, 3 * W_MIX), lambda b, j: (0, 0)),
                  pl.BlockSpec((2, COLS_BA), lambda b, j: (0, 0)),
                  pl.BlockSpec((1, HEAD_DIM), lambda b, j: (0, 0)),
                  pl.BlockSpec((2, c, c), lambda b, j: (0, 0, 0)),
                  pl.BlockSpec((W_MIX, W_MIX), lambda b, j: (0, 0))],
        out_specs=[pl.BlockSpec((bb, c, W_MIX), lambda b, j: (b, j, 0)),
                   pl.BlockSpec((bb, N_HEADS, HEAD_DIM, HEAD_DIM), lambda b, j: (b, 0, 0, 0))],
        out_shape=[jax.ShapeDtypeStruct((batch, seq, W_MIX), F32),
                   jax.ShapeDtypeStruct((batch, N_HEADS, HEAD_DIM, HEAD_DIM), F32)],
        scratch_shapes=[pltpu.VMEM((bb, N_HEADS, HEAD_DIM, HEAD_DIM), F32),
                        pltpu.VMEM((bb, c + 8, 3 * W_MIX), F32)],
        compiler_params=_cparams("parallel", "arbitrary"),
        name="gdn_prompt",
    )(pa3, pba3, conv_w, head_params, gain, jnp.asarray(tri), jnp.asarray(ones))
    return o.reshape(batch * seq, W_MIX), jnp.swapaxes(st, -1, -2)


def _gdn_head_params(a_log, dt_bias):
    neg_a = jnp.pad(-jnp.exp(a_log.astype(F32)), (4, COLS_BA - 8))
    dtb = jnp.pad(dt_bias.astype(F32), (4, COLS_BA - 8))
    return jnp.stack([neg_a, dtb], axis=0)


def _outproj_kernel(*refs, n_branch):
    x_ref, oa_ref, ob_ref, oc_ref = refs[0:4]
    d_refs = refs[4:4 + n_branch]
    l_refs = refs[4 + n_branch:4 + 2 * n_branch - (0 if n_branch > 1 else 1)]
    w_ref, o_ref = refs[-2], refs[-1]
    if n_branch == 1:
        od = d_refs[0][...]
    else:
        lses = [r[...] for r in l_refs]
        m = functools.reduce(jnp.maximum, lses)
        es = [jnp.exp(l - m) for l in lses]
        num = sum(e * r[...] for e, r in zip(es, d_refs))
        od = num / sum(es)
    acc = x_ref[...]
    for i, part in enumerate((oa_ref[...], ob_ref[...], oc_ref[...], od)):
        acc = acc + _mm(part, w_ref[i * W_MIX:(i + 1) * W_MIX, :])
    o_ref[...] = acc


def _outproj(x2, oa, ob, oc, ods, lses, w_bf, tm):
    n = x2.shape[0]
    n_branch = len(ods)
    row = lambda w: pl.BlockSpec((tm, w), lambda i: (i, 0))
    args = [x2, oa, ob, oc, *ods, *lses, w_bf]
    specs = [row(D_MODEL)] + [row(W_MIX)] * (3 + n_branch + len(lses))
    specs.append(pl.BlockSpec((4 * W_MIX, D_MODEL), lambda i: (0, 0)))
    return pl.pallas_call(
        functools.partial(_outproj_kernel, n_branch=n_branch),
        grid=(n // tm,),
        in_specs=specs,
        out_specs=row(D_MODEL),
        out_shape=jax.ShapeDtypeStruct((n, D_MODEL), F32),
        compiler_params=_cparams("parallel"),
        name="outproj",
    )(*args)


def _ffn_kernel(x_ref, g_ref, wg_ref, wu_ref, wd_ref, o_ref, h_scr, acc_scr):
    f = pl.program_id(1)

    @pl.when(f == 0)
    def _():
        h_scr[...] = _rms_rows(x_ref[...], g_ref[...]).astype(BF16)
        acc_scr[...] = jnp.zeros_like(acc_scr)

    h = h_scr[...]
    gate = jnp.dot(h, wg_ref[...], preferred_element_type=F32)
    up = jnp.dot(h, wu_ref[...], preferred_element_type=F32)
    acc_scr[...] += _mm(_silu(gate) * up, wd_ref[...])

    @pl.when(f == pl.num_programs(1) - 1)
    def _():
        o_ref[...] = x_ref[...] + acc_scr[...]


def _ffn(x2, gain, w_gu_bf, w_down_bf, tm, tf=256):
    n = x2.shape[0]
    nf = D_FF // tf
    return pl.pallas_call(
        _ffn_kernel,
        grid=(n // tm, nf),
        in_specs=[pl.BlockSpec((tm, D_MODEL), lambda i, f: (i, 0)),
                  pl.BlockSpec((1, D_MODEL), lambda i, f: (0, 0)),
                  pl.BlockSpec((D_MODEL, tf), lambda i, f: (0, f)),
                  pl.BlockSpec((D_MODEL, tf), lambda i, f: (0, nf + f)),
                  pl.BlockSpec((tf, D_MODEL), lambda i, f: (f, 0))],
        out_specs=pl.BlockSpec((tm, D_MODEL), lambda i, f: (i, 0)),
        out_shape=jax.ShapeDtypeStruct((n, D_MODEL), F32),
        scratch_shapes=[pltpu.VMEM((tm, D_MODEL), BF16), pltpu.VMEM((tm, D_MODEL), F32)],
        compiler_params=_cparams("parallel", "arbitrary"),
        name="ffn",
    )(x2, gain, w_gu_bf, w_gu_bf, w_down_bf)


ROW_TILE = (8, 128)


def _router_kernel(x_ref, g_ref, wr_ref, br_ref, h_ref, route_ref):
    i = pl.program_id(0)
    last = pl.num_programs(0) - 1
    tm = x_ref.shape[0]

    @pl.when(i < last)
    def _():
        h = _rms_rows(x_ref[...], g_ref[...])
        h_ref[...] = h.reshape(tm, *ROW_TILE)
        lane = lax.broadcasted_iota(jnp.int32, (tm, 128), 1).astype(F32)
        logits = _mm_f32(h, wr_ref[...]) + br_ref[...]
        m1 = jnp.max(logits, axis=-1, keepdims=True)
        i1 = jnp.min(jnp.where(logits == m1, lane, 128.0), axis=-1, keepdims=True)
        rest = jnp.where(lane == i1, NEG, logits)
        m2 = jnp.max(rest, axis=-1, keepdims=True)
        i2 = jnp.min(jnp.where(rest == m2, lane, 128.0), axis=-1, keepdims=True)
        e2 = jnp.exp(m2 - m1)
        route_ref[...] = (jnp.where(lane == 0.0, i1, 0.0) + jnp.where(lane == 1.0, i2, 0.0)
                          + jnp.where(lane == 2.0, 1.0 / (1.0 + e2), 0.0)
                          + jnp.where(lane == 3.0, e2 / (1.0 + e2), 0.0))

    @pl.when(i == last)
    def _():
        h_ref[...] = jnp.zeros(h_ref.shape, F32)
        route_ref[...] = jnp.zeros(route_ref.shape, F32)


def _gather_rows_kernel(tok_ref, h_hbm, o_ref, sem):
    tg = o_ref.shape[0]
    base = pl.program_id(0) * tg

    def issue(r, carry):
        pltpu.make_async_copy(h_hbm.at[tok_ref[base + r]], o_ref.at[r], sem).start()
        return carry

    lax.fori_loop(0, tg, issue, 0)
    pltpu.make_async_copy(h_hbm.at[pl.ds(0, tg)], o_ref, sem).wait()


def _expert_kernel(be_ref, x_ref, wg_ref, wu_ref, wd_ref, o_ref, xb_scr, acc_scr):
    f = pl.program_id(1)
    tm = x_ref.shape[0]

    @pl.when(f == 0)
    def _():
        xb_scr[...] = x_ref[...].reshape(tm, D_MODEL).astype(BF16)
        acc_scr[...] = jnp.zeros_like(acc_scr)

    xb = xb_scr[...]
    gate = jnp.dot(xb, wg_ref[0], preferred_element_type=F32)
    up = jnp.dot(xb, wu_ref[0], preferred_element_type=F32)
    acc_scr[...] += _mm(_silu(gate) * up, wd_ref[0])

    @pl.when(f == pl.num_programs(1) - 1)
    def _():
        o_ref[...] = acc_scr[...].reshape(tm, *ROW_TILE)


def _combine_kernel(pos_ref, x_ref, route_ref, y_hbm, o_ref, first_scr, second_scr, sems):
    tc = x_ref.shape[0]
    base = pl.program_id(0) * (2 * tc)

    def issue(r, carry):
        pltpu.make_async_copy(y_hbm.at[pos_ref[base + 2 * r]], first_scr.at[r], sems.at[0]).start()
        pltpu.make_async_copy(y_hbm.at[pos_ref[base + 2 * r + 1]], second_scr.at[r], sems.at[1]).start()
        return carry

    lax.fori_loop(0, tc, issue, 0)
    pltpu.make_async_copy(y_hbm.at[pl.ds(0, tc)], first_scr, sems.at[0]).wait()
    pltpu.make_async_copy(y_hbm.at[pl.ds(0, tc)], second_scr, sems.at[1]).wait()
    route = route_ref[...]
    o_ref[...] = (x_ref[...] + route[:, 2:3] * first_scr[...].reshape(tc, D_MODEL)
                  + route[:, 3:4] * second_scr[...].reshape(tc, D_MODEL))


def _moe_routing(route, n, tm_rows):
    e_flat = route[:n, 0:TOP_K].astype(jnp.int32).reshape(-1)
    n_assign = n * TOP_K
    onehot = (e_flat[:, None] == jnp.arange(N_EXPERTS, dtype=jnp.int32)[None, :]).astype(jnp.int32)
    csum = jnp.cumsum(onehot, axis=0)
    rank = jnp.sum(csum * onehot, axis=1) - 1
    counts = csum[-1]
    padded = (counts + tm_rows - 1) // tm_rows * tm_rows
    pad_end = jnp.cumsum(padded)
    pad_start = pad_end - padded
    start = jnp.cumsum(counts) - counts
    dest = pad_start[e_flat] + rank
    n_blocks = -(-(n_assign + N_EXPERTS * (tm_rows - 1)) // tm_rows)
    blk_e = jnp.minimum(jnp.sum(jnp.arange(n_blocks, dtype=jnp.int32)[:, None] * tm_rows >= pad_end[None, :], axis=1),
                        N_EXPERTS - 1).astype(jnp.int32)
    order = jnp.argsort(e_flat, stable=True).astype(jnp.int32)
    rows = jnp.arange(n_blocks * tm_rows, dtype=jnp.int32)
    row_e = jnp.repeat(blk_e, tm_rows)
    offset = rows - pad_start[row_e]
    valid = (offset < counts[row_e]) & (rows < pad_end[N_EXPERTS - 1])
    src = jnp.clip(start[row_e] + offset, 0, n_assign - 1)
    row_tok = jnp.where(valid, order[src] // TOP_K, n).astype(jnp.int32)
    return row_tok, dest.astype(jnp.int32), blk_e, n_blocks


def _moe(x2, gain, w_router, b_router, w_gu_bf, w_down_bf, tm):
    n = x2.shape[0]
    wr = jnp.pad(w_router.astype(F32), ((0, 0), (0, 128 - N_EXPERTS)))
    br = jnp.pad(b_router.astype(F32), (0, 128 - N_EXPERTS), constant_values=NEG).reshape(1, 128)
    nt = n // tm
    h3, route = pl.pallas_call(
        _router_kernel,
        grid=(nt + 1,),
        in_specs=[pl.BlockSpec((tm, D_MODEL), lambda i: (jnp.minimum(i, nt - 1), 0)),
                  pl.BlockSpec((1, D_MODEL), lambda i: (0, 0)),
                  pl.BlockSpec((D_MODEL, 128), lambda i: (0, 0)),
                  pl.BlockSpec((1, 128), lambda i: (0, 0))],
        out_specs=[pl.BlockSpec((tm, *ROW_TILE), lambda i: (i, 0, 0)),
                   pl.BlockSpec((tm, 128), lambda i: (i, 0))],
        out_shape=[jax.ShapeDtypeStruct((n + tm, *ROW_TILE), F32),
                   jax.ShapeDtypeStruct((n + tm, 128), F32)],
        compiler_params=_cparams("arbitrary"),
        name="moe_router",
    )(x2, gain, wr, br)

    row_tok, dest, blk_e, n_blocks = _moe_routing(route, n, MOE_ROWS)
    n_rows = n_blocks * MOE_ROWS

    x_rows = pl.pallas_call(
        _gather_rows_kernel,
        grid_spec=pltpu.PrefetchScalarGridSpec(
            num_scalar_prefetch=1,
            grid=(n_blocks,),
            in_specs=[pl.BlockSpec(memory_space=pl.ANY)],
            out_specs=pl.BlockSpec((MOE_ROWS, *ROW_TILE), lambda i, tok: (i, 0, 0)),
            scratch_shapes=[pltpu.SemaphoreType.DMA(())]),
        out_shape=jax.ShapeDtypeStruct((n_rows, *ROW_TILE), F32),
        compiler_params=_cparams("arbitrary"),
        name="moe_gather",
    )(row_tok, h3)

    nf = D_FF_E // MOE_FF_TILE
    y_rows = pl.pallas_call(
        _expert_kernel,
        grid_spec=pltpu.PrefetchScalarGridSpec(
            num_scalar_prefetch=1,
            grid=(n_blocks, nf),
            in_specs=[pl.BlockSpec((MOE_ROWS, *ROW_TILE), lambda i, f, be: (i, 0, 0)),
                      pl.BlockSpec((1, D_MODEL, MOE_FF_TILE), lambda i, f, be: (be[i], 0, f)),
                      pl.BlockSpec((1, D_MODEL, MOE_FF_TILE), lambda i, f, be: (be[i], 0, nf + f)),
                      pl.BlockSpec((1, MOE_FF_TILE, D_MODEL), lambda i, f, be: (be[i], f, 0))],
            out_specs=pl.BlockSpec((MOE_ROWS, *ROW_TILE), lambda i, f, be: (i, 0, 0)),
            scratch_shapes=[pltpu.VMEM((MOE_ROWS, D_MODEL), BF16), pltpu.VMEM((MOE_ROWS, D_MODEL), F32)]),
        out_shape=jax.ShapeDtypeStruct((n_rows, *ROW_TILE), F32),
        compiler_params=_cparams("arbitrary", "arbitrary"),
        name="moe_experts",
    )(blk_e, x_rows, w_gu_bf, w_gu_bf, w_down_bf)

    return pl.pallas_call(
        _combine_kernel,
        grid_spec=pltpu.PrefetchScalarGridSpec(
            num_scalar_prefetch=1,
            grid=(nt,),
            in_specs=[pl.BlockSpec((tm, D_MODEL), lambda i, pos: (i, 0)),
                      pl.BlockSpec((tm, 128), lambda i, pos: (i, 0)),
                      pl.BlockSpec(memory_space=pl.ANY)],
            out_specs=pl.BlockSpec((tm, D_MODEL), lambda i, pos: (i, 0)),
            scratch_shapes=[pltpu.VMEM((tm, *ROW_TILE), F32), pltpu.VMEM((tm, *ROW_TILE), F32),
                            pltpu.SemaphoreType.DMA((2,))]),
        out_shape=jax.ShapeDtypeStruct((n, D_MODEL), F32),
        compiler_params=_cparams("arbitrary"),
        name="moe_combine",
    )(dest, x2, route, y_rows)


def _norm_kernel(x_ref, g_ref, o_ref):
    o_ref[...] = _rms_rows(x_ref[...], g_ref[...])


def _final_norm(x2, gain, tm):
    n = x2.shape[0]
    return pl.pallas_call(
        _norm_kernel,
        grid=(n // tm,),
        in_specs=[pl.BlockSpec((tm, D_MODEL), lambda i: (i, 0)),
                  pl.BlockSpec((1, D_MODEL), lambda i: (0, 0))],
        out_specs=pl.BlockSpec((tm, D_MODEL), lambda i: (i, 0)),
        out_shape=jax.ShapeDtypeStruct((n, D_MODEL), F32),
        compiler_params=_cparams("parallel"),
        name="final_norm",
    )(x2, gain)


ROWS_T = 8
QROWS = N_HEADS * ROWS_T


def _softmax_two_parts(sc, sn, vc, vn, valid_c, valid_n, sink):
    sc = jnp.where(valid_c, sc, NEG)
    sn = jnp.where(valid_n, sn, NEG)
    m = jnp.maximum(jnp.max(sc, axis=-1, keepdims=True), jnp.max(sn, axis=-1, keepdims=True))
    if sink is not None:
        m = jnp.maximum(m, sink)
    pc = jnp.exp(sc - m)
    pn = jnp.exp(sn - m)
    l = jnp.sum(pc, axis=-1, keepdims=True) + jnp.sum(pn, axis=-1, keepdims=True)
    if sink is not None:
        l = l + jnp.exp(sink - m)
    return (_mm(pc, vc) + _mm(pn, vn)) / l, m + jnp.log(l)


def _sattn_kernel(qd_ref, kdn_ref, vdn_ref, kdc_ref, vdc_ref, qb_ref, kbn_ref, vbn_ref, kbc_ref, vbc_ref,
                  sink_ref, od_ref, ob_ref, *, bb, t_new):
    def dist_maps(n_cache):
        row = lax.broadcasted_iota(jnp.int32, (QROWS, n_cache), 0)
        col = lax.broadcasted_iota(jnp.int32, (QROWS, n_cache), 1)
        d_cache = n_cache + (row & (ROWS_T - 1)) - col
        rown = lax.broadcasted_iota(jnp.int32, (QROWS, 128), 0)
        coln = lax.broadcasted_iota(jnp.int32, (QROWS, 128), 1)
        d_new = (rown & (ROWS_T - 1)) - coln
        return d_cache, d_new, (coln < t_new) & (d_new >= 0)

    dc_d, dn_d, ok_new_d = dist_maps(CACHE_D)
    dc_b, dn_b, ok_new_b = dist_maps(CACHE_B)
    hrow = lax.broadcasted_iota(jnp.int32, (QROWS, W_MIX), 0) >> int(math.log2(ROWS_T))
    hlane = lax.broadcasted_iota(jnp.int32, (QROWS, W_MIX), 1) >> int(math.log2(HEAD_DIM))
    own = hrow == hlane
    zpad_d = jnp.zeros((128 - ROWS_T, W_MIX), F32)
    zpad_b = jnp.zeros((128 - ROWS_T, KV_B * HEAD_DIM), F32)
    zq = jnp.zeros((ROWS_T, HEAD_DIM), F32)
    for i in range(bb):
        q = qd_ref[i] * SCALE
        qblk = jnp.where(own, jnp.concatenate([q] * N_HEADS, axis=0), 0.0)
        kc, vc = kdc_ref[i], vdc_ref[i]
        kn = jnp.concatenate([kdn_ref[i], zpad_d], axis=0)
        vn = jnp.concatenate([vdn_ref[i], zpad_d], axis=0)
        sc = _mm_nt(qblk, kc)
        sn = _mm_nt(qblk, kn)
        outs, lses = [], []
        for r in DILATIONS:
            ok_c = (dc_d <= WIN * r) & ((dc_d & (r - 1)) == 0)
            ok_n = ok_new_d & ((dn_d & (r - 1)) == 0)
            o, lse = _softmax_two_parts(sc, sn, vc, vn, ok_c, ok_n, None)
            outs.append(o)
            lses.append(lse)
        m = functools.reduce(jnp.maximum, lses)
        es = [jnp.exp(l - m) for l in lses]
        o = sum(e * t for e, t in zip(es, outs)) / sum(es)
        o = jnp.where(own, o, 0.0)
        od_ref[i] = sum(o[h * ROWS_T:(h + 1) * ROWS_T] for h in range(N_HEADS))
        qb = qb_ref[i] * SCALE
        blocks = []
        for h in range(N_HEADS):
            qh = qb[:, h * HEAD_DIM:(h + 1) * HEAD_DIM]
            blocks.append(jnp.concatenate([qh, zq] if h < N_HEADS // KV_B else [zq, qh], axis=-1))
        qblk = jnp.concatenate(blocks, axis=0)
        kn = jnp.concatenate([kbn_ref[i], zpad_b], axis=0)
        vn = jnp.concatenate([vbn_ref[i], zpad_b], axis=0)
        o, _ = _softmax_two_parts(_mm_nt(qblk, kbc_ref[i]), _mm_nt(qblk, kn), vbc_ref[i], vn,
                                  dc_b <= WIN, ok_new_b, sink_ref[...])
        pieces = []
        for h in range(N_HEADS):
            g = h // (N_HEADS // KV_B)
            pieces.append(o[h * ROWS_T:(h + 1) * ROWS_T, g * HEAD_DIM:(g + 1) * HEAD_DIM])
        ob_ref[i] = jnp.concatenate(pieces, axis=-1)


def _pad_rows(t, rows):
    return jnp.pad(t, ((0, 0), (0, rows - t.shape[1]), (0, 0)))


def _sample_attention(pb_s, pd_s, cbk, cbv, cdk, cdv, sinks, bb=2):
    bs, t_new, _ = pb_s.shape
    pb8, pd8 = _pad_rows(pb_s, ROWS_T), _pad_rows(pd_s, ROWS_T)
    sink_col = jnp.repeat(sinks.astype(F32), ROWS_T).reshape(QROWS, 1)
    wb = KV_B * HEAD_DIM
    blk = lambda rows, w, c: pl.BlockSpec((bb, rows, w), lambda i: (i, 0, c))
    od, ob = pl.pallas_call(
        functools.partial(_sattn_kernel, bb=bb, t_new=t_new),
        grid=(bs // bb,),
        in_specs=[blk(ROWS_T, W_MIX, 0), blk(ROWS_T, W_MIX, 1), blk(ROWS_T, W_MIX, 2),
                  blk(CACHE_D, W_MIX, 0), blk(CACHE_D, W_MIX, 0),
                  blk(ROWS_T, W_MIX, 0), blk(ROWS_T, wb, 2), blk(ROWS_T, wb, 3),
                  blk(CACHE_B, wb, 0), blk(CACHE_B, wb, 0),
                  pl.BlockSpec((QROWS, 1), lambda i: (0, 0))],
        out_specs=[blk(ROWS_T, W_MIX, 0), blk(ROWS_T, W_MIX, 0)],
        out_shape=[jax.ShapeDtypeStruct((bs, ROWS_T, W_MIX), F32)] * 2,
        compiler_params=_cparams("parallel"),
        name="sample_attn",
    )(pd8, pd8, pd8, cdk, cdv, pb8, pb8, pb8, cbk, cbv, sink_col)
    return (ob[:, :t_new].reshape(bs * t_new, W_MIX), od[:, :t_new].reshape(bs * t_new, W_MIX))


def _srec_kernel(xq_ref, xk_ref, xv_ref, bq_ref, bk_ref, bv_ref, cwq_ref, cwk_ref, cwv_ref, ga_ref,
                 ba_ref, hp_ref, gna_ref, sa_ref, qc_ref, fc_ref, ic_ref, gcg_ref, lb_ref, gnc_ref, sc_ref,
                 oa_ref, sa_out, oc_ref, sc_out, q_scr, k_scr, d_scr, *, t_new):
    nb = sa_ref.shape[-1]
    zero = jnp.zeros((HEAD_DIM, nb), F32)

    def conv(x_ref, b_ref, cw_ref, t):
        y = None
        for tap in range(CONV_A):
            pos = t + tap
            src = b_ref[pos] if pos < CONV_A - 1 else x_ref[pos - (CONV_A - 1)]
            term = cw_ref[tap] * src
            y = term if y is None else y + term
        return _silu(y)

    def l2(x):
        return x * lax.rsqrt(jnp.sum(x * x, axis=0, keepdims=True) + EPS)

    def gated_norm(o, gate, gain):
        return o * lax.rsqrt(jnp.mean(o * o, axis=0, keepdims=True) + EPS) * gain * _silu(gate)

    sa_out[0] = sa_ref[0]
    for t in range(t_new):
        q_scr[...] = l2(conv(xq_ref, bq_ref, cwq_ref, t)) * SCALE
        k_scr[...] = l2(conv(xk_ref, bk_ref, cwk_ref, t))
        v = conv(xv_ref, bv_ref, cwv_ref, t)
        beta = _sigmoid(ba_ref[0, t:t + 1, :])
        dec = jnp.exp(hp_ref[0, 0:1, :] * _softplus(ba_ref[0, t_new + t:t_new + t + 1, :] + hp_ref[0, 1:2, :]))

        def decay_and_read(kk, acc):
            s = sa_out[0, kk] * dec
            sa_out[0, kk] = s
            return acc + k_scr[pl.ds(kk, 1), :] * s

        err = (v - lax.fori_loop(0, HEAD_DIM, decay_and_read, zero)) * beta

        def write_and_query(kk, acc):
            s = sa_out[0, kk] + k_scr[pl.ds(kk, 1), :] * err
            sa_out[0, kk] = s
            return acc + q_scr[pl.ds(kk, 1), :] * s

        o = lax.fori_loop(0, HEAD_DIM, write_and_query, zero)
        oa_ref[t] = gated_norm(o, ga_ref[t], gna_ref[...])

    sc_out[0] = sc_ref[0]
    for t in range(t_new):
        lb = lb_ref[...]
        f = lb + (1.0 - lb) * _sigmoid(fc_ref[t])
        q_scr[...] = qc_ref[t]
        k_scr[...] = 1.0 - f
        d_scr[...] = jnp.exp(jnp.log(f))
        v = ic_ref[t]

        def update(kk, acc):
            s = sc_out[0, kk] * d_scr[pl.ds(kk, 1), :] + k_scr[pl.ds(kk, 1), :] * v
            sc_out[0, kk] = s
            return acc + q_scr[pl.ds(kk, 1), :] * s

        o = lax.fori_loop(0, HEAD_DIM, update, zero)
        oc_ref[t] = gated_norm(o, gcg_ref[t], gnc_ref[...])


def _sample_recurrences(pa_s, pba_s, pc_s, conv_buf, s_a, s_c, conv_w, a_log, dt_bias, norm_a, lb, norm_c):
    bs, t_new, _ = pa_s.shape
    lanes_last = lambda t: jnp.transpose(t, (1, 2, 0))
    pa_t = lanes_last(pa_s)
    pc_t = lanes_last(pc_s)
    buf_t = lanes_last(conv_buf.astype(F32))
    ba = jnp.transpose(pba_s[:, :, 0:8], (2, 1, 0))
    ba = jnp.concatenate([ba[0:N_HEADS], ba[N_HEADS:2 * N_HEADS]], axis=1)
    hp = jnp.stack([-jnp.exp(a_log.astype(F32)), dt_bias.astype(F32)], axis=1)
    hp = jnp.broadcast_to(hp[:, :, None], (N_HEADS, 2, bs))
    cw = conv_w.astype(F32)[:, :, None]
    sa_t = jnp.transpose(s_a.astype(F32), (1, 2, 3, 0))
    sc_t = jnp.transpose(s_c.astype(F32), (1, 2, 3, 0))
    col = lambda v: v.astype(F32).reshape(-1, 1)
    hd = HEAD_DIM
    feat = lambda rows, off: pl.BlockSpec((rows, hd, bs), lambda h: (0, off + h, 0))
    cwspec = lambda off: pl.BlockSpec((CONV_A, hd, 1), lambda h: (0, off + h, 0))
    per_head = lambda rows: pl.BlockSpec((1, rows, bs), lambda h: (h, 0, 0))
    state = pl.BlockSpec((1, hd, hd, bs), lambda h: (h, 0, 0, 0))
    vec = pl.BlockSpec((hd, 1), lambda h: (0, 0))
    nh = N_HEADS
    oa, sa_n, oc, sc_n = pl.pallas_call(
        functools.partial(_srec_kernel, t_new=t_new),
        grid=(N_HEADS,),
        in_specs=[feat(t_new, 0), feat(t_new, nh), feat(t_new, 2 * nh),
                  feat(CONV_A - 1, 0), feat(CONV_A - 1, nh), feat(CONV_A - 1, 2 * nh),
                  cwspec(0), cwspec(nh), cwspec(2 * nh),
                  feat(t_new, 3 * nh), per_head(2 * t_new), per_head(2), vec, state,
                  feat(t_new, 0), feat(t_new, nh), feat(t_new, 2 * nh), feat(t_new, 3 * nh),
                  pl.BlockSpec((hd, 1), lambda h: (h, 0)), vec, state],
        out_specs=[feat(t_new, 0), state, feat(t_new, 0), state],
        out_shape=[jax.ShapeDtypeStruct((t_new, W_MIX, bs), F32),
                   jax.ShapeDtypeStruct((N_HEADS, hd, hd, bs), F32),
                   jax.ShapeDtypeStruct((t_new, W_MIX, bs), F32),
                   jax.ShapeDtypeStruct((N_HEADS, hd, hd, bs), F32)],
        scratch_shapes=[pltpu.VMEM((hd, bs), F32)] * 3,
        compiler_params=_cparams("parallel"),
        name="sample_recurrences",
    )(pa_t, pa_t, pa_t, buf_t, buf_t, buf_t, cw, cw, cw, pa_t, ba, hp, col(norm_a), sa_t,
      pc_t, pc_t, pc_t, pc_t, col(lb), col(norm_c), sc_t)
    rows_first = lambda t: jnp.transpose(t, (2, 0, 1)).reshape(bs * t_new, W_MIX)
    back = lambda t: jnp.transpose(t, (3, 0, 1, 2))
    return rows_first(oa), back(sa_n), rows_first(oc), back(sc_n)


def _heads(t, n):
    return t.reshape(t.shape[0], t.shape[1], n, HEAD_DIM)


def _prompt_mixers(pa, pb, pc, pd, pba, batch, seq, conv_w, head_params, norm_a, sinks, lb, norm_c):
    pa3 = pa.reshape(batch, seq, COLS_A)
    pb3 = pb.reshape(batch, seq, COLS_B)
    pc3 = pc.reshape(batch, seq, COLS_C)
    pd3 = pd.reshape(batch, seq, COLS_D)
    oa, s_a = _gdn_prompt(pa3, pba.reshape(batch, seq, COLS_BA), conv_w, head_params, norm_a)
    oc, s_c = _hgrn_prompt(pc3, lb, norm_c)
    sink_row = jnp.pad(sinks.astype(F32), (0, 128 - N_HEADS)).reshape(1, 128)
    (ob,) = _band_attention(pb3, batch, seq, 1, 0, 256, 384, KV_B, sink_row, False)
    ods, lses = [], []
    for r in DILATIONS:
        o, lse = _band_attention(pd3, batch, seq, r, 0, 256, 512, N_HEADS, None, True)
        ods.append(o)
        lses.append(lse)
    nb, nd = min(CACHE_B, seq), min(CACHE_D, seq)
    state = (_heads(pb3[:, seq - nb:, 256:384], KV_B), _heads(pb3[:, seq - nb:, 384:512], KV_B),
             _heads(pd3[:, seq - nd:, 256:512], N_HEADS), _heads(pd3[:, seq - nd:, 512:768], N_HEADS),
             pa3[:, seq - (CONV_A - 1):, 0:3 * W_MIX], s_a, s_c)
    return oa, ob, oc, ods, lses, state


def _sample_mixers(pa, pb, pc, pd, pba, bs, t_new, caches, conv_w, a_log, dt_bias, norm_a, sinks, lb, norm_c):
    cbk, cbv, cdk, cdv, conv_buf, s_a, s_c = caches
    pa3 = pa.reshape(bs, t_new, COLS_A)
    pb3 = pb.reshape(bs, t_new, COLS_B)
    pc3 = pc.reshape(bs, t_new, COLS_C)
    pd3 = pd.reshape(bs, t_new, COLS_D)
    flat = lambda c: c.reshape(c.shape[0], c.shape[1], -1)
    ob, od = _sample_attention(pb3, pd3, flat(cbk), flat(cbv), flat(cdk), flat(cdv), sinks)
    oa, s_a_new, oc, s_c_new = _sample_recurrences(
        pa3, pba.reshape(bs, t_new, COLS_BA), pc3, conv_buf, s_a, s_c, conv_w, a_log, dt_bias, norm_a, lb, norm_c)
    roll = lambda cache, new, n: jnp.concatenate([cache[:, t_new:], _heads(new, n).astype(cache.dtype)], axis=1)
    conv_all = jnp.concatenate([conv_buf.astype(F32), pa3[:, :, 0:3 * W_MIX]], axis=1)
    state = (roll(cbk, pb3[:, :, 256:384], KV_B), roll(cbv, pb3[:, :, 384:512], KV_B),
             roll(cdk, pd3[:, :, 256:512], N_HEADS), roll(cdv, pd3[:, :, 512:768], N_HEADS),
             conv_all[:, -(CONV_A - 1):].astype(conv_buf.dtype), s_a_new.astype(s_a.dtype),
             s_c_new.astype(s_c.dtype))
    return oa, ob, oc, od, state


def kernel(x_prompt, x_sample, cache_b_k, cache_b_v, cache_d_k, cache_d_v, state_a_conv, state_a_s, state_c_s, norm_mix, w_in, conv_a, a_log, dt_bias, norm_a, sinks_b, lb_logits, norm_c, w_out, norm_ffn, w_ffn_gu, w_ffn_down, w_router, b_router, w_moe_gu, w_moe_down, norm_final):
    depth = w_in.shape[0]
    batch, seq, _ = x_prompt.shape
    bs, t_new, _ = x_sample.shape
    lb_p = jax.nn.softmax(lb_logits.astype(F32), axis=0)
    lower_bounds = jnp.cumsum(lb_p, axis=0) - lb_p[0]
    xp = x_prompt.reshape(batch * seq, D_MODEL)
    xs = x_sample.reshape(bs * t_new, D_MODEL)
    tm_p, tm_s = 512, 256
    row = lambda v: v.astype(F32).reshape(1, -1)
    prompt_states, sample_states = [], []
    for l in range(depth):
        w_in_l = _permute_w_in(w_in[l])
        w_out_l = w_out[l].astype(BF16)
        conv_w = conv_a[l].astype(F32)
        head_params = _gdn_head_params(a_log[l], dt_bias[l])
        lb = lower_bounds[l]
        mix = (row(norm_a[l]), sinks_b[l], row(lb), row(norm_c[l]))

        projs = _inproj(xp, row(norm_mix[l]), w_in_l, tm_p)
        oa, ob, oc, ods, lses, st_p = _prompt_mixers(*projs, batch, seq, conv_w, head_params, *mix)
        xp = _outproj(xp, oa, ob, oc, ods, lses, w_out_l, tm_p)
        prompt_states.append(st_p)

        projs = _inproj(xs, row(norm_mix[l]), w_in_l, tm_s)
        caches = (cache_b_k[l], cache_b_v[l], cache_d_k[l], cache_d_v[l], state_a_conv[l], state_a_s[l],
                  state_c_s[l])
        oa, ob, oc, od, st_s = _sample_mixers(*projs, bs, t_new, caches, conv_w, a_log[l], dt_bias[l],
                                              norm_a[l], sinks_b[l], lb, norm_c[l])
        xs = _outproj(xs, oa, ob, oc, [od], [], w_out_l, tm_s)
        sample_states.append(st_s)

        if l % 2 == 0:
            w_gu = w_ffn_gu[l // 2].astype(BF16)
            w_dn = w_ffn_down[l // 2].astype(BF16)
            xp = _ffn(xp, row(norm_ffn[l]), w_gu, w_dn, 1024)
            xs = _ffn(xs, row(norm_ffn[l]), w_gu, w_dn, 512)
        else:
            w_gu = w_moe_gu[l // 2].astype(BF16)
            w_dn = w_moe_down[l // 2].astype(BF16)
            xp = _moe(xp, row(norm_ffn[l]), w_router[l // 2], b_router[l // 2], w_gu, w_dn, 512)
            xs = _moe(xs, row(norm_ffn[l]), w_router[l // 2], b_router[l // 2], w_gu, w_dn, 512)
    y_prompt = _final_norm(xp, row(norm_final), 1024).reshape(batch, seq, D_MODEL)
    y_sample = _final_norm(xs, row(norm_final), 512).reshape(bs, t_new, D_MODEL)
    stack = lambda states: [jnp.stack(t, 0) for t in zip(*states)]
    return (y_prompt, y_sample, *stack(prompt_states), *stack(sample_states))
```

```python
import functools
import math

import numpy as np
import jax
import jax.numpy as jnp
from jax import lax
from jax.experimental import pallas as pl
from jax.experimental.pallas import tpu as pltpu

F32 = jnp.float32
BF16 = jnp.bfloat16
HIGHEST = lax.Precision.HIGHEST

D_MODEL = 1024
HEAD_DIM = 64
N_HEADS = 4
KV_B = 2
W_MIX = N_HEADS * HEAD_DIM
CONV_A = 4
WIN = 128
DILATIONS = (1, 4, 16)
CACHE_D = 2048
CACHE_B = 128
D_FF = 2816
N_EXPERTS = 8
TOP_K = 2
D_FF_E = 3584
EPS = 1e-6
SCALE = HEAD_DIM ** -0.5
NEG = -1e30

COLS_A = 1024
COLS_B = 512
COLS_C = 1024
COLS_D = 768
COLS_BA = 128
COLS_ALL = COLS_A + COLS_B + COLS_C + COLS_D + COLS_BA

CHUNK = 64
BATCH_BLOCK = 4
MOE_ROWS = 512
MOE_FF_TILE = 896
VMEM_LIMIT = 56 * 1024 * 1024


def _cparams(*sem):
    return pltpu.CompilerParams(dimension_semantics=sem, vmem_limit_bytes=VMEM_LIMIT)


def _mm(a, b):
    return jnp.dot(a.astype(BF16), b.astype(BF16), preferred_element_type=F32)


def _mm_nt(a, b):
    return lax.dot_general(a.astype(BF16), b.astype(BF16), (((1,), (1,)), ((), ())),
                           preferred_element_type=F32)


def _mm_tn(a, b):
    return lax.dot_general(a.astype(BF16), b.astype(BF16), (((0,), (0,)), ((), ())),
                           preferred_element_type=F32)


def _split_bf16(a):
    hi = a.astype(BF16)
    return hi, (a - hi.astype(F32)).astype(BF16)


def _mm_3pass(a, b):
    ah, al = _split_bf16(a)
    bh, bl = _split_bf16(b)
    dot = functools.partial(jnp.dot, preferred_element_type=F32)
    return dot(ah, bh) + (dot(al, bh) + dot(ah, bl))


def _mm_f32(a, b):
    return jnp.dot(a, b, precision=HIGHEST, preferred_element_type=F32)


def _sigmoid(x):
    return 1.0 / (1.0 + jnp.exp(-x))


def _silu(x):
    return x * _sigmoid(x)


def _softplus(x):
    return jnp.maximum(x, 0.0) + jnp.log(1.0 + jnp.exp(-jnp.abs(x)))


def _rms_rows(x, gain):
    return x * lax.rsqrt(jnp.mean(x * x, axis=-1, keepdims=True) + EPS) * gain


def _inproj_kernel(x_ref, g_ref, w_ref, oa_ref, ob_ref, oc_ref, od_ref, oba_ref, *strided):
    h = _rms_rows(x_ref[...], g_ref[...]).astype(BF16)
    c = 0
    for o_ref in (oa_ref, ob_ref, oc_ref, od_ref, oba_ref):
        n = o_ref.shape[1]
        o_ref[...] = jnp.dot(h, w_ref[:, c:c + n], preferred_element_type=F32)
        c += n
    if strided:
        *strided, chunk_scr = strided
        n_chunks = chunk_scr.shape[0]
        for ch in range(n_chunks):
            chunk_scr[ch] = od_ref[:, ch * 128:(ch + 1) * 128]
        for o_ref in strided:
            r, rows = o_ref.shape[1], o_ref.shape[2]
            for rho in range(r):
                for ch in range(n_chunks):
                    o_ref[0, rho, :, ch * 128:(ch + 1) * 128] = chunk_scr[ch, pl.ds(rho, rows, stride=r), :]


def _inproj(x2, gain, w_perm, tm, strided_for=None):
    n = x2.shape[0]
    widths = (COLS_A, COLS_B, COLS_C, COLS_D, COLS_BA)
    out_specs = [pl.BlockSpec((tm, w), lambda i: (i, 0)) for w in widths]
    out_shape = [jax.ShapeDtypeStruct((n, w), F32) for w in widths]
    scratch = []
    if strided_for is not None:
        batch, seq = strided_for
        per_seq = seq // tm
        for r in DILATIONS[1:]:
            out_specs.append(pl.BlockSpec((1, r, tm // r, COLS_D), lambda i: (i // per_seq, 0, i % per_seq, 0)))
            out_shape.append(jax.ShapeDtypeStruct((batch, r, seq // r, COLS_D), F32))
        scratch = [pltpu.VMEM((COLS_D // 128, tm, 128), F32)]
    return pl.pallas_call(
        _inproj_kernel,
        grid=(n // tm,),
        in_specs=[pl.BlockSpec((tm, D_MODEL), lambda i: (i, 0)),
                  pl.BlockSpec((1, D_MODEL), lambda i: (0, 0)),
                  pl.BlockSpec((D_MODEL, COLS_ALL), lambda i: (0, 0))],
        out_specs=out_specs,
        out_shape=out_shape,
        scratch_shapes=scratch,
        compiler_params=_cparams("parallel"),
        name="inproj",
    )(x2, gain, w_perm)


def _permute_w_in(w):
    ba = jnp.pad(w[:, 1024:1032], ((0, 0), (0, COLS_BA - 8)))
    return jnp.concatenate([w[:, 0:1024], w[:, 1032:3336], ba], axis=1).astype(BF16)


def _band_kernel(q_ref, kp_ref, kc_ref, vp_ref, vc_ref, sink_ref, *out_refs, kv, with_sink, with_lse):
    o_ref = out_refs[0]
    n = pl.program_id(2)
    nb = q_ref.shape[0]
    row = lax.broadcasted_iota(jnp.int32, (WIN, 2 * WIN), 0)
    col = lax.broadcasted_iota(jnp.int32, (WIN, 2 * WIN), 1)
    dist = row + WIN - col
    valid = (dist >= 0) & (dist <= WIN) & ((col >= WIN) | (n > 0))
    group = N_HEADS // kv
    qs = [q_ref[b] * SCALE for b in range(nb)]
    kcat = [jnp.concatenate([kp_ref[b], kc_ref[b]], axis=0) for b in range(nb)]
    vcat = [jnp.concatenate([vp_ref[b], vc_ref[b]], axis=0) for b in range(nb)]
    units = [(b, h) for b in range(nb) for h in range(N_HEADS)]
    sl = lambda h: slice(h * HEAD_DIM, (h + 1) * HEAD_DIM)
    s = [jnp.where(valid, _mm_nt(qs[b][:, sl(h)], kcat[b][:, sl(h // group)]), NEG) for b, h in units]
    m = [jnp.max(t, axis=-1, keepdims=True) for t in s]
    if with_sink:
        sink = [sink_ref[:, h:h + 1] for _, h in units]
        m = [jnp.maximum(a, b) for a, b in zip(m, sink)]
    p = [jnp.exp(t - a) for t, a in zip(s, m)]
    l = [jnp.sum(t, axis=-1, keepdims=True) for t in p]
    if with_sink:
        l = [a + jnp.exp(b - c) for a, b, c in zip(l, sink, m)]
    o = [_mm(t, vcat[b][:, sl(h // group)]) / a for t, a, (b, h) in zip(p, l, units)]
    for b in range(nb):
        o_ref[b] = jnp.concatenate(o[b * N_HEADS:(b + 1) * N_HEADS], axis=-1)
        if with_lse:
            lse = [jnp.broadcast_to(m[i] + jnp.log(l[i]), (WIN, HEAD_DIM)) for i in range(b * N_HEADS, (b + 1) * N_HEADS)]
            out_refs[1][b] = jnp.concatenate(lse, axis=-1)


def _band_attention(p4, q_col, k_col, v_col, kv, sinks, with_lse):
    batch, r, ln, width = p4.shape
    bb = BATCH_BLOCK if batch % BATCH_BLOCK == 0 else 1
    wq, wk = W_MIX, kv * HEAD_DIM
    qb, kb, vb = q_col // wq, k_col // wk, v_col // wk
    cur = lambda off: (lambda b, rho, n: (b, rho, n, off))
    prev = lambda off: (lambda b, rho, n: (b, rho, jnp.maximum(n - 1, 0), off))
    out_spec = pl.BlockSpec((bb, None, WIN, W_MIX), lambda b, rho, n: (b, rho, n, 0))
    out_shape = jax.ShapeDtypeStruct((batch, r, ln, W_MIX), F32)
    n_out = 2 if with_lse else 1
    return pl.pallas_call(
        functools.partial(_band_kernel, kv=kv, with_sink=sinks is not None, with_lse=with_lse),
        grid=(batch // bb, r, ln // WIN),
        in_specs=[pl.BlockSpec((bb, None, WIN, wq), cur(qb)),
                  pl.BlockSpec((bb, None, WIN, wk), prev(kb)),
                  pl.BlockSpec((bb, None, WIN, wk), cur(kb)),
                  pl.BlockSpec((bb, None, WIN, wk), prev(vb)),
                  pl.BlockSpec((bb, None, WIN, wk), cur(vb)),
                  pl.BlockSpec((1, 128), lambda b, rho, n: (0, 0))],
        out_specs=[out_spec] * n_out,
        out_shape=[out_shape] * n_out,
        compiler_params=_cparams("parallel", "parallel", "arbitrary"),
        name=f"band_r{r}",
    )(p4, p4, p4, p4, p4, sinks if sinks is not None else jnp.zeros((1, 128), F32))


def _gated_norm_rows(o, gate, gain):
    return _rms_rows(o, gain) * _silu(gate)


def _hgrn_constants(c):
    halves = []
    h = c // 2
    while h >= 1:
        halves.append(h)
        h //= 2
    t = np.arange(c)[:, None]
    u = np.arange(c)[None, :]
    mats = [(u <= t).astype(np.float32)]
    level = np.full((c, c), -1, np.int32)
    level[np.arange(c), np.arange(c)] = 0
    for li, h in enumerate(halves, 1):
        mid = (t // (2 * h)) * (2 * h) + h
        second = (t % (2 * h)) >= h
        mats.append(np.where(second, (u > mid) & (u <= t), (u > t) & (u <= mid)).astype(np.float32))
        pair = (t // (2 * h) == u // (2 * h)) & ((t % (2 * h)) >= h) & ((u % (2 * h)) < h)
        level[pair] = li
    return np.concatenate(mats, axis=0), level, len(halves)


def _hgrn_kernel(pc_ref, lb_ref, gain_ref, mat_ref, lvl_ref, o_ref, st_ref, s_scr, *, c, n_levels):
    j = pl.program_id(1)
    nb = pc_ref.shape[0]

    @pl.when(j == 0)
    def _():
        s_scr[...] = jnp.zeros_like(s_scr)

    lb = lb_ref[...]
    lvl = lvl_ref[...]
    xs = [pc_ref[b] for b in range(nb)]
    fs = [lb + (1.0 - lb) * _sigmoid(x[:, 256:512]) for x in xs]
    sums = [_mm_f32(mat_ref[...], -jnp.log(f)) for f in fs]
    gcs = [-s[0:c] for s in sums]
    g_last = [gc[c - 1:c, :] for gc in gcs]
    q_dec = [x[:, 0:256] * jnp.exp(gc) for x, gc in zip(xs, gcs)]
    k_dec = [(1.0 - f) * jnp.exp(gl - gc) for f, gl, gc in zip(fs, g_last, gcs)]
    units = [(b, h) for b in range(nb) for h in range(N_HEADS)]
    sl = lambda h: slice(h * HEAD_DIM, (h + 1) * HEAD_DIM)
    qh = [xs[b][:, sl(h)] for b, h in units]
    kh = [1.0 - fs[b][:, sl(h)] for b, h in units]
    vh = [xs[b][:, 512 + h * HEAD_DIM:512 + (h + 1) * HEAD_DIM] for b, h in units]
    a = [jnp.where(lvl == 0, _mm_nt(q, k), 0.0) for q, k in zip(qh, kh)]
    for li in range(1, n_levels + 1):
        damp = [jnp.exp(-sums[b][li * c:(li + 1) * c, sl(h)]) for b, h in units]
        part = [_mm_nt(q * d, k * d) for q, k, d in zip(qh, kh, damp)]
        a = [acc + jnp.where(lvl == li, p, 0.0) for acc, p in zip(a, part)]
    st = [s_scr[b, h] for b, h in units]
    o_inter = [_mm_nt(q_dec[b][:, sl(h)], t) for (b, h), t in zip(units, st)]
    o = [oi + _mm(aa, v) for oi, aa, v in zip(o_inter, a, vh)]
    upd = [_mm_tn(v, k_dec[b][:, sl(h)]) for (b, h), v in zip(units, vh)]
    for i, (b, h) in enumerate(units):
        s_scr[b, h] = jnp.exp(g_last[b][:, sl(h)]) * st[i] + upd[i]
    for b in range(nb):
        outs = [_gated_norm_rows(o[b * N_HEADS + h], xs[b][:, 768 + h * HEAD_DIM:768 + (h + 1) * HEAD_DIM],
                                 gain_ref[...]) for h in range(N_HEADS)]
        o_ref[b] = jnp.concatenate(outs, axis=-1)

    @pl.when(j == pl.num_programs(1) - 1)
    def _():
        st_ref[...] = s_scr[...]


def _hgrn_prompt(pc3, lb, gain):
    batch, seq, _ = pc3.shape
    c = CHUNK
    bb = BATCH_BLOCK if batch % BATCH_BLOCK == 0 else 1
    mat, level, n_levels = _hgrn_constants(c)
    o, st = pl.pallas_call(
        functools.partial(_hgrn_kernel, c=c, n_levels=n_levels),
        grid=(batch // bb, seq // c),
        in_specs=[pl.BlockSpec((bb, c, COLS_C), lambda b, j: (b, j, 0)),
                  pl.BlockSpec((1, W_MIX), lambda b, j: (0, 0)),
                  pl.BlockSpec((1, HEAD_DIM), lambda b, j: (0, 0)),
                  pl.BlockSpec(mat.shape, lambda b, j: (0, 0)),
                  pl.BlockSpec(level.shape, lambda b, j: (0, 0))],
        out_specs=[pl.BlockSpec((bb, c, W_MIX), lambda b, j: (b, j, 0)),
                   pl.BlockSpec((bb, N_HEADS, HEAD_DIM, HEAD_DIM), lambda b, j: (b, 0, 0, 0))],
        out_shape=[jax.ShapeDtypeStruct((batch, seq, W_MIX), F32),
                   jax.ShapeDtypeStruct((batch, N_HEADS, HEAD_DIM, HEAD_DIM), F32)],
        scratch_shapes=[pltpu.VMEM((bb, N_HEADS, HEAD_DIM, HEAD_DIM), F32)],
        compiler_params=_cparams("parallel", "arbitrary"),
        name="hgrn_prompt",
    )(pc3, lb, gain, jnp.asarray(mat), jnp.asarray(level))
    return o.reshape(batch * seq, W_MIX), jnp.swapaxes(st, -1, -2)


def _unit_lower_solve(lows, rhss, c):
    xs = [rhs - _mm_3pass(low, rhs) for low, rhs in zip(lows, rhss)]
    ps = lows
    span = 2
    while span < c:
        mm = _mm_3pass if span == 2 else _mm
        ps = [mm(p, p) for p in ps]
        xs = [x + mm(p, x) for p, x in zip(ps, xs)]
        span *= 2
    return xs


def _gdn_kernel(pa_ref, pba_ref, cw_ref, hp_ref, gain_ref, tri_ref, ones_ref, o_ref, st_ref,
                s_scr, buf_scr, *, c):
    j = pl.program_id(1)
    pad = 8
    nb = pa_ref.shape[0]

    @pl.when(j == 0)
    def _():
        s_scr[...] = jnp.zeros_like(s_scr)
        buf_scr[:, 0:pad, :] = jnp.zeros((nb, pad, 3 * W_MIX), F32)

    ones = ones_ref[...]
    row = lax.broadcasted_iota(jnp.int32, (c, c), 0)
    col = lax.broadcasted_iota(jnp.int32, (c, c), 1)
    qs, ks, vs, gates, betas, gcs, gcts = [], [], [], [], [], [], []
    for b in range(nb):
        x = pa_ref[b, :, 0:3 * W_MIX]
        gates.append(pa_ref[b, :, 3 * W_MIX:4 * W_MIX])
        buf_scr[b, pad:pad + c, :] = x
        y = cw_ref[CONV_A - 1:CONV_A, :] * x
        for tap in range(CONV_A - 1):
            back = CONV_A - 1 - tap
            y = y + cw_ref[tap:tap + 1, :] * buf_scr[b, pad - back:pad - back + c, :]
        buf_scr[b, 0:pad, :] = buf_scr[b, c:c + pad, :]
        y = _silu(y)
        qs.append(y[:, 0:W_MIX])
        ks.append(y[:, W_MIX:2 * W_MIX])
        vs.append(y[:, 2 * W_MIX:3 * W_MIX])
        ba = pba_ref[b]
        betas.append(_sigmoid(ba))
        g = hp_ref[0:1, :] * _softplus(ba + hp_ref[1:2, :])
        gcs.append(_mm_f32(tri_ref[0], g))
        gcts.append(lax.dot_general(g, tri_ref[1], (((0,), (0,)), ((), ())), precision=HIGHEST,
                                    preferred_element_type=F32))
    qs = [q * lax.rsqrt(_mm_f32(q * q, ones) + EPS) * SCALE for q in qs]
    ks = [k * lax.rsqrt(_mm_f32(k * k, ones) + EPS) for k in ks]
    units = [(b, h) for b in range(nb) for h in range(N_HEADS)]
    sl = lambda h: slice(h * HEAD_DIM, (h + 1) * HEAD_DIM)
    qh = [qs[b][:, sl(h)] for b, h in units]
    kh = [ks[b][:, sl(h)] for b, h in units]
    vh = [vs[b][:, sl(h)] for b, h in units]
    b_col = [betas[b][:, h:h + 1] for b, h in units]
    g_col = [gcs[b][:, 4 + h:5 + h] for b, h in units]
    g_row = [gcts[b][4 + h:5 + h, :] for b, h in units]
    decay = [jnp.where(row >= col, jnp.exp(jnp.minimum(gc_ - gr_, 0.0)), 0.0) for gc_, gr_ in zip(g_col, g_row)]
    kk = [_mm_nt(k, k) for k in kh]
    qk = [_mm_nt(q, k) for q, k in zip(qh, kh)]
    low = [jnp.where(row > col, bc * a * d, 0.0) for bc, a, d in zip(b_col, kk, decay)]
    eg = [jnp.exp(gc_) for gc_ in g_col]
    rhs = [jnp.concatenate([v * bc, k * (bc * e)], axis=-1) for v, k, bc, e in zip(vh, kh, b_col, eg)]
    sol = _unit_lower_solve(low, rhs, c)
    intra = [a * d for a, d in zip(qk, decay)]
    g_last = [gc_[c - 1:c, :] for gc_ in g_col]
    st = [s_scr[b, h] for b, h in units]
    u = [s[:, 0:HEAD_DIM] - _mm_nt(s[:, HEAD_DIM:2 * HEAD_DIM], t) for s, t in zip(sol, st)]
    o_inter = [_mm_nt(q * e, t) for q, e, t in zip(qh, eg, st)]
    o = [oi + _mm(a, uu) for oi, a, uu in zip(o_inter, intra, u)]
    k_dec = [k * jnp.exp(gl - gc_) for k, gl, gc_ in zip(kh, g_last, g_col)]
    upd = [_mm_tn(uu, kd) for uu, kd in zip(u, k_dec)]
    for i, (b, h) in enumerate(units):
        s_scr[b, h] = jnp.exp(g_last[i]) * st[i] + upd[i]
    for b in range(nb):
        outs = [_gated_norm_rows(o[b * N_HEADS + h], gates[b][:, sl(h)], gain_ref[...]) for h in range(N_HEADS)]
        o_ref[b] = jnp.concatenate(outs, axis=-1)

    @pl.when(j == pl.num_programs(1) - 1)
    def _():
        st_ref[...] = s_scr[...]


def _gdn_prompt(pa3, pba3, conv_w, head_params, gain):
    batch, seq, _ = pa3.shape
    c = CHUNK
    bb = BATCH_BLOCK if batch % BATCH_BLOCK == 0 else 1
    lower = np.tril(np.ones((c, c), np.float32))
    tri = np.stack([lower, lower.T], axis=0)
    ones = np.kron(np.eye(N_HEADS, dtype=np.float32), np.ones((HEAD_DIM, HEAD_DIM), np.float32))
    o, st = pl.pallas_call(
        functools.partial(_gdn_kernel, c=c),
        grid=(batch // bb, seq // c),
        in_specs=[pl.BlockSpec((bb, c, COLS_A), lambda b, j: (b, j, 0)),
                  pl.BlockSpec((bb, c, COLS_BA), lambda b, j: (b, j, 0)),
                  pl.BlockSpec((CONV_A, 3 * W_MIX), lambda b, j: (0, 0)),
                  pl.BlockSpec((2, COLS_BA), lambda b, j: (0, 0)),
                  pl.BlockSpec((1, HEAD_DIM), lambda b, j: (0, 0)),
                  pl.BlockSpec((2, c, c), lambda b, j: (0, 0, 0)),
                  pl.BlockSpec((W_MIX, W_MIX), lambda b, j: (0, 0))],
        out_specs=[pl.BlockSpec((bb, c, W_MIX), lambda b, j: (b, j, 0)),
                   pl.BlockSpec((bb, N_HEADS, HEAD_DIM, HEAD_DIM), lambda b, j: (b, 0, 0, 0))],
        out_shape=[jax.ShapeDtypeStruct((batch, seq, W_MIX), F32),
                   jax.ShapeDtypeStruct((batch, N_HEADS, HEAD_DIM, HEAD_DIM), F32)],
        scratch_shapes=[pltpu.VMEM((bb, N_HEADS, HEAD_DIM, HEAD_DIM), F32),
                        pltpu.VMEM((bb, c + 8, 3 * W_MIX), F32)],
        compiler_params=_cparams("parallel", "arbitrary"),
        name="gdn_prompt",
    )(pa3, pba3, conv_w, head_params, gain, jnp.asarray(tri), jnp.asarray(ones))
    return o.reshape(batch * seq, W_MIX), jnp.swapaxes(st, -1, -2)


def _gdn_head_params(a_log, dt_bias):
    neg_a = jnp.pad(-jnp.exp(a_log.astype(F32)), (4, COLS_BA - 8))
    dtb = jnp.pad(dt_bias.astype(F32), (4, COLS_BA - 8))
    return jnp.stack([neg_a, dtb], axis=0)


def _outproj_kernel(*refs, n_branch):
    x_ref, oa_ref, ob_ref, oc_ref = refs[0:4]
    tm = x_ref.shape[0]
    if n_branch == 1:
        od = refs[4][...]
        w_ref, o_ref = refs[5], refs[6]
    else:
        d_refs = refs[4:4 + n_branch]
        l_refs = refs[4 + n_branch:4 + 2 * n_branch]
        w_ref, o_ref = refs[4 + 2 * n_branch], refs[5 + 2 * n_branch]
        scratch = refs[6 + 2 * n_branch:]

        def token_order(ref, scr):
            r, rows = ref.shape[1], ref.shape[2]
            if r == 1:
                return ref[0, 0]
            n_chunks = scr.shape[0]
            for rho in range(r):
                for ch in range(n_chunks):
                    scr[ch, pl.ds(rho, rows, stride=r), :] = ref[0, rho, :, ch * 128:(ch + 1) * 128]
            return jnp.concatenate([scr[ch] for ch in range(n_chunks)], axis=-1)

        outs = [token_order(ref, scratch[2 * i]) for i, ref in enumerate(d_refs)]
        lses = [token_order(ref, scratch[2 * i + 1]) for i, ref in enumerate(l_refs)]
        m = functools.reduce(jnp.maximum, lses)
        es = [jnp.exp(l - m) for l in lses]
        od = sum(e * o for e, o in zip(es, outs)) / sum(es)
    acc = x_ref[...]
    for i, part in enumerate((oa_ref[...], ob_ref[...], oc_ref[...], od)):
        acc = acc + _mm(part, w_ref[i * W_MIX:(i + 1) * W_MIX, :])
    o_ref[...] = acc


def _outproj(x2, oa, ob, oc, ods, lses, w_bf, tm):
    n = x2.shape[0]
    n_branch = len(ods)
    row = lambda w: pl.BlockSpec((tm, w), lambda i: (i, 0))
    specs = [row(D_MODEL)] + [row(W_MIX)] * 3
    scratch = []
    if n_branch == 1:
        specs.append(row(W_MIX))
    else:
        per_seq = (ods[0].shape[1] * ods[0].shape[2]) // tm
        for t in (*ods, *lses):
            r = t.shape[1]
            specs.append(pl.BlockSpec((1, r, tm // r, W_MIX), lambda i: (i // per_seq, 0, i % per_seq, 0)))
        scratch = [pltpu.VMEM((W_MIX // 128, tm, 128), F32)] * (2 * n_branch)
    specs.append(pl.BlockSpec((4 * W_MIX, D_MODEL), lambda i: (0, 0)))
    return pl.pallas_call(
        functools.partial(_outproj_kernel, n_branch=n_branch),
        grid=(n // tm,),
        in_specs=specs,
        out_specs=row(D_MODEL),
        out_shape=jax.ShapeDtypeStruct((n, D_MODEL), F32),
        scratch_shapes=scratch,
        compiler_params=_cparams("parallel"),
        name="outproj",
    )(x2, oa, ob, oc, *ods, *lses, w_bf)


def _ffn_kernel(x_ref, g_ref, wg_ref, wu_ref, wd_ref, o_ref, h_scr, acc_scr):
    f = pl.program_id(1)

    @pl.when(f == 0)
    def _():
        h_scr[...] = _rms_rows(x_ref[...], g_ref[...]).astype(BF16)
        acc_scr[...] = jnp.zeros_like(acc_scr)

    h = h_scr[...]
    gate = jnp.dot(h, wg_ref[...], preferred_element_type=F32)
    up = jnp.dot(h, wu_ref[...], preferred_element_type=F32)
    acc_scr[...] += _mm(_silu(gate) * up, wd_ref[...])

    @pl.when(f == pl.num_programs(1) - 1)
    def _():
        o_ref[...] = x_ref[...] + acc_scr[...]


def _ffn(x2, gain, w_gu_bf, w_down_bf, tm, tf=256):
    n = x2.shape[0]
    nf = D_FF // tf
    return pl.pallas_call(
        _ffn_kernel,
        grid=(n // tm, nf),
        in_specs=[pl.BlockSpec((tm, D_MODEL), lambda i, f: (i, 0)),
                  pl.BlockSpec((1, D_MODEL), lambda i, f: (0, 0)),
                  pl.BlockSpec((D_MODEL, tf), lambda i, f: (0, f)),
                  pl.BlockSpec((D_MODEL, tf), lambda i, f: (0, nf + f)),
                  pl.BlockSpec((tf, D_MODEL), lambda i, f: (f, 0))],
        out_specs=pl.BlockSpec((tm, D_MODEL), lambda i, f: (i, 0)),
        out_shape=jax.ShapeDtypeStruct((n, D_MODEL), F32),
        scratch_shapes=[pltpu.VMEM((tm, D_MODEL), BF16), pltpu.VMEM((tm, D_MODEL), F32)],
        compiler_params=_cparams("parallel", "arbitrary"),
        name="ffn",
    )(x2, gain, w_gu_bf, w_gu_bf, w_down_bf)


ROW_TILE = (8, 128)


def _router_kernel(x_ref, g_ref, wr_ref, br_ref, h_ref, route_ref):
    i = pl.program_id(0)
    last = pl.num_programs(0) - 1
    tm = x_ref.shape[0]

    @pl.when(i < last)
    def _():
        h = _rms_rows(x_ref[...], g_ref[...])
        h_ref[...] = h.reshape(tm, *ROW_TILE)
        lane = lax.broadcasted_iota(jnp.int32, (tm, 128), 1).astype(F32)
        logits = _mm_f32(h, wr_ref[...]) + br_ref[...]
        m1 = jnp.max(logits, axis=-1, keepdims=True)
        i1 = jnp.min(jnp.where(logits == m1, lane, 128.0), axis=-1, keepdims=True)
        rest = jnp.where(lane == i1, NEG, logits)
        m2 = jnp.max(rest, axis=-1, keepdims=True)
        i2 = jnp.min(jnp.where(rest == m2, lane, 128.0), axis=-1, keepdims=True)
        e2 = jnp.exp(m2 - m1)
        route_ref[...] = (jnp.where(lane == 0.0, i1, 0.0) + jnp.where(lane == 1.0, i2, 0.0)
                          + jnp.where(lane == 2.0, 1.0 / (1.0 + e2), 0.0)
                          + jnp.where(lane == 3.0, e2 / (1.0 + e2), 0.0))

    @pl.when(i == last)
    def _():
        h_ref[...] = jnp.zeros(h_ref.shape, F32)
        route_ref[...] = jnp.zeros(route_ref.shape, F32)


def _gather_rows_kernel(tok_ref, h_hbm, o_ref, sem):
    tg = o_ref.shape[0]
    base = pl.program_id(0) * tg

    def issue(r, carry):
        pltpu.make_async_copy(h_hbm.at[tok_ref[base + r]], o_ref.at[r], sem).start()
        return carry

    lax.fori_loop(0, tg, issue, 0)
    pltpu.make_async_copy(h_hbm.at[pl.ds(0, tg)], o_ref, sem).wait()


def _expert_kernel(be_ref, x_ref, wg_ref, wu_ref, wd_ref, o_ref, xb_scr, acc_scr):
    f = pl.program_id(1)
    tm = x_ref.shape[0]

    @pl.when(f == 0)
    def _():
        xb_scr[...] = x_ref[...].reshape(tm, D_MODEL).astype(BF16)
        acc_scr[...] = jnp.zeros_like(acc_scr)

    xb = xb_scr[...]
    gate = jnp.dot(xb, wg_ref[0], preferred_element_type=F32)
    up = jnp.dot(xb, wu_ref[0], preferred_element_type=F32)
    acc_scr[...] += _mm(_silu(gate) * up, wd_ref[0])

    @pl.when(f == pl.num_programs(1) - 1)
    def _():
        o_ref[...] = acc_scr[...].reshape(tm, *ROW_TILE)


def _combine_kernel(pos_ref, x_ref, route_ref, y_hbm, o_ref, first_scr, second_scr, sems):
    tc = x_ref.shape[0]
    base = pl.program_id(0) * (2 * tc)

    def issue(r, carry):
        pltpu.make_async_copy(y_hbm.at[pos_ref[base + 2 * r]], first_scr.at[r], sems.at[0]).start()
        pltpu.make_async_copy(y_hbm.at[pos_ref[base + 2 * r + 1]], second_scr.at[r], sems.at[1]).start()
        return carry

    lax.fori_loop(0, tc, issue, 0)
    pltpu.make_async_copy(y_hbm.at[pl.ds(0, tc)], first_scr, sems.at[0]).wait()
    pltpu.make_async_copy(y_hbm.at[pl.ds(0, tc)], second_scr, sems.at[1]).wait()
    route = route_ref[...]
    o_ref[...] = (x_ref[...] + route[:, 2:3] * first_scr[...].reshape(tc, D_MODEL)
                  + route[:, 3:4] * second_scr[...].reshape(tc, D_MODEL))


def _moe_routing(route, n, tm_rows):
    e_flat = route[:n, 0:TOP_K].astype(jnp.int32).reshape(-1)
    n_assign = n * TOP_K
    onehot = (e_flat[:, None] == jnp.arange(N_EXPERTS, dtype=jnp.int32)[None, :]).astype(jnp.int32)
    csum = jnp.cumsum(onehot, axis=0)
    rank = jnp.sum(csum * onehot, axis=1) - 1
    counts = csum[-1]
    padded = (counts + tm_rows - 1) // tm_rows * tm_rows
    pad_end = jnp.cumsum(padded)
    pad_start = pad_end - padded
    start = jnp.cumsum(counts) - counts
    dest = pad_start[e_flat] + rank
    n_blocks = -(-(n_assign + N_EXPERTS * (tm_rows - 1)) // tm_rows)
    blk_e = jnp.minimum(jnp.sum(jnp.arange(n_blocks, dtype=jnp.int32)[:, None] * tm_rows >= pad_end[None, :], axis=1),
                        N_EXPERTS - 1).astype(jnp.int32)
    order = jnp.argsort(e_flat, stable=True).astype(jnp.int32)
    rows = jnp.arange(n_blocks * tm_rows, dtype=jnp.int32)
    row_e = jnp.repeat(blk_e, tm_rows)
    offset = rows - pad_start[row_e]
    valid = (offset < counts[row_e]) & (rows < pad_end[N_EXPERTS - 1])
    src = jnp.clip(start[row_e] + offset, 0, n_assign - 1)
    row_tok = jnp.where(valid, order[src] // TOP_K, n).astype(jnp.int32)
    return row_tok, dest.astype(jnp.int32), blk_e, n_blocks


def _moe(x2, gain, w_router, b_router, w_gu_bf, w_down_bf, tm):
    n = x2.shape[0]
    rows = min(MOE_ROWS, max(128, n * TOP_K // N_EXPERTS))
    wr = jnp.pad(w_router.astype(F32), ((0, 0), (0, 128 - N_EXPERTS)))
    br = jnp.pad(b_router.astype(F32), (0, 128 - N_EXPERTS), constant_values=NEG).reshape(1, 128)
    nt = n // tm
    h3, route = pl.pallas_call(
        _router_kernel,
        grid=(nt + 1,),
        in_specs=[pl.BlockSpec((tm, D_MODEL), lambda i: (jnp.minimum(i, nt - 1), 0)),
                  pl.BlockSpec((1, D_MODEL), lambda i: (0, 0)),
                  pl.BlockSpec((D_MODEL, 128), lambda i: (0, 0)),
                  pl.BlockSpec((1, 128), lambda i: (0, 0))],
        out_specs=[pl.BlockSpec((tm, *ROW_TILE), lambda i: (i, 0, 0)),
                   pl.BlockSpec((tm, 128), lambda i: (i, 0))],
        out_shape=[jax.ShapeDtypeStruct((n + tm, *ROW_TILE), F32),
                   jax.ShapeDtypeStruct((n + tm, 128), F32)],
        compiler_params=_cparams("arbitrary"),
        name="moe_router",
    )(x2, gain, wr, br)

    row_tok, dest, blk_e, n_blocks = _moe_routing(route, n, rows)
    n_rows = n_blocks * rows

    x_rows = pl.pallas_call(
        _gather_rows_kernel,
        grid_spec=pltpu.PrefetchScalarGridSpec(
            num_scalar_prefetch=1,
            grid=(n_blocks,),
            in_specs=[pl.BlockSpec(memory_space=pl.ANY)],
            out_specs=pl.BlockSpec((rows, *ROW_TILE), lambda i, tok: (i, 0, 0)),
            scratch_shapes=[pltpu.SemaphoreType.DMA(())]),
        out_shape=jax.ShapeDtypeStruct((n_rows, *ROW_TILE), F32),
        compiler_params=_cparams("arbitrary"),
        name="moe_gather",
    )(row_tok, h3)

    nf = D_FF_E // MOE_FF_TILE
    y_rows = pl.pallas_call(
        _expert_kernel,
        grid_spec=pltpu.PrefetchScalarGridSpec(
            num_scalar_prefetch=1,
            grid=(n_blocks, nf),
            in_specs=[pl.BlockSpec((rows, *ROW_TILE), lambda i, f, be: (i, 0, 0)),
                      pl.BlockSpec((1, D_MODEL, MOE_FF_TILE), lambda i, f, be: (be[i], 0, f)),
                      pl.BlockSpec((1, D_MODEL, MOE_FF_TILE), lambda i, f, be: (be[i], 0, nf + f)),
                      pl.BlockSpec((1, MOE_FF_TILE, D_MODEL), lambda i, f, be: (be[i], f, 0))],
            out_specs=pl.BlockSpec((rows, *ROW_TILE), lambda i, f, be: (i, 0, 0)),
            scratch_shapes=[pltpu.VMEM((rows, D_MODEL), BF16), pltpu.VMEM((rows, D_MODEL), F32)]),
        out_shape=jax.ShapeDtypeStruct((n_rows, *ROW_TILE), F32),
        compiler_params=_cparams("arbitrary", "arbitrary"),
        name="moe_experts",
    )(blk_e, x_rows, w_gu_bf, w_gu_bf, w_down_bf)

    return pl.pallas_call(
        _combine_kernel,
        grid_spec=pltpu.PrefetchScalarGridSpec(
            num_scalar_prefetch=1,
            grid=(nt,),
            in_specs=[pl.BlockSpec((tm, D_MODEL), lambda i, pos: (i, 0)),
                      pl.BlockSpec((tm, 128), lambda i, pos: (i, 0)),
                      pl.BlockSpec(memory_space=pl.ANY)],
            out_specs=pl.BlockSpec((tm, D_MODEL), lambda i, pos: (i, 0)),
            scratch_shapes=[pltpu.VMEM((tm, *ROW_TILE), F32), pltpu.VMEM((tm, *ROW_TILE), F32),
                            pltpu.SemaphoreType.DMA((2,))]),
        out_shape=jax.ShapeDtypeStruct((n, D_MODEL), F32),
        compiler_params=_cparams("arbitrary"),
        name="moe_combine",
    )(dest, x2, route, y_rows)


def _norm_kernel(x_ref, g_ref, o_ref):
    o_ref[...] = _rms_rows(x_ref[...], g_ref[...])


def _final_norm(x2, gain, tm):
    n = x2.shape[0]
    return pl.pallas_call(
        _norm_kernel,
        grid=(n // tm,),
        in_specs=[pl.BlockSpec((tm, D_MODEL), lambda i: (i, 0)),
                  pl.BlockSpec((1, D_MODEL), lambda i: (0, 0))],
        out_specs=pl.BlockSpec((tm, D_MODEL), lambda i: (i, 0)),
        out_shape=jax.ShapeDtypeStruct((n, D_MODEL), F32),
        compiler_params=_cparams("parallel"),
        name="final_norm",
    )(x2, gain)


ROWS_T = 8
QROWS = N_HEADS * ROWS_T


def _softmax_two_parts(sc, sn, vc, vn, valid_c, valid_n, sink):
    sc = jnp.where(valid_c, sc, NEG)
    sn = jnp.where(valid_n, sn, NEG)
    m = jnp.maximum(jnp.max(sc, axis=-1, keepdims=True), jnp.max(sn, axis=-1, keepdims=True))
    if sink is not None:
        m = jnp.maximum(m, sink)
    pc = jnp.exp(sc - m)
    pn = jnp.exp(sn - m)
    l = jnp.sum(pc, axis=-1, keepdims=True) + jnp.sum(pn, axis=-1, keepdims=True)
    if sink is not None:
        l = l + jnp.exp(sink - m)
    return (_mm(pc, vc) + _mm(pn, vn)) / l, m + jnp.log(l)


def _sattn_kernel(qd_ref, kdn_ref, vdn_ref, kdc_ref, vdc_ref, qb_ref, kbn_ref, vbn_ref, kbc_ref, vbc_ref,
                  sink_ref, od_ref, ob_ref, *, bb, t_new):
    def dist_maps(n_cache):
        row = lax.broadcasted_iota(jnp.int32, (QROWS, n_cache), 0)
        col = lax.broadcasted_iota(jnp.int32, (QROWS, n_cache), 1)
        d_cache = n_cache + (row & (ROWS_T - 1)) - col
        rown = lax.broadcasted_iota(jnp.int32, (QROWS, 128), 0)
        coln = lax.broadcasted_iota(jnp.int32, (QROWS, 128), 1)
        d_new = (rown & (ROWS_T - 1)) - coln
        return d_cache, d_new, (coln < t_new) & (d_new >= 0)

    dc_d, dn_d, ok_new_d = dist_maps(CACHE_D)
    dc_b, dn_b, ok_new_b = dist_maps(CACHE_B)
    hrow = lax.broadcasted_iota(jnp.int32, (QROWS, W_MIX), 0) >> int(math.log2(ROWS_T))
    hlane = lax.broadcasted_iota(jnp.int32, (QROWS, W_MIX), 1) >> int(math.log2(HEAD_DIM))
    own = hrow == hlane
    zpad_d = jnp.zeros((128 - ROWS_T, W_MIX), F32)
    zpad_b = jnp.zeros((128 - ROWS_T, KV_B * HEAD_DIM), F32)
    zq = jnp.zeros((ROWS_T, HEAD_DIM), F32)
    for i in range(bb):
        q = qd_ref[i] * SCALE
        qblk = jnp.where(own, jnp.concatenate([q] * N_HEADS, axis=0), 0.0)
        kc, vc = kdc_ref[i], vdc_ref[i]
        kn = jnp.concatenate([kdn_ref[i], zpad_d], axis=0)
        vn = jnp.concatenate([vdn_ref[i], zpad_d], axis=0)
        sc = _mm_nt(qblk, kc)
        sn = _mm_nt(qblk, kn)
        outs, lses = [], []
        for r in DILATIONS:
            ok_c = (dc_d <= WIN * r) & ((dc_d & (r - 1)) == 0)
            ok_n = ok_new_d & ((dn_d & (r - 1)) == 0)
            o, lse = _softmax_two_parts(sc, sn, vc, vn, ok_c, ok_n, None)
            outs.append(o)
            lses.append(lse)
        m = functools.reduce(jnp.maximum, lses)
        es = [jnp.exp(l - m) for l in lses]
        o = sum(e * t for e, t in zip(es, outs)) / sum(es)
        o = jnp.where(own, o, 0.0)
        od_ref[i] = sum(o[h * ROWS_T:(h + 1) * ROWS_T] for h in range(N_HEADS))
        qb = qb_ref[i] * SCALE
        blocks = []
        for h in range(N_HEADS):
            qh = qb[:, h * HEAD_DIM:(h + 1) * HEAD_DIM]
            blocks.append(jnp.concatenate([qh, zq] if h < N_HEADS // KV_B else [zq, qh], axis=-1))
        qblk = jnp.concatenate(blocks, axis=0)
        kn = jnp.concatenate([kbn_ref[i], zpad_b], axis=0)
        vn = jnp.concatenate([vbn_ref[i], zpad_b], axis=0)
        o, _ = _softmax_two_parts(_mm_nt(qblk, kbc_ref[i]), _mm_nt(qblk, kn), vbc_ref[i], vn,
                                  dc_b <= WIN, ok_new_b, sink_ref[...])
        pieces = []
        for h in range(N_HEADS):
            g = h // (N_HEADS // KV_B)
            pieces.append(o[h * ROWS_T:(h + 1) * ROWS_T, g * HEAD_DIM:(g + 1) * HEAD_DIM])
        ob_ref[i] = jnp.concatenate(pieces, axis=-1)


def _pad_rows(t, rows):
    return jnp.pad(t, ((0, 0), (0, rows - t.shape[1]), (0, 0)))


def _sample_attention(pb_s, pd_s, cbk, cbv, cdk, cdv, sinks, bb=2):
    bs, t_new, _ = pb_s.shape
    pb8, pd8 = _pad_rows(pb_s, ROWS_T), _pad_rows(pd_s, ROWS_T)
    sink_col = jnp.repeat(sinks.astype(F32), ROWS_T).reshape(QROWS, 1)
    wb = KV_B * HEAD_DIM
    blk = lambda rows, w, c: pl.BlockSpec((bb, rows, w), lambda i: (i, 0, c))
    od, ob = pl.pallas_call(
        functools.partial(_sattn_kernel, bb=bb, t_new=t_new),
        grid=(bs // bb,),
        in_specs=[blk(ROWS_T, W_MIX, 0), blk(ROWS_T, W_MIX, 1), blk(ROWS_T, W_MIX, 2),
                  blk(CACHE_D, W_MIX, 0), blk(CACHE_D, W_MIX, 0),
                  blk(ROWS_T, W_MIX, 0), blk(ROWS_T, wb, 2), blk(ROWS_T, wb, 3),
                  blk(CACHE_B, wb, 0), blk(CACHE_B, wb, 0),
                  pl.BlockSpec((QROWS, 1), lambda i: (0, 0))],
        out_specs=[blk(ROWS_T, W_MIX, 0), blk(ROWS_T, W_MIX, 0)],
        out_shape=[jax.ShapeDtypeStruct((bs, ROWS_T, W_MIX), F32)] * 2,
        compiler_params=_cparams("parallel"),
        name="sample_attn",
    )(pd8, pd8, pd8, cdk, cdv, pb8, pb8, pb8, cbk, cbv, sink_col)
    return (ob[:, :t_new].reshape(bs * t_new, W_MIX), od[:, :t_new].reshape(bs * t_new, W_MIX))


def _srec_kernel(xq_ref, xk_ref, xv_ref, bq_ref, bk_ref, bv_ref, cwq_ref, cwk_ref, cwv_ref, ga_ref,
                 ba_ref, hp_ref, gna_ref, sa_ref, qc_ref, fc_ref, ic_ref, gcg_ref, lb_ref, gnc_ref, sc_ref,
                 oa_ref, sa_out, oc_ref, sc_out, q_scr, k_scr, d_scr, *, t_new):
    nb = sa_ref.shape[-1]
    zero = jnp.zeros((HEAD_DIM, nb), F32)

    def conv(x_ref, b_ref, cw_ref, t):
        y = None
        for tap in range(CONV_A):
            pos = t + tap
            src = b_ref[pos] if pos < CONV_A - 1 else x_ref[pos - (CONV_A - 1)]
            term = cw_ref[tap] * src
            y = term if y is None else y + term
        return _silu(y)

    def l2(x):
        return x * lax.rsqrt(jnp.sum(x * x, axis=0, keepdims=True) + EPS)

    def gated_norm(o, gate, gain):
        return o * lax.rsqrt(jnp.mean(o * o, axis=0, keepdims=True) + EPS) * gain * _silu(gate)

    sa_out[0] = sa_ref[0]
    for t in range(t_new):
        q_scr[...] = l2(conv(xq_ref, bq_ref, cwq_ref, t)) * SCALE
        k_scr[...] = l2(conv(xk_ref, bk_ref, cwk_ref, t))
        v = conv(xv_ref, bv_ref, cwv_ref, t)
        beta = _sigmoid(ba_ref[0, t:t + 1, :])
        dec = jnp.exp(hp_ref[0, 0:1, :] * _softplus(ba_ref[0, t_new + t:t_new + t + 1, :] + hp_ref[0, 1:2, :]))

        def decay_and_read(kk, acc):
            s = sa_out[0, kk] * dec
            sa_out[0, kk] = s
            return acc + k_scr[pl.ds(kk, 1), :] * s

        err = (v - lax.fori_loop(0, HEAD_DIM, decay_and_read, zero)) * beta

        def write_and_query(kk, acc):
            s = sa_out[0, kk] + k_scr[pl.ds(kk, 1), :] * err
            sa_out[0, kk] = s
            return acc + q_scr[pl.ds(kk, 1), :] * s

        o = lax.fori_loop(0, HEAD_DIM, write_and_query, zero)
        oa_ref[t] = gated_norm(o, ga_ref[t], gna_ref[...])

    sc_out[0] = sc_ref[0]
    for t in range(t_new):
        lb = lb_ref[...]
        f = lb + (1.0 - lb) * _sigmoid(fc_ref[t])
        q_scr[...] = qc_ref[t]
        k_scr[...] = 1.0 - f
        d_scr[...] = jnp.exp(jnp.log(f))
        v = ic_ref[t]

        def update(kk, acc):
            s = sc_out[0, kk] * d_scr[pl.ds(kk, 1), :] + k_scr[pl.ds(kk, 1), :] * v
            sc_out[0, kk] = s
            return acc + q_scr[pl.ds(kk, 1), :] * s

        o = lax.fori_loop(0, HEAD_DIM, update, zero)
        oc_ref[t] = gated_norm(o, gcg_ref[t], gnc_ref[...])


def _sample_recurrences(pa_s, pba_s, pc_s, conv_buf, s_a, s_c, conv_w, a_log, dt_bias, norm_a, lb, norm_c):
    bs, t_new, _ = pa_s.shape
    lanes_last = lambda t: jnp.transpose(t, (1, 2, 0))
    pa_t = lanes_last(pa_s)
    pc_t = lanes_last(pc_s)
    buf_t = lanes_last(conv_buf.astype(F32))
    ba = jnp.transpose(pba_s[:, :, 0:8], (2, 1, 0))
    ba = jnp.concatenate([ba[0:N_HEADS], ba[N_HEADS:2 * N_HEADS]], axis=1)
    hp = jnp.stack([-jnp.exp(a_log.astype(F32)), dt_bias.astype(F32)], axis=1)
    hp = jnp.broadcast_to(hp[:, :, None], (N_HEADS, 2, bs))
    cw = conv_w.astype(F32)[:, :, None]
    sa_t = jnp.transpose(s_a.astype(F32), (1, 2, 3, 0))
    sc_t = jnp.transpose(s_c.astype(F32), (1, 2, 3, 0))
    col = lambda v: v.astype(F32).reshape(-1, 1)
    hd = HEAD_DIM
    feat = lambda rows, off: pl.BlockSpec((rows, hd, bs), lambda h: (0, off + h, 0))
    cwspec = lambda off: pl.BlockSpec((CONV_A, hd, 1), lambda h: (0, off + h, 0))
    per_head = lambda rows: pl.BlockSpec((1, rows, bs), lambda h: (h, 0, 0))
    state = pl.BlockSpec((1, hd, hd, bs), lambda h: (h, 0, 0, 0))
    vec = pl.BlockSpec((hd, 1), lambda h: (0, 0))
    nh = N_HEADS
    oa, sa_n, oc, sc_n = pl.pallas_call(
        functools.partial(_srec_kernel, t_new=t_new),
        grid=(N_HEADS,),
        in_specs=[feat(t_new, 0), feat(t_new, nh), feat(t_new, 2 * nh),
                  feat(CONV_A - 1, 0), feat(CONV_A - 1, nh), feat(CONV_A - 1, 2 * nh),
                  cwspec(0), cwspec(nh), cwspec(2 * nh),
                  feat(t_new, 3 * nh), per_head(2 * t_new), per_head(2), vec, state,
                  feat(t_new, 0), feat(t_new, nh), feat(t_new, 2 * nh), feat(t_new, 3 * nh),
                  pl.BlockSpec((hd, 1), lambda h: (h, 0)), vec, state],
        out_specs=[feat(t_new, 0), state, feat(t_new, 0), state],
        out_shape=[jax.ShapeDtypeStruct((t_new, W_MIX, bs), F32),
                   jax.ShapeDtypeStruct((N_HEADS, hd, hd, bs), F32),
                   jax.ShapeDtypeStruct((t_new, W_MIX, bs), F32),
                   jax.ShapeDtypeStruct((N_HEADS, hd, hd, bs), F32)],
        scratch_shapes=[pltpu.VMEM((hd, bs), F32)] * 3,
        compiler_params=_cparams("parallel"),
        name="sample_recurrences",
    )(pa_t, pa_t, pa_t, buf_t, buf_t, buf_t, cw, cw, cw, pa_t, ba, hp, col(norm_a), sa_t,
      pc_t, pc_t, pc_t, pc_t, col(lb), col(norm_c), sc_t)
    rows_first = lambda t: jnp.transpose(t, (2, 0, 1)).reshape(bs * t_new, W_MIX)
    back = lambda t: jnp.transpose(t, (3, 0, 1, 2))
    return rows_first(oa), back(sa_n), rows_first(oc), back(sc_n)


def _heads(t, n):
    return t.reshape(t.shape[0], t.shape[1], n, HEAD_DIM)


def _prompt_mixers(pa, pb, pc, pd, pba, pd_strided, batch, seq, conv_w, head_params, norm_a, sinks, lb, norm_c):
    pa3 = pa.reshape(batch, seq, COLS_A)
    pb3 = pb.reshape(batch, seq, COLS_B)
    pc3 = pc.reshape(batch, seq, COLS_C)
    pd3 = pd.reshape(batch, seq, COLS_D)
    oa, s_a = _gdn_prompt(pa3, pba.reshape(batch, seq, COLS_BA), conv_w, head_params, norm_a)
    oc, s_c = _hgrn_prompt(pc3, lb, norm_c)
    sink_row = jnp.pad(sinks.astype(F32), (0, 128 - N_HEADS)).reshape(1, 128)
    (ob,) = _band_attention(pb3[:, None], 0, 256, 384, KV_B, sink_row, False)
    ods, lses = [], []
    for p4 in (pd3[:, None], *pd_strided):
        o, lse = _band_attention(p4, 0, 256, 512, N_HEADS, None, True)
        ods.append(o)
        lses.append(lse)
    nb, nd = min(CACHE_B, seq), min(CACHE_D, seq)
    state = (_heads(pb3[:, seq - nb:, 256:384], KV_B), _heads(pb3[:, seq - nb:, 384:512], KV_B),
             _heads(pd3[:, seq - nd:, 256:512], N_HEADS), _heads(pd3[:, seq - nd:, 512:768], N_HEADS),
             pa3[:, seq - (CONV_A - 1):, 0:3 * W_MIX], s_a, s_c)
    return oa, ob.reshape(batch * seq, W_MIX), oc, ods, lses, state


def _sample_mixers(pa, pb, pc, pd, pba, bs, t_new, caches, conv_w, a_log, dt_bias, norm_a, sinks, lb, norm_c):
    cbk, cbv, cdk, cdv, conv_buf, s_a, s_c = caches
    pa3 = pa.reshape(bs, t_new, COLS_A)
    pb3 = pb.reshape(bs, t_new, COLS_B)
    pc3 = pc.reshape(bs, t_new, COLS_C)
    pd3 = pd.reshape(bs, t_new, COLS_D)
    flat = lambda c: c.reshape(c.shape[0], c.shape[1], -1)
    ob, od = _sample_attention(pb3, pd3, flat(cbk), flat(cbv), flat(cdk), flat(cdv), sinks)
    oa, s_a_new, oc, s_c_new = _sample_recurrences(
        pa3, pba.reshape(bs, t_new, COLS_BA), pc3, conv_buf, s_a, s_c, conv_w, a_log, dt_bias, norm_a, lb, norm_c)
    roll = lambda cache, new, n: jnp.concatenate([cache[:, t_new:], _heads(new, n).astype(cache.dtype)], axis=1)
    conv_all = jnp.concatenate([conv_buf.astype(F32), pa3[:, :, 0:3 * W_MIX]], axis=1)
    state = (roll(cbk, pb3[:, :, 256:384], KV_B), roll(cbv, pb3[:, :, 384:512], KV_B),
             roll(cdk, pd3[:, :, 256:512], N_HEADS), roll(cdv, pd3[:, :, 512:768], N_HEADS),
             conv_all[:, -(CONV_A - 1):].astype(conv_buf.dtype), s_a_new.astype(s_a.dtype),
             s_c_new.astype(s_c.dtype))
    return oa, ob, oc, od, state


def kernel(x_prompt, x_sample, cache_b_k, cache_b_v, cache_d_k, cache_d_v, state_a_conv, state_a_s, state_c_s, norm_mix, w_in, conv_a, a_log, dt_bias, norm_a, sinks_b, lb_logits, norm_c, w_out, norm_ffn, w_ffn_gu, w_ffn_down, w_router, b_router, w_moe_gu, w_moe_down, norm_final):
    depth = w_in.shape[0]
    batch, seq, _ = x_prompt.shape
    bs, t_new, _ = x_sample.shape
    lb_p = jax.nn.softmax(lb_logits.astype(F32), axis=0)
    lower_bounds = jnp.cumsum(lb_p, axis=0) - lb_p[0]
    xp = x_prompt.reshape(batch * seq, D_MODEL)
    xs = x_sample.reshape(bs * t_new, D_MODEL)
    tm_p, tm_s = 512, 256
    row = lambda v: v.astype(F32).reshape(1, -1)
    prompt_states, sample_states = [], []
    for l in range(depth):
        w_in_l = _permute_w_in(w_in[l])
        w_out_l = w_out[l].astype(BF16)
        conv_w = conv_a[l].astype(F32)
        head_params = _gdn_head_params(a_log[l], dt_bias[l])
        lb = lower_bounds[l]
        mix = (row(norm_a[l]), sinks_b[l], row(lb), row(norm_c[l]))

        projs = _inproj(xp, row(norm_mix[l]), w_in_l, tm_p, strided_for=(batch, seq))
        oa, ob, oc, ods, lses, st_p = _prompt_mixers(*projs[0:5], projs[5:], batch, seq, conv_w, head_params, *mix)
        xp = _outproj(xp, oa, ob, oc, ods, lses, w_out_l, tm_p)
        prompt_states.append(st_p)

        projs = _inproj(xs, row(norm_mix[l]), w_in_l, tm_s)
        caches = (cache_b_k[l], cache_b_v[l], cache_d_k[l], cache_d_v[l], state_a_conv[l], state_a_s[l],
                  state_c_s[l])
        oa, ob, oc, od, st_s = _sample_mixers(*projs, bs, t_new, caches, conv_w, a_log[l], dt_bias[l],
                                              norm_a[l], sinks_b[l], lb, norm_c[l])
        xs = _outproj(xs, oa, ob, oc, [od], [], w_out_l, tm_s)
        sample_states.append(st_s)

        if l % 2 == 0:
            w_gu = w_ffn_gu[l // 2].astype(BF16)
            w_dn = w_ffn_down[l // 2].astype(BF16)
            xp = _ffn(xp, row(norm_ffn[l]), w_gu, w_dn, 1024)
            xs = _ffn(xs, row(norm_ffn[l]), w_gu, w_dn, 512)
        else:
            w_gu = w_moe_gu[l // 2].astype(BF16)
            w_dn = w_moe_down[l // 2].astype(BF16)
            xp = _moe(xp, row(norm_ffn[l]), w_router[l // 2], b_router[l // 2], w_gu, w_dn, 512)
            xs = _moe(xs, row(norm_ffn[l]), w_router[l // 2], b_router[l // 2], w_gu, w_dn, 512)
    y_prompt = _final_norm(xp, row(norm_final), 1024).reshape(batch, seq, D_MODEL)
    y_sample = _final_norm(xs, row(norm_final), 512).reshape(bs, t_new, D_MODEL)
    stack = lambda states: [jnp.stack(t, 0) for t in zip(*states)]
    return (y_prompt, y_sample, *stack(prompt_states), *stack(sample_states))
```

```python
import functools
import math

import numpy as np
import jax
import jax.numpy as jnp
from jax import lax
from jax.experimental import pallas as pl
from jax.experimental.pallas import tpu as pltpu

F32 = jnp.float32
BF16 = jnp.bfloat16
HIGHEST = lax.Precision.HIGHEST

D_MODEL = 1024
HEAD_DIM = 64
N_HEADS = 4
KV_B = 2
W_MIX = N_HEADS * HEAD_DIM
CONV_A = 4
WIN = 128
DILATIONS = (1, 4, 16)
CACHE_D = 2048
CACHE_B = 128
D_FF = 2816
N_EXPERTS = 8
TOP_K = 2
D_FF_E = 3584
EPS = 1e-6
SCALE = HEAD_DIM ** -0.5
NEG = -1e30

COLS_A = 1024
COLS_B = 512
COLS_C = 1024
COLS_D = 768
COLS_BA = 128
COLS_ALL = COLS_A + COLS_B + COLS_C + COLS_D + COLS_BA

CHUNK = 64
BATCH_BLOCK = 4
MOE_ROWS = 512
MOE_FF_TILE = 896
VMEM_LIMIT = 56 * 1024 * 1024


def _cparams(*sem):
    return pltpu.CompilerParams(dimension_semantics=sem, vmem_limit_bytes=VMEM_LIMIT)


def _mm(a, b):
    return jnp.dot(a.astype(BF16), b.astype(BF16), preferred_element_type=F32)


def _mm_nt(a, b):
    return lax.dot_general(a.astype(BF16), b.astype(BF16), (((1,), (1,)), ((), ())),
                           preferred_element_type=F32)


def _mm_tn(a, b):
    return lax.dot_general(a.astype(BF16), b.astype(BF16), (((0,), (0,)), ((), ())),
                           preferred_element_type=F32)


def _split_bf16(a):
    hi = a.astype(BF16)
    return hi, (a - hi.astype(F32)).astype(BF16)


def _mm_3pass(a, b):
    ah, al = _split_bf16(a)
    bh, bl = _split_bf16(b)
    dot = functools.partial(jnp.dot, preferred_element_type=F32)
    return dot(ah, bh) + (dot(al, bh) + dot(ah, bl))


def _mm_f32(a, b):
    return jnp.dot(a, b, precision=HIGHEST, preferred_element_type=F32)


def _sigmoid(x):
    return 1.0 / (1.0 + jnp.exp(-x))


def _silu(x):
    return x * _sigmoid(x)


def _softplus(x):
    return jnp.maximum(x, 0.0) + jnp.log(1.0 + jnp.exp(-jnp.abs(x)))


def _rms_rows(x, gain):
    return x * lax.rsqrt(jnp.mean(x * x, axis=-1, keepdims=True) + EPS) * gain


def _inproj_kernel(x_ref, g_ref, w_ref, oa_ref, ob_ref, oc_ref, od_ref, oba_ref, *strided):
    h = _rms_rows(x_ref[...], g_ref[...]).astype(BF16)
    c = 0
    for o_ref in (oa_ref, ob_ref, oc_ref, od_ref, oba_ref):
        n = o_ref.shape[1]
        o_ref[...] = jnp.dot(h, w_ref[:, c:c + n], preferred_element_type=F32)
        c += n
    if strided:
        *strided, chunk_scr = strided
        n_chunks = chunk_scr.shape[0]
        for ch in range(n_chunks):
            chunk_scr[ch] = od_ref[:, ch * 128:(ch + 1) * 128]
        for o_ref in strided:
            r, rows = o_ref.shape[1], o_ref.shape[2]
            for rho in range(r):
                for ch in range(n_chunks):
                    o_ref[0, rho, :, ch * 128:(ch + 1) * 128] = chunk_scr[ch, pl.ds(rho, rows, stride=r), :]


def _inproj(x2, gain, w_perm, tm, strided_for=None):
    n = x2.shape[0]
    widths = (COLS_A, COLS_B, COLS_C, COLS_D, COLS_BA)
    out_specs = [pl.BlockSpec((tm, w), lambda i: (i, 0)) for w in widths]
    out_shape = [jax.ShapeDtypeStruct((n, w), F32) for w in widths]
    scratch = []
    if strided_for is not None:
        batch, seq = strided_for
        per_seq = seq // tm
        for r in DILATIONS[1:]:
            out_specs.append(pl.BlockSpec((1, r, tm // r, COLS_D), lambda i: (i // per_seq, 0, i % per_seq, 0)))
            out_shape.append(jax.ShapeDtypeStruct((batch, r, seq // r, COLS_D), F32))
        scratch = [pltpu.VMEM((COLS_D // 128, tm, 128), F32)]
    return pl.pallas_call(
        _inproj_kernel,
        grid=(n // tm,),
        in_specs=[pl.BlockSpec((tm, D_MODEL), lambda i: (i, 0)),
                  pl.BlockSpec((1, D_MODEL), lambda i: (0, 0)),
                  pl.BlockSpec((D_MODEL, COLS_ALL), lambda i: (0, 0))],
        out_specs=out_specs,
        out_shape=out_shape,
        scratch_shapes=scratch,
        compiler_params=_cparams("parallel"),
        name="inproj",
    )(x2, gain, w_perm)


def _permute_w_in(w):
    ba = jnp.pad(w[:, 1024:1032], ((0, 0), (0, COLS_BA - 8)))
    return jnp.concatenate([w[:, 0:1024], w[:, 1032:3336], ba], axis=1).astype(BF16)


def _band_kernel(q_ref, kp_ref, kc_ref, vp_ref, vc_ref, sink_ref, *out_refs, kv, with_sink, with_lse):
    o_ref = out_refs[0]
    n = pl.program_id(2)
    nb = q_ref.shape[0]
    row = lax.broadcasted_iota(jnp.int32, (WIN, 2 * WIN), 0)
    col = lax.broadcasted_iota(jnp.int32, (WIN, 2 * WIN), 1)
    dist = row + WIN - col
    valid = (dist >= 0) & (dist <= WIN) & ((col >= WIN) | (n > 0))
    group = N_HEADS // kv
    qs = [q_ref[b] * SCALE for b in range(nb)]
    kcat = [jnp.concatenate([kp_ref[b], kc_ref[b]], axis=0) for b in range(nb)]
    vcat = [jnp.concatenate([vp_ref[b], vc_ref[b]], axis=0) for b in range(nb)]
    units = [(b, h) for b in range(nb) for h in range(N_HEADS)]
    sl = lambda h: slice(h * HEAD_DIM, (h + 1) * HEAD_DIM)
    s = [jnp.where(valid, _mm_nt(qs[b][:, sl(h)], kcat[b][:, sl(h // group)]), NEG) for b, h in units]
    m = [jnp.max(t, axis=-1, keepdims=True) for t in s]
    if with_sink:
        sink = [sink_ref[:, h:h + 1] for _, h in units]
        m = [jnp.maximum(a, b) for a, b in zip(m, sink)]
    p = [jnp.exp(t - a) for t, a in zip(s, m)]
    l = [jnp.sum(t, axis=-1, keepdims=True) for t in p]
    if with_sink:
        l = [a + jnp.exp(b - c) for a, b, c in zip(l, sink, m)]
    o = [_mm(t, vcat[b][:, sl(h // group)]) / a for t, a, (b, h) in zip(p, l, units)]
    for b in range(nb):
        o_ref[b] = jnp.concatenate(o[b * N_HEADS:(b + 1) * N_HEADS], axis=-1)
        if with_lse:
            lse = [jnp.broadcast_to(m[i] + jnp.log(l[i]), (WIN, HEAD_DIM)) for i in range(b * N_HEADS, (b + 1) * N_HEADS)]
            out_refs[1][b] = jnp.concatenate(lse, axis=-1)


def _band_attention(p4, q_col, k_col, v_col, kv, sinks, with_lse):
    batch, r, ln, width = p4.shape
    bb = BATCH_BLOCK if batch % BATCH_BLOCK == 0 else 1
    wq, wk = W_MIX, kv * HEAD_DIM
    qb, kb, vb = q_col // wq, k_col // wk, v_col // wk
    cur = lambda off: (lambda b, rho, n: (b, rho, n, off))
    prev = lambda off: (lambda b, rho, n: (b, rho, jnp.maximum(n - 1, 0), off))
    out_spec = pl.BlockSpec((bb, None, WIN, W_MIX), lambda b, rho, n: (b, rho, n, 0))
    out_shape = jax.ShapeDtypeStruct((batch, r, ln, W_MIX), F32)
    n_out = 2 if with_lse else 1
    return pl.pallas_call(
        functools.partial(_band_kernel, kv=kv, with_sink=sinks is not None, with_lse=with_lse),
        grid=(batch // bb, r, ln // WIN),
        in_specs=[pl.BlockSpec((bb, None, WIN, wq), cur(qb)),
                  pl.BlockSpec((bb, None, WIN, wk), prev(kb)),
                  pl.BlockSpec((bb, None, WIN, wk), cur(kb)),
                  pl.BlockSpec((bb, None, WIN, wk), prev(vb)),
                  pl.BlockSpec((bb, None, WIN, wk), cur(vb)),
                  pl.BlockSpec((1, 128), lambda b, rho, n: (0, 0))],
        out_specs=[out_spec] * n_out,
        out_shape=[out_shape] * n_out,
        compiler_params=_cparams("parallel", "parallel", "arbitrary"),
        name=f"band_r{r}",
    )(p4, p4, p4, p4, p4, sinks if sinks is not None else jnp.zeros((1, 128), F32))


def _gated_norm_rows(o, gate, gain):
    return _rms_rows(o, gain) * _silu(gate)


def _hgrn_constants(c):
    halves = []
    h = c // 2
    while h >= 1:
        halves.append(h)
        h //= 2
    t = np.arange(c)[:, None]
    u = np.arange(c)[None, :]
    mats = [(u <= t).astype(np.float32)]
    level = np.full((c, c), -1, np.int32)
    level[np.arange(c), np.arange(c)] = 0
    for li, h in enumerate(halves, 1):
        mid = (t // (2 * h)) * (2 * h) + h
        second = (t % (2 * h)) >= h
        mats.append(np.where(second, (u > mid) & (u <= t), (u > t) & (u <= mid)).astype(np.float32))
        pair = (t // (2 * h) == u // (2 * h)) & ((t % (2 * h)) >= h) & ((u % (2 * h)) < h)
        level[pair] = li
    return np.concatenate(mats, axis=0), level, len(halves)


def _hgrn_kernel(pc_ref, lb_ref, gain_ref, mat_ref, lvl_ref, o_ref, st_ref, s_scr, *, c, n_levels):
    j = pl.program_id(1)
    nb = pc_ref.shape[0]

    @pl.when(j == 0)
    def _():
        s_scr[...] = jnp.zeros_like(s_scr)

    lb = lb_ref[...]
    lvl = lvl_ref[...]
    xs = [pc_ref[b] for b in range(nb)]
    fs = [lb + (1.0 - lb) * _sigmoid(x[:, 256:512]) for x in xs]
    sums = [_mm_f32(mat_ref[...], -jnp.log(f)) for f in fs]
    gcs = [-s[0:c] for s in sums]
    g_last = [gc[c - 1:c, :] for gc in gcs]
    q_dec = [x[:, 0:256] * jnp.exp(gc) for x, gc in zip(xs, gcs)]
    k_dec = [(1.0 - f) * jnp.exp(gl - gc) for f, gl, gc in zip(fs, g_last, gcs)]
    units = [(b, h) for b in range(nb) for h in range(N_HEADS)]
    sl = lambda h: slice(h * HEAD_DIM, (h + 1) * HEAD_DIM)
    qh = [xs[b][:, sl(h)] for b, h in units]
    kh = [1.0 - fs[b][:, sl(h)] for b, h in units]
    vh = [xs[b][:, 512 + h * HEAD_DIM:512 + (h + 1) * HEAD_DIM] for b, h in units]
    a = [jnp.where(lvl == 0, _mm_nt(q, k), 0.0) for q, k in zip(qh, kh)]
    for li in range(1, n_levels + 1):
        damp = [jnp.exp(-sums[b][li * c:(li + 1) * c, sl(h)]) for b, h in units]
        part = [_mm_nt(q * d, k * d) for q, k, d in zip(qh, kh, damp)]
        a = [acc + jnp.where(lvl == li, p, 0.0) for acc, p in zip(a, part)]
    st = [s_scr[b, h] for b, h in units]
    o_inter = [_mm_nt(q_dec[b][:, sl(h)], t) for (b, h), t in zip(units, st)]
    o = [oi + _mm(aa, v) for oi, aa, v in zip(o_inter, a, vh)]
    upd = [_mm_tn(v, k_dec[b][:, sl(h)]) for (b, h), v in zip(units, vh)]
    for i, (b, h) in enumerate(units):
        s_scr[b, h] = jnp.exp(g_last[b][:, sl(h)]) * st[i] + upd[i]
    for b in range(nb):
        outs = [_gated_norm_rows(o[b * N_HEADS + h], xs[b][:, 768 + h * HEAD_DIM:768 + (h + 1) * HEAD_DIM],
                                 gain_ref[...]) for h in range(N_HEADS)]
        o_ref[b] = jnp.concatenate(outs, axis=-1)

    @pl.when(j == pl.num_programs(1) - 1)
    def _():
        st_ref[...] = s_scr[...]


def _hgrn_prompt(pc3, lb, gain):
    batch, seq, _ = pc3.shape
    c = CHUNK
    bb = BATCH_BLOCK if batch % BATCH_BLOCK == 0 else 1
    mat, level, n_levels = _hgrn_constants(c)
    o, st = pl.pallas_call(
        functools.partial(_hgrn_kernel, c=c, n_levels=n_levels),
        grid=(batch // bb, seq // c),
        in_specs=[pl.BlockSpec((bb, c, COLS_C), lambda b, j: (b, j, 0)),
                  pl.BlockSpec((1, W_MIX), lambda b, j: (0, 0)),
                  pl.BlockSpec((1, HEAD_DIM), lambda b, j: (0, 0)),
                  pl.BlockSpec(mat.shape, lambda b, j: (0, 0)),
                  pl.BlockSpec(level.shape, lambda b, j: (0, 0))],
        out_specs=[pl.BlockSpec((bb, c, W_MIX), lambda b, j: (b, j, 0)),
                   pl.BlockSpec((bb, N_HEADS, HEAD_DIM, HEAD_DIM), lambda b, j: (b, 0, 0, 0))],
        out_shape=[jax.ShapeDtypeStruct((batch, seq, W_MIX), F32),
                   jax.ShapeDtypeStruct((batch, N_HEADS, HEAD_DIM, HEAD_DIM), F32)],
        scratch_shapes=[pltpu.VMEM((bb, N_HEADS, HEAD_DIM, HEAD_DIM), F32)],
        compiler_params=_cparams("parallel", "arbitrary"),
        name="hgrn_prompt",
    )(pc3, lb, gain, jnp.asarray(mat), jnp.asarray(level))
    return o.reshape(batch * seq, W_MIX), jnp.swapaxes(st, -1, -2)


def _unit_lower_solve(lows, rhss, c):
    xs = [rhs - _mm_3pass(low, rhs) for low, rhs in zip(lows, rhss)]
    ps = lows
    span = 2
    while span < c:
        mm = _mm_3pass if span == 2 else _mm
        ps = [mm(p, p) for p in ps]
        xs = [x + mm(p, x) for p, x in zip(ps, xs)]
        span *= 2
    return xs


def _gdn_kernel(pa_ref, pba_ref, cw_ref, hp_ref, gain_ref, tri_ref, ones_ref, o_ref, st_ref,
                s_scr, buf_scr, *, c):
    j = pl.program_id(1)
    pad = 8
    nb = pa_ref.shape[0]

    @pl.when(j == 0)
    def _():
        s_scr[...] = jnp.zeros_like(s_scr)
        buf_scr[:, 0:pad, :] = jnp.zeros((nb, pad, 3 * W_MIX), F32)

    ones = ones_ref[...]
    row = lax.broadcasted_iota(jnp.int32, (c, c), 0)
    col = lax.broadcasted_iota(jnp.int32, (c, c), 1)
    qs, ks, vs, gates, betas, gcs, gcts = [], [], [], [], [], [], []
    for b in range(nb):
        x = pa_ref[b, :, 0:3 * W_MIX]
        gates.append(pa_ref[b, :, 3 * W_MIX:4 * W_MIX])
        buf_scr[b, pad:pad + c, :] = x
        y = cw_ref[CONV_A - 1:CONV_A, :] * x
        for tap in range(CONV_A - 1):
            back = CONV_A - 1 - tap
            y = y + cw_ref[tap:tap + 1, :] * buf_scr[b, pad - back:pad - back + c, :]
        buf_scr[b, 0:pad, :] = buf_scr[b, c:c + pad, :]
        y = _silu(y)
        qs.append(y[:, 0:W_MIX])
        ks.append(y[:, W_MIX:2 * W_MIX])
        vs.append(y[:, 2 * W_MIX:3 * W_MIX])
        ba = pba_ref[b]
        betas.append(_sigmoid(ba))
        g = hp_ref[0:1, :] * _softplus(ba + hp_ref[1:2, :])
        gcs.append(_mm_f32(tri_ref[0], g))
        gcts.append(lax.dot_general(g, tri_ref[1], (((0,), (0,)), ((), ())), precision=HIGHEST,
                                    preferred_element_type=F32))
    qs = [q * lax.rsqrt(_mm_f32(q * q, ones) + EPS) * SCALE for q in qs]
    ks = [k * lax.rsqrt(_mm_f32(k * k, ones) + EPS) for k in ks]
    units = [(b, h) for b in range(nb) for h in range(N_HEADS)]
    sl = lambda h: slice(h * HEAD_DIM, (h + 1) * HEAD_DIM)
    qh = [qs[b][:, sl(h)] for b, h in units]
    kh = [ks[b][:, sl(h)] for b, h in units]
    vh = [vs[b][:, sl(h)] for b, h in units]
    b_col = [betas[b][:, h:h + 1] for b, h in units]
    g_col = [gcs[b][:, 4 + h:5 + h] for b, h in units]
    g_row = [gcts[b][4 + h:5 + h, :] for b, h in units]
    decay = [jnp.where(row >= col, jnp.exp(jnp.minimum(gc_ - gr_, 0.0)), 0.0) for gc_, gr_ in zip(g_col, g_row)]
    kk = [_mm_nt(k, k) for k in kh]
    qk = [_mm_nt(q, k) for q, k in zip(qh, kh)]
    low = [jnp.where(row > col, bc * a * d, 0.0) for bc, a, d in zip(b_col, kk, decay)]
    eg = [jnp.exp(gc_) for gc_ in g_col]
    rhs = [jnp.concatenate([v * bc, k * (bc * e)], axis=-1) for v, k, bc, e in zip(vh, kh, b_col, eg)]
    sol = _unit_lower_solve(low, rhs, c)
    intra = [a * d for a, d in zip(qk, decay)]
    g_last = [gc_[c - 1:c, :] for gc_ in g_col]
    st = [s_scr[b, h] for b, h in units]
    u = [s[:, 0:HEAD_DIM] - _mm_nt(s[:, HEAD_DIM:2 * HEAD_DIM], t) for s, t in zip(sol, st)]
    o_inter = [_mm_nt(q * e, t) for q, e, t in zip(qh, eg, st)]
    o = [oi + _mm(a, uu) for oi, a, uu in zip(o_inter, intra, u)]
    k_dec = [k * jnp.exp(gl - gc_) for k, gl, gc_ in zip(kh, g_last, g_col)]
    upd = [_mm_tn(uu, kd) for uu, kd in zip(u, k_dec)]
    for i, (b, h) in enumerate(units):
        s_scr[b, h] = jnp.exp(g_last[i]) * st[i] + upd[i]
    for b in range(nb):
        outs = [_gated_norm_rows(o[b * N_HEADS + h], gates[b][:, sl(h)], gain_ref[...]) for h in range(N_HEADS)]
        o_ref[b] = jnp.concatenate(outs, axis=-1)

    @pl.when(j == pl.num_programs(1) - 1)
    def _():
        st_ref[...] = s_scr[...]


def _gdn_prompt(pa3, pba3, conv_w, head_params, gain):
    batch, seq, _ = pa3.shape
    c = CHUNK
    bb = BATCH_BLOCK if batch % BATCH_BLOCK == 0 else 1
    lower = np.tril(np.ones((c, c), np.float32))
    tri = np.stack([lower, lower.T], axis=0)
    ones = np.kron(np.eye(N_HEADS, dtype=np.float32), np.ones((HEAD_DIM, HEAD_DIM), np.float32))
    o, st = pl.pallas_call(
        functools.partial(_gdn_kernel, c=c),
        grid=(batch // bb, seq // c),
        in_specs=[pl.BlockSpec((bb, c, COLS_A), lambda b, j: (b, j, 0)),
                  pl.BlockSpec((bb, c, COLS_BA), lambda b, j: (b, j, 0)),
                  pl.BlockSpec((CONV_A, 3 * W_MIX), lambda b, j: (0, 0)),
                  pl.BlockSpec((2, COLS_BA), lambda b, j: (0, 0)),
                  pl.BlockSpec((1, HEAD_DIM), lambda b, j: (0, 0)),
                  pl.BlockSpec((2, c, c), lambda b, j: (0, 0, 0)),
                  pl.BlockSpec((W_MIX, W_MIX), lambda b, j: (0, 0))],
        out_specs=[pl.BlockSpec((bb, c, W_MIX), lambda b, j: (b, j, 0)),
                   pl.BlockSpec((bb, N_HEADS, HEAD_DIM, HEAD_DIM), lambda b, j: (b, 0, 0, 0))],
        out_shape=[jax.ShapeDtypeStruct((batch, seq, W_MIX), F32),
                   jax.ShapeDtypeStruct((batch, N_HEADS, HEAD_DIM, HEAD_DIM), F32)],
        scratch_shapes=[pltpu.VMEM((bb, N_HEADS, HEAD_DIM, HEAD_DIM), F32),
                        pltpu.VMEM((bb, c + 8, 3 * W_MIX), F32)],
        compiler_params=_cparams("parallel", "arbitrary"),
        name="gdn_prompt",
    )(pa3, pba3, conv_w, head_params, gain, jnp.asarray(tri), jnp.asarray(ones))
    return o.reshape(batch * seq, W_MIX), jnp.swapaxes(st, -1, -2)


def _gdn_head_params(a_log, dt_bias):
    neg_a = jnp.pad(-jnp.exp(a_log.astype(F32)), (4, COLS_BA - 8))
    dtb = jnp.pad(dt_bias.astype(F32), (4, COLS_BA - 8))
    return jnp.stack([neg_a, dtb], axis=0)


def _outproj_kernel(*refs, n_branch):
    x_ref, oa_ref, ob_ref, oc_ref = refs[0:4]
    tm = x_ref.shape[0]
    if n_branch == 1:
        od = refs[4][...]
        w_ref, o_ref = refs[5], refs[6]
    else:
        d_refs = refs[4:4 + n_branch]
        l_refs = refs[4 + n_branch:4 + 2 * n_branch]
        w_ref, o_ref = refs[4 + 2 * n_branch], refs[5 + 2 * n_branch]
        scratch = refs[6 + 2 * n_branch:]

        def token_order(ref, scr):
            r, rows = ref.shape[1], ref.shape[2]
            if r == 1:
                return ref[0, 0]
            n_chunks = scr.shape[0]
            for rho in range(r):
                for ch in range(n_chunks):
                    scr[ch, pl.ds(rho, rows, stride=r), :] = ref[0, rho, :, ch * 128:(ch + 1) * 128]
            return jnp.concatenate([scr[ch] for ch in range(n_chunks)], axis=-1)

        outs = [token_order(ref, scratch[2 * i]) for i, ref in enumerate(d_refs)]
        lses = [token_order(ref, scratch[2 * i + 1]) for i, ref in enumerate(l_refs)]
        m = functools.reduce(jnp.maximum, lses)
        es = [jnp.exp(l - m) for l in lses]
        od = sum(e * o for e, o in zip(es, outs)) / sum(es)
    acc = x_ref[...]
    for i, part in enumerate((oa_ref[...], ob_ref[...], oc_ref[...], od)):
        acc = acc + _mm(part, w_ref[i * W_MIX:(i + 1) * W_MIX, :])
    o_ref[...] = acc


def _outproj(x2, oa, ob, oc, ods, lses, w_bf, tm):
    n = x2.shape[0]
    n_branch = len(ods)
    row = lambda w: pl.BlockSpec((tm, w), lambda i: (i, 0))
    specs = [row(D_MODEL)] + [row(W_MIX)] * 3
    scratch = []
    if n_branch == 1:
        specs.append(row(W_MIX))
    else:
        per_seq = (ods[0].shape[1] * ods[0].shape[2]) // tm
        for t in (*ods, *lses):
            r = t.shape[1]
            specs.append(pl.BlockSpec((1, r, tm // r, W_MIX), lambda i: (i // per_seq, 0, i % per_seq, 0)))
        scratch = [pltpu.VMEM((W_MIX // 128, tm, 128), F32)] * (2 * n_branch)
    specs.append(pl.BlockSpec((4 * W_MIX, D_MODEL), lambda i: (0, 0)))
    return pl.pallas_call(
        functools.partial(_outproj_kernel, n_branch=n_branch),
        grid=(n // tm,),
        in_specs=specs,
        out_specs=row(D_MODEL),
        out_shape=jax.ShapeDtypeStruct((n, D_MODEL), F32),
        scratch_shapes=scratch,
        compiler_params=_cparams("parallel"),
        name="outproj",
    )(x2, oa, ob, oc, *ods, *lses, w_bf)


def _ffn_kernel(x_ref, g_ref, wg_ref, wu_ref, wd_ref, o_ref, h_scr, acc_scr):
    f = pl.program_id(1)

    @pl.when(f == 0)
    def _():
        h_scr[...] = _rms_rows(x_ref[...], g_ref[...]).astype(BF16)
        acc_scr[...] = jnp.zeros_like(acc_scr)

    h = h_scr[...]
    gate = jnp.dot(h, wg_ref[...], preferred_element_type=F32)
    up = jnp.dot(h, wu_ref[...], preferred_element_type=F32)
    acc_scr[...] += _mm(_silu(gate) * up, wd_ref[...])

    @pl.when(f == pl.num_programs(1) - 1)
    def _():
        o_ref[...] = x_ref[...] + acc_scr[...]


def _ffn(x2, gain, w_gu_bf, w_down_bf, tm, tf=256):
    n = x2.shape[0]
    nf = D_FF // tf
    return pl.pallas_call(
        _ffn_kernel,
        grid=(n // tm, nf),
        in_specs=[pl.BlockSpec((tm, D_MODEL), lambda i, f: (i, 0)),
                  pl.BlockSpec((1, D_MODEL), lambda i, f: (0, 0)),
                  pl.BlockSpec((D_MODEL, tf), lambda i, f: (0, f)),
                  pl.BlockSpec((D_MODEL, tf), lambda i, f: (0, nf + f)),
                  pl.BlockSpec((tf, D_MODEL), lambda i, f: (f, 0))],
        out_specs=pl.BlockSpec((tm, D_MODEL), lambda i, f: (i, 0)),
        out_shape=jax.ShapeDtypeStruct((n, D_MODEL), F32),
        scratch_shapes=[pltpu.VMEM((tm, D_MODEL), BF16), pltpu.VMEM((tm, D_MODEL), F32)],
        compiler_params=_cparams("parallel", "arbitrary"),
        name="ffn",
    )(x2, gain, w_gu_bf, w_gu_bf, w_down_bf)


ROW_TILE = (8, 128)


def _router_kernel(x_ref, g_ref, wr_ref, br_ref, h_ref, route_ref):
    i = pl.program_id(0)
    last = pl.num_programs(0) - 1
    tm = x_ref.shape[0]

    @pl.when(i < last)
    def _():
        h = _rms_rows(x_ref[...], g_ref[...])
        h_ref[...] = h.reshape(tm, *ROW_TILE)
        lane = lax.broadcasted_iota(jnp.int32, (tm, 128), 1).astype(F32)
        logits = _mm_f32(h, wr_ref[...]) + br_ref[...]
        m1 = jnp.max(logits, axis=-1, keepdims=True)
        i1 = jnp.min(jnp.where(logits == m1, lane, 128.0), axis=-1, keepdims=True)
        rest = jnp.where(lane == i1, NEG, logits)
        m2 = jnp.max(rest, axis=-1, keepdims=True)
        i2 = jnp.min(jnp.where(rest == m2, lane, 128.0), axis=-1, keepdims=True)
        e2 = jnp.exp(m2 - m1)
        route_ref[...] = (jnp.where(lane == 0.0, i1, 0.0) + jnp.where(lane == 1.0, i2, 0.0)
                          + jnp.where(lane == 2.0, 1.0 / (1.0 + e2), 0.0)
                          + jnp.where(lane == 3.0, e2 / (1.0 + e2), 0.0))

    @pl.when(i == last)
    def _():
        h_ref[...] = jnp.zeros(h_ref.shape, F32)
        route_ref[...] = jnp.zeros(route_ref.shape, F32)


def _gather_rows_kernel(tok_ref, h_hbm, o_ref, sem):
    tg = o_ref.shape[0]
    base = pl.program_id(0) * tg

    def issue(r, carry):
        pltpu.make_async_copy(h_hbm.at[tok_ref[base + r]], o_ref.at[r], sem).start()
        return carry

    lax.fori_loop(0, tg, issue, 0)
    pltpu.make_async_copy(h_hbm.at[pl.ds(0, tg)], o_ref, sem).wait()


def _expert_kernel(be_ref, x_ref, wg_ref, wu_ref, wd_ref, o_ref, xb_scr, acc_scr):
    f = pl.program_id(1)
    tm = x_ref.shape[0]

    @pl.when(f == 0)
    def _():
        xb_scr[...] = x_ref[...].reshape(tm, D_MODEL).astype(BF16)
        acc_scr[...] = jnp.zeros_like(acc_scr)

    xb = xb_scr[...]
    gate = jnp.dot(xb, wg_ref[0], preferred_element_type=F32)
    up = jnp.dot(xb, wu_ref[0], preferred_element_type=F32)
    acc_scr[...] += _mm(_silu(gate) * up, wd_ref[0])

    @pl.when(f == pl.num_programs(1) - 1)
    def _():
        o_ref[...] = acc_scr[...].reshape(tm, *ROW_TILE)


def _combine_kernel(pos_ref, x_ref, route_ref, y_hbm, o_ref, first_scr, second_scr, sems):
    tc = x_ref.shape[0]
    base = pl.program_id(0) * (2 * tc)

    def issue(r, carry):
        pltpu.make_async_copy(y_hbm.at[pos_ref[base + 2 * r]], first_scr.at[r], sems.at[0]).start()
        pltpu.make_async_copy(y_hbm.at[pos_ref[base + 2 * r + 1]], second_scr.at[r], sems.at[1]).start()
        return carry

    lax.fori_loop(0, tc, issue, 0)
    pltpu.make_async_copy(y_hbm.at[pl.ds(0, tc)], first_scr, sems.at[0]).wait()
    pltpu.make_async_copy(y_hbm.at[pl.ds(0, tc)], second_scr, sems.at[1]).wait()
    route = route_ref[...]
    o_ref[...] = (x_ref[...] + route[:, 2:3] * first_scr[...].reshape(tc, D_MODEL)
                  + route[:, 3:4] * second_scr[...].reshape(tc, D_MODEL))


def _moe_routing(route, n, tm_rows):
    e_flat = route[:n, 0:TOP_K].astype(jnp.int32).reshape(-1)
    n_assign = n * TOP_K
    onehot = (e_flat[:, None] == jnp.arange(N_EXPERTS, dtype=jnp.int32)[None, :]).astype(jnp.int32)
    csum = jnp.cumsum(onehot, axis=0)
    rank = jnp.sum(csum * onehot, axis=1) - 1
    counts = csum[-1]
    padded = (counts + tm_rows - 1) // tm_rows * tm_rows
    pad_end = jnp.cumsum(padded)
    pad_start = pad_end - padded
    start = jnp.cumsum(counts) - counts
    dest = pad_start[e_flat] + rank
    n_blocks = -(-(n_assign + N_EXPERTS * (tm_rows - 1)) // tm_rows)
    blk_e = jnp.minimum(jnp.sum(jnp.arange(n_blocks, dtype=jnp.int32)[:, None] * tm_rows >= pad_end[None, :], axis=1),
                        N_EXPERTS - 1).astype(jnp.int32)
    order = jnp.argsort(e_flat, stable=True).astype(jnp.int32)
    rows = jnp.arange(n_blocks * tm_rows, dtype=jnp.int32)
    row_e = jnp.repeat(blk_e, tm_rows)
    offset = rows - pad_start[row_e]
    valid = (offset < counts[row_e]) & (rows < pad_end[N_EXPERTS - 1])
    src = jnp.clip(start[row_e] + offset, 0, n_assign - 1)
    row_tok = jnp.where(valid, order[src] // TOP_K, n).astype(jnp.int32)
    return row_tok, dest.astype(jnp.int32), blk_e, n_blocks


def _moe(x2, gain, w_router, b_router, w_gu_bf, w_down_bf, tm):
    n = x2.shape[0]
    rows = min(MOE_ROWS, max(128, n * TOP_K // N_EXPERTS))
    wr = jnp.pad(w_router.astype(F32), ((0, 0), (0, 128 - N_EXPERTS)))
    br = jnp.pad(b_router.astype(F32), (0, 128 - N_EXPERTS), constant_values=NEG).reshape(1, 128)
    nt = n // tm
    h3, route = pl.pallas_call(
        _router_kernel,
        grid=(nt + 1,),
        in_specs=[pl.BlockSpec((tm, D_MODEL), lambda i: (jnp.minimum(i, nt - 1), 0)),
                  pl.BlockSpec((1, D_MODEL), lambda i: (0, 0)),
                  pl.BlockSpec((D_MODEL, 128), lambda i: (0, 0)),
                  pl.BlockSpec((1, 128), lambda i: (0, 0))],
        out_specs=[pl.BlockSpec((tm, *ROW_TILE), lambda i: (i, 0, 0)),
                   pl.BlockSpec((tm, 128), lambda i: (i, 0))],
        out_shape=[jax.ShapeDtypeStruct((n + tm, *ROW_TILE), F32),
                   jax.ShapeDtypeStruct((n + tm, 128), F32)],
        compiler_params=_cparams("arbitrary"),
        name="moe_router",
    )(x2, gain, wr, br)

    row_tok, dest, blk_e, n_blocks = _moe_routing(route, n, rows)
    n_rows = n_blocks * rows

    x_rows = pl.pallas_call(
        _gather_rows_kernel,
        grid_spec=pltpu.PrefetchScalarGridSpec(
            num_scalar_prefetch=1,
            grid=(n_blocks,),
            in_specs=[pl.BlockSpec(memory_space=pl.ANY)],
            out_specs=pl.BlockSpec((rows, *ROW_TILE), lambda i, tok: (i, 0, 0)),
            scratch_shapes=[pltpu.SemaphoreType.DMA(())]),
        out_shape=jax.ShapeDtypeStruct((n_rows, *ROW_TILE), F32),
        compiler_params=_cparams("arbitrary"),
        name="moe_gather",
    )(row_tok, h3)

    nf = D_FF_E // MOE_FF_TILE
    y_rows = pl.pallas_call(
        _expert_kernel,
        grid_spec=pltpu.PrefetchScalarGridSpec(
            num_scalar_prefetch=1,
            grid=(n_blocks, nf),
            in_specs=[pl.BlockSpec((rows, *ROW_TILE), lambda i, f, be: (i, 0, 0)),
                      pl.BlockSpec((1, D_MODEL, MOE_FF_TILE), lambda i, f, be: (be[i], 0, f)),
                      pl.BlockSpec((1, D_MODEL, MOE_FF_TILE), lambda i, f, be: (be[i], 0, nf + f)),
                      pl.BlockSpec((1, MOE_FF_TILE, D_MODEL), lambda i, f, be: (be[i], f, 0))],
            out_specs=pl.BlockSpec((rows, *ROW_TILE), lambda i, f, be: (i, 0, 0)),
            scratch_shapes=[pltpu.VMEM((rows, D_MODEL), BF16), pltpu.VMEM((rows, D_MODEL), F32)]),
        out_shape=jax.ShapeDtypeStruct((n_rows, *ROW_TILE), F32),
        compiler_params=_cparams("arbitrary", "arbitrary"),
        name="moe_experts",
    )(blk_e, x_rows, w_gu_bf, w_gu_bf, w_down_bf)

    return pl.pallas_call(
        _combine_kernel,
        grid_spec=pltpu.PrefetchScalarGridSpec(
            num_scalar_prefetch=1,
            grid=(nt,),
            in_specs=[pl.BlockSpec((tm, D_MODEL), lambda i, pos: (i, 0)),
                      pl.BlockSpec((tm, 128), lambda i, pos: (i, 0)),
                      pl.BlockSpec(memory_space=pl.ANY)],
            out_specs=pl.BlockSpec((tm, D_MODEL), lambda i, pos: (i, 0)),
            scratch_shapes=[pltpu.VMEM((tm, *ROW_TILE), F32), pltpu.VMEM((tm, *ROW_TILE), F32),
                            pltpu.SemaphoreType.DMA((2,))]),
        out_shape=jax.ShapeDtypeStruct((n, D_MODEL), F32),
        compiler_params=_cparams("arbitrary"),
        name="moe_combine",
    )(dest, x2, route, y_rows)


def _norm_kernel(x_ref, g_ref, o_ref):
    o_ref[...] = _rms_rows(x_ref[...], g_ref[...])


def _final_norm(x2, gain, tm):
    n = x2.shape[0]
    return pl.pallas_call(
        _norm_kernel,
        grid=(n // tm,),
        in_specs=[pl.BlockSpec((tm, D_MODEL), lambda i: (i, 0)),
                  pl.BlockSpec((1, D_MODEL), lambda i: (0, 0))],
        out_specs=pl.BlockSpec((tm, D_MODEL), lambda i: (i, 0)),
        out_shape=jax.ShapeDtypeStruct((n, D_MODEL), F32),
        compiler_params=_cparams("parallel"),
        name="final_norm",
    )(x2, gain)


ROWS_T = 8
LANE_TILE = 128


def _sattn_kernel(qd_ref, qb_ref, kd_all, vd_all, kdn_all, vdn_all, kb_all, vb_all, kbn_all, vbn_all, sink_ref,
                  od_ref, ob_ref, *rolled_refs, t_new, here):
    kd_ref, vd_ref, kdn_ref, vdn_ref, kb_ref, vb_ref, kbn_ref, vbn_ref = (
        r.at[here] for r in (kd_all, vd_all, kdn_all, vdn_all, kb_all, vb_all, kbn_all, vbn_all))
    bb = qd_ref.shape[0]
    first_new = LANE_TILE - t_new

    def distances(n_cache):
        row = lax.broadcasted_iota(jnp.int32, (ROWS_T, n_cache), 0)
        col = lax.broadcasted_iota(jnp.int32, (ROWS_T, n_cache), 1)
        rown = lax.broadcasted_iota(jnp.int32, (ROWS_T, LANE_TILE), 0)
        j = lax.broadcasted_iota(jnp.int32, (ROWS_T, LANE_TILE), 1) - first_new
        d_new = rown - j
        return n_cache + row - col, d_new, (j >= 0) & (d_new >= 0)

    def softmax_parts(sc, sn, ok_c, ok_n, sink):
        sc = jnp.where(ok_c, sc, NEG)
        sn = jnp.where(ok_n, sn, NEG)
        m = jnp.maximum(jnp.max(sc, axis=-1, keepdims=True), jnp.max(sn, axis=-1, keepdims=True))
        if sink is not None:
            m = jnp.maximum(m, sink)
        pc = jnp.exp(sc - m)
        pn = jnp.exp(sn - m)
        l = jnp.sum(pc, axis=-1, keepdims=True) + jnp.sum(pn, axis=-1, keepdims=True)
        if sink is not None:
            l = l + jnp.exp(sink - m)
        return pc, pn, l, m + jnp.log(l)

    sl = lambda h: slice(h * HEAD_DIM, (h + 1) * HEAD_DIM)
    units = [(i, h) for i in range(bb) for h in range(N_HEADS)]
    dc, dn, ok_new = distances(kd_ref.shape[-1])
    qs = [qd_ref[i][:, sl(h)] * SCALE for i, h in units]
    sc = [_mm(q, kd_ref[i, h]) for q, (i, h) in zip(qs, units)]
    sn = [_mm(q, kdn_ref[i, h]) for q, (i, h) in zip(qs, units)]
    pcs, pns, ls, lses = [], [], [], []
    for r in DILATIONS:
        ok_c = (dc <= WIN * r) & ((dc & (r - 1)) == 0)
        ok_n = ok_new & ((dn & (r - 1)) == 0)
        parts = [softmax_parts(a, b, ok_c, ok_n, None) for a, b in zip(sc, sn)]
        pcs.append([p[0] for p in parts])
        pns.append([p[1] for p in parts])
        ls.append([p[2] for p in parts])
        lses.append([p[3] for p in parts])
    nbr = len(DILATIONS)
    acc = [_mm_nt(jnp.concatenate([pcs[r][u] for r in range(nbr)], axis=0), vd_ref[i, h])
           + _mm_nt(jnp.concatenate([pns[r][u] for r in range(nbr)], axis=0), vdn_ref[i, h])
           for u, (i, h) in enumerate(units)]
    outs = []
    for u in range(len(units)):
        m = functools.reduce(jnp.maximum, [lses[r][u] for r in range(nbr)])
        es = [jnp.exp(lses[r][u] - m) for r in range(nbr)]
        num = sum(es[r] * (acc[u][r * ROWS_T:(r + 1) * ROWS_T] / ls[r][u]) for r in range(nbr))
        outs.append(num / sum(es))
    for i in range(bb):
        od_ref[i] = jnp.concatenate(outs[i * N_HEADS:(i + 1) * N_HEADS], axis=-1)
    group = N_HEADS // KV_B
    dc, dn, ok_new = distances(kb_ref.shape[-1])
    qs = [qb_ref[i][:, sl(h)] * SCALE for i, h in units]
    parts = [softmax_parts(_mm(q, kb_ref[i, h // group]), _mm(q, kbn_ref[i, h // group]), dc <= WIN, ok_new,
                           sink_ref[:, h:h + 1]) for q, (i, h) in zip(qs, units)]
    outs = [(_mm_nt(p[0], vb_ref[i, h // group]) + _mm_nt(p[1], vbn_ref[i, h // group])) / p[2]
            for p, (i, h) in zip(parts, units)]
    for i in range(bb):
        ob_ref[i] = jnp.concatenate(outs[i * N_HEADS:(i + 1) * N_HEADS], axis=-1)
    lane = lax.broadcasted_iota(jnp.int32, (HEAD_DIM, LANE_TILE), 1)
    if not rolled_refs:
        return
    for src, new, dst in zip((kd_all, vd_all, kb_all, vb_all), (kdn_all, vdn_all, kbn_all, vbn_all), rolled_refs):
        n = src.shape[-1]
        for layer in range(src.shape[0]):
            for i in range(bb):
                for h in range(src.shape[2]):
                    rolled = pltpu.roll(src[layer, i, h], n - t_new, axis=1)
                    if n > LANE_TILE:
                        dst[layer, i, h, :, 0:n - LANE_TILE] = rolled[:, 0:n - LANE_TILE]
                    dst[layer, i, h, :, n - LANE_TILE:n] = jnp.where(lane >= first_new, new[layer, i, h],
                                                                     rolled[:, n - LANE_TILE:n])


def _pad_rows(t, rows):
    return jnp.pad(t, ((0, 0), (0, rows - t.shape[1]), (0, 0)))


def _new_columns(t, n_heads):
    bs, t_new, _ = t.shape
    x = jnp.transpose(t.reshape(bs, t_new, n_heads, HEAD_DIM), (0, 2, 3, 1))
    return jnp.pad(x, ((0, 0), (0, 0), (0, 0), (LANE_TILE - t_new, 0)))


def _sample_attention(layer, pb_s, pd_s, caches_t, new_cols, sinks):
    bs, t_new, _ = pb_s.shape
    depth = caches_t[0].shape[0]
    last = layer == depth - 1
    bb = 1 if last else 2
    pb8, pd8 = _pad_rows(pb_s, ROWS_T), _pad_rows(pd_s, ROWS_T)
    sink_row = jnp.pad(sinks.astype(F32), (0, 128 - N_HEADS)).reshape(1, 128)
    cbk, cbv, cdk, cdv = caches_t
    if last:
        news = [jnp.stack([layer_cols[k] for layer_cols in new_cols], axis=0) for k in range(4)]
    else:
        news = [new_cols[layer][k][None] for k in range(4)]
    n_slab = depth if last else 1
    qblk = lambda: pl.BlockSpec((bb, ROWS_T, W_MIX), lambda i: (i, 0, 0))
    slab = lambda c: pl.BlockSpec((n_slab, bb) + c.shape[2:], lambda i: (0 if last else layer, i, 0, 0, 0))
    newblk = lambda c: pl.BlockSpec((n_slab, bb) + c.shape[2:], lambda i: (0, i, 0, 0, 0))
    rolled_specs = [slab(c) for c in (cdk, cdv, cbk, cbv)] if last else []
    rolled_shapes = [jax.ShapeDtypeStruct(c.shape, c.dtype) for c in (cdk, cdv, cbk, cbv)] if last else []
    res = pl.pallas_call(
        functools.partial(_sattn_kernel, t_new=t_new, here=layer if last else 0),
        grid=(bs // bb,),
        in_specs=[qblk(), qblk(), slab(cdk), slab(cdv), newblk(news[0]), newblk(news[1]),
                  slab(cbk), slab(cbv), newblk(news[2]), newblk(news[3]),
                  pl.BlockSpec((1, 128), lambda i: (0, 0))],
        out_specs=[qblk(), qblk()] + rolled_specs,
        out_shape=[jax.ShapeDtypeStruct((bs, ROWS_T, W_MIX), F32)] * 2 + rolled_shapes,
        compiler_params=_cparams("arbitrary"),
        name="sample_attn",
    )(pd8, pb8, cdk, cdv, news[0], news[1], cbk, cbv, news[2], news[3], sink_row)
    od, ob = res[0], res[1]
    return (ob[:, :t_new].reshape(bs * t_new, W_MIX), od[:, :t_new].reshape(bs * t_new, W_MIX), tuple(res[2:]))


def _srec_kernel(xq_ref, xk_ref, xv_ref, bq_ref, bk_ref, bv_ref, cwq_ref, cwk_ref, cwv_ref, ga_ref,
                 ba_ref, hp_ref, gna_ref, sa_ref, qc_ref, fc_ref, ic_ref, gcg_ref, lb_ref, gnc_ref, sc_ref,
                 oa_ref, sa_out, oc_ref, sc_out, q_scr, k_scr, d_scr, *, t_new):
    nb = sa_ref.shape[-1]
    zero = jnp.zeros((HEAD_DIM, nb), F32)

    def conv(x_ref, b_ref, cw_ref, t):
        y = None
        for tap in range(CONV_A):
            pos = t + tap
            src = b_ref[pos] if pos < CONV_A - 1 else x_ref[pos - (CONV_A - 1)]
            term = cw_ref[tap] * src
            y = term if y is None else y + term
        return _silu(y)

    def l2(x):
        return x * lax.rsqrt(jnp.sum(x * x, axis=0, keepdims=True) + EPS)

    def gated_norm(o, gate, gain):
        return o * lax.rsqrt(jnp.mean(o * o, axis=0, keepdims=True) + EPS) * gain * _silu(gate)

    sa_out[0] = sa_ref[0]
    for t in range(t_new):
        q_scr[...] = l2(conv(xq_ref, bq_ref, cwq_ref, t)) * SCALE
        k_scr[...] = l2(conv(xk_ref, bk_ref, cwk_ref, t))
        v = conv(xv_ref, bv_ref, cwv_ref, t)
        beta = _sigmoid(ba_ref[0, t:t + 1, :])
        dec = jnp.exp(hp_ref[0, 0:1, :] * _softplus(ba_ref[0, t_new + t:t_new + t + 1, :] + hp_ref[0, 1:2, :]))

        def decay_and_read(kk, acc):
            s = sa_out[0, kk] * dec
            sa_out[0, kk] = s
            return acc + k_scr[pl.ds(kk, 1), :] * s

        err = (v - lax.fori_loop(0, HEAD_DIM, decay_and_read, zero)) * beta

        def write_and_query(kk, acc):
            s = sa_out[0, kk] + k_scr[pl.ds(kk, 1), :] * err
            sa_out[0, kk] = s
            return acc + q_scr[pl.ds(kk, 1), :] * s

        o = lax.fori_loop(0, HEAD_DIM, write_and_query, zero)
        oa_ref[t] = gated_norm(o, ga_ref[t], gna_ref[...])

    sc_out[0] = sc_ref[0]
    for t in range(t_new):
        lb = lb_ref[...]
        f = lb + (1.0 - lb) * _sigmoid(fc_ref[t])
        q_scr[...] = qc_ref[t]
        k_scr[...] = 1.0 - f
        d_scr[...] = jnp.exp(jnp.log(f))
        v = ic_ref[t]

        def update(kk, acc):
            s = sc_out[0, kk] * d_scr[pl.ds(kk, 1), :] + k_scr[pl.ds(kk, 1), :] * v
            sc_out[0, kk] = s
            return acc + q_scr[pl.ds(kk, 1), :] * s

        o = lax.fori_loop(0, HEAD_DIM, update, zero)
        oc_ref[t] = gated_norm(o, gcg_ref[t], gnc_ref[...])


def _sample_recurrences(pa_s, pba_s, pc_s, conv_buf, s_a, s_c, conv_w, a_log, dt_bias, norm_a, lb, norm_c):
    bs, t_new, _ = pa_s.shape
    lanes_last = lambda t: jnp.transpose(t, (1, 2, 0))
    pa_t = lanes_last(pa_s)
    pc_t = lanes_last(pc_s)
    buf_t = lanes_last(conv_buf.astype(F32))
    ba = jnp.transpose(pba_s[:, :, 0:8], (2, 1, 0))
    ba = jnp.concatenate([ba[0:N_HEADS], ba[N_HEADS:2 * N_HEADS]], axis=1)
    hp = jnp.stack([-jnp.exp(a_log.astype(F32)), dt_bias.astype(F32)], axis=1)
    hp = jnp.broadcast_to(hp[:, :, None], (N_HEADS, 2, bs))
    cw = conv_w.astype(F32)[:, :, None]
    sa_t = jnp.transpose(s_a.astype(F32), (1, 2, 3, 0))
    sc_t = jnp.transpose(s_c.astype(F32), (1, 2, 3, 0))
    col = lambda v: v.astype(F32).reshape(-1, 1)
    hd = HEAD_DIM
    feat = lambda rows, off: pl.BlockSpec((rows, hd, bs), lambda h: (0, off + h, 0))
    cwspec = lambda off: pl.BlockSpec((CONV_A, hd, 1), lambda h: (0, off + h, 0))
    per_head = lambda rows: pl.BlockSpec((1, rows, bs), lambda h: (h, 0, 0))
    state = pl.BlockSpec((1, hd, hd, bs), lambda h: (h, 0, 0, 0))
    vec = pl.BlockSpec((hd, 1), lambda h: (0, 0))
    nh = N_HEADS
    oa, sa_n, oc, sc_n = pl.pallas_call(
        functools.partial(_srec_kernel, t_new=t_new),
        grid=(N_HEADS,),
        in_specs=[feat(t_new, 0), feat(t_new, nh), feat(t_new, 2 * nh),
                  feat(CONV_A - 1, 0), feat(CONV_A - 1, nh), feat(CONV_A - 1, 2 * nh),
                  cwspec(0), cwspec(nh), cwspec(2 * nh),
                  feat(t_new, 3 * nh), per_head(2 * t_new), per_head(2), vec, state,
                  feat(t_new, 0), feat(t_new, nh), feat(t_new, 2 * nh), feat(t_new, 3 * nh),
                  pl.BlockSpec((hd, 1), lambda h: (h, 0)), vec, state],
        out_specs=[feat(t_new, 0), state, feat(t_new, 0), state],
        out_shape=[jax.ShapeDtypeStruct((t_new, W_MIX, bs), F32),
                   jax.ShapeDtypeStruct((N_HEADS, hd, hd, bs), F32),
                   jax.ShapeDtypeStruct((t_new, W_MIX, bs), F32),
                   jax.ShapeDtypeStruct((N_HEADS, hd, hd, bs), F32)],
        scratch_shapes=[pltpu.VMEM((hd, bs), F32)] * 3,
        compiler_params=_cparams("parallel"),
        name="sample_recurrences",
    )(pa_t, pa_t, pa_t, buf_t, buf_t, buf_t, cw, cw, cw, pa_t, ba, hp, col(norm_a), sa_t,
      pc_t, pc_t, pc_t, pc_t, col(lb), col(norm_c), sc_t)
    rows_first = lambda t: jnp.transpose(t, (2, 0, 1)).reshape(bs * t_new, W_MIX)
    back = lambda t: jnp.transpose(t, (3, 0, 1, 2))
    return rows_first(oa), back(sa_n), rows_first(oc), back(sc_n)


def _heads(t, n):
    return t.reshape(t.shape[0], t.shape[1], n, HEAD_DIM)


def _prompt_mixers(pa, pb, pc, pd, pba, pd_strided, batch, seq, conv_w, head_params, norm_a, sinks, lb, norm_c):
    pa3 = pa.reshape(batch, seq, COLS_A)
    pb3 = pb.reshape(batch, seq, COLS_B)
    pc3 = pc.reshape(batch, seq, COLS_C)
    pd3 = pd.reshape(batch, seq, COLS_D)
    oa, s_a = _gdn_prompt(pa3, pba.reshape(batch, seq, COLS_BA), conv_w, head_params, norm_a)
    oc, s_c = _hgrn_prompt(pc3, lb, norm_c)
    sink_row = jnp.pad(sinks.astype(F32), (0, 128 - N_HEADS)).reshape(1, 128)
    (ob,) = _band_attention(pb3[:, None], 0, 256, 384, KV_B, sink_row, False)
    ods, lses = [], []
    for p4 in (pd3[:, None], *pd_strided):
        o, lse = _band_attention(p4, 0, 256, 512, N_HEADS, None, True)
        ods.append(o)
        lses.append(lse)
    nb, nd = min(CACHE_B, seq), min(CACHE_D, seq)
    state = (_heads(pb3[:, seq - nb:, 256:384], KV_B), _heads(pb3[:, seq - nb:, 384:512], KV_B),
             _heads(pd3[:, seq - nd:, 256:512], N_HEADS), _heads(pd3[:, seq - nd:, 512:768], N_HEADS),
             pa3[:, seq - (CONV_A - 1):, 0:3 * W_MIX], s_a, s_c)
    return oa, ob.reshape(batch * seq, W_MIX), oc, ods, lses, state


def _sample_mixers(layer, pa, pb, pc, pd, pba, bs, t_new, caches_t, new_cols, states, conv_w, a_log, dt_bias,
                   norm_a, sinks, lb, norm_c):
    conv_buf, s_a, s_c = states
    pa3 = pa.reshape(bs, t_new, COLS_A)
    pb3 = pb.reshape(bs, t_new, COLS_B)
    pc3 = pc.reshape(bs, t_new, COLS_C)
    pd3 = pd.reshape(bs, t_new, COLS_D)
    new_cols.append((_new_columns(pd3[:, :, 256:512], N_HEADS), _new_columns(pd3[:, :, 512:768], N_HEADS),
                     _new_columns(pb3[:, :, 256:384], KV_B), _new_columns(pb3[:, :, 384:512], KV_B)))
    ob, od, rolled = _sample_attention(layer, pb3, pd3, caches_t, new_cols, sinks)
    oa, s_a_new, oc, s_c_new = _sample_recurrences(
        pa3, pba.reshape(bs, t_new, COLS_BA), pc3, conv_buf, s_a, s_c, conv_w, a_log, dt_bias, norm_a, lb, norm_c)
    conv_all = jnp.concatenate([conv_buf.astype(F32), pa3[:, :, 0:3 * W_MIX]], axis=1)
    state = (conv_all[:, -(CONV_A - 1):].astype(conv_buf.dtype), s_a_new.astype(s_a.dtype),
             s_c_new.astype(s_c.dtype))
    return oa, ob, oc, od, rolled, state


def kernel(x_prompt, x_sample, cache_b_k, cache_b_v, cache_d_k, cache_d_v, state_a_conv, state_a_s, state_c_s, norm_mix, w_in, conv_a, a_log, dt_bias, norm_a, sinks_b, lb_logits, norm_c, w_out, norm_ffn, w_ffn_gu, w_ffn_down, w_router, b_router, w_moe_gu, w_moe_down, norm_final):
    depth = w_in.shape[0]
    batch, seq, _ = x_prompt.shape
    bs, t_new, _ = x_sample.shape
    lb_p = jax.nn.softmax(lb_logits.astype(F32), axis=0)
    lower_bounds = jnp.cumsum(lb_p, axis=0) - lb_p[0]
    xp = x_prompt.reshape(batch * seq, D_MODEL)
    xs = x_sample.reshape(bs * t_new, D_MODEL)
    tm_p, tm_s = 512, 256
    row = lambda v: v.astype(F32).reshape(1, -1)
    prompt_states, sample_states = [], []
    rows_last = lambda c: jnp.transpose(c, (0, 1, 3, 4, 2))
    caches_t = tuple(rows_last(c) for c in (cache_b_k, cache_b_v, cache_d_k, cache_d_v))
    new_cols = []
    for l in range(depth):
        w_in_l = _permute_w_in(w_in[l])
        w_out_l = w_out[l].astype(BF16)
        conv_w = conv_a[l].astype(F32)
        head_params = _gdn_head_params(a_log[l], dt_bias[l])
        lb = lower_bounds[l]
        mix = (row(norm_a[l]), sinks_b[l], row(lb), row(norm_c[l]))

        projs = _inproj(xp, row(norm_mix[l]), w_in_l, tm_p, strided_for=(batch, seq))
        oa, ob, oc, ods, lses, st_p = _prompt_mixers(*projs[0:5], projs[5:], batch, seq, conv_w, head_params, *mix)
        xp = _outproj(xp, oa, ob, oc, ods, lses, w_out_l, tm_p)
        prompt_states.append(st_p)

        projs = _inproj(xs, row(norm_mix[l]), w_in_l, tm_s)
        oa, ob, oc, od, rolled, st_s = _sample_mixers(
            l, *projs, bs, t_new, caches_t, new_cols, (state_a_conv[l], state_a_s[l], state_c_s[l]), conv_w,
            a_log[l], dt_bias[l], norm_a[l], sinks_b[l], lb, norm_c[l])
        xs = _outproj(xs, oa, ob, oc, [od], [], w_out_l, tm_s)
        sample_states.append(st_s)

        if l % 2 == 0:
            w_gu = w_ffn_gu[l // 2].astype(BF16)
            w_dn = w_ffn_down[l // 2].astype(BF16)
            xp = _ffn(xp, row(norm_ffn[l]), w_gu, w_dn, 1024)
            xs = _ffn(xs, row(norm_ffn[l]), w_gu, w_dn, 512)
        else:
            w_gu = w_moe_gu[l // 2].astype(BF16)
            w_dn = w_moe_down[l // 2].astype(BF16)
            xp = _moe(xp, row(norm_ffn[l]), w_router[l // 2], b_router[l // 2], w_gu, w_dn, 512)
            xs = _moe(xs, row(norm_ffn[l]), w_router[l // 2], b_router[l // 2], w_gu, w_dn, 512)
    y_prompt = _final_norm(xp, row(norm_final), 1024).reshape(batch, seq, D_MODEL)
    y_sample = _final_norm(xs, row(norm_final), 512).reshape(bs, t_new, D_MODEL)
    stack = lambda states: [jnp.stack(t, 0) for t in zip(*states)]
    rows_back = lambda c, ref: jnp.transpose(c, (0, 1, 4, 2, 3)).astype(ref.dtype)
    d_k, d_v, b_k, b_v = rolled
    sample_caches = [rows_back(b_k, cache_b_k), rows_back(b_v, cache_b_v), rows_back(d_k, cache_d_k),
                     rows_back(d_v, cache_d_v)]
    return (y_prompt, y_sample, *stack(prompt_states), *sample_caches, *stack(sample_states))
```

```python
import functools
import math

import numpy as np
import jax
import jax.numpy as jnp
from jax import lax
from jax.experimental import pallas as pl
from jax.experimental.pallas import tpu as pltpu

F32 = jnp.float32
BF16 = jnp.bfloat16
HIGHEST = lax.Precision.HIGHEST

D_MODEL = 1024
HEAD_DIM = 64
N_HEADS = 4
KV_B = 2
W_MIX = N_HEADS * HEAD_DIM
CONV_A = 4
WIN = 128
DILATIONS = (1, 4, 16)
CACHE_D = 2048
CACHE_B = 128
D_FF = 2816
N_EXPERTS = 8
TOP_K = 2
D_FF_E = 3584
EPS = 1e-6
SCALE = HEAD_DIM ** -0.5
NEG = -1e30

COLS_A = 1024
COLS_B = 512
COLS_C = 1024
COLS_D = 768
COLS_BA = 128
COLS_ALL = COLS_A + COLS_B + COLS_C + COLS_D + COLS_BA

CHUNK = 64
BATCH_BLOCK = 4
FFN_FF_TILE = 1408
MOE_ROWS = 512
MOE_FF_TILE = 1792
VMEM_LIMIT = 56 * 1024 * 1024


def _cparams(*sem):
    return pltpu.CompilerParams(dimension_semantics=sem, vmem_limit_bytes=VMEM_LIMIT)


def _mm(a, b):
    return jnp.dot(a.astype(BF16), b.astype(BF16), preferred_element_type=F32)


def _mm_nt(a, b):
    return lax.dot_general(a.astype(BF16), b.astype(BF16), (((1,), (1,)), ((), ())),
                           preferred_element_type=F32)


def _mm_tn(a, b):
    return lax.dot_general(a.astype(BF16), b.astype(BF16), (((0,), (0,)), ((), ())),
                           preferred_element_type=F32)


def _split_bf16(a):
    hi = a.astype(BF16)
    return hi, (a - hi.astype(F32)).astype(BF16)


def _mm_3pass(a, b):
    ah, al = _split_bf16(a)
    bh, bl = _split_bf16(b)
    dot = functools.partial(jnp.dot, preferred_element_type=F32)
    return dot(ah, bh) + (dot(al, bh) + dot(ah, bl))


def _mm_f32(a, b):
    return jnp.dot(a, b, precision=HIGHEST, preferred_element_type=F32)


def _split3_bf16(a):
    hi = a.astype(BF16)
    rest = a - hi.astype(F32)
    mid = rest.astype(BF16)
    return hi, mid, (rest - mid.astype(F32)).astype(BF16)


def _mm_exact_lhs(sel, b, dims=(((1,), (0,)), ((), ()))):
    sel = sel.astype(BF16)
    return sum(lax.dot_general(sel, t, dims, preferred_element_type=F32) for t in _split3_bf16(b))


def _mm_exact_rhs(a, sel, dims=(((1,), (0,)), ((), ()))):
    sel = sel.astype(BF16)
    return sum(lax.dot_general(t, sel, dims, preferred_element_type=F32) for t in _split3_bf16(a))


def _sigmoid(x):
    return 1.0 / (1.0 + jnp.exp(-x))


def _silu(x):
    return x * _sigmoid(x)


def _softplus(x):
    return jnp.maximum(x, 0.0) + jnp.log(1.0 + jnp.exp(-jnp.abs(x)))


def _rms_rows(x, gain):
    return x * lax.rsqrt(jnp.mean(x * x, axis=-1, keepdims=True) + EPS) * gain


def _inproj_kernel(x_ref, g_ref, w_ref, oa_ref, ob_ref, oc_ref, od_ref, oba_ref, *strided):
    h = _rms_rows(x_ref[...], g_ref[...]).astype(BF16)
    c = 0
    for o_ref in (oa_ref, ob_ref, oc_ref, od_ref, oba_ref):
        n = o_ref.shape[1]
        o_ref[...] = jnp.dot(h, w_ref[:, c:c + n], preferred_element_type=F32)
        c += n
    if strided:
        *strided, chunk_scr = strided
        n_chunks = chunk_scr.shape[0]
        for ch in range(n_chunks):
            chunk_scr[ch] = od_ref[:, ch * 128:(ch + 1) * 128]
        for o_ref in strided:
            r, rows = o_ref.shape[1], o_ref.shape[2]
            for rho in range(r):
                for ch in range(n_chunks):
                    o_ref[0, rho, :, ch * 128:(ch + 1) * 128] = chunk_scr[ch, pl.ds(rho, rows, stride=r), :]


def _inproj(x2, gain, w_perm, tm, strided_for=None):
    n = x2.shape[0]
    widths = (COLS_A, COLS_B, COLS_C, COLS_D, COLS_BA)
    out_specs = [pl.BlockSpec((tm, w), lambda i: (i, 0)) for w in widths]
    out_shape = [jax.ShapeDtypeStruct((n, w), F32) for w in widths]
    scratch = []
    if strided_for is not None:
        batch, seq = strided_for
        per_seq = seq // tm
        for r in DILATIONS[1:]:
            out_specs.append(pl.BlockSpec((1, r, tm // r, COLS_D), lambda i: (i // per_seq, 0, i % per_seq, 0)))
            out_shape.append(jax.ShapeDtypeStruct((batch, r, seq // r, COLS_D), F32))
        scratch = [pltpu.VMEM((COLS_D // 128, tm, 128), F32)]
    return pl.pallas_call(
        _inproj_kernel,
        grid=(n // tm,),
        in_specs=[pl.BlockSpec((tm, D_MODEL), lambda i: (i, 0)),
                  pl.BlockSpec((1, D_MODEL), lambda i: (0, 0)),
                  pl.BlockSpec((D_MODEL, COLS_ALL), lambda i: (0, 0))],
        out_specs=out_specs,
        out_shape=out_shape,
        scratch_shapes=scratch,
        compiler_params=_cparams("parallel"),
        name="inproj",
    )(x2, gain, w_perm)


def _permute_w_in(w):
    ba = jnp.pad(w[:, 1024:1032], ((0, 0), (0, COLS_BA - 8)))
    return jnp.concatenate([w[:, 0:1024], w[:, 1032:3336], ba], axis=1).astype(BF16)


def _band_kernel(q_ref, kp_ref, kc_ref, vp_ref, vc_ref, sink_ref, *out_refs, kv, with_sink, with_lse):
    o_ref = out_refs[0]
    n = pl.program_id(2)
    nb = q_ref.shape[0]
    row = lax.broadcasted_iota(jnp.int32, (WIN, 2 * WIN), 0)
    col = lax.broadcasted_iota(jnp.int32, (WIN, 2 * WIN), 1)
    dist = row + WIN - col
    valid = (dist >= 0) & (dist <= WIN) & ((col >= WIN) | (n > 0))
    group = N_HEADS // kv
    qs = [q_ref[b] * SCALE for b in range(nb)]
    kcat = [jnp.concatenate([kp_ref[b], kc_ref[b]], axis=0) for b in range(nb)]
    vcat = [jnp.concatenate([vp_ref[b], vc_ref[b]], axis=0) for b in range(nb)]
    units = [(b, h) for b in range(nb) for h in range(N_HEADS)]
    sl = lambda h: slice(h * HEAD_DIM, (h + 1) * HEAD_DIM)
    s = [jnp.where(valid, _mm_nt(qs[b][:, sl(h)], kcat[b][:, sl(h // group)]), NEG) for b, h in units]
    m = [jnp.max(t, axis=-1, keepdims=True) for t in s]
    if with_sink:
        sink = [sink_ref[:, h:h + 1] for _, h in units]
        m = [jnp.maximum(a, b) for a, b in zip(m, sink)]
    p = [jnp.exp(t - a) for t, a in zip(s, m)]
    l = [jnp.sum(t, axis=-1, keepdims=True) for t in p]
    if with_sink:
        l = [a + jnp.exp(b - c) for a, b, c in zip(l, sink, m)]
    o = [_mm(t, vcat[b][:, sl(h // group)]) / a for t, a, (b, h) in zip(p, l, units)]
    for b in range(nb):
        o_ref[b] = jnp.concatenate(o[b * N_HEADS:(b + 1) * N_HEADS], axis=-1)
        if with_lse:
            lse = [jnp.broadcast_to(m[i] + jnp.log(l[i]), (WIN, HEAD_DIM)) for i in range(b * N_HEADS, (b + 1) * N_HEADS)]
            out_refs[1][b] = jnp.concatenate(lse, axis=-1)


def _band_attention(p4, q_col, k_col, v_col, kv, sinks, with_lse):
    batch, r, ln, width = p4.shape
    bb = BATCH_BLOCK if batch % BATCH_BLOCK == 0 else 1
    wq, wk = W_MIX, kv * HEAD_DIM
    qb, kb, vb = q_col // wq, k_col // wk, v_col // wk
    cur = lambda off: (lambda b, rho, n: (b, rho, n, off))
    prev = lambda off: (lambda b, rho, n: (b, rho, jnp.maximum(n - 1, 0), off))
    out_spec = pl.BlockSpec((bb, None, WIN, W_MIX), lambda b, rho, n: (b, rho, n, 0))
    out_shape = jax.ShapeDtypeStruct((batch, r, ln, W_MIX), F32)
    n_out = 2 if with_lse else 1
    return pl.pallas_call(
        functools.partial(_band_kernel, kv=kv, with_sink=sinks is not None, with_lse=with_lse),
        grid=(batch // bb, r, ln // WIN),
        in_specs=[pl.BlockSpec((bb, None, WIN, wq), cur(qb)),
                  pl.BlockSpec((bb, None, WIN, wk), prev(kb)),
                  pl.BlockSpec((bb, None, WIN, wk), cur(kb)),
                  pl.BlockSpec((bb, None, WIN, wk), prev(vb)),
                  pl.BlockSpec((bb, None, WIN, wk), cur(vb)),
                  pl.BlockSpec((1, 128), lambda b, rho, n: (0, 0))],
        out_specs=[out_spec] * n_out,
        out_shape=[out_shape] * n_out,
        compiler_params=_cparams("parallel", "parallel", "arbitrary"),
        name=f"band_r{r}",
    )(p4, p4, p4, p4, p4, sinks if sinks is not None else jnp.zeros((1, 128), F32))


def _gated_norm_rows(o, gate, gain):
    return _rms_rows(o, gain) * _silu(gate)


def _hgrn_constants(c):
    halves = []
    h = c // 2
    while h >= 1:
        halves.append(h)
        h //= 2
    t = np.arange(c)[:, None]
    u = np.arange(c)[None, :]
    mats = [(u <= t).astype(np.float32)]
    level = np.full((c, c), -1, np.int32)
    level[np.arange(c), np.arange(c)] = 0
    for li, h in enumerate(halves, 1):
        mid = (t // (2 * h)) * (2 * h) + h
        second = (t % (2 * h)) >= h
        mats.append(np.where(second, (u > mid) & (u <= t), (u > t) & (u <= mid)).astype(np.float32))
        pair = (t // (2 * h) == u // (2 * h)) & ((t % (2 * h)) >= h) & ((u % (2 * h)) < h)
        level[pair] = li
    return np.concatenate(mats, axis=0), level, len(halves)


def _hgrn_kernel(pc_ref, lb_ref, gain_ref, mat_ref, lvl_ref, o_ref, st_ref, s_scr, *, c, n_levels):
    j = pl.program_id(1)
    nb = pc_ref.shape[0]

    @pl.when(j == 0)
    def _():
        s_scr[...] = jnp.zeros_like(s_scr)

    lb = lb_ref[...]
    lvl = lvl_ref[...]
    xs = [pc_ref[b] for b in range(nb)]
    fs = [lb + (1.0 - lb) * _sigmoid(x[:, 256:512]) for x in xs]
    sums = [_mm_exact_lhs(mat_ref[...], -jnp.log(f)) for f in fs]
    gcs = [-s[0:c] for s in sums]
    g_last = [gc[c - 1:c, :] for gc in gcs]
    q_dec = [x[:, 0:256] * jnp.exp(gc) for x, gc in zip(xs, gcs)]
    k_dec = [(1.0 - f) * jnp.exp(gl - gc) for f, gl, gc in zip(fs, g_last, gcs)]
    units = [(b, h) for b in range(nb) for h in range(N_HEADS)]
    sl = lambda h: slice(h * HEAD_DIM, (h + 1) * HEAD_DIM)
    qh = [xs[b][:, sl(h)] for b, h in units]
    kh = [1.0 - fs[b][:, sl(h)] for b, h in units]
    vh = [xs[b][:, 512 + h * HEAD_DIM:512 + (h + 1) * HEAD_DIM] for b, h in units]
    a = [jnp.where(lvl == 0, _mm_nt(q, k), 0.0) for q, k in zip(qh, kh)]
    for li in range(1, n_levels + 1):
        damp = [jnp.exp(-sums[b][li * c:(li + 1) * c, sl(h)]) for b, h in units]
        part = [_mm_nt(q * d, k * d) for q, k, d in zip(qh, kh, damp)]
        a = [acc + jnp.where(lvl == li, p, 0.0) for acc, p in zip(a, part)]
    st = [s_scr[b, h] for b, h in units]
    o_inter = [_mm_nt(q_dec[b][:, sl(h)], t) for (b, h), t in zip(units, st)]
    o = [oi + _mm(aa, v) for oi, aa, v in zip(o_inter, a, vh)]
    upd = [_mm_tn(v, k_dec[b][:, sl(h)]) for (b, h), v in zip(units, vh)]
    for i, (b, h) in enumerate(units):
        s_scr[b, h] = jnp.exp(g_last[b][:, sl(h)]) * st[i] + upd[i]
    for b in range(nb):
        outs = [_gated_norm_rows(o[b * N_HEADS + h], xs[b][:, 768 + h * HEAD_DIM:768 + (h + 1) * HEAD_DIM],
                                 gain_ref[...]) for h in range(N_HEADS)]
        o_ref[b] = jnp.concatenate(outs, axis=-1)

    @pl.when(j == pl.num_programs(1) - 1)
    def _():
        st_ref[...] = s_scr[...]


def _hgrn_prompt(pc3, lb, gain):
    batch, seq, _ = pc3.shape
    c = CHUNK
    bb = BATCH_BLOCK if batch % BATCH_BLOCK == 0 else 1
    mat, level, n_levels = _hgrn_constants(c)
    o, st = pl.pallas_call(
        functools.partial(_hgrn_kernel, c=c, n_levels=n_levels),
        grid=(batch // bb, seq // c),
        in_specs=[pl.BlockSpec((bb, c, COLS_C), lambda b, j: (b, j, 0)),
                  pl.BlockSpec((1, W_MIX), lambda b, j: (0, 0)),
                  pl.BlockSpec((1, HEAD_DIM), lambda b, j: (0, 0)),
                  pl.BlockSpec(mat.shape, lambda b, j: (0, 0)),
                  pl.BlockSpec(level.shape, lambda b, j: (0, 0))],
        out_specs=[pl.BlockSpec((bb, c, W_MIX), lambda b, j: (b, j, 0)),
                   pl.BlockSpec((bb, N_HEADS, HEAD_DIM, HEAD_DIM), lambda b, j: (b, 0, 0, 0))],
        out_shape=[jax.ShapeDtypeStruct((batch, seq, W_MIX), F32),
                   jax.ShapeDtypeStruct((batch, N_HEADS, HEAD_DIM, HEAD_DIM), F32)],
        scratch_shapes=[pltpu.VMEM((bb, N_HEADS, HEAD_DIM, HEAD_DIM), F32)],
        compiler_params=_cparams("parallel", "arbitrary"),
        name="hgrn_prompt",
    )(pc3, lb, gain, jnp.asarray(mat), jnp.asarray(level))
    return o.reshape(batch * seq, W_MIX), jnp.swapaxes(st, -1, -2)


def _unit_lower_solve(lows, rhss, c):
    xs = [rhs - _mm_3pass(low, rhs) for low, rhs in zip(lows, rhss)]
    ps = lows
    span = 2
    while span < c:
        mm = _mm_3pass if span == 2 else _mm
        ps = [mm(p, p) for p in ps]
        xs = [x + mm(p, x) for p, x in zip(ps, xs)]
        span *= 2
    return xs


def _gdn_kernel(pa_ref, pba_ref, cw_ref, hp_ref, gain_ref, tri_ref, ones_ref, o_ref, st_ref,
                s_scr, buf_scr, *, c):
    j = pl.program_id(1)
    pad = 8
    nb = pa_ref.shape[0]

    @pl.when(j == 0)
    def _():
        s_scr[...] = jnp.zeros_like(s_scr)
        buf_scr[:, 0:pad, :] = jnp.zeros((nb, pad, 3 * W_MIX), F32)

    ones = ones_ref[...]
    row = lax.broadcasted_iota(jnp.int32, (c, c), 0)
    col = lax.broadcasted_iota(jnp.int32, (c, c), 1)
    qs, ks, vs, gates, betas, gcs, gcts = [], [], [], [], [], [], []
    for b in range(nb):
        x = pa_ref[b, :, 0:3 * W_MIX]
        gates.append(pa_ref[b, :, 3 * W_MIX:4 * W_MIX])
        buf_scr[b, pad:pad + c, :] = x
        y = cw_ref[CONV_A - 1:CONV_A, :] * x
        for tap in range(CONV_A - 1):
            back = CONV_A - 1 - tap
            y = y + cw_ref[tap:tap + 1, :] * buf_scr[b, pad - back:pad - back + c, :]
        buf_scr[b, 0:pad, :] = buf_scr[b, c:c + pad, :]
        y = _silu(y)
        qs.append(y[:, 0:W_MIX])
        ks.append(y[:, W_MIX:2 * W_MIX])
        vs.append(y[:, 2 * W_MIX:3 * W_MIX])
        ba = pba_ref[b]
        betas.append(_sigmoid(ba))
        g = hp_ref[0:1, :] * _softplus(ba + hp_ref[1:2, :])
        gcs.append(_mm_exact_lhs(tri_ref[0], g))
        gcts.append(_mm_exact_rhs(g, tri_ref[1], (((0,), (0,)), ((), ()))))
    qs = [q * lax.rsqrt(_mm_exact_rhs(q * q, ones) + EPS) * SCALE for q in qs]
    ks = [k * lax.rsqrt(_mm_exact_rhs(k * k, ones) + EPS) for k in ks]
    units = [(b, h) for b in range(nb) for h in range(N_HEADS)]
    sl = lambda h: slice(h * HEAD_DIM, (h + 1) * HEAD_DIM)
    qh = [qs[b][:, sl(h)] for b, h in units]
    kh = [ks[b][:, sl(h)] for b, h in units]
    vh = [vs[b][:, sl(h)] for b, h in units]
    b_col = [betas[b][:, h:h + 1] for b, h in units]
    g_col = [gcs[b][:, 4 + h:5 + h] for b, h in units]
    g_row = [gcts[b][4 + h:5 + h, :] for b, h in units]
    decay = [jnp.where(row >= col, jnp.exp(jnp.minimum(gc_ - gr_, 0.0)), 0.0) for gc_, gr_ in zip(g_col, g_row)]
    kk = [_mm_nt(k, k) for k in kh]
    qk = [_mm_nt(q, k) for q, k in zip(qh, kh)]
    low = [jnp.where(row > col, bc * a * d, 0.0) for bc, a, d in zip(b_col, kk, decay)]
    eg = [jnp.exp(gc_) for gc_ in g_col]
    rhs = [jnp.concatenate([v * bc, k * (bc * e)], axis=-1) for v, k, bc, e in zip(vh, kh, b_col, eg)]
    sol = _unit_lower_solve(low, rhs, c)
    intra = [a * d for a, d in zip(qk, decay)]
    g_last = [gc_[c - 1:c, :] for gc_ in g_col]
    st = [s_scr[b, h] for b, h in units]
    u = [s[:, 0:HEAD_DIM] - _mm_nt(s[:, HEAD_DIM:2 * HEAD_DIM], t) for s, t in zip(sol, st)]
    o_inter = [_mm_nt(q * e, t) for q, e, t in zip(qh, eg, st)]
    o = [oi + _mm(a, uu) for oi, a, uu in zip(o_inter, intra, u)]
    k_dec = [k * jnp.exp(gl - gc_) for k, gl, gc_ in zip(kh, g_last, g_col)]
    upd = [_mm_tn(uu, kd) for uu, kd in zip(u, k_dec)]
    for i, (b, h) in enumerate(units):
        s_scr[b, h] = jnp.exp(g_last[i]) * st[i] + upd[i]
    for b in range(nb):
        outs = [_gated_norm_rows(o[b * N_HEADS + h], gates[b][:, sl(h)], gain_ref[...]) for h in range(N_HEADS)]
        o_ref[b] = jnp.concatenate(outs, axis=-1)

    @pl.when(j == pl.num_programs(1) - 1)
    def _():
        st_ref[...] = s_scr[...]


def _gdn_prompt(pa3, pba3, conv_w, head_params, gain):
    batch, seq, _ = pa3.shape
    c = CHUNK
    bb = BATCH_BLOCK if batch % BATCH_BLOCK == 0 else 1
    lower = np.tril(np.ones((c, c), np.float32))
    tri = np.stack([lower, lower.T], axis=0)
    ones = np.kron(np.eye(N_HEADS, dtype=np.float32), np.ones((HEAD_DIM, HEAD_DIM), np.float32))
    o, st = pl.pallas_call(
        functools.partial(_gdn_kernel, c=c),
        grid=(batch // bb, seq // c),
        in_specs=[pl.BlockSpec((bb, c, COLS_A), lambda b, j: (b, j, 0)),
                  pl.BlockSpec((bb, c, COLS_BA), lambda b, j: (b, j, 0)),
                  pl.BlockSpec((CONV_A, 3 * W_MIX), lambda b, j: (0, 0)),
                  pl.BlockSpec((2, COLS_BA), lambda b, j: (0, 0)),
                  pl.BlockSpec((1, HEAD_DIM), lambda b, j: (0, 0)),
                  pl.BlockSpec((2, c, c), lambda b, j: (0, 0, 0)),
                  pl.BlockSpec((W_MIX, W_MIX), lambda b, j: (0, 0))],
        out_specs=[pl.BlockSpec((bb, c, W_MIX), lambda b, j: (b, j, 0)),
                   pl.BlockSpec((bb, N_HEADS, HEAD_DIM, HEAD_DIM), lambda b, j: (b, 0, 0, 0))],
        out_shape=[jax.ShapeDtypeStruct((batch, seq, W_MIX), F32),
                   jax.ShapeDtypeStruct((batch, N_HEADS, HEAD_DIM, HEAD_DIM), F32)],
        scratch_shapes=[pltpu.VMEM((bb, N_HEADS, HEAD_DIM, HEAD_DIM), F32),
                        pltpu.VMEM((bb, c + 8, 3 * W_MIX), F32)],
        compiler_params=_cparams("parallel", "arbitrary"),
        name="gdn_prompt",
    )(pa3, pba3, conv_w, head_params, gain, jnp.asarray(tri), jnp.asarray(ones))
    return o.reshape(batch * seq, W_MIX), jnp.swapaxes(st, -1, -2)


def _gdn_head_params(a_log, dt_bias):
    neg_a = jnp.pad(-jnp.exp(a_log.astype(F32)), (4, COLS_BA - 8))
    dtb = jnp.pad(dt_bias.astype(F32), (4, COLS_BA - 8))
    return jnp.stack([neg_a, dtb], axis=0)


def _outproj_kernel(*refs, n_branch):
    x_ref, oa_ref, ob_ref, oc_ref = refs[0:4]
    tm = x_ref.shape[0]
    if n_branch == 1:
        od = refs[4][...]
        w_ref, o_ref = refs[5], refs[6]
    else:
        d_refs = refs[4:4 + n_branch]
        l_refs = refs[4 + n_branch:4 + 2 * n_branch]
        w_ref, o_ref = refs[4 + 2 * n_branch], refs[5 + 2 * n_branch]
        scratch = refs[6 + 2 * n_branch:]

        def token_order(ref, scr):
            r, rows = ref.shape[1], ref.shape[2]
            if r == 1:
                return ref[0, 0]
            n_chunks = scr.shape[0]
            for rho in range(r):
                for ch in range(n_chunks):
                    scr[ch, pl.ds(rho, rows, stride=r), :] = ref[0, rho, :, ch * 128:(ch + 1) * 128]
            return jnp.concatenate([scr[ch] for ch in range(n_chunks)], axis=-1)

        outs = [token_order(ref, scratch[2 * i]) for i, ref in enumerate(d_refs)]
        lses = [token_order(ref, scratch[2 * i + 1]) for i, ref in enumerate(l_refs)]
        m = functools.reduce(jnp.maximum, lses)
        es = [jnp.exp(l - m) for l in lses]
        od = sum(e * o for e, o in zip(es, outs)) / sum(es)
    acc = x_ref[...]
    for i, part in enumerate((oa_ref[...], ob_ref[...], oc_ref[...], od)):
        acc = acc + _mm(part, w_ref[i * W_MIX:(i + 1) * W_MIX, :])
    o_ref[...] = acc


def _outproj(x2, oa, ob, oc, ods, lses, w_bf, tm):
    n = x2.shape[0]
    n_branch = len(ods)
    row = lambda w: pl.BlockSpec((tm, w), lambda i: (i, 0))
    specs = [row(D_MODEL)] + [row(W_MIX)] * 3
    scratch = []
    if n_branch == 1:
        specs.append(row(W_MIX))
    else:
        per_seq = (ods[0].shape[1] * ods[0].shape[2]) // tm
        for t in (*ods, *lses):
            r = t.shape[1]
            specs.append(pl.BlockSpec((1, r, tm // r, W_MIX), lambda i: (i // per_seq, 0, i % per_seq, 0)))
        scratch = [pltpu.VMEM((W_MIX // 128, tm, 128), F32)] * (2 * n_branch)
    specs.append(pl.BlockSpec((4 * W_MIX, D_MODEL), lambda i: (0, 0)))
    return pl.pallas_call(
        functools.partial(_outproj_kernel, n_branch=n_branch),
        grid=(n // tm,),
        in_specs=specs,
        out_specs=row(D_MODEL),
        out_shape=jax.ShapeDtypeStruct((n, D_MODEL), F32),
        scratch_shapes=scratch,
        compiler_params=_cparams("parallel"),
        name="outproj",
    )(x2, oa, ob, oc, *ods, *lses, w_bf)


def _ffn_kernel(x_ref, g_ref, wg_ref, wu_ref, wd_ref, o_ref, h_scr, acc_scr):
    f = pl.program_id(1)

    @pl.when(f == 0)
    def _():
        h_scr[...] = _rms_rows(x_ref[...], g_ref[...]).astype(BF16)
        acc_scr[...] = jnp.zeros_like(acc_scr)

    h = h_scr[...]
    gate = jnp.dot(h, wg_ref[...], preferred_element_type=F32)
    up = jnp.dot(h, wu_ref[...], preferred_element_type=F32)
    acc_scr[...] += _mm(_silu(gate) * up, wd_ref[...])

    @pl.when(f == pl.num_programs(1) - 1)
    def _():
        o_ref[...] = x_ref[...] + acc_scr[...]


def _ffn(x2, gain, w_gu_bf, w_down_bf, tm, tf):
    n = x2.shape[0]
    nf = D_FF // tf
    return pl.pallas_call(
        _ffn_kernel,
        grid=(n // tm, nf),
        in_specs=[pl.BlockSpec((tm, D_MODEL), lambda i, f: (i, 0)),
                  pl.BlockSpec((1, D_MODEL), lambda i, f: (0, 0)),
                  pl.BlockSpec((D_MODEL, tf), lambda i, f: (0, f)),
                  pl.BlockSpec((D_MODEL, tf), lambda i, f: (0, nf + f)),
                  pl.BlockSpec((tf, D_MODEL), lambda i, f: (f, 0))],
        out_specs=pl.BlockSpec((tm, D_MODEL), lambda i, f: (i, 0)),
        out_shape=jax.ShapeDtypeStruct((n, D_MODEL), F32),
        scratch_shapes=[pltpu.VMEM((tm, D_MODEL), BF16), pltpu.VMEM((tm, D_MODEL), F32)],
        compiler_params=_cparams("parallel", "arbitrary"),
        name="ffn",
    )(x2, gain, w_gu_bf, w_gu_bf, w_down_bf)


ROW_TILE = (8, 128)


def _router_kernel(x_ref, g_ref, wr_ref, br_ref, h_ref, route_ref):
    i = pl.program_id(0)
    last = pl.num_programs(0) - 1
    tm = x_ref.shape[0]

    @pl.when(i < last)
    def _():
        h = _rms_rows(x_ref[...], g_ref[...])
        h_ref[...] = h.reshape(tm, *ROW_TILE)
        lane = lax.broadcasted_iota(jnp.int32, (tm, 128), 1).astype(F32)
        logits = _mm_f32(h, wr_ref[...]) + br_ref[...]
        m1 = jnp.max(logits, axis=-1, keepdims=True)
        i1 = jnp.min(jnp.where(logits == m1, lane, 128.0), axis=-1, keepdims=True)
        rest = jnp.where(lane == i1, NEG, logits)
        m2 = jnp.max(rest, axis=-1, keepdims=True)
        i2 = jnp.min(jnp.where(rest == m2, lane, 128.0), axis=-1, keepdims=True)
        e2 = jnp.exp(m2 - m1)
        route_ref[...] = (jnp.where(lane == 0.0, i1, 0.0) + jnp.where(lane == 1.0, i2, 0.0)
                          + jnp.where(lane == 2.0, 1.0 / (1.0 + e2), 0.0)
                          + jnp.where(lane == 3.0, e2 / (1.0 + e2), 0.0))

    @pl.when(i == last)
    def _():
        h_ref[...] = jnp.zeros(h_ref.shape, F32)
        route_ref[...] = jnp.zeros(route_ref.shape, F32)


def _gather_rows_kernel(tok_ref, h_hbm, o_ref, sem):
    tg = o_ref.shape[0]
    base = pl.program_id(0) * tg

    def issue(r, carry):
        pltpu.make_async_copy(h_hbm.at[tok_ref[base + r]], o_ref.at[r], sem).start()
        return carry

    lax.fori_loop(0, tg, issue, 0)
    pltpu.make_async_copy(h_hbm.at[pl.ds(0, tg)], o_ref, sem).wait()


def _expert_kernel(be_ref, x_ref, wg_ref, wu_ref, wd_ref, o_ref, xb_scr, acc_scr):
    f = pl.program_id(1)
    tm = x_ref.shape[0]

    @pl.when(f == 0)
    def _():
        xb_scr[...] = x_ref[...].reshape(tm, D_MODEL).astype(BF16)
        acc_scr[...] = jnp.zeros_like(acc_scr)

    xb = xb_scr[...]
    gate = jnp.dot(xb, wg_ref[0], preferred_element_type=F32)
    up = jnp.dot(xb, wu_ref[0], preferred_element_type=F32)
    acc_scr[...] += _mm(_silu(gate) * up, wd_ref[0])

    @pl.when(f == pl.num_programs(1) - 1)
    def _():
        o_ref[...] = acc_scr[...].reshape(tm, *ROW_TILE)


def _combine_kernel(pos_ref, x_ref, route_ref, y_hbm, *rest, with_final_norm):
    final_ref = rest[0] if with_final_norm else None
    o_ref, first_scr, second_scr, sems = rest[int(with_final_norm):]
    tc = x_ref.shape[0]
    base = pl.program_id(0) * (2 * tc)

    def issue(r, carry):
        pltpu.make_async_copy(y_hbm.at[pos_ref[base + 2 * r]], first_scr.at[r], sems.at[0]).start()
        pltpu.make_async_copy(y_hbm.at[pos_ref[base + 2 * r + 1]], second_scr.at[r], sems.at[1]).start()
        return carry

    lax.fori_loop(0, tc, issue, 0)
    pltpu.make_async_copy(y_hbm.at[pl.ds(0, tc)], first_scr, sems.at[0]).wait()
    pltpu.make_async_copy(y_hbm.at[pl.ds(0, tc)], second_scr, sems.at[1]).wait()
    route = route_ref[...]
    y = (x_ref[...] + route[:, 2:3] * first_scr[...].reshape(tc, D_MODEL)
         + route[:, 3:4] * second_scr[...].reshape(tc, D_MODEL))
    o_ref[...] = y if final_ref is None else _rms_rows(y, final_ref[...])


def _moe_routing(route, n, tm_rows):
    e_flat = route[:n, 0:TOP_K].astype(jnp.int32).reshape(-1)
    n_assign = n * TOP_K
    onehot = (e_flat[:, None] == jnp.arange(N_EXPERTS, dtype=jnp.int32)[None, :]).astype(jnp.int32)
    csum = jnp.cumsum(onehot, axis=0)
    rank = jnp.sum(csum * onehot, axis=1) - 1
    counts = csum[-1]
    padded = (counts + tm_rows - 1) // tm_rows * tm_rows
    pad_end = jnp.cumsum(padded)
    pad_start = pad_end - padded
    start = jnp.cumsum(counts) - counts
    dest = pad_start[e_flat] + rank
    n_blocks = -(-(n_assign + N_EXPERTS * (tm_rows - 1)) // tm_rows)
    blk_e = jnp.minimum(jnp.sum(jnp.arange(n_blocks, dtype=jnp.int32)[:, None] * tm_rows >= pad_end[None, :], axis=1),
                        N_EXPERTS - 1).astype(jnp.int32)
    order = jnp.argsort(e_flat, stable=True).astype(jnp.int32)
    rows = jnp.arange(n_blocks * tm_rows, dtype=jnp.int32)
    row_e = jnp.repeat(blk_e, tm_rows)
    offset = rows - pad_start[row_e]
    valid = (offset < counts[row_e]) & (rows < pad_end[N_EXPERTS - 1])
    src = jnp.clip(start[row_e] + offset, 0, n_assign - 1)
    row_tok = jnp.where(valid, order[src] // TOP_K, n).astype(jnp.int32)
    return row_tok, dest.astype(jnp.int32), blk_e, n_blocks


def _moe(x2, gain, w_router, b_router, w_gu_bf, w_down_bf, tm, final_gain=None):
    n = x2.shape[0]
    rows = min(MOE_ROWS, max(128, n * TOP_K // N_EXPERTS))
    wr = jnp.pad(w_router.astype(F32), ((0, 0), (0, 128 - N_EXPERTS)))
    br = jnp.pad(b_router.astype(F32), (0, 128 - N_EXPERTS), constant_values=NEG).reshape(1, 128)
    nt = n // tm
    h3, route = pl.pallas_call(
        _router_kernel,
        grid=(nt + 1,),
        in_specs=[pl.BlockSpec((tm, D_MODEL), lambda i: (jnp.minimum(i, nt - 1), 0)),
                  pl.BlockSpec((1, D_MODEL), lambda i: (0, 0)),
                  pl.BlockSpec((D_MODEL, 128), lambda i: (0, 0)),
                  pl.BlockSpec((1, 128), lambda i: (0, 0))],
        out_specs=[pl.BlockSpec((tm, *ROW_TILE), lambda i: (i, 0, 0)),
                   pl.BlockSpec((tm, 128), lambda i: (i, 0))],
        out_shape=[jax.ShapeDtypeStruct((n + tm, *ROW_TILE), F32),
                   jax.ShapeDtypeStruct((n + tm, 128), F32)],
        compiler_params=_cparams("arbitrary"),
        name="moe_router",
    )(x2, gain, wr, br)

    row_tok, dest, blk_e, n_blocks = _moe_routing(route, n, rows)
    n_rows = n_blocks * rows

    x_rows = pl.pallas_call(
        _gather_rows_kernel,
        grid_spec=pltpu.PrefetchScalarGridSpec(
            num_scalar_prefetch=1,
            grid=(n_blocks,),
            in_specs=[pl.BlockSpec(memory_space=pl.ANY)],
            out_specs=pl.BlockSpec((rows, *ROW_TILE), lambda i, tok: (i, 0, 0)),
            scratch_shapes=[pltpu.SemaphoreType.DMA(())]),
        out_shape=jax.ShapeDtypeStruct((n_rows, *ROW_TILE), F32),
        compiler_params=_cparams("arbitrary"),
        name="moe_gather",
    )(row_tok, h3)

    nf = D_FF_E // MOE_FF_TILE
    y_rows = pl.pallas_call(
        _expert_kernel,
        grid_spec=pltpu.PrefetchScalarGridSpec(
            num_scalar_prefetch=1,
            grid=(n_blocks, nf),
            in_specs=[pl.BlockSpec((rows, *ROW_TILE), lambda i, f, be: (i, 0, 0)),
                      pl.BlockSpec((1, D_MODEL, MOE_FF_TILE), lambda i, f, be: (be[i], 0, f)),
                      pl.BlockSpec((1, D_MODEL, MOE_FF_TILE), lambda i, f, be: (be[i], 0, nf + f)),
                      pl.BlockSpec((1, MOE_FF_TILE, D_MODEL), lambda i, f, be: (be[i], f, 0))],
            out_specs=pl.BlockSpec((rows, *ROW_TILE), lambda i, f, be: (i, 0, 0)),
            scratch_shapes=[pltpu.VMEM((rows, D_MODEL), BF16), pltpu.VMEM((rows, D_MODEL), F32)]),
        out_shape=jax.ShapeDtypeStruct((n_rows, *ROW_TILE), F32),
        compiler_params=_cparams("arbitrary", "arbitrary"),
        name="moe_experts",
    )(blk_e, x_rows, w_gu_bf, w_gu_bf, w_down_bf)

    final = [] if final_gain is None else [final_gain]
    return pl.pallas_call(
        functools.partial(_combine_kernel, with_final_norm=bool(final)),
        grid_spec=pltpu.PrefetchScalarGridSpec(
            num_scalar_prefetch=1,
            grid=(nt,),
            in_specs=[pl.BlockSpec((tm, D_MODEL), lambda i, pos: (i, 0)),
                      pl.BlockSpec((tm, 128), lambda i, pos: (i, 0)),
                      pl.BlockSpec(memory_space=pl.ANY)]
            + [pl.BlockSpec((1, D_MODEL), lambda i, pos: (0, 0))] * len(final),
            out_specs=pl.BlockSpec((tm, D_MODEL), lambda i, pos: (i, 0)),
            scratch_shapes=[pltpu.VMEM((tm, *ROW_TILE), F32), pltpu.VMEM((tm, *ROW_TILE), F32),
                            pltpu.SemaphoreType.DMA((2,))]),
        out_shape=jax.ShapeDtypeStruct((n, D_MODEL), F32),
        compiler_params=_cparams("arbitrary"),
        name="moe_combine",
    )(dest, x2, route, y_rows, *final)


def _norm_kernel(x_ref, g_ref, o_ref):
    o_ref[...] = _rms_rows(x_ref[...], g_ref[...])


def _final_norm(x2, gain, tm):
    n = x2.shape[0]
    return pl.pallas_call(
        _norm_kernel,
        grid=(n // tm,),
        in_specs=[pl.BlockSpec((tm, D_MODEL), lambda i: (i, 0)),
                  pl.BlockSpec((1, D_MODEL), lambda i: (0, 0))],
        out_specs=pl.BlockSpec((tm, D_MODEL), lambda i: (i, 0)),
        out_shape=jax.ShapeDtypeStruct((n, D_MODEL), F32),
        compiler_params=_cparams("parallel"),
        name="final_norm",
    )(x2, gain)


ROWS_T = 8
LANE_TILE = 128


def _sattn_kernel(qd_ref, qb_ref, kd_all, vd_all, kdn_all, vdn_all, kb_all, vb_all, kbn_all, vbn_all, sink_ref,
                  od_ref, ob_ref, *rolled_refs, t_new, here):
    kd_ref, vd_ref, kdn_ref, vdn_ref, kb_ref, vb_ref, kbn_ref, vbn_ref = (
        r.at[here] for r in (kd_all, vd_all, kdn_all, vdn_all, kb_all, vb_all, kbn_all, vbn_all))
    bb = qd_ref.shape[0]
    first_new = LANE_TILE - t_new

    def distances(n_cache):
        row = lax.broadcasted_iota(jnp.int32, (ROWS_T, n_cache), 0)
        col = lax.broadcasted_iota(jnp.int32, (ROWS_T, n_cache), 1)
        rown = lax.broadcasted_iota(jnp.int32, (ROWS_T, LANE_TILE), 0)
        j = lax.broadcasted_iota(jnp.int32, (ROWS_T, LANE_TILE), 1) - first_new
        d_new = rown - j
        return n_cache + row - col, d_new, (j >= 0) & (d_new >= 0)

    def softmax_parts(sc, sn, ok_c, ok_n, sink):
        sc = jnp.where(ok_c, sc, NEG)
        sn = jnp.where(ok_n, sn, NEG)
        m = jnp.maximum(jnp.max(sc, axis=-1, keepdims=True), jnp.max(sn, axis=-1, keepdims=True))
        if sink is not None:
            m = jnp.maximum(m, sink)
        pc = jnp.exp(sc - m)
        pn = jnp.exp(sn - m)
        l = jnp.sum(pc, axis=-1, keepdims=True) + jnp.sum(pn, axis=-1, keepdims=True)
        if sink is not None:
            l = l + jnp.exp(sink - m)
        return pc, pn, l, m + jnp.log(l)

    sl = lambda h: slice(h * HEAD_DIM, (h + 1) * HEAD_DIM)
    units = [(i, h) for i in range(bb) for h in range(N_HEADS)]
    dc, dn, ok_new = distances(kd_ref.shape[-1])
    qs = [qd_ref[i][:, sl(h)] * SCALE for i, h in units]
    sc = [_mm(q, kd_ref[i, h]) for q, (i, h) in zip(qs, units)]
    sn = [_mm(q, kdn_ref[i, h]) for q, (i, h) in zip(qs, units)]
    pcs, pns, ls, lses = [], [], [], []
    for r in DILATIONS:
        ok_c = (dc <= WIN * r) & ((dc & (r - 1)) == 0)
        ok_n = ok_new & ((dn & (r - 1)) == 0)
        parts = [softmax_parts(a, b, ok_c, ok_n, None) for a, b in zip(sc, sn)]
        pcs.append([p[0] for p in parts])
        pns.append([p[1] for p in parts])
        ls.append([p[2] for p in parts])
        lses.append([p[3] for p in parts])
    nbr = len(DILATIONS)
    acc = [_mm_nt(jnp.concatenate([pcs[r][u] for r in range(nbr)], axis=0), vd_ref[i, h])
           + _mm_nt(jnp.concatenate([pns[r][u] for r in range(nbr)], axis=0), vdn_ref[i, h])
           for u, (i, h) in enumerate(units)]
    outs = []
    for u in range(len(units)):
        m = functools.reduce(jnp.maximum, [lses[r][u] for r in range(nbr)])
        es = [jnp.exp(lses[r][u] - m) for r in range(nbr)]
        num = sum(es[r] * (acc[u][r * ROWS_T:(r + 1) * ROWS_T] / ls[r][u]) for r in range(nbr))
        outs.append(num / sum(es))
    for i in range(bb):
        od_ref[i] = jnp.concatenate(outs[i * N_HEADS:(i + 1) * N_HEADS], axis=-1)
    group = N_HEADS // KV_B
    dc, dn, ok_new = distances(kb_ref.shape[-1])
    qs = [qb_ref[i][:, sl(h)] * SCALE for i, h in units]
    parts = [softmax_parts(_mm(q, kb_ref[i, h // group]), _mm(q, kbn_ref[i, h // group]), dc <= WIN, ok_new,
                           sink_ref[:, h:h + 1]) for q, (i, h) in zip(qs, units)]
    outs = [(_mm_nt(p[0], vb_ref[i, h // group]) + _mm_nt(p[1], vbn_ref[i, h // group])) / p[2]
            for p, (i, h) in zip(parts, units)]
    for i in range(bb):
        ob_ref[i] = jnp.concatenate(outs[i * N_HEADS:(i + 1) * N_HEADS], axis=-1)
    lane = lax.broadcasted_iota(jnp.int32, (HEAD_DIM, LANE_TILE), 1)
    if not rolled_refs:
        return
    for src, new, dst in zip((kd_all, vd_all, kb_all, vb_all), (kdn_all, vdn_all, kbn_all, vbn_all), rolled_refs):
        n = src.shape[-1]
        for layer in range(src.shape[0]):
            for i in range(bb):
                for h in range(src.shape[2]):
                    rolled = pltpu.roll(src[layer, i, h], n - t_new, axis=1)
                    if n > LANE_TILE:
                        dst[layer, i, h, :, 0:n - LANE_TILE] = rolled[:, 0:n - LANE_TILE]
                    dst[layer, i, h, :, n - LANE_TILE:n] = jnp.where(lane >= first_new, new[layer, i, h],
                                                                     rolled[:, n - LANE_TILE:n])


def _pad_rows(t, rows):
    return jnp.pad(t, ((0, 0), (0, rows - t.shape[1]), (0, 0)))


def _new_columns(t, n_heads):
    bs, t_new, _ = t.shape
    x = jnp.transpose(t.reshape(bs, t_new, n_heads, HEAD_DIM), (0, 2, 3, 1))
    return jnp.pad(x, ((0, 0), (0, 0), (0, 0), (LANE_TILE - t_new, 0)))


def _sample_attention(layer, pb_s, pd_s, caches_t, new_cols, sinks):
    bs, t_new, _ = pb_s.shape
    depth = caches_t[0].shape[0]
    last = layer == depth - 1
    bb = 1 if last else 2
    pb8, pd8 = _pad_rows(pb_s, ROWS_T), _pad_rows(pd_s, ROWS_T)
    sink_row = jnp.pad(sinks.astype(F32), (0, 128 - N_HEADS)).reshape(1, 128)
    cbk, cbv, cdk, cdv = caches_t
    if last:
        news = [jnp.stack([layer_cols[k] for layer_cols in new_cols], axis=0) for k in range(4)]
    else:
        news = [new_cols[layer][k][None] for k in range(4)]
    n_slab = depth if last else 1
    qblk = lambda: pl.BlockSpec((bb, ROWS_T, W_MIX), lambda i: (i, 0, 0))
    slab = lambda c: pl.BlockSpec((n_slab, bb) + c.shape[2:], lambda i: (0 if last else layer, i, 0, 0, 0))
    newblk = lambda c: pl.BlockSpec((n_slab, bb) + c.shape[2:], lambda i: (0, i, 0, 0, 0))
    rolled_specs = [slab(c) for c in (cdk, cdv, cbk, cbv)] if last else []
    rolled_shapes = [jax.ShapeDtypeStruct(c.shape, c.dtype) for c in (cdk, cdv, cbk, cbv)] if last else []
    res = pl.pallas_call(
        functools.partial(_sattn_kernel, t_new=t_new, here=layer if last else 0),
        grid=(bs // bb,),
        in_specs=[qblk(), qblk(), slab(cdk), slab(cdv), newblk(news[0]), newblk(news[1]),
                  slab(cbk), slab(cbv), newblk(news[2]), newblk(news[3]),
                  pl.BlockSpec((1, 128), lambda i: (0, 0))],
        out_specs=[qblk(), qblk()] + rolled_specs,
        out_shape=[jax.ShapeDtypeStruct((bs, ROWS_T, W_MIX), F32)] * 2 + rolled_shapes,
        compiler_params=_cparams("arbitrary"),
        name="sample_attn",
    )(pd8, pb8, cdk, cdv, news[0], news[1], cbk, cbv, news[2], news[3], sink_row)
    od, ob = res[0], res[1]
    return (ob[:, :t_new].reshape(bs * t_new, W_MIX), od[:, :t_new].reshape(bs * t_new, W_MIX), tuple(res[2:]))


def _srec_kernel(xq_ref, xk_ref, xv_ref, bq_ref, bk_ref, bv_ref, cwq_ref, cwk_ref, cwv_ref, ga_ref,
                 ba_ref, hp_ref, gna_ref, sa_ref, qc_ref, fc_ref, ic_ref, gcg_ref, lb_ref, gnc_ref, sc_ref,
                 oa_ref, sa_out, oc_ref, sc_out, q_scr, k_scr, d_scr, *, t_new):
    nb = sa_ref.shape[-1]
    zero = jnp.zeros((HEAD_DIM, nb), F32)

    def conv(x_ref, b_ref, cw_ref, t):
        y = None
        for tap in range(CONV_A):
            pos = t + tap
            src = b_ref[pos] if pos < CONV_A - 1 else x_ref[pos - (CONV_A - 1)]
            term = cw_ref[tap] * src
            y = term if y is None else y + term
        return _silu(y)

    def l2(x):
        return x * lax.rsqrt(jnp.sum(x * x, axis=0, keepdims=True) + EPS)

    def gated_norm(o, gate, gain):
        return o * lax.rsqrt(jnp.mean(o * o, axis=0, keepdims=True) + EPS) * gain * _silu(gate)

    sa_out[0] = sa_ref[0]
    for t in range(t_new):
        q_scr[...] = l2(conv(xq_ref, bq_ref, cwq_ref, t)) * SCALE
        k_scr[...] = l2(conv(xk_ref, bk_ref, cwk_ref, t))
        v = conv(xv_ref, bv_ref, cwv_ref, t)
        beta = _sigmoid(ba_ref[0, t:t + 1, :])
        dec = jnp.exp(hp_ref[0, 0:1, :] * _softplus(ba_ref[0, t_new + t:t_new + t + 1, :] + hp_ref[0, 1:2, :]))

        def decay_and_read(kk, acc):
            s = sa_out[0, kk] * dec
            sa_out[0, kk] = s
            return acc + k_scr[pl.ds(kk, 1), :] * s

        err = (v - lax.fori_loop(0, HEAD_DIM, decay_and_read, zero)) * beta

        def write_and_query(kk, acc):
            s = sa_out[0, kk] + k_scr[pl.ds(kk, 1), :] * err
            sa_out[0, kk] = s
            return acc + q_scr[pl.ds(kk, 1), :] * s

        o = lax.fori_loop(0, HEAD_DIM, write_and_query, zero)
        oa_ref[t] = gated_norm(o, ga_ref[t], gna_ref[...])

    sc_out[0] = sc_ref[0]
    for t in range(t_new):
        lb = lb_ref[...]
        f = lb + (1.0 - lb) * _sigmoid(fc_ref[t])
        q_scr[...] = qc_ref[t]
        k_scr[...] = 1.0 - f
        d_scr[...] = jnp.exp(jnp.log(f))
        v = ic_ref[t]

        def update(kk, acc):
            s = sc_out[0, kk] * d_scr[pl.ds(kk, 1), :] + k_scr[pl.ds(kk, 1), :] * v
            sc_out[0, kk] = s
            return acc + q_scr[pl.ds(kk, 1), :] * s

        o = lax.fori_loop(0, HEAD_DIM, update, zero)
        oc_ref[t] = gated_norm(o, gcg_ref[t], gnc_ref[...])


def _sample_recurrences(pa_s, pba_s, pc_s, conv_buf, s_a, s_c, conv_w, a_log, dt_bias, norm_a, lb, norm_c):
    bs, t_new, _ = pa_s.shape
    lanes_last = lambda t: jnp.transpose(t, (1, 2, 0))
    pa_t = lanes_last(pa_s)
    pc_t = lanes_last(pc_s)
    buf_t = lanes_last(conv_buf.astype(F32))
    ba = jnp.transpose(pba_s[:, :, 0:8], (2, 1, 0))
    ba = jnp.concatenate([ba[0:N_HEADS], ba[N_HEADS:2 * N_HEADS]], axis=1)
    hp = jnp.stack([-jnp.exp(a_log.astype(F32)), dt_bias.astype(F32)], axis=1)
    hp = jnp.broadcast_to(hp[:, :, None], (N_HEADS, 2, bs))
    cw = conv_w.astype(F32)[:, :, None]
    sa_t = jnp.transpose(s_a.astype(F32), (1, 2, 3, 0))
    sc_t = jnp.transpose(s_c.astype(F32), (1, 2, 3, 0))
    col = lambda v: v.astype(F32).reshape(-1, 1)
    hd = HEAD_DIM
    feat = lambda rows, off: pl.BlockSpec((rows, hd, bs), lambda h: (0, off + h, 0))
    cwspec = lambda off: pl.BlockSpec((CONV_A, hd, 1), lambda h: (0, off + h, 0))
    per_head = lambda rows: pl.BlockSpec((1, rows, bs), lambda h: (h, 0, 0))
    state = pl.BlockSpec((1, hd, hd, bs), lambda h: (h, 0, 0, 0))
    vec = pl.BlockSpec((hd, 1), lambda h: (0, 0))
    nh = N_HEADS
    oa, sa_n, oc, sc_n = pl.pallas_call(
        functools.partial(_srec_kernel, t_new=t_new),
        grid=(N_HEADS,),
        in_specs=[feat(t_new, 0), feat(t_new, nh), feat(t_new, 2 * nh),
                  feat(CONV_A - 1, 0), feat(CONV_A - 1, nh), feat(CONV_A - 1, 2 * nh),
                  cwspec(0), cwspec(nh), cwspec(2 * nh),
                  feat(t_new, 3 * nh), per_head(2 * t_new), per_head(2), vec, state,
                  feat(t_new, 0), feat(t_new, nh), feat(t_new, 2 * nh), feat(t_new, 3 * nh),
                  pl.BlockSpec((hd, 1), lambda h: (h, 0)), vec, state],
        out_specs=[feat(t_new, 0), state, feat(t_new, 0), state],
        out_shape=[jax.ShapeDtypeStruct((t_new, W_MIX, bs), F32),
                   jax.ShapeDtypeStruct((N_HEADS, hd, hd, bs), F32),
                   jax.ShapeDtypeStruct((t_new, W_MIX, bs), F32),
                   jax.ShapeDtypeStruct((N_HEADS, hd, hd, bs), F32)],
        scratch_shapes=[pltpu.VMEM((hd, bs), F32)] * 3,
        compiler_params=_cparams("parallel"),
        name="sample_recurrences",
    )(pa_t, pa_t, pa_t, buf_t, buf_t, buf_t, cw, cw, cw, pa_t, ba, hp, col(norm_a), sa_t,
      pc_t, pc_t, pc_t, pc_t, col(lb), col(norm_c), sc_t)
    rows_first = lambda t: jnp.transpose(t, (2, 0, 1)).reshape(bs * t_new, W_MIX)
    back = lambda t: jnp.transpose(t, (3, 0, 1, 2))
    return rows_first(oa), back(sa_n), rows_first(oc), back(sc_n)


def _heads(t, n):
    return t.reshape(t.shape[0], t.shape[1], n, HEAD_DIM)


def _prompt_mixers(pa, pb, pc, pd, pba, pd_strided, batch, seq, conv_w, head_params, norm_a, sinks, lb, norm_c):
    pa3 = pa.reshape(batch, seq, COLS_A)
    pb3 = pb.reshape(batch, seq, COLS_B)
    pc3 = pc.reshape(batch, seq, COLS_C)
    pd3 = pd.reshape(batch, seq, COLS_D)
    oa, s_a = _gdn_prompt(pa3, pba.reshape(batch, seq, COLS_BA), conv_w, head_params, norm_a)
    oc, s_c = _hgrn_prompt(pc3, lb, norm_c)
    sink_row = jnp.pad(sinks.astype(F32), (0, 128 - N_HEADS)).reshape(1, 128)
    (ob,) = _band_attention(pb3[:, None], 0, 256, 384, KV_B, sink_row, False)
    ods, lses = [], []
    for p4 in (pd3[:, None], *pd_strided):
        o, lse = _band_attention(p4, 0, 256, 512, N_HEADS, None, True)
        ods.append(o)
        lses.append(lse)
    nb, nd = min(CACHE_B, seq), min(CACHE_D, seq)
    state = (_heads(pb3[:, seq - nb:, 256:384], KV_B), _heads(pb3[:, seq - nb:, 384:512], KV_B),
             _heads(pd3[:, seq - nd:, 256:512], N_HEADS), _heads(pd3[:, seq - nd:, 512:768], N_HEADS),
             pa3[:, seq - (CONV_A - 1):, 0:3 * W_MIX], s_a, s_c)
    return oa, ob.reshape(batch * seq, W_MIX), oc, ods, lses, state


def _sample_mixers(layer, pa, pb, pc, pd, pba, bs, t_new, caches_t, new_cols, states, conv_w, a_log, dt_bias,
                   norm_a, sinks, lb, norm_c):
    conv_buf, s_a, s_c = states
    pa3 = pa.reshape(bs, t_new, COLS_A)
    pb3 = pb.reshape(bs, t_new, COLS_B)
    pc3 = pc.reshape(bs, t_new, COLS_C)
    pd3 = pd.reshape(bs, t_new, COLS_D)
    new_cols.append((_new_columns(pd3[:, :, 256:512], N_HEADS), _new_columns(pd3[:, :, 512:768], N_HEADS),
                     _new_columns(pb3[:, :, 256:384], KV_B), _new_columns(pb3[:, :, 384:512], KV_B)))
    ob, od, rolled = _sample_attention(layer, pb3, pd3, caches_t, new_cols, sinks)
    oa, s_a_new, oc, s_c_new = _sample_recurrences(
        pa3, pba.reshape(bs, t_new, COLS_BA), pc3, conv_buf, s_a, s_c, conv_w, a_log, dt_bias, norm_a, lb, norm_c)
    conv_all = jnp.concatenate([conv_buf.astype(F32), pa3[:, :, 0:3 * W_MIX]], axis=1)
    state = (conv_all[:, -(CONV_A - 1):].astype(conv_buf.dtype), s_a_new.astype(s_a.dtype),
             s_c_new.astype(s_c.dtype))
    return oa, ob, oc, od, rolled, state


def kernel(x_prompt, x_sample, cache_b_k, cache_b_v, cache_d_k, cache_d_v, state_a_conv, state_a_s, state_c_s, norm_mix, w_in, conv_a, a_log, dt_bias, norm_a, sinks_b, lb_logits, norm_c, w_out, norm_ffn, w_ffn_gu, w_ffn_down, w_router, b_router, w_moe_gu, w_moe_down, norm_final):
    depth = w_in.shape[0]
    batch, seq, _ = x_prompt.shape
    bs, t_new, _ = x_sample.shape
    lb_p = jax.nn.softmax(lb_logits.astype(F32), axis=0)
    lower_bounds = jnp.cumsum(lb_p, axis=0) - lb_p[0]
    xp = x_prompt.reshape(batch * seq, D_MODEL)
    xs = x_sample.reshape(bs * t_new, D_MODEL)
    tm_p, tm_s = 512, 256
    row = lambda v: v.astype(F32).reshape(1, -1)
    prompt_states, sample_states = [], []
    rows_last = lambda c: jnp.transpose(c, (0, 1, 3, 4, 2))
    caches_t = tuple(rows_last(c) for c in (cache_b_k, cache_b_v, cache_d_k, cache_d_v))
    new_cols = []
    for l in range(depth):
        w_in_l = _permute_w_in(w_in[l])
        w_out_l = w_out[l].astype(BF16)
        conv_w = conv_a[l].astype(F32)
        head_params = _gdn_head_params(a_log[l], dt_bias[l])
        lb = lower_bounds[l]
        mix = (row(norm_a[l]), sinks_b[l], row(lb), row(norm_c[l]))

        projs = _inproj(xp, row(norm_mix[l]), w_in_l, tm_p, strided_for=(batch, seq))
        oa, ob, oc, ods, lses, st_p = _prompt_mixers(*projs[0:5], projs[5:], batch, seq, conv_w, head_params, *mix)
        xp = _outproj(xp, oa, ob, oc, ods, lses, w_out_l, tm_p)
        prompt_states.append(st_p)

        projs = _inproj(xs, row(norm_mix[l]), w_in_l, tm_s)
        oa, ob, oc, od, rolled, st_s = _sample_mixers(
            l, *projs, bs, t_new, caches_t, new_cols, (state_a_conv[l], state_a_s[l], state_c_s[l]), conv_w,
            a_log[l], dt_bias[l], norm_a[l], sinks_b[l], lb, norm_c[l])
        xs = _outproj(xs, oa, ob, oc, [od], [], w_out_l, tm_s)
        sample_states.append(st_s)

        if l % 2 == 0:
            w_gu = w_ffn_gu[l // 2].astype(BF16)
            w_dn = w_ffn_down[l // 2].astype(BF16)
            xp = _ffn(xp, row(norm_ffn[l]), w_gu, w_dn, 512, FFN_FF_TILE)
            xs = _ffn(xs, row(norm_ffn[l]), w_gu, w_dn, 512, FFN_FF_TILE)
        else:
            w_gu = w_moe_gu[l // 2].astype(BF16)
            w_dn = w_moe_down[l // 2].astype(BF16)
            final = row(norm_final) if l == depth - 1 else None
            xp = _moe(xp, row(norm_ffn[l]), w_router[l // 2], b_router[l // 2], w_gu, w_dn, 512, final)
            xs = _moe(xs, row(norm_ffn[l]), w_router[l // 2], b_router[l // 2], w_gu, w_dn, 512, final)
    if depth % 2 == 1:
        xp = _final_norm(xp, row(norm_final), 1024)
        xs = _final_norm(xs, row(norm_final), 512)
    y_prompt = xp.reshape(batch, seq, D_MODEL)
    y_sample = xs.reshape(bs, t_new, D_MODEL)
    stack = lambda states: [jnp.stack(t, 0) for t in zip(*states)]
    rows_back = lambda c, ref: jnp.transpose(c, (0, 1, 4, 2, 3)).astype(ref.dtype)
    d_k, d_v, b_k, b_v = rolled
    sample_caches = [rows_back(b_k, cache_b_k), rows_back(b_v, cache_b_v), rows_back(d_k, cache_d_k),
                     rows_back(d_v, cache_d_v)]
    return (y_prompt, y_sample, *stack(prompt_states), *sample_caches, *stack(sample_states))
```

```python
import functools
import math

import numpy as np
import jax
import jax.numpy as jnp
from jax import lax
from jax.experimental import pallas as pl
from jax.experimental.pallas import tpu as pltpu

F32 = jnp.float32
BF16 = jnp.bfloat16
HIGHEST = lax.Precision.HIGHEST

D_MODEL = 1024
HEAD_DIM = 64
N_HEADS = 4
KV_B = 2
W_MIX = N_HEADS * HEAD_DIM
CONV_A = 4
WIN = 128
DILATIONS = (1, 4, 16)
CACHE_D = 2048
CACHE_B = 128
D_FF = 2816
N_EXPERTS = 8
TOP_K = 2
D_FF_E = 3584
EPS = 1e-6
SCALE = HEAD_DIM ** -0.5
NEG = -1e30

COLS_A = 1024
COLS_B = 512
COLS_C = 1024
COLS_D = 768
COLS_BA = 128
COLS_ALL = COLS_A + COLS_B + COLS_C + COLS_D + COLS_BA

CHUNK = 64
BATCH_BLOCK = 4
FFN_FF_TILE = 1408
MOE_ROWS = 512
MOE_FF_TILE = 1792
VMEM_LIMIT = 56 * 1024 * 1024


def _cparams(*sem):
    return pltpu.CompilerParams(dimension_semantics=sem, vmem_limit_bytes=VMEM_LIMIT)


def _mm(a, b):
    return jnp.dot(a.astype(BF16), b.astype(BF16), preferred_element_type=F32)


def _mm_nt(a, b):
    return lax.dot_general(a.astype(BF16), b.astype(BF16), (((1,), (1,)), ((), ())),
                           preferred_element_type=F32)


def _mm_tn(a, b):
    return lax.dot_general(a.astype(BF16), b.astype(BF16), (((0,), (0,)), ((), ())),
                           preferred_element_type=F32)


def _split_bf16(a):
    hi = a.astype(BF16)
    return hi, (a - hi.astype(F32)).astype(BF16)


def _mm_3pass(a, b):
    ah, al = _split_bf16(a)
    bh, bl = _split_bf16(b)
    dot = functools.partial(jnp.dot, preferred_element_type=F32)
    return dot(ah, bh) + (dot(al, bh) + dot(ah, bl))


def _mm_f32(a, b):
    return jnp.dot(a, b, precision=HIGHEST, preferred_element_type=F32)


def _split3_bf16(a):
    hi = a.astype(BF16)
    rest = a - hi.astype(F32)
    mid = rest.astype(BF16)
    return hi, mid, (rest - mid.astype(F32)).astype(BF16)


def _mm_exact_lhs(sel, b, dims=(((1,), (0,)), ((), ()))):
    sel = sel.astype(BF16)
    return sum(lax.dot_general(sel, t, dims, preferred_element_type=F32) for t in _split3_bf16(b))


def _mm_exact_rhs(a, sel, dims=(((1,), (0,)), ((), ()))):
    sel = sel.astype(BF16)
    return sum(lax.dot_general(t, sel, dims, preferred_element_type=F32) for t in _split3_bf16(a))


def _sigmoid(x):
    return 1.0 / (1.0 + jnp.exp(-x))


def _silu(x):
    return x * _sigmoid(x)


def _softplus(x):
    return jnp.maximum(x, 0.0) + jnp.log(1.0 + jnp.exp(-jnp.abs(x)))


def _rms_rows(x, gain):
    return x * lax.rsqrt(jnp.mean(x * x, axis=-1, keepdims=True) + EPS) * gain


def _inproj_kernel(x_ref, g_ref, w_ref, oa_ref, ob_ref, oc_ref, od_ref, oba_ref, *strided):
    h = _rms_rows(x_ref[...], g_ref[...]).astype(BF16)
    c = 0
    for o_ref in (oa_ref, ob_ref, oc_ref, od_ref, oba_ref):
        n = o_ref.shape[1]
        o_ref[...] = jnp.dot(h, w_ref[:, c:c + n], preferred_element_type=F32)
        c += n
    if strided:
        *strided, chunk_scr = strided
        n_chunks = chunk_scr.shape[0]
        for ch in range(n_chunks):
            chunk_scr[ch] = od_ref[:, ch * 128:(ch + 1) * 128]
        for o_ref in strided:
            r, rows = o_ref.shape[1], o_ref.shape[2]
            for rho in range(r):
                for ch in range(n_chunks):
                    o_ref[0, rho, :, ch * 128:(ch + 1) * 128] = chunk_scr[ch, pl.ds(rho, rows, stride=r), :]


def _inproj(x2, gain, w_perm, tm, strided_for=None):
    n = x2.shape[0]
    widths = (COLS_A, COLS_B, COLS_C, COLS_D, COLS_BA)
    out_specs = [pl.BlockSpec((tm, w), lambda i: (i, 0)) for w in widths]
    out_shape = [jax.ShapeDtypeStruct((n, w), F32) for w in widths]
    scratch = []
    if strided_for is not None:
        batch, seq = strided_for
        per_seq = seq // tm
        for r in DILATIONS[1:]:
            out_specs.append(pl.BlockSpec((1, r, tm // r, COLS_D), lambda i: (i // per_seq, 0, i % per_seq, 0)))
            out_shape.append(jax.ShapeDtypeStruct((batch, r, seq // r, COLS_D), F32))
        scratch = [pltpu.VMEM((COLS_D // 128, tm, 128), F32)]
    return pl.pallas_call(
        _inproj_kernel,
        grid=(n // tm,),
        in_specs=[pl.BlockSpec((tm, D_MODEL), lambda i: (i, 0)),
                  pl.BlockSpec((1, D_MODEL), lambda i: (0, 0)),
                  pl.BlockSpec((D_MODEL, COLS_ALL), lambda i: (0, 0))],
        out_specs=out_specs,
        out_shape=out_shape,
        scratch_shapes=scratch,
        compiler_params=_cparams("parallel"),
        name="inproj",
    )(x2, gain, w_perm)


def _permute_w_in(w):
    ba = jnp.pad(w[:, 1024:1032], ((0, 0), (0, COLS_BA - 8)))
    return jnp.concatenate([w[:, 0:1024], w[:, 1032:3336], ba], axis=1).astype(BF16)


def _band_kernel(q_ref, kp_ref, kc_ref, vp_ref, vc_ref, sink_ref, *out_refs, kv, with_sink, with_lse):
    o_ref = out_refs[0]
    n = pl.program_id(2)
    nb = q_ref.shape[0]
    row = lax.broadcasted_iota(jnp.int32, (WIN, 2 * WIN), 0)
    col = lax.broadcasted_iota(jnp.int32, (WIN, 2 * WIN), 1)
    dist = row + WIN - col
    valid = (dist >= 0) & (dist <= WIN) & ((col >= WIN) | (n > 0))
    group = N_HEADS // kv
    qs = [q_ref[b] * SCALE for b in range(nb)]
    kcat = [jnp.concatenate([kp_ref[b], kc_ref[b]], axis=0) for b in range(nb)]
    vcat = [jnp.concatenate([vp_ref[b], vc_ref[b]], axis=0) for b in range(nb)]
    units = [(b, h) for b in range(nb) for h in range(N_HEADS)]
    sl = lambda h: slice(h * HEAD_DIM, (h + 1) * HEAD_DIM)
    s = [jnp.where(valid, _mm_nt(qs[b][:, sl(h)], kcat[b][:, sl(h // group)]), NEG) for b, h in units]
    m = [jnp.max(t, axis=-1, keepdims=True) for t in s]
    if with_sink:
        sink = [sink_ref[:, h:h + 1] for _, h in units]
        m = [jnp.maximum(a, b) for a, b in zip(m, sink)]
    p = [jnp.exp(t - a) for t, a in zip(s, m)]
    l = [jnp.sum(t, axis=-1, keepdims=True) for t in p]
    if with_sink:
        l = [a + jnp.exp(b - c) for a, b, c in zip(l, sink, m)]
    o = [_mm(t, vcat[b][:, sl(h // group)]) / a for t, a, (b, h) in zip(p, l, units)]
    for b in range(nb):
        o_ref[b] = jnp.concatenate(o[b * N_HEADS:(b + 1) * N_HEADS], axis=-1)
        if with_lse:
            lse = [jnp.broadcast_to(m[i] + jnp.log(l[i]), (WIN, HEAD_DIM)) for i in range(b * N_HEADS, (b + 1) * N_HEADS)]
            out_refs[1][b] = jnp.concatenate(lse, axis=-1)


def _band_attention(p4, q_col, k_col, v_col, kv, sinks, with_lse):
    batch, r, ln, width = p4.shape
    bb = BATCH_BLOCK if batch % BATCH_BLOCK == 0 else 1
    wq, wk = W_MIX, kv * HEAD_DIM
    qb, kb, vb = q_col // wq, k_col // wk, v_col // wk
    cur = lambda off: (lambda b, rho, n: (b, rho, n, off))
    prev = lambda off: (lambda b, rho, n: (b, rho, jnp.maximum(n - 1, 0), off))
    out_spec = pl.BlockSpec((bb, None, WIN, W_MIX), lambda b, rho, n: (b, rho, n, 0))
    out_shape = jax.ShapeDtypeStruct((batch, r, ln, W_MIX), F32)
    n_out = 2 if with_lse else 1
    return pl.pallas_call(
        functools.partial(_band_kernel, kv=kv, with_sink=sinks is not None, with_lse=with_lse),
        grid=(batch // bb, r, ln // WIN),
        in_specs=[pl.BlockSpec((bb, None, WIN, wq), cur(qb)),
                  pl.BlockSpec((bb, None, WIN, wk), prev(kb)),
                  pl.BlockSpec((bb, None, WIN, wk), cur(kb)),
                  pl.BlockSpec((bb, None, WIN, wk), prev(vb)),
                  pl.BlockSpec((bb, None, WIN, wk), cur(vb)),
                  pl.BlockSpec((1, 128), lambda b, rho, n: (0, 0))],
        out_specs=[out_spec] * n_out,
        out_shape=[out_shape] * n_out,
        compiler_params=_cparams("parallel", "parallel", "arbitrary"),
        name=f"band_r{r}",
    )(p4, p4, p4, p4, p4, sinks if sinks is not None else jnp.zeros((1, 128), F32))


def _gated_norm_rows(o, gate, gain):
    return _rms_rows(o, gain) * _silu(gate)


def _hgrn_constants(c):
    halves = []
    h = c // 2
    while h >= 1:
        halves.append(h)
        h //= 2
    t = np.arange(c)[:, None]
    u = np.arange(c)[None, :]
    mats = [(u <= t).astype(np.float32)]
    level = np.full((c, c), -1, np.int32)
    level[np.arange(c), np.arange(c)] = 0
    for li, h in enumerate(halves, 1):
        mid = (t // (2 * h)) * (2 * h) + h
        second = (t % (2 * h)) >= h
        mats.append(np.where(second, (u > mid) & (u <= t), (u > t) & (u <= mid)).astype(np.float32))
        pair = (t // (2 * h) == u // (2 * h)) & ((t % (2 * h)) >= h) & ((u % (2 * h)) < h)
        level[pair] = li
    return np.concatenate(mats, axis=0), level, len(halves)


def _hgrn_kernel(pc_ref, lb_ref, gain_ref, mat_ref, lvl_ref, o_ref, st_ref, s_scr, *, c, n_levels):
    j = pl.program_id(1)
    nb = pc_ref.shape[0]

    @pl.when(j == 0)
    def _():
        s_scr[...] = jnp.zeros_like(s_scr)

    lb = lb_ref[...]
    lvl = lvl_ref[...]
    xs = [pc_ref[b] for b in range(nb)]
    fs = [lb + (1.0 - lb) * _sigmoid(x[:, 256:512]) for x in xs]
    sums = [_mm_exact_lhs(mat_ref[...], -jnp.log(f)) for f in fs]
    gcs = [-s[0:c] for s in sums]
    g_last = [gc[c - 1:c, :] for gc in gcs]
    q_dec = [x[:, 0:256] * jnp.exp(gc) for x, gc in zip(xs, gcs)]
    k_dec = [(1.0 - f) * jnp.exp(gl - gc) for f, gl, gc in zip(fs, g_last, gcs)]
    units = [(b, h) for b in range(nb) for h in range(N_HEADS)]
    sl = lambda h: slice(h * HEAD_DIM, (h + 1) * HEAD_DIM)
    qh = [xs[b][:, sl(h)] for b, h in units]
    kh = [1.0 - fs[b][:, sl(h)] for b, h in units]
    vh = [xs[b][:, 512 + h * HEAD_DIM:512 + (h + 1) * HEAD_DIM] for b, h in units]
    a = [jnp.where(lvl == 0, _mm_nt(q, k), 0.0) for q, k in zip(qh, kh)]
    for li in range(1, n_levels + 1):
        damp = [jnp.exp(-sums[b][li * c:(li + 1) * c, sl(h)]) for b, h in units]
        part = [_mm_nt(q * d, k * d) for q, k, d in zip(qh, kh, damp)]
        a = [acc + jnp.where(lvl == li, p, 0.0) for acc, p in zip(a, part)]
    st = [s_scr[b, h] for b, h in units]
    o_inter = [_mm_nt(q_dec[b][:, sl(h)], t) for (b, h), t in zip(units, st)]
    o = [oi + _mm(aa, v) for oi, aa, v in zip(o_inter, a, vh)]
    upd = [_mm_tn(v, k_dec[b][:, sl(h)]) for (b, h), v in zip(units, vh)]
    for i, (b, h) in enumerate(units):
        s_scr[b, h] = jnp.exp(g_last[b][:, sl(h)]) * st[i] + upd[i]
    for b in range(nb):
        outs = [_gated_norm_rows(o[b * N_HEADS + h], xs[b][:, 768 + h * HEAD_DIM:768 + (h + 1) * HEAD_DIM],
                                 gain_ref[...]) for h in range(N_HEADS)]
        o_ref[b] = jnp.concatenate(outs, axis=-1)

    @pl.when(j == pl.num_programs(1) - 1)
    def _():
        st_ref[...] = s_scr[...]


def _hgrn_prompt(pc3, lb, gain):
    batch, seq, _ = pc3.shape
    c = CHUNK
    bb = BATCH_BLOCK if batch % BATCH_BLOCK == 0 else 1
    mat, level, n_levels = _hgrn_constants(c)
    o, st = pl.pallas_call(
        functools.partial(_hgrn_kernel, c=c, n_levels=n_levels),
        grid=(batch // bb, seq // c),
        in_specs=[pl.BlockSpec((bb, c, COLS_C), lambda b, j: (b, j, 0)),
                  pl.BlockSpec((1, W_MIX), lambda b, j: (0, 0)),
                  pl.BlockSpec((1, HEAD_DIM), lambda b, j: (0, 0)),
                  pl.BlockSpec(mat.shape, lambda b, j: (0, 0)),
                  pl.BlockSpec(level.shape, lambda b, j: (0, 0))],
        out_specs=[pl.BlockSpec((bb, c, W_MIX), lambda b, j: (b, j, 0)),
                   pl.BlockSpec((bb, N_HEADS, HEAD_DIM, HEAD_DIM), lambda b, j: (b, 0, 0, 0))],
        out_shape=[jax.ShapeDtypeStruct((batch, seq, W_MIX), F32),
                   jax.ShapeDtypeStruct((batch, N_HEADS, HEAD_DIM, HEAD_DIM), F32)],
        scratch_shapes=[pltpu.VMEM((bb, N_HEADS, HEAD_DIM, HEAD_DIM), F32)],
        compiler_params=_cparams("parallel", "arbitrary"),
        name="hgrn_prompt",
    )(pc3, lb, gain, jnp.asarray(mat), jnp.asarray(level))
    return o.reshape(batch * seq, W_MIX), jnp.swapaxes(st, -1, -2)


def _unit_lower_solve(lows, rhss, c):
    xs = [rhs - _mm_3pass(low, rhs) for low, rhs in zip(lows, rhss)]
    ps = lows
    span = 2
    while span < c:
        mm = _mm_3pass if span == 2 else _mm
        ps = [mm(p, p) for p in ps]
        xs = [x + mm(p, x) for p, x in zip(ps, xs)]
        span *= 2
    return xs


def _gdn_kernel(pa_ref, pba_ref, cw_ref, hp_ref, gain_ref, tri_ref, ones_ref, o_ref, st_ref,
                s_scr, buf_scr, *, c):
    j = pl.program_id(1)
    pad = 8
    nb = pa_ref.shape[0]

    @pl.when(j == 0)
    def _():
        s_scr[...] = jnp.zeros_like(s_scr)
        buf_scr[:, 0:pad, :] = jnp.zeros((nb, pad, 3 * W_MIX), F32)

    ones = ones_ref[...]
    row = lax.broadcasted_iota(jnp.int32, (c, c), 0)
    col = lax.broadcasted_iota(jnp.int32, (c, c), 1)
    qs, ks, vs, gates, betas, gcs, gcts = [], [], [], [], [], [], []
    for b in range(nb):
        x = pa_ref[b, :, 0:3 * W_MIX]
        gates.append(pa_ref[b, :, 3 * W_MIX:4 * W_MIX])
        buf_scr[b, pad:pad + c, :] = x
        y = cw_ref[CONV_A - 1:CONV_A, :] * x
        for tap in range(CONV_A - 1):
            back = CONV_A - 1 - tap
            y = y + cw_ref[tap:tap + 1, :] * buf_scr[b, pad - back:pad - back + c, :]
        buf_scr[b, 0:pad, :] = buf_scr[b, c:c + pad, :]
        y = _silu(y)
        qs.append(y[:, 0:W_MIX])
        ks.append(y[:, W_MIX:2 * W_MIX])
        vs.append(y[:, 2 * W_MIX:3 * W_MIX])
        ba = pba_ref[b]
        betas.append(_sigmoid(ba))
        g = hp_ref[0:1, :] * _softplus(ba + hp_ref[1:2, :])
        gcs.append(_mm_exact_lhs(tri_ref[0], g))
        gcts.append(_mm_exact_rhs(g, tri_ref[1], (((0,), (0,)), ((), ()))))
    qs = [q * lax.rsqrt(_mm_exact_rhs(q * q, ones) + EPS) * SCALE for q in qs]
    ks = [k * lax.rsqrt(_mm_exact_rhs(k * k, ones) + EPS) for k in ks]
    units = [(b, h) for b in range(nb) for h in range(N_HEADS)]
    sl = lambda h: slice(h * HEAD_DIM, (h + 1) * HEAD_DIM)
    qh = [qs[b][:, sl(h)] for b, h in units]
    kh = [ks[b][:, sl(h)] for b, h in units]
    vh = [vs[b][:, sl(h)] for b, h in units]
    b_col = [betas[b][:, h:h + 1] for b, h in units]
    g_col = [gcs[b][:, 4 + h:5 + h] for b, h in units]
    g_row = [gcts[b][4 + h:5 + h, :] for b, h in units]
    decay = [jnp.where(row >= col, jnp.exp(jnp.minimum(gc_ - gr_, 0.0)), 0.0) for gc_, gr_ in zip(g_col, g_row)]
    kk = [_mm_nt(k, k) for k in kh]
    qk = [_mm_nt(q, k) for q, k in zip(qh, kh)]
    low = [jnp.where(row > col, bc * a * d, 0.0) for bc, a, d in zip(b_col, kk, decay)]
    eg = [jnp.exp(gc_) for gc_ in g_col]
    rhs = [jnp.concatenate([v * bc, k * (bc * e)], axis=-1) for v, k, bc, e in zip(vh, kh, b_col, eg)]
    sol = _unit_lower_solve(low, rhs, c)
    intra = [a * d for a, d in zip(qk, decay)]
    g_last = [gc_[c - 1:c, :] for gc_ in g_col]
    st = [s_scr[b, h] for b, h in units]
    u = [s[:, 0:HEAD_DIM] - _mm_nt(s[:, HEAD_DIM:2 * HEAD_DIM], t) for s, t in zip(sol, st)]
    o_inter = [_mm_nt(q * e, t) for q, e, t in zip(qh, eg, st)]
    o = [oi + _mm(a, uu) for oi, a, uu in zip(o_inter, intra, u)]
    k_dec = [k * jnp.exp(gl - gc_) for k, gl, gc_ in zip(kh, g_last, g_col)]
    upd = [_mm_tn(uu, kd) for uu, kd in zip(u, k_dec)]
    for i, (b, h) in enumerate(units):
        s_scr[b, h] = jnp.exp(g_last[i]) * st[i] + upd[i]
    for b in range(nb):
        outs = [_gated_norm_rows(o[b * N_HEADS + h], gates[b][:, sl(h)], gain_ref[...]) for h in range(N_HEADS)]
        o_ref[b] = jnp.concatenate(outs, axis=-1)

    @pl.when(j == pl.num_programs(1) - 1)
    def _():
        st_ref[...] = s_scr[...]


def _gdn_prompt(pa3, pba3, conv_w, head_params, gain):
    batch, seq, _ = pa3.shape
    c = CHUNK
    bb = BATCH_BLOCK if batch % BATCH_BLOCK == 0 else 1
    lower = np.tril(np.ones((c, c), np.float32))
    tri = np.stack([lower, lower.T], axis=0)
    ones = np.kron(np.eye(N_HEADS, dtype=np.float32), np.ones((HEAD_DIM, HEAD_DIM), np.float32))
    o, st = pl.pallas_call(
        functools.partial(_gdn_kernel, c=c),
        grid=(batch // bb, seq // c),
        in_specs=[pl.BlockSpec((bb, c, COLS_A), lambda b, j: (b, j, 0)),
                  pl.BlockSpec((bb, c, COLS_BA), lambda b, j: (b, j, 0)),
                  pl.BlockSpec((CONV_A, 3 * W_MIX), lambda b, j: (0, 0)),
                  pl.BlockSpec((2, COLS_BA), lambda b, j: (0, 0)),
                  pl.BlockSpec((1, HEAD_DIM), lambda b, j: (0, 0)),
                  pl.BlockSpec((2, c, c), lambda b, j: (0, 0, 0)),
                  pl.BlockSpec((W_MIX, W_MIX), lambda b, j: (0, 0))],
        out_specs=[pl.BlockSpec((bb, c, W_MIX), lambda b, j: (b, j, 0)),
                   pl.BlockSpec((bb, N_HEADS, HEAD_DIM, HEAD_DIM), lambda b, j: (b, 0, 0, 0))],
        out_shape=[jax.ShapeDtypeStruct((batch, seq, W_MIX), F32),
                   jax.ShapeDtypeStruct((batch, N_HEADS, HEAD_DIM, HEAD_DIM), F32)],
        scratch_shapes=[pltpu.VMEM((bb, N_HEADS, HEAD_DIM, HEAD_DIM), F32),
                        pltpu.VMEM((bb, c + 8, 3 * W_MIX), F32)],
        compiler_params=_cparams("parallel", "arbitrary"),
        name="gdn_prompt",
    )(pa3, pba3, conv_w, head_params, gain, jnp.asarray(tri), jnp.asarray(ones))
    return o.reshape(batch * seq, W_MIX), jnp.swapaxes(st, -1, -2)


def _gdn_head_params(a_log, dt_bias):
    neg_a = jnp.pad(-jnp.exp(a_log.astype(F32)), (4, COLS_BA - 8))
    dtb = jnp.pad(dt_bias.astype(F32), (4, COLS_BA - 8))
    return jnp.stack([neg_a, dtb], axis=0)


def _outproj_kernel(*refs, n_branch):
    x_ref, oa_ref, ob_ref, oc_ref = refs[0:4]
    tm = x_ref.shape[0]
    if n_branch == 1:
        od = refs[4][...]
        w_ref, o_ref = refs[5], refs[6]
    else:
        d_refs = refs[4:4 + n_branch]
        l_refs = refs[4 + n_branch:4 + 2 * n_branch]
        w_ref, o_ref = refs[4 + 2 * n_branch], refs[5 + 2 * n_branch]
        scratch = refs[6 + 2 * n_branch:]

        def token_order(ref, scr):
            r, rows = ref.shape[1], ref.shape[2]
            if r == 1:
                return ref[0, 0]
            n_chunks = scr.shape[0]
            for rho in range(r):
                for ch in range(n_chunks):
                    scr[ch, pl.ds(rho, rows, stride=r), :] = ref[0, rho, :, ch * 128:(ch + 1) * 128]
            return jnp.concatenate([scr[ch] for ch in range(n_chunks)], axis=-1)

        outs = [token_order(ref, scratch[2 * i]) for i, ref in enumerate(d_refs)]
        lses = [token_order(ref, scratch[2 * i + 1]) for i, ref in enumerate(l_refs)]
        m = functools.reduce(jnp.maximum, lses)
        es = [jnp.exp(l - m) for l in lses]
        od = sum(e * o for e, o in zip(es, outs)) / sum(es)
    acc = x_ref[...]
    for i, part in enumerate((oa_ref[...], ob_ref[...], oc_ref[...], od)):
        acc = acc + _mm(part, w_ref[i * W_MIX:(i + 1) * W_MIX, :])
    o_ref[...] = acc


def _outproj(x2, oa, ob, oc, ods, lses, w_bf, tm):
    n = x2.shape[0]
    n_branch = len(ods)
    row = lambda w: pl.BlockSpec((tm, w), lambda i: (i, 0))
    specs = [row(D_MODEL)] + [row(W_MIX)] * 3
    scratch = []
    if n_branch == 1:
        specs.append(row(W_MIX))
    else:
        per_seq = (ods[0].shape[1] * ods[0].shape[2]) // tm
        for t in (*ods, *lses):
            r = t.shape[1]
            specs.append(pl.BlockSpec((1, r, tm // r, W_MIX), lambda i: (i // per_seq, 0, i % per_seq, 0)))
        scratch = [pltpu.VMEM((W_MIX // 128, tm, 128), F32)] * (2 * n_branch)
    specs.append(pl.BlockSpec((4 * W_MIX, D_MODEL), lambda i: (0, 0)))
    return pl.pallas_call(
        functools.partial(_outproj_kernel, n_branch=n_branch),
        grid=(n // tm,),
        in_specs=specs,
        out_specs=row(D_MODEL),
        out_shape=jax.ShapeDtypeStruct((n, D_MODEL), F32),
        scratch_shapes=scratch,
        compiler_params=_cparams("parallel"),
        name="outproj",
    )(x2, oa, ob, oc, *ods, *lses, w_bf)


def _ffn_kernel(x_ref, g_ref, wg_ref, wu_ref, wd_ref, o_ref, h_scr, acc_scr):
    f = pl.program_id(1)

    @pl.when(f == 0)
    def _():
        h_scr[...] = _rms_rows(x_ref[...], g_ref[...]).astype(BF16)
        acc_scr[...] = jnp.zeros_like(acc_scr)

    h = h_scr[...]
    gate = jnp.dot(h, wg_ref[...], preferred_element_type=F32)
    up = jnp.dot(h, wu_ref[...], preferred_element_type=F32)
    acc_scr[...] += _mm(_silu(gate) * up, wd_ref[...])

    @pl.when(f == pl.num_programs(1) - 1)
    def _():
        o_ref[...] = x_ref[...] + acc_scr[...]


def _ffn(x2, gain, w_gu_bf, w_down_bf, tm, tf):
    n = x2.shape[0]
    nf = D_FF // tf
    return pl.pallas_call(
        _ffn_kernel,
        grid=(n // tm, nf),
        in_specs=[pl.BlockSpec((tm, D_MODEL), lambda i, f: (i, 0)),
                  pl.BlockSpec((1, D_MODEL), lambda i, f: (0, 0)),
                  pl.BlockSpec((D_MODEL, tf), lambda i, f: (0, f)),
                  pl.BlockSpec((D_MODEL, tf), lambda i, f: (0, nf + f)),
                  pl.BlockSpec((tf, D_MODEL), lambda i, f: (f, 0))],
        out_specs=pl.BlockSpec((tm, D_MODEL), lambda i, f: (i, 0)),
        out_shape=jax.ShapeDtypeStruct((n, D_MODEL), F32),
        scratch_shapes=[pltpu.VMEM((tm, D_MODEL), BF16), pltpu.VMEM((tm, D_MODEL), F32)],
        compiler_params=_cparams("parallel", "arbitrary"),
        name="ffn",
    )(x2, gain, w_gu_bf, w_gu_bf, w_down_bf)


ROW_TILE = (8, 128)


def _router_kernel(x_ref, g_ref, wr_ref, br_ref, h_ref, route_ref):
    i = pl.program_id(0)
    last = pl.num_programs(0) - 1
    tm = x_ref.shape[0]

    @pl.when(i < last)
    def _():
        h = _rms_rows(x_ref[...], g_ref[...])
        h_ref[...] = h.reshape(tm, *ROW_TILE)
        lane = lax.broadcasted_iota(jnp.int32, (tm, 128), 1).astype(F32)
        logits = _mm_f32(h, wr_ref[...]) + br_ref[...]
        m1 = jnp.max(logits, axis=-1, keepdims=True)
        i1 = jnp.min(jnp.where(logits == m1, lane, 128.0), axis=-1, keepdims=True)
        rest = jnp.where(lane == i1, NEG, logits)
        m2 = jnp.max(rest, axis=-1, keepdims=True)
        i2 = jnp.min(jnp.where(rest == m2, lane, 128.0), axis=-1, keepdims=True)
        e2 = jnp.exp(m2 - m1)
        route_ref[...] = (jnp.where(lane == 0.0, i1, 0.0) + jnp.where(lane == 1.0, i2, 0.0)
                          + jnp.where(lane == 2.0, 1.0 / (1.0 + e2), 0.0)
                          + jnp.where(lane == 3.0, e2 / (1.0 + e2), 0.0))

    @pl.when(i == last)
    def _():
        h_ref[...] = jnp.zeros(h_ref.shape, F32)
        route_ref[...] = jnp.zeros(route_ref.shape, F32)


def _gather_rows_kernel(tok_ref, h_hbm, o_ref, sem):
    tg = o_ref.shape[0]
    base = pl.program_id(0) * tg

    def issue(pair, carry):
        for queue in range(2):
            r = 2 * pair + queue
            pltpu.async_copy(h_hbm.at[tok_ref[base + r]], o_ref.at[r], sem, priority=queue)
        return carry

    lax.fori_loop(0, tg // 2, issue, 0)
    pltpu.make_async_copy(h_hbm.at[pl.ds(0, tg)], o_ref, sem).wait()


def _expert_kernel(be_ref, x_ref, wg_ref, wu_ref, wd_ref, o_ref, xb_scr, acc_scr):
    f = pl.program_id(1)
    tm = x_ref.shape[0]

    @pl.when(f == 0)
    def _():
        xb_scr[...] = x_ref[...].reshape(tm, D_MODEL).astype(BF16)
        acc_scr[...] = jnp.zeros_like(acc_scr)

    xb = xb_scr[...]
    gate = jnp.dot(xb, wg_ref[0], preferred_element_type=F32)
    up = jnp.dot(xb, wu_ref[0], preferred_element_type=F32)
    acc_scr[...] += _mm(_silu(gate) * up, wd_ref[0])

    @pl.when(f == pl.num_programs(1) - 1)
    def _():
        o_ref[...] = acc_scr[...].reshape(tm, *ROW_TILE)


def _combine_kernel(pos_ref, x_ref, route_ref, y_hbm, *rest, with_final_norm):
    final_ref = rest[0] if with_final_norm else None
    o_ref, first_scr, second_scr, sems = rest[int(with_final_norm):]
    tc = x_ref.shape[0]
    base = pl.program_id(0) * (2 * tc)

    def issue(r, carry):
        pltpu.async_copy(y_hbm.at[pos_ref[base + 2 * r]], first_scr.at[r], sems.at[0], priority=0)
        pltpu.async_copy(y_hbm.at[pos_ref[base + 2 * r + 1]], second_scr.at[r], sems.at[1], priority=1)
        return carry

    lax.fori_loop(0, tc, issue, 0)
    pltpu.make_async_copy(y_hbm.at[pl.ds(0, tc)], first_scr, sems.at[0]).wait()
    pltpu.make_async_copy(y_hbm.at[pl.ds(0, tc)], second_scr, sems.at[1]).wait()
    route = route_ref[...]
    y = (x_ref[...] + route[:, 2:3] * first_scr[...].reshape(tc, D_MODEL)
         + route[:, 3:4] * second_scr[...].reshape(tc, D_MODEL))
    o_ref[...] = y if final_ref is None else _rms_rows(y, final_ref[...])


def _moe_routing(route, n, tm_rows):
    e_flat = route[:n, 0:TOP_K].astype(jnp.int32).reshape(-1)
    n_assign = n * TOP_K
    onehot = (e_flat[:, None] == jnp.arange(N_EXPERTS, dtype=jnp.int32)[None, :]).astype(jnp.int32)
    csum = jnp.cumsum(onehot, axis=0)
    rank = jnp.sum(csum * onehot, axis=1) - 1
    counts = csum[-1]
    padded = (counts + tm_rows - 1) // tm_rows * tm_rows
    pad_end = jnp.cumsum(padded)
    pad_start = pad_end - padded
    start = jnp.cumsum(counts) - counts
    dest = pad_start[e_flat] + rank
    n_blocks = -(-(n_assign + N_EXPERTS * (tm_rows - 1)) // tm_rows)
    blk_e = jnp.minimum(jnp.sum(jnp.arange(n_blocks, dtype=jnp.int32)[:, None] * tm_rows >= pad_end[None, :], axis=1),
                        N_EXPERTS - 1).astype(jnp.int32)
    order = jnp.argsort(e_flat, stable=True).astype(jnp.int32)
    rows = jnp.arange(n_blocks * tm_rows, dtype=jnp.int32)
    row_e = jnp.repeat(blk_e, tm_rows)
    offset = rows - pad_start[row_e]
    valid = (offset < counts[row_e]) & (rows < pad_end[N_EXPERTS - 1])
    src = jnp.clip(start[row_e] + offset, 0, n_assign - 1)
    row_tok = jnp.where(valid, order[src] // TOP_K, n).astype(jnp.int32)
    return row_tok, dest.astype(jnp.int32), blk_e, n_blocks


def _moe(x2, gain, w_router, b_router, w_gu_bf, w_down_bf, tm, final_gain=None):
    n = x2.shape[0]
    rows = min(MOE_ROWS, max(128, n * TOP_K // N_EXPERTS))
    wr = jnp.pad(w_router.astype(F32), ((0, 0), (0, 128 - N_EXPERTS)))
    br = jnp.pad(b_router.astype(F32), (0, 128 - N_EXPERTS), constant_values=NEG).reshape(1, 128)
    nt = n // tm
    h3, route = pl.pallas_call(
        _router_kernel,
        grid=(nt + 1,),
        in_specs=[pl.BlockSpec((tm, D_MODEL), lambda i: (jnp.minimum(i, nt - 1), 0)),
                  pl.BlockSpec((1, D_MODEL), lambda i: (0, 0)),
                  pl.BlockSpec((D_MODEL, 128), lambda i: (0, 0)),
                  pl.BlockSpec((1, 128), lambda i: (0, 0))],
        out_specs=[pl.BlockSpec((tm, *ROW_TILE), lambda i: (i, 0, 0)),
                   pl.BlockSpec((tm, 128), lambda i: (i, 0))],
        out_shape=[jax.ShapeDtypeStruct((n + tm, *ROW_TILE), F32),
                   jax.ShapeDtypeStruct((n + tm, 128), F32)],
        compiler_params=_cparams("arbitrary"),
        name="moe_router",
    )(x2, gain, wr, br)

    row_tok, dest, blk_e, n_blocks = _moe_routing(route, n, rows)
    n_rows = n_blocks * rows

    x_rows = pl.pallas_call(
        _gather_rows_kernel,
        grid_spec=pltpu.PrefetchScalarGridSpec(
            num_scalar_prefetch=1,
            grid=(n_blocks,),
            in_specs=[pl.BlockSpec(memory_space=pl.ANY)],
            out_specs=pl.BlockSpec((rows, *ROW_TILE), lambda i, tok: (i, 0, 0)),
            scratch_shapes=[pltpu.SemaphoreType.DMA(())]),
        out_shape=jax.ShapeDtypeStruct((n_rows, *ROW_TILE), F32),
        compiler_params=_cparams("arbitrary"),
        name="moe_gather",
    )(row_tok, h3)

    nf = D_FF_E // MOE_FF_TILE
    y_rows = pl.pallas_call(
        _expert_kernel,
        grid_spec=pltpu.PrefetchScalarGridSpec(
            num_scalar_prefetch=1,
            grid=(n_blocks, nf),
            in_specs=[pl.BlockSpec((rows, *ROW_TILE), lambda i, f, be: (i, 0, 0)),
                      pl.BlockSpec((1, D_MODEL, MOE_FF_TILE), lambda i, f, be: (be[i], 0, f)),
                      pl.BlockSpec((1, D_MODEL, MOE_FF_TILE), lambda i, f, be: (be[i], 0, nf + f)),
                      pl.BlockSpec((1, MOE_FF_TILE, D_MODEL), lambda i, f, be: (be[i], f, 0))],
            out_specs=pl.BlockSpec((rows, *ROW_TILE), lambda i, f, be: (i, 0, 0)),
            scratch_shapes=[pltpu.VMEM((rows, D_MODEL), BF16), pltpu.VMEM((rows, D_MODEL), F32)]),
        out_shape=jax.ShapeDtypeStruct((n_rows, *ROW_TILE), F32),
        compiler_params=_cparams("arbitrary", "arbitrary"),
        name="moe_experts",
    )(blk_e, x_rows, w_gu_bf, w_gu_bf, w_down_bf)

    final = [] if final_gain is None else [final_gain]
    return pl.pallas_call(
        functools.partial(_combine_kernel, with_final_norm=bool(final)),
        grid_spec=pltpu.PrefetchScalarGridSpec(
            num_scalar_prefetch=1,
            grid=(nt,),
            in_specs=[pl.BlockSpec((tm, D_MODEL), lambda i, pos: (i, 0)),
                      pl.BlockSpec((tm, 128), lambda i, pos: (i, 0)),
                      pl.BlockSpec(memory_space=pl.ANY)]
            + [pl.BlockSpec((1, D_MODEL), lambda i, pos: (0, 0))] * len(final),
            out_specs=pl.BlockSpec((tm, D_MODEL), lambda i, pos: (i, 0)),
            scratch_shapes=[pltpu.VMEM((tm, *ROW_TILE), F32), pltpu.VMEM((tm, *ROW_TILE), F32),
                            pltpu.SemaphoreType.DMA((2,))]),
        out_shape=jax.ShapeDtypeStruct((n, D_MODEL), F32),
        compiler_params=_cparams("arbitrary"),
        name="moe_combine",
    )(dest, x2, route, y_rows, *final)


def _norm_kernel(x_ref, g_ref, o_ref):
    o_ref[...] = _rms_rows(x_ref[...], g_ref[...])


def _final_norm(x2, gain, tm):
    n = x2.shape[0]
    return pl.pallas_call(
        _norm_kernel,
        grid=(n // tm,),
        in_specs=[pl.BlockSpec((tm, D_MODEL), lambda i: (i, 0)),
                  pl.BlockSpec((1, D_MODEL), lambda i: (0, 0))],
        out_specs=pl.BlockSpec((tm, D_MODEL), lambda i: (i, 0)),
        out_shape=jax.ShapeDtypeStruct((n, D_MODEL), F32),
        compiler_params=_cparams("parallel"),
        name="final_norm",
    )(x2, gain)


ROWS_T = 8
LANE_TILE = 128


def _sattn_kernel(qd_ref, qb_ref, kd_all, vd_all, kdn_all, vdn_all, kb_all, vb_all, kbn_all, vbn_all, sink_ref,
                  od_ref, ob_ref, *rolled_refs, t_new, here):
    kd_ref, vd_ref, kdn_ref, vdn_ref, kb_ref, vb_ref, kbn_ref, vbn_ref = (
        r.at[here] for r in (kd_all, vd_all, kdn_all, vdn_all, kb_all, vb_all, kbn_all, vbn_all))
    bb = qd_ref.shape[0]
    first_new = LANE_TILE - t_new

    def distances(n_cache):
        row = lax.broadcasted_iota(jnp.int32, (ROWS_T, n_cache), 0)
        col = lax.broadcasted_iota(jnp.int32, (ROWS_T, n_cache), 1)
        rown = lax.broadcasted_iota(jnp.int32, (ROWS_T, LANE_TILE), 0)
        j = lax.broadcasted_iota(jnp.int32, (ROWS_T, LANE_TILE), 1) - first_new
        d_new = rown - j
        return n_cache + row - col, d_new, (j >= 0) & (d_new >= 0)

    def softmax_parts(sc, sn, ok_c, ok_n, sink):
        sc = jnp.where(ok_c, sc, NEG)
        sn = jnp.where(ok_n, sn, NEG)
        m = jnp.maximum(jnp.max(sc, axis=-1, keepdims=True), jnp.max(sn, axis=-1, keepdims=True))
        if sink is not None:
            m = jnp.maximum(m, sink)
        pc = jnp.exp(sc - m)
        pn = jnp.exp(sn - m)
        l = jnp.sum(pc, axis=-1, keepdims=True) + jnp.sum(pn, axis=-1, keepdims=True)
        if sink is not None:
            l = l + jnp.exp(sink - m)
        return pc, pn, l, m + jnp.log(l)

    sl = lambda h: slice(h * HEAD_DIM, (h + 1) * HEAD_DIM)
    units = [(i, h) for i in range(bb) for h in range(N_HEADS)]
    dc, dn, ok_new = distances(kd_ref.shape[-1])
    qs = [qd_ref[i][:, sl(h)] * SCALE for i, h in units]
    sc = [_mm(q, kd_ref[i, h]) for q, (i, h) in zip(qs, units)]
    sn = [_mm(q, kdn_ref[i, h]) for q, (i, h) in zip(qs, units)]
    pcs, pns, ls, lses = [], [], [], []
    for r in DILATIONS:
        ok_c = (dc <= WIN * r) & ((dc & (r - 1)) == 0)
        ok_n = ok_new & ((dn & (r - 1)) == 0)
        parts = [softmax_parts(a, b, ok_c, ok_n, None) for a, b in zip(sc, sn)]
        pcs.append([p[0] for p in parts])
        pns.append([p[1] for p in parts])
        ls.append([p[2] for p in parts])
        lses.append([p[3] for p in parts])
    nbr = len(DILATIONS)
    acc = [_mm_nt(jnp.concatenate([pcs[r][u] for r in range(nbr)], axis=0), vd_ref[i, h])
           + _mm_nt(jnp.concatenate([pns[r][u] for r in range(nbr)], axis=0), vdn_ref[i, h])
           for u, (i, h) in enumerate(units)]
    outs = []
    for u in range(len(units)):
        m = functools.reduce(jnp.maximum, [lses[r][u] for r in range(nbr)])
        es = [jnp.exp(lses[r][u] - m) for r in range(nbr)]
        num = sum(es[r] * (acc[u][r * ROWS_T:(r + 1) * ROWS_T] / ls[r][u]) for r in range(nbr))
        outs.append(num / sum(es))
    for i in range(bb):
        od_ref[i] = jnp.concatenate(outs[i * N_HEADS:(i + 1) * N_HEADS], axis=-1)
    group = N_HEADS // KV_B
    dc, dn, ok_new = distances(kb_ref.shape[-1])
    qs = [qb_ref[i][:, sl(h)] * SCALE for i, h in units]
    parts = [softmax_parts(_mm(q, kb_ref[i, h // group]), _mm(q, kbn_ref[i, h // group]), dc <= WIN, ok_new,
                           sink_ref[:, h:h + 1]) for q, (i, h) in zip(qs, units)]
    outs = [(_mm_nt(p[0], vb_ref[i, h // group]) + _mm_nt(p[1], vbn_ref[i, h // group])) / p[2]
            for p, (i, h) in zip(parts, units)]
    for i in range(bb):
        ob_ref[i] = jnp.concatenate(outs[i * N_HEADS:(i + 1) * N_HEADS], axis=-1)
    lane = lax.broadcasted_iota(jnp.int32, (HEAD_DIM, LANE_TILE), 1)
    if not rolled_refs:
        return
    for src, new, dst in zip((kd_all, vd_all, kb_all, vb_all), (kdn_all, vdn_all, kbn_all, vbn_all), rolled_refs):
        n = src.shape[-1]
        for layer in range(src.shape[0]):
            for i in range(bb):
                for h in range(src.shape[2]):
                    rolled = pltpu.roll(src[layer, i, h], n - t_new, axis=1)
                    if n > LANE_TILE:
                        dst[layer, i, h, :, 0:n - LANE_TILE] = rolled[:, 0:n - LANE_TILE]
                    dst[layer, i, h, :, n - LANE_TILE:n] = jnp.where(lane >= first_new, new[layer, i, h],
                                                                     rolled[:, n - LANE_TILE:n])


def _pad_rows(t, rows):
    return jnp.pad(t, ((0, 0), (0, rows - t.shape[1]), (0, 0)))


def _new_columns(t, n_heads):
    bs, t_new, _ = t.shape
    x = jnp.transpose(t.reshape(bs, t_new, n_heads, HEAD_DIM), (0, 2, 3, 1))
    return jnp.pad(x, ((0, 0), (0, 0), (0, 0), (LANE_TILE - t_new, 0)))


def _sample_attention(layer, pb_s, pd_s, caches_t, new_cols, sinks):
    bs, t_new, _ = pb_s.shape
    depth = caches_t[0].shape[0]
    last = layer == depth - 1
    bb = 1 if last else 2
    pb8, pd8 = _pad_rows(pb_s, ROWS_T), _pad_rows(pd_s, ROWS_T)
    sink_row = jnp.pad(sinks.astype(F32), (0, 128 - N_HEADS)).reshape(1, 128)
    cbk, cbv, cdk, cdv = caches_t
    if last:
        news = [jnp.stack([layer_cols[k] for layer_cols in new_cols], axis=0) for k in range(4)]
    else:
        news = [new_cols[layer][k][None] for k in range(4)]
    n_slab = depth if last else 1
    qblk = lambda: pl.BlockSpec((bb, ROWS_T, W_MIX), lambda i: (i, 0, 0))
    slab = lambda c: pl.BlockSpec((n_slab, bb) + c.shape[2:], lambda i: (0 if last else layer, i, 0, 0, 0))
    newblk = lambda c: pl.BlockSpec((n_slab, bb) + c.shape[2:], lambda i: (0, i, 0, 0, 0))
    rolled_specs = [slab(c) for c in (cdk, cdv, cbk, cbv)] if last else []
    rolled_shapes = [jax.ShapeDtypeStruct(c.shape, c.dtype) for c in (cdk, cdv, cbk, cbv)] if last else []
    res = pl.pallas_call(
        functools.partial(_sattn_kernel, t_new=t_new, here=layer if last else 0),
        grid=(bs // bb,),
        in_specs=[qblk(), qblk(), slab(cdk), slab(cdv), newblk(news[0]), newblk(news[1]),
                  slab(cbk), slab(cbv), newblk(news[2]), newblk(news[3]),
                  pl.BlockSpec((1, 128), lambda i: (0, 0))],
        out_specs=[qblk(), qblk()] + rolled_specs,
        out_shape=[jax.ShapeDtypeStruct((bs, ROWS_T, W_MIX), F32)] * 2 + rolled_shapes,
        compiler_params=_cparams("arbitrary"),
        name="sample_attn",
    )(pd8, pb8, cdk, cdv, news[0], news[1], cbk, cbv, news[2], news[3], sink_row)
    od, ob = res[0], res[1]
    return (ob[:, :t_new].reshape(bs * t_new, W_MIX), od[:, :t_new].reshape(bs * t_new, W_MIX), tuple(res[2:]))


def _srec_kernel(xq_ref, xk_ref, xv_ref, bq_ref, bk_ref, bv_ref, cwq_ref, cwk_ref, cwv_ref, ga_ref,
                 ba_ref, hp_ref, gna_ref, sa_ref, qc_ref, fc_ref, ic_ref, gcg_ref, lb_ref, gnc_ref, sc_ref,
                 oa_ref, sa_out, oc_ref, sc_out, q_scr, k_scr, d_scr, *, t_new):
    nb = sa_ref.shape[-1]
    zero = jnp.zeros((HEAD_DIM, nb), F32)

    def conv(x_ref, b_ref, cw_ref, t):
        y = None
        for tap in range(CONV_A):
            pos = t + tap
            src = b_ref[pos] if pos < CONV_A - 1 else x_ref[pos - (CONV_A - 1)]
            term = cw_ref[tap] * src
            y = term if y is None else y + term
        return _silu(y)

    def l2(x):
        return x * lax.rsqrt(jnp.sum(x * x, axis=0, keepdims=True) + EPS)

    def gated_norm(o, gate, gain):
        return o * lax.rsqrt(jnp.mean(o * o, axis=0, keepdims=True) + EPS) * gain * _silu(gate)

    sa_out[0] = sa_ref[0]
    for t in range(t_new):
        q_scr[...] = l2(conv(xq_ref, bq_ref, cwq_ref, t)) * SCALE
        k_scr[...] = l2(conv(xk_ref, bk_ref, cwk_ref, t))
        v = conv(xv_ref, bv_ref, cwv_ref, t)
        beta = _sigmoid(ba_ref[0, t:t + 1, :])
        dec = jnp.exp(hp_ref[0, 0:1, :] * _softplus(ba_ref[0, t_new + t:t_new + t + 1, :] + hp_ref[0, 1:2, :]))

        def decay_and_read(kk, acc):
            s = sa_out[0, kk] * dec
            sa_out[0, kk] = s
            return acc + k_scr[pl.ds(kk, 1), :] * s

        err = (v - lax.fori_loop(0, HEAD_DIM, decay_and_read, zero)) * beta

        def write_and_query(kk, acc):
            s = sa_out[0, kk] + k_scr[pl.ds(kk, 1), :] * err
            sa_out[0, kk] = s
            return acc + q_scr[pl.ds(kk, 1), :] * s

        o = lax.fori_loop(0, HEAD_DIM, write_and_query, zero)
        oa_ref[t] = gated_norm(o, ga_ref[t], gna_ref[...])

    sc_out[0] = sc_ref[0]
    for t in range(t_new):
        lb = lb_ref[...]
        f = lb + (1.0 - lb) * _sigmoid(fc_ref[t])
        q_scr[...] = qc_ref[t]
        k_scr[...] = 1.0 - f
        d_scr[...] = jnp.exp(jnp.log(f))
        v = ic_ref[t]

        def update(kk, acc):
            s = sc_out[0, kk] * d_scr[pl.ds(kk, 1), :] + k_scr[pl.ds(kk, 1), :] * v
            sc_out[0, kk] = s
            return acc + q_scr[pl.ds(kk, 1), :] * s

        o = lax.fori_loop(0, HEAD_DIM, update, zero)
        oc_ref[t] = gated_norm(o, gcg_ref[t], gnc_ref[...])


def _sample_recurrences(pa_s, pba_s, pc_s, conv_buf, s_a, s_c, conv_w, a_log, dt_bias, norm_a, lb, norm_c):
    bs, t_new, _ = pa_s.shape
    lanes_last = lambda t: jnp.transpose(t, (1, 2, 0))
    pa_t = lanes_last(pa_s)
    pc_t = lanes_last(pc_s)
    buf_t = lanes_last(conv_buf.astype(F32))
    ba = jnp.transpose(pba_s[:, :, 0:8], (2, 1, 0))
    ba = jnp.concatenate([ba[0:N_HEADS], ba[N_HEADS:2 * N_HEADS]], axis=1)
    hp = jnp.stack([-jnp.exp(a_log.astype(F32)), dt_bias.astype(F32)], axis=1)
    hp = jnp.broadcast_to(hp[:, :, None], (N_HEADS, 2, bs))
    cw = conv_w.astype(F32)[:, :, None]
    sa_t = jnp.transpose(s_a.astype(F32), (1, 2, 3, 0))
    sc_t = jnp.transpose(s_c.astype(F32), (1, 2, 3, 0))
    col = lambda v: v.astype(F32).reshape(-1, 1)
    hd = HEAD_DIM
    feat = lambda rows, off: pl.BlockSpec((rows, hd, bs), lambda h: (0, off + h, 0))
    cwspec = lambda off: pl.BlockSpec((CONV_A, hd, 1), lambda h: (0, off + h, 0))
    per_head = lambda rows: pl.BlockSpec((1, rows, bs), lambda h: (h, 0, 0))
    state = pl.BlockSpec((1, hd, hd, bs), lambda h: (h, 0, 0, 0))
    vec = pl.BlockSpec((hd, 1), lambda h: (0, 0))
    nh = N_HEADS
    oa, sa_n, oc, sc_n = pl.pallas_call(
        functools.partial(_srec_kernel, t_new=t_new),
        grid=(N_HEADS,),
        in_specs=[feat(t_new, 0), feat(t_new, nh), feat(t_new, 2 * nh),
                  feat(CONV_A - 1, 0), feat(CONV_A - 1, nh), feat(CONV_A - 1, 2 * nh),
                  cwspec(0), cwspec(nh), cwspec(2 * nh),
                  feat(t_new, 3 * nh), per_head(2 * t_new), per_head(2), vec, state,
                  feat(t_new, 0), feat(t_new, nh), feat(t_new, 2 * nh), feat(t_new, 3 * nh),
                  pl.BlockSpec((hd, 1), lambda h: (h, 0)), vec, state],
        out_specs=[feat(t_new, 0), state, feat(t_new, 0), state],
        out_shape=[jax.ShapeDtypeStruct((t_new, W_MIX, bs), F32),
                   jax.ShapeDtypeStruct((N_HEADS, hd, hd, bs), F32),
                   jax.ShapeDtypeStruct((t_new, W_MIX, bs), F32),
                   jax.ShapeDtypeStruct((N_HEADS, hd, hd, bs), F32)],
        scratch_shapes=[pltpu.VMEM((hd, bs), F32)] * 3,
        compiler_params=_cparams("parallel"),
        name="sample_recurrences",
    )(pa_t, pa_t, pa_t, buf_t, buf_t, buf_t, cw, cw, cw, pa_t, ba, hp, col(norm_a), sa_t,
      pc_t, pc_t, pc_t, pc_t, col(lb), col(norm_c), sc_t)
    rows_first = lambda t: jnp.transpose(t, (2, 0, 1)).reshape(bs * t_new, W_MIX)
    back = lambda t: jnp.transpose(t, (3, 0, 1, 2))
    return rows_first(oa), back(sa_n), rows_first(oc), back(sc_n)


def _heads(t, n):
    return t.reshape(t.shape[0], t.shape[1], n, HEAD_DIM)


def _prompt_mixers(pa, pb, pc, pd, pba, pd_strided, batch, seq, conv_w, head_params, norm_a, sinks, lb, norm_c):
    pa3 = pa.reshape(batch, seq, COLS_A)
    pb3 = pb.reshape(batch, seq, COLS_B)
    pc3 = pc.reshape(batch, seq, COLS_C)
    pd3 = pd.reshape(batch, seq, COLS_D)
    oa, s_a = _gdn_prompt(pa3, pba.reshape(batch, seq, COLS_BA), conv_w, head_params, norm_a)
    oc, s_c = _hgrn_prompt(pc3, lb, norm_c)
    sink_row = jnp.pad(sinks.astype(F32), (0, 128 - N_HEADS)).reshape(1, 128)
    (ob,) = _band_attention(pb3[:, None], 0, 256, 384, KV_B, sink_row, False)
    ods, lses = [], []
    for p4 in (pd3[:, None], *pd_strided):
        o, lse = _band_attention(p4, 0, 256, 512, N_HEADS, None, True)
        ods.append(o)
        lses.append(lse)
    nb, nd = min(CACHE_B, seq), min(CACHE_D, seq)
    state = (_heads(pb3[:, seq - nb:, 256:384], KV_B), _heads(pb3[:, seq - nb:, 384:512], KV_B),
             _heads(pd3[:, seq - nd:, 256:512], N_HEADS), _heads(pd3[:, seq - nd:, 512:768], N_HEADS),
             pa3[:, seq - (CONV_A - 1):, 0:3 * W_MIX], s_a, s_c)
    return oa, ob.reshape(batch * seq, W_MIX), oc, ods, lses, state


def _sample_mixers(layer, pa, pb, pc, pd, pba, bs, t_new, caches_t, new_cols, states, conv_w, a_log, dt_bias,
                   norm_a, sinks, lb, norm_c):
    conv_buf, s_a, s_c = states
    pa3 = pa.reshape(bs, t_new, COLS_A)
    pb3 = pb.reshape(bs, t_new, COLS_B)
    pc3 = pc.reshape(bs, t_new, COLS_C)
    pd3 = pd.reshape(bs, t_new, COLS_D)
    new_cols.append((_new_columns(pd3[:, :, 256:512], N_HEADS), _new_columns(pd3[:, :, 512:768], N_HEADS),
                     _new_columns(pb3[:, :, 256:384], KV_B), _new_columns(pb3[:, :, 384:512], KV_B)))
    ob, od, rolled = _sample_attention(layer, pb3, pd3, caches_t, new_cols, sinks)
    oa, s_a_new, oc, s_c_new = _sample_recurrences(
        pa3, pba.reshape(bs, t_new, COLS_BA), pc3, conv_buf, s_a, s_c, conv_w, a_log, dt_bias, norm_a, lb, norm_c)
    conv_all = jnp.concatenate([conv_buf.astype(F32), pa3[:, :, 0:3 * W_MIX]], axis=1)
    state = (conv_all[:, -(CONV_A - 1):].astype(conv_buf.dtype), s_a_new.astype(s_a.dtype),
             s_c_new.astype(s_c.dtype))
    return oa, ob, oc, od, rolled, state


def kernel(x_prompt, x_sample, cache_b_k, cache_b_v, cache_d_k, cache_d_v, state_a_conv, state_a_s, state_c_s, norm_mix, w_in, conv_a, a_log, dt_bias, norm_a, sinks_b, lb_logits, norm_c, w_out, norm_ffn, w_ffn_gu, w_ffn_down, w_router, b_router, w_moe_gu, w_moe_down, norm_final):
    depth = w_in.shape[0]
    batch, seq, _ = x_prompt.shape
    bs, t_new, _ = x_sample.shape
    lb_p = jax.nn.softmax(lb_logits.astype(F32), axis=0)
    lower_bounds = jnp.cumsum(lb_p, axis=0) - lb_p[0]
    xp = x_prompt.reshape(batch * seq, D_MODEL)
    xs = x_sample.reshape(bs * t_new, D_MODEL)
    tm_p, tm_s = 512, 256
    row = lambda v: v.astype(F32).reshape(1, -1)
    prompt_states, sample_states = [], []
    rows_last = lambda c: jnp.transpose(c, (0, 1, 3, 4, 2))
    caches_t = tuple(rows_last(c) for c in (cache_b_k, cache_b_v, cache_d_k, cache_d_v))
    new_cols = []
    for l in range(depth):
        w_in_l = _permute_w_in(w_in[l])
        w_out_l = w_out[l].astype(BF16)
        conv_w = conv_a[l].astype(F32)
        head_params = _gdn_head_params(a_log[l], dt_bias[l])
        lb = lower_bounds[l]
        mix = (row(norm_a[l]), sinks_b[l], row(lb), row(norm_c[l]))

        projs = _inproj(xp, row(norm_mix[l]), w_in_l, tm_p, strided_for=(batch, seq))
        oa, ob, oc, ods, lses, st_p = _prompt_mixers(*projs[0:5], projs[5:], batch, seq, conv_w, head_params, *mix)
        xp = _outproj(xp, oa, ob, oc, ods, lses, w_out_l, tm_p)
        prompt_states.append(st_p)

        projs = _inproj(xs, row(norm_mix[l]), w_in_l, tm_s)
        oa, ob, oc, od, rolled, st_s = _sample_mixers(
            l, *projs, bs, t_new, caches_t, new_cols, (state_a_conv[l], state_a_s[l], state_c_s[l]), conv_w,
            a_log[l], dt_bias[l], norm_a[l], sinks_b[l], lb, norm_c[l])
        xs = _outproj(xs, oa, ob, oc, [od], [], w_out_l, tm_s)
        sample_states.append(st_s)

        if l % 2 == 0:
            w_gu = w_ffn_gu[l // 2].astype(BF16)
            w_dn = w_ffn_down[l // 2].astype(BF16)
            xp = _ffn(xp, row(norm_ffn[l]), w_gu, w_dn, 512, FFN_FF_TILE)
            xs = _ffn(xs, row(norm_ffn[l]), w_gu, w_dn, 512, FFN_FF_TILE)
        else:
            w_gu = w_moe_gu[l // 2].astype(BF16)
            w_dn = w_moe_down[l // 2].astype(BF16)
            final = row(norm_final) if l == depth - 1 else None
            xp = _moe(xp, row(norm_ffn[l]), w_router[l // 2], b_router[l // 2], w_gu, w_dn, 512, final)
            xs = _moe(xs, row(norm_ffn[l]), w_router[l // 2], b_router[l // 2], w_gu, w_dn, 512, final)
    if depth % 2 == 1:
        xp = _final_norm(xp, row(norm_final), 1024)
        xs = _final_norm(xs, row(norm_final), 512)
    y_prompt = xp.reshape(batch, seq, D_MODEL)
    y_sample = xs.reshape(bs, t_new, D_MODEL)
    stack = lambda states: [jnp.stack(t, 0) for t in zip(*states)]
    rows_back = lambda c, ref: jnp.transpose(c, (0, 1, 4, 2, 3)).astype(ref.dtype)
    d_k, d_v, b_k, b_v = rolled
    sample_caches = [rows_back(b_k, cache_b_k), rows_back(b_v, cache_b_v), rows_back(d_k, cache_d_k),
                     rows_back(d_v, cache_d_v)]
    return (y_prompt, y_sample, *stack(prompt_states), *sample_caches, *stack(sample_states))
```

```python
import functools
import math

import numpy as np
import jax
import jax.numpy as jnp
from jax import lax
from jax.experimental import pallas as pl
from jax.experimental.pallas import tpu as pltpu

F32 = jnp.float32
BF16 = jnp.bfloat16
HIGHEST = lax.Precision.HIGHEST

D_MODEL = 1024
HEAD_DIM = 64
N_HEADS = 4
KV_B = 2
W_MIX = N_HEADS * HEAD_DIM
CONV_A = 4
WIN = 128
DILATIONS = (1, 4, 16)
CACHE_D = 2048
CACHE_B = 128
D_FF = 2816
N_EXPERTS = 8
TOP_K = 2
D_FF_E = 3584
EPS = 1e-6
SCALE = HEAD_DIM ** -0.5
NEG = -1e30

COLS_A = 1024
COLS_B = 512
COLS_C = 1024
COLS_D = 768
COLS_BA = 128
COLS_ALL = COLS_A + COLS_B + COLS_C + COLS_D + COLS_BA

CHUNK = 64
BATCH_BLOCK = 4
FFN_FF_TILE = 1408
MOE_ROWS = 512
MOE_FF_TILE = 1792
VMEM_LIMIT = 56 * 1024 * 1024


def _cparams(*sem):
    return pltpu.CompilerParams(dimension_semantics=sem, vmem_limit_bytes=VMEM_LIMIT)


def _mm(a, b):
    return jnp.dot(a.astype(BF16), b.astype(BF16), preferred_element_type=F32)


def _mm_nt(a, b):
    return lax.dot_general(a.astype(BF16), b.astype(BF16), (((1,), (1,)), ((), ())),
                           preferred_element_type=F32)


def _mm_tn(a, b):
    return lax.dot_general(a.astype(BF16), b.astype(BF16), (((0,), (0,)), ((), ())),
                           preferred_element_type=F32)


def _split_bf16(a):
    hi = a.astype(BF16)
    return hi, (a - hi.astype(F32)).astype(BF16)


def _mm_3pass(a, b):
    ah, al = _split_bf16(a)
    bh, bl = _split_bf16(b)
    dot = functools.partial(jnp.dot, preferred_element_type=F32)
    return dot(ah, bh) + (dot(al, bh) + dot(ah, bl))


def _mm_f32(a, b):
    return jnp.dot(a, b, precision=HIGHEST, preferred_element_type=F32)


def _split3_bf16(a):
    hi = a.astype(BF16)
    rest = a - hi.astype(F32)
    mid = rest.astype(BF16)
    return hi, mid, (rest - mid.astype(F32)).astype(BF16)


def _mm_exact_lhs(sel, b, dims=(((1,), (0,)), ((), ()))):
    sel = sel.astype(BF16)
    return sum(lax.dot_general(sel, t, dims, preferred_element_type=F32) for t in _split3_bf16(b))


def _mm_exact_rhs(a, sel, dims=(((1,), (0,)), ((), ()))):
    sel = sel.astype(BF16)
    return sum(lax.dot_general(t, sel, dims, preferred_element_type=F32) for t in _split3_bf16(a))


def _sigmoid(x):
    return 1.0 / (1.0 + jnp.exp(-x))


def _silu(x):
    return x * _sigmoid(x)


def _softplus(x):
    return jnp.maximum(x, 0.0) + jnp.log(1.0 + jnp.exp(-jnp.abs(x)))


def _rms_rows(x, gain):
    return x * lax.rsqrt(jnp.mean(x * x, axis=-1, keepdims=True) + EPS) * gain


def _inproj_kernel(x_ref, g_ref, w_ref, oa_ref, ob_ref, oc_ref, od_ref, oba_ref, *strided):
    h = _rms_rows(x_ref[...], g_ref[...]).astype(BF16)
    c = 0
    for o_ref in (oa_ref, ob_ref, oc_ref, od_ref, oba_ref):
        n = o_ref.shape[1]
        o_ref[...] = jnp.dot(h, w_ref[:, c:c + n], preferred_element_type=F32)
        c += n
    if strided:
        *strided, chunk_scr = strided
        n_chunks = chunk_scr.shape[0]
        for ch in range(n_chunks):
            chunk_scr[ch] = od_ref[:, ch * 128:(ch + 1) * 128]
        for o_ref in strided:
            r, rows = o_ref.shape[1], o_ref.shape[2]
            for rho in range(r):
                for ch in range(n_chunks):
                    o_ref[0, rho, :, ch * 128:(ch + 1) * 128] = chunk_scr[ch, pl.ds(rho, rows, stride=r), :]


def _inproj(x2, gain, w_perm, tm, strided_for=None):
    n = x2.shape[0]
    widths = (COLS_A, COLS_B, COLS_C, COLS_D, COLS_BA)
    out_specs = [pl.BlockSpec((tm, w), lambda i: (i, 0)) for w in widths]
    out_shape = [jax.ShapeDtypeStruct((n, w), F32) for w in widths]
    scratch = []
    if strided_for is not None:
        batch, seq = strided_for
        per_seq = seq // tm
        for r in DILATIONS[1:]:
            out_specs.append(pl.BlockSpec((1, r, tm // r, COLS_D), lambda i: (i // per_seq, 0, i % per_seq, 0)))
            out_shape.append(jax.ShapeDtypeStruct((batch, r, seq // r, COLS_D), F32))
        scratch = [pltpu.VMEM((COLS_D // 128, tm, 128), F32)]
    return pl.pallas_call(
        _inproj_kernel,
        grid=(n // tm,),
        in_specs=[pl.BlockSpec((tm, D_MODEL), lambda i: (i, 0)),
                  pl.BlockSpec((1, D_MODEL), lambda i: (0, 0)),
                  pl.BlockSpec((D_MODEL, COLS_ALL), lambda i: (0, 0))],
        out_specs=out_specs,
        out_shape=out_shape,
        scratch_shapes=scratch,
        compiler_params=_cparams("parallel"),
        name="inproj",
    )(x2, gain, w_perm)


def _permute_w_in(w):
    ba = jnp.pad(w[:, 1024:1032], ((0, 0), (0, COLS_BA - 8)))
    return jnp.concatenate([w[:, 0:1024], w[:, 1032:3336], ba], axis=1).astype(BF16)


def _band_kernel(q_ref, kp_ref, kc_ref, vp_ref, vc_ref, sink_ref, *out_refs, kv, with_sink, with_lse):
    o_ref = out_refs[0]
    n = pl.program_id(2)
    nb = q_ref.shape[0]
    row = lax.broadcasted_iota(jnp.int32, (WIN, 2 * WIN), 0)
    col = lax.broadcasted_iota(jnp.int32, (WIN, 2 * WIN), 1)
    dist = row + WIN - col
    valid = (dist >= 0) & (dist <= WIN) & ((col >= WIN) | (n > 0))
    group = N_HEADS // kv
    qs = [q_ref[b] * SCALE for b in range(nb)]
    kcat = [jnp.concatenate([kp_ref[b], kc_ref[b]], axis=0) for b in range(nb)]
    vcat = [jnp.concatenate([vp_ref[b], vc_ref[b]], axis=0) for b in range(nb)]
    units = [(b, h) for b in range(nb) for h in range(N_HEADS)]
    sl = lambda h: slice(h * HEAD_DIM, (h + 1) * HEAD_DIM)
    s = [jnp.where(valid, _mm_nt(qs[b][:, sl(h)], kcat[b][:, sl(h // group)]), NEG) for b, h in units]
    m = [jnp.max(t, axis=-1, keepdims=True) for t in s]
    if with_sink:
        sink = [sink_ref[:, h:h + 1] for _, h in units]
        m = [jnp.maximum(a, b) for a, b in zip(m, sink)]
    p = [jnp.exp(t - a) for t, a in zip(s, m)]
    l = [jnp.sum(t, axis=-1, keepdims=True) for t in p]
    if with_sink:
        l = [a + jnp.exp(b - c) for a, b, c in zip(l, sink, m)]
    o = [_mm(t, vcat[b][:, sl(h // group)]) / a for t, a, (b, h) in zip(p, l, units)]
    for b in range(nb):
        o_ref[b] = jnp.concatenate(o[b * N_HEADS:(b + 1) * N_HEADS], axis=-1)
        if with_lse:
            lse = [jnp.broadcast_to(m[i] + jnp.log(l[i]), (WIN, HEAD_DIM)) for i in range(b * N_HEADS, (b + 1) * N_HEADS)]
            out_refs[1][b] = jnp.concatenate(lse, axis=-1)


def _band_attention(p4, q_col, k_col, v_col, kv, sinks, with_lse):
    batch, r, ln, width = p4.shape
    bb = BATCH_BLOCK if batch % BATCH_BLOCK == 0 else 1
    wq, wk = W_MIX, kv * HEAD_DIM
    qb, kb, vb = q_col // wq, k_col // wk, v_col // wk
    cur = lambda off: (lambda b, rho, n: (b, rho, n, off))
    prev = lambda off: (lambda b, rho, n: (b, rho, jnp.maximum(n - 1, 0), off))
    out_spec = pl.BlockSpec((bb, None, WIN, W_MIX), lambda b, rho, n: (b, rho, n, 0))
    out_shape = jax.ShapeDtypeStruct((batch, r, ln, W_MIX), F32)
    n_out = 2 if with_lse else 1
    return pl.pallas_call(
        functools.partial(_band_kernel, kv=kv, with_sink=sinks is not None, with_lse=with_lse),
        grid=(batch // bb, r, ln // WIN),
        in_specs=[pl.BlockSpec((bb, None, WIN, wq), cur(qb)),
                  pl.BlockSpec((bb, None, WIN, wk), prev(kb)),
                  pl.BlockSpec((bb, None, WIN, wk), cur(kb)),
                  pl.BlockSpec((bb, None, WIN, wk), prev(vb)),
                  pl.BlockSpec((bb, None, WIN, wk), cur(vb)),
                  pl.BlockSpec((1, 128), lambda b, rho, n: (0, 0))],
        out_specs=[out_spec] * n_out,
        out_shape=[out_shape] * n_out,
        compiler_params=_cparams("parallel", "parallel", "arbitrary"),
        name=f"band_r{r}",
    )(p4, p4, p4, p4, p4, sinks if sinks is not None else jnp.zeros((1, 128), F32))


def _gated_norm_rows(o, gate, gain):
    return _rms_rows(o, gain) * _silu(gate)


def _hgrn_constants(c):
    halves = []
    h = c // 2
    while h >= 1:
        halves.append(h)
        h //= 2
    t = np.arange(c)[:, None]
    u = np.arange(c)[None, :]
    mats = [(u <= t).astype(np.float32)]
    level = np.full((c, c), -1, np.int32)
    level[np.arange(c), np.arange(c)] = 0
    for li, h in enumerate(halves, 1):
        mid = (t // (2 * h)) * (2 * h) + h
        second = (t % (2 * h)) >= h
        mats.append(np.where(second, (u > mid) & (u <= t), (u > t) & (u <= mid)).astype(np.float32))
        pair = (t // (2 * h) == u // (2 * h)) & ((t % (2 * h)) >= h) & ((u % (2 * h)) < h)
        level[pair] = li
    return np.concatenate(mats, axis=0), level, len(halves)


def _hgrn_kernel(pc_ref, lb_ref, gain_ref, mat_ref, lvl_ref, o_ref, st_ref, s_scr, *, c, n_levels):
    j = pl.program_id(1)
    nb = pc_ref.shape[0]

    @pl.when(j == 0)
    def _():
        s_scr[...] = jnp.zeros_like(s_scr)

    lb = lb_ref[...]
    lvl = lvl_ref[...]
    xs = [pc_ref[b] for b in range(nb)]
    fs = [lb + (1.0 - lb) * _sigmoid(x[:, 256:512]) for x in xs]
    sums = [_mm_exact_lhs(mat_ref[...], -jnp.log(f)) for f in fs]
    gcs = [-s[0:c] for s in sums]
    g_last = [gc[c - 1:c, :] for gc in gcs]
    q_dec = [x[:, 0:256] * jnp.exp(gc) for x, gc in zip(xs, gcs)]
    k_dec = [(1.0 - f) * jnp.exp(gl - gc) for f, gl, gc in zip(fs, g_last, gcs)]
    units = [(b, h) for b in range(nb) for h in range(N_HEADS)]
    sl = lambda h: slice(h * HEAD_DIM, (h + 1) * HEAD_DIM)
    qh = [xs[b][:, sl(h)] for b, h in units]
    kh = [1.0 - fs[b][:, sl(h)] for b, h in units]
    vh = [xs[b][:, 512 + h * HEAD_DIM:512 + (h + 1) * HEAD_DIM] for b, h in units]
    a = [jnp.where(lvl == 0, _mm_nt(q, k), 0.0) for q, k in zip(qh, kh)]
    for li in range(1, n_levels + 1):
        damp = [jnp.exp(-sums[b][li * c:(li + 1) * c, sl(h)]) for b, h in units]
        part = [_mm_nt(q * d, k * d) for q, k, d in zip(qh, kh, damp)]
        a = [acc + jnp.where(lvl == li, p, 0.0) for acc, p in zip(a, part)]
    st = [s_scr[b, h] for b, h in units]
    o_inter = [_mm_nt(q_dec[b][:, sl(h)], t) for (b, h), t in zip(units, st)]
    o = [oi + _mm(aa, v) for oi, aa, v in zip(o_inter, a, vh)]
    upd = [_mm_tn(v, k_dec[b][:, sl(h)]) for (b, h), v in zip(units, vh)]
    for i, (b, h) in enumerate(units):
        s_scr[b, h] = jnp.exp(g_last[b][:, sl(h)]) * st[i] + upd[i]
    for b in range(nb):
        outs = [_gated_norm_rows(o[b * N_HEADS + h], xs[b][:, 768 + h * HEAD_DIM:768 + (h + 1) * HEAD_DIM],
                                 gain_ref[...]) for h in range(N_HEADS)]
        o_ref[b] = jnp.concatenate(outs, axis=-1)

    @pl.when(j == pl.num_programs(1) - 1)
    def _():
        st_ref[...] = s_scr[...]


def _hgrn_prompt(pc3, lb, gain):
    batch, seq, _ = pc3.shape
    c = CHUNK
    bb = BATCH_BLOCK if batch % BATCH_BLOCK == 0 else 1
    mat, level, n_levels = _hgrn_constants(c)
    o, st = pl.pallas_call(
        functools.partial(_hgrn_kernel, c=c, n_levels=n_levels),
        grid=(batch // bb, seq // c),
        in_specs=[pl.BlockSpec((bb, c, COLS_C), lambda b, j: (b, j, 0)),
                  pl.BlockSpec((1, W_MIX), lambda b, j: (0, 0)),
                  pl.BlockSpec((1, HEAD_DIM), lambda b, j: (0, 0)),
                  pl.BlockSpec(mat.shape, lambda b, j: (0, 0)),
                  pl.BlockSpec(level.shape, lambda b, j: (0, 0))],
        out_specs=[pl.BlockSpec((bb, c, W_MIX), lambda b, j: (b, j, 0)),
                   pl.BlockSpec((bb, N_HEADS, HEAD_DIM, HEAD_DIM), lambda b, j: (b, 0, 0, 0))],
        out_shape=[jax.ShapeDtypeStruct((batch, seq, W_MIX), F32),
                   jax.ShapeDtypeStruct((batch, N_HEADS, HEAD_DIM, HEAD_DIM), F32)],
        scratch_shapes=[pltpu.VMEM((bb, N_HEADS, HEAD_DIM, HEAD_DIM), F32)],
        compiler_params=_cparams("parallel", "arbitrary"),
        name="hgrn_prompt",
    )(pc3, lb, gain, jnp.asarray(mat), jnp.asarray(level))
    return o.reshape(batch * seq, W_MIX), jnp.swapaxes(st, -1, -2)


def _unit_lower_solve(lows, rhss, c):
    xs = [rhs - _mm_3pass(low, rhs) for low, rhs in zip(lows, rhss)]
    ps = lows
    span = 2
    while span < c:
        mm = _mm_3pass if span == 2 else _mm
        ps = [mm(p, p) for p in ps]
        xs = [x + mm(p, x) for p, x in zip(ps, xs)]
        span *= 2
    return xs


def _gdn_kernel(pa_ref, pba_ref, cw_ref, hp_ref, gain_ref, tri_ref, ones_ref, o_ref, st_ref,
                s_scr, buf_scr, *, c):
    j = pl.program_id(1)
    pad = 8
    nb = pa_ref.shape[0]

    @pl.when(j == 0)
    def _():
        s_scr[...] = jnp.zeros_like(s_scr)
        buf_scr[:, 0:pad, :] = jnp.zeros((nb, pad, 3 * W_MIX), F32)

    ones = ones_ref[...]
    row = lax.broadcasted_iota(jnp.int32, (c, c), 0)
    col = lax.broadcasted_iota(jnp.int32, (c, c), 1)
    qs, ks, vs, gates, betas, gcs, gcts = [], [], [], [], [], [], []
    for b in range(nb):
        x = pa_ref[b, :, 0:3 * W_MIX]
        gates.append(pa_ref[b, :, 3 * W_MIX:4 * W_MIX])
        buf_scr[b, pad:pad + c, :] = x
        y = cw_ref[CONV_A - 1:CONV_A, :] * x
        for tap in range(CONV_A - 1):
            back = CONV_A - 1 - tap
            y = y + cw_ref[tap:tap + 1, :] * buf_scr[b, pad - back:pad - back + c, :]
        buf_scr[b, 0:pad, :] = buf_scr[b, c:c + pad, :]
        y = _silu(y)
        qs.append(y[:, 0:W_MIX])
        ks.append(y[:, W_MIX:2 * W_MIX])
        vs.append(y[:, 2 * W_MIX:3 * W_MIX])
        ba = pba_ref[b]
        betas.append(_sigmoid(ba))
        g = hp_ref[0:1, :] * _softplus(ba + hp_ref[1:2, :])
        gcs.append(_mm_exact_lhs(tri_ref[0], g))
        gcts.append(_mm_exact_rhs(g, tri_ref[1], (((0,), (0,)), ((), ()))))
    qs = [q * lax.rsqrt(_mm_exact_rhs(q * q, ones) + EPS) * SCALE for q in qs]
    ks = [k * lax.rsqrt(_mm_exact_rhs(k * k, ones) + EPS) for k in ks]
    units = [(b, h) for b in range(nb) for h in range(N_HEADS)]
    sl = lambda h: slice(h * HEAD_DIM, (h + 1) * HEAD_DIM)
    qh = [qs[b][:, sl(h)] for b, h in units]
    kh = [ks[b][:, sl(h)] for b, h in units]
    vh = [vs[b][:, sl(h)] for b, h in units]
    b_col = [betas[b][:, h:h + 1] for b, h in units]
    g_col = [gcs[b][:, 4 + h:5 + h] for b, h in units]
    g_row = [gcts[b][4 + h:5 + h, :] for b, h in units]
    decay = [jnp.where(row >= col, jnp.exp(jnp.minimum(gc_ - gr_, 0.0)), 0.0) for gc_, gr_ in zip(g_col, g_row)]
    kk = [_mm_nt(k, k) for k in kh]
    qk = [_mm_nt(q, k) for q, k in zip(qh, kh)]
    low = [jnp.where(row > col, bc * a * d, 0.0) for bc, a, d in zip(b_col, kk, decay)]
    eg = [jnp.exp(gc_) for gc_ in g_col]
    rhs = [jnp.concatenate([v * bc, k * (bc * e)], axis=-1) for v, k, bc, e in zip(vh, kh, b_col, eg)]
    sol = _unit_lower_solve(low, rhs, c)
    intra = [a * d for a, d in zip(qk, decay)]
    g_last = [gc_[c - 1:c, :] for gc_ in g_col]
    st = [s_scr[b, h] for b, h in units]
    u = [s[:, 0:HEAD_DIM] - _mm_nt(s[:, HEAD_DIM:2 * HEAD_DIM], t) for s, t in zip(sol, st)]
    o_inter = [_mm_nt(q * e, t) for q, e, t in zip(qh, eg, st)]
    o = [oi + _mm(a, uu) for oi, a, uu in zip(o_inter, intra, u)]
    k_dec = [k * jnp.exp(gl - gc_) for k, gl, gc_ in zip(kh, g_last, g_col)]
    upd = [_mm_tn(uu, kd) for uu, kd in zip(u, k_dec)]
    for i, (b, h) in enumerate(units):
        s_scr[b, h] = jnp.exp(g_last[i]) * st[i] + upd[i]
    for b in range(nb):
        outs = [_gated_norm_rows(o[b * N_HEADS + h], gates[b][:, sl(h)], gain_ref[...]) for h in range(N_HEADS)]
        o_ref[b] = jnp.concatenate(outs, axis=-1)

    @pl.when(j == pl.num_programs(1) - 1)
    def _():
        st_ref[...] = s_scr[...]


def _gdn_prompt(pa3, pba3, conv_w, head_params, gain):
    batch, seq, _ = pa3.shape
    c = CHUNK
    bb = BATCH_BLOCK if batch % BATCH_BLOCK == 0 else 1
    lower = np.tril(np.ones((c, c), np.float32))
    tri = np.stack([lower, lower.T], axis=0)
    ones = np.kron(np.eye(N_HEADS, dtype=np.float32), np.ones((HEAD_DIM, HEAD_DIM), np.float32))
    o, st = pl.pallas_call(
        functools.partial(_gdn_kernel, c=c),
        grid=(batch // bb, seq // c),
        in_specs=[pl.BlockSpec((bb, c, COLS_A), lambda b, j: (b, j, 0)),
                  pl.BlockSpec((bb, c, COLS_BA), lambda b, j: (b, j, 0)),
                  pl.BlockSpec((CONV_A, 3 * W_MIX), lambda b, j: (0, 0)),
                  pl.BlockSpec((2, COLS_BA), lambda b, j: (0, 0)),
                  pl.BlockSpec((1, HEAD_DIM), lambda b, j: (0, 0)),
                  pl.BlockSpec((2, c, c), lambda b, j: (0, 0, 0)),
                  pl.BlockSpec((W_MIX, W_MIX), lambda b, j: (0, 0))],
        out_specs=[pl.BlockSpec((bb, c, W_MIX), lambda b, j: (b, j, 0)),
                   pl.BlockSpec((bb, N_HEADS, HEAD_DIM, HEAD_DIM), lambda b, j: (b, 0, 0, 0))],
        out_shape=[jax.ShapeDtypeStruct((batch, seq, W_MIX), F32),
                   jax.ShapeDtypeStruct((batch, N_HEADS, HEAD_DIM, HEAD_DIM), F32)],
        scratch_shapes=[pltpu.VMEM((bb, N_HEADS, HEAD_DIM, HEAD_DIM), F32),
                        pltpu.VMEM((bb, c + 8, 3 * W_MIX), F32)],
        compiler_params=_cparams("parallel", "arbitrary"),
        name="gdn_prompt",
    )(pa3, pba3, conv_w, head_params, gain, jnp.asarray(tri), jnp.asarray(ones))
    return o.reshape(batch * seq, W_MIX), jnp.swapaxes(st, -1, -2)


def _gdn_head_params(a_log, dt_bias):
    neg_a = jnp.pad(-jnp.exp(a_log.astype(F32)), (4, COLS_BA - 8))
    dtb = jnp.pad(dt_bias.astype(F32), (4, COLS_BA - 8))
    return jnp.stack([neg_a, dtb], axis=0)


def _outproj_kernel(*refs, n_branch):
    x_ref, oa_ref, ob_ref, oc_ref = refs[0:4]
    tm = x_ref.shape[0]
    if n_branch == 1:
        od = refs[4][...]
        w_ref, o_ref = refs[5], refs[6]
    else:
        d_refs = refs[4:4 + n_branch]
        l_refs = refs[4 + n_branch:4 + 2 * n_branch]
        w_ref, o_ref = refs[4 + 2 * n_branch], refs[5 + 2 * n_branch]
        scratch = refs[6 + 2 * n_branch:]

        def token_order(ref, scr):
            r, rows = ref.shape[1], ref.shape[2]
            if r == 1:
                return ref[0, 0]
            n_chunks = scr.shape[0]
            for rho in range(r):
                for ch in range(n_chunks):
                    scr[ch, pl.ds(rho, rows, stride=r), :] = ref[0, rho, :, ch * 128:(ch + 1) * 128]
            return jnp.concatenate([scr[ch] for ch in range(n_chunks)], axis=-1)

        outs = [token_order(ref, scratch[2 * i]) for i, ref in enumerate(d_refs)]
        lses = [token_order(ref, scratch[2 * i + 1]) for i, ref in enumerate(l_refs)]
        m = functools.reduce(jnp.maximum, lses)
        es = [jnp.exp(l - m) for l in lses]
        od = sum(e * o for e, o in zip(es, outs)) / sum(es)
    acc = x_ref[...]
    for i, part in enumerate((oa_ref[...], ob_ref[...], oc_ref[...], od)):
        acc = acc + _mm(part, w_ref[i * W_MIX:(i + 1) * W_MIX, :])
    o_ref[...] = acc


def _outproj(x2, oa, ob, oc, ods, lses, w_bf, tm):
    n = x2.shape[0]
    n_branch = len(ods)
    row = lambda w: pl.BlockSpec((tm, w), lambda i: (i, 0))
    specs = [row(D_MODEL)] + [row(W_MIX)] * 3
    scratch = []
    if n_branch == 1:
        specs.append(row(W_MIX))
    else:
        per_seq = (ods[0].shape[1] * ods[0].shape[2]) // tm
        for t in (*ods, *lses):
            r = t.shape[1]
            specs.append(pl.BlockSpec((1, r, tm // r, W_MIX), lambda i: (i // per_seq, 0, i % per_seq, 0)))
        scratch = [pltpu.VMEM((W_MIX // 128, tm, 128), F32)] * (2 * n_branch)
    specs.append(pl.BlockSpec((4 * W_MIX, D_MODEL), lambda i: (0, 0)))
    return pl.pallas_call(
        functools.partial(_outproj_kernel, n_branch=n_branch),
        grid=(n // tm,),
        in_specs=specs,
        out_specs=row(D_MODEL),
        out_shape=jax.ShapeDtypeStruct((n, D_MODEL), F32),
        scratch_shapes=scratch,
        compiler_params=_cparams("parallel"),
        name="outproj",
    )(x2, oa, ob, oc, *ods, *lses, w_bf)


def _ffn_kernel(x_ref, g_ref, wg_ref, wu_ref, wd_ref, o_ref, h_scr, acc_scr):
    f = pl.program_id(1)

    @pl.when(f == 0)
    def _():
        h_scr[...] = _rms_rows(x_ref[...], g_ref[...]).astype(BF16)
        acc_scr[...] = jnp.zeros_like(acc_scr)

    h = h_scr[...]
    gate = jnp.dot(h, wg_ref[...], preferred_element_type=F32)
    up = jnp.dot(h, wu_ref[...], preferred_element_type=F32)
    acc_scr[...] += _mm(_silu(gate) * up, wd_ref[...])

    @pl.when(f == pl.num_programs(1) - 1)
    def _():
        o_ref[...] = x_ref[...] + acc_scr[...]


def _ffn(x2, gain, w_gu_bf, w_down_bf, tm, tf):
    n = x2.shape[0]
    nf = D_FF // tf
    return pl.pallas_call(
        _ffn_kernel,
        grid=(n // tm, nf),
        in_specs=[pl.BlockSpec((tm, D_MODEL), lambda i, f: (i, 0)),
                  pl.BlockSpec((1, D_MODEL), lambda i, f: (0, 0)),
                  pl.BlockSpec((D_MODEL, tf), lambda i, f: (0, f)),
                  pl.BlockSpec((D_MODEL, tf), lambda i, f: (0, nf + f)),
                  pl.BlockSpec((tf, D_MODEL), lambda i, f: (f, 0))],
        out_specs=pl.BlockSpec((tm, D_MODEL), lambda i, f: (i, 0)),
        out_shape=jax.ShapeDtypeStruct((n, D_MODEL), F32),
        scratch_shapes=[pltpu.VMEM((tm, D_MODEL), BF16), pltpu.VMEM((tm, D_MODEL), F32)],
        compiler_params=_cparams("parallel", "arbitrary"),
        name="ffn",
    )(x2, gain, w_gu_bf, w_gu_bf, w_down_bf)


ROW_TILE = (8, 128)


def _router_kernel(x_ref, g_ref, wr_ref, br_ref, h_ref, route_ref):
    i = pl.program_id(0)
    last = pl.num_programs(0) - 1
    tm = x_ref.shape[0]

    @pl.when(i < last)
    def _():
        h = _rms_rows(x_ref[...], g_ref[...])
        h_ref[...] = h.reshape(tm, *ROW_TILE)
        lane = lax.broadcasted_iota(jnp.int32, (tm, 128), 1).astype(F32)
        logits = _mm_f32(h, wr_ref[...]) + br_ref[...]
        m1 = jnp.max(logits, axis=-1, keepdims=True)
        i1 = jnp.min(jnp.where(logits == m1, lane, 128.0), axis=-1, keepdims=True)
        rest = jnp.where(lane == i1, NEG, logits)
        m2 = jnp.max(rest, axis=-1, keepdims=True)
        i2 = jnp.min(jnp.where(rest == m2, lane, 128.0), axis=-1, keepdims=True)
        e2 = jnp.exp(m2 - m1)
        route_ref[...] = (jnp.where(lane == 0.0, i1, 0.0) + jnp.where(lane == 1.0, i2, 0.0)
                          + jnp.where(lane == 2.0, 1.0 / (1.0 + e2), 0.0)
                          + jnp.where(lane == 3.0, e2 / (1.0 + e2), 0.0))

    @pl.when(i == last)
    def _():
        h_ref[...] = jnp.zeros(h_ref.shape, F32)
        route_ref[...] = jnp.zeros(route_ref.shape, F32)


def _expert_kernel(be_ref, tok_ref, h_hbm, wg_ref, wu_ref, wd_ref, o_ref, xbuf, xb_scr, acc_scr, sems):
    i = pl.program_id(0)
    f = pl.program_id(1)
    tm = xb_scr.shape[0]

    def gather(block, slot):
        base = block * tm

        def issue(pair, carry):
            for queue in range(2):
                r = 2 * pair + queue
                pltpu.async_copy(h_hbm.at[tok_ref[base + r]], xbuf.at[slot, r], sems.at[slot], priority=queue)
            return carry

        lax.fori_loop(0, tm // 2, issue, 0)

    @pl.when(f == 0)
    def _():
        slot = i % 2

        @pl.when(i == 0)
        def _():
            gather(0, 0)

        pltpu.make_async_copy(h_hbm.at[pl.ds(0, tm)], xbuf.at[slot], sems.at[slot]).wait()
        xb_scr[...] = xbuf[slot].reshape(tm, D_MODEL).astype(BF16)
        acc_scr[...] = jnp.zeros_like(acc_scr)

        @pl.when(i + 1 < pl.num_programs(0))
        def _():
            gather(i + 1, 1 - slot)

    xb = xb_scr[...]
    gate = jnp.dot(xb, wg_ref[0], preferred_element_type=F32)
    up = jnp.dot(xb, wu_ref[0], preferred_element_type=F32)
    acc_scr[...] += _mm(_silu(gate) * up, wd_ref[0])

    @pl.when(f == pl.num_programs(1) - 1)
    def _():
        o_ref[...] = acc_scr[...].reshape(tm, *ROW_TILE)


def _combine_kernel(pos_ref, x_ref, route_ref, y_hbm, *rest, with_final_norm):
    final_ref = rest[0] if with_final_norm else None
    o_ref, first_scr, second_scr, sems = rest[int(with_final_norm):]
    tc = x_ref.shape[0]
    base = pl.program_id(0) * (2 * tc)

    def issue(r, carry):
        pltpu.async_copy(y_hbm.at[pos_ref[base + 2 * r]], first_scr.at[r], sems.at[0], priority=0)
        pltpu.async_copy(y_hbm.at[pos_ref[base + 2 * r + 1]], second_scr.at[r], sems.at[1], priority=1)
        return carry

    lax.fori_loop(0, tc, issue, 0)
    pltpu.make_async_copy(y_hbm.at[pl.ds(0, tc)], first_scr, sems.at[0]).wait()
    pltpu.make_async_copy(y_hbm.at[pl.ds(0, tc)], second_scr, sems.at[1]).wait()
    route = route_ref[...]
    y = (x_ref[...] + route[:, 2:3] * first_scr[...].reshape(tc, D_MODEL)
         + route[:, 3:4] * second_scr[...].reshape(tc, D_MODEL))
    o_ref[...] = y if final_ref is None else _rms_rows(y, final_ref[...])


def _moe_routing(route, n, tm_rows):
    e_flat = route[:n, 0:TOP_K].astype(jnp.int32).reshape(-1)
    n_assign = n * TOP_K
    onehot = (e_flat[:, None] == jnp.arange(N_EXPERTS, dtype=jnp.int32)[None, :]).astype(jnp.int32)
    csum = jnp.cumsum(onehot, axis=0)
    rank = jnp.sum(csum * onehot, axis=1) - 1
    counts = csum[-1]
    padded = (counts + tm_rows - 1) // tm_rows * tm_rows
    pad_end = jnp.cumsum(padded)
    pad_start = pad_end - padded
    start = jnp.cumsum(counts) - counts
    dest = pad_start[e_flat] + rank
    n_blocks = -(-(n_assign + N_EXPERTS * (tm_rows - 1)) // tm_rows)
    blk_e = jnp.minimum(jnp.sum(jnp.arange(n_blocks, dtype=jnp.int32)[:, None] * tm_rows >= pad_end[None, :], axis=1),
                        N_EXPERTS - 1).astype(jnp.int32)
    order = jnp.argsort(e_flat, stable=True).astype(jnp.int32)
    rows = jnp.arange(n_blocks * tm_rows, dtype=jnp.int32)
    row_e = jnp.repeat(blk_e, tm_rows)
    offset = rows - pad_start[row_e]
    valid = (offset < counts[row_e]) & (rows < pad_end[N_EXPERTS - 1])
    src = jnp.clip(start[row_e] + offset, 0, n_assign - 1)
    row_tok = jnp.where(valid, order[src] // TOP_K, n).astype(jnp.int32)
    return row_tok, dest.astype(jnp.int32), blk_e, n_blocks


def _moe(x2, gain, w_router, b_router, w_gu_bf, w_down_bf, tm, final_gain=None):
    n = x2.shape[0]
    rows = min(MOE_ROWS, max(128, n * TOP_K // N_EXPERTS))
    wr = jnp.pad(w_router.astype(F32), ((0, 0), (0, 128 - N_EXPERTS)))
    br = jnp.pad(b_router.astype(F32), (0, 128 - N_EXPERTS), constant_values=NEG).reshape(1, 128)
    nt = n // tm
    h3, route = pl.pallas_call(
        _router_kernel,
        grid=(nt + 1,),
        in_specs=[pl.BlockSpec((tm, D_MODEL), lambda i: (jnp.minimum(i, nt - 1), 0)),
                  pl.BlockSpec((1, D_MODEL), lambda i: (0, 0)),
                  pl.BlockSpec((D_MODEL, 128), lambda i: (0, 0)),
                  pl.BlockSpec((1, 128), lambda i: (0, 0))],
        out_specs=[pl.BlockSpec((tm, *ROW_TILE), lambda i: (i, 0, 0)),
                   pl.BlockSpec((tm, 128), lambda i: (i, 0))],
        out_shape=[jax.ShapeDtypeStruct((n + tm, *ROW_TILE), F32),
                   jax.ShapeDtypeStruct((n + tm, 128), F32)],
        compiler_params=_cparams("arbitrary"),
        name="moe_router",
    )(x2, gain, wr, br)

    row_tok, dest, blk_e, n_blocks = _moe_routing(route, n, rows)
    n_rows = n_blocks * rows

    nf = D_FF_E // MOE_FF_TILE
    y_rows = pl.pallas_call(
        _expert_kernel,
        grid_spec=pltpu.PrefetchScalarGridSpec(
            num_scalar_prefetch=2,
            grid=(n_blocks, nf),
            in_specs=[pl.BlockSpec(memory_space=pl.ANY),
                      pl.BlockSpec((1, D_MODEL, MOE_FF_TILE), lambda i, f, be, tok: (be[i], 0, f)),
                      pl.BlockSpec((1, D_MODEL, MOE_FF_TILE), lambda i, f, be, tok: (be[i], 0, nf + f)),
                      pl.BlockSpec((1, MOE_FF_TILE, D_MODEL), lambda i, f, be, tok: (be[i], f, 0))],
            out_specs=pl.BlockSpec((rows, *ROW_TILE), lambda i, f, be, tok: (i, 0, 0)),
            scratch_shapes=[pltpu.VMEM((2, rows, *ROW_TILE), F32), pltpu.VMEM((rows, D_MODEL), BF16),
                            pltpu.VMEM((rows, D_MODEL), F32), pltpu.SemaphoreType.DMA((2,))]),
        out_shape=jax.ShapeDtypeStruct((n_rows, *ROW_TILE), F32),
        compiler_params=_cparams("arbitrary", "arbitrary"),
        name="moe_experts",
    )(blk_e, row_tok, h3, w_gu_bf, w_gu_bf, w_down_bf)

    final = [] if final_gain is None else [final_gain]
    return pl.pallas_call(
        functools.partial(_combine_kernel, with_final_norm=bool(final)),
        grid_spec=pltpu.PrefetchScalarGridSpec(
            num_scalar_prefetch=1,
            grid=(nt,),
            in_specs=[pl.BlockSpec((tm, D_MODEL), lambda i, pos: (i, 0)),
                      pl.BlockSpec((tm, 128), lambda i, pos: (i, 0)),
                      pl.BlockSpec(memory_space=pl.ANY)]
            + [pl.BlockSpec((1, D_MODEL), lambda i, pos: (0, 0))] * len(final),
            out_specs=pl.BlockSpec((tm, D_MODEL), lambda i, pos: (i, 0)),
            scratch_shapes=[pltpu.VMEM((tm, *ROW_TILE), F32), pltpu.VMEM((tm, *ROW_TILE), F32),
                            pltpu.SemaphoreType.DMA((2,))]),
        out_shape=jax.ShapeDtypeStruct((n, D_MODEL), F32),
        compiler_params=_cparams("arbitrary"),
        name="moe_combine",
    )(dest, x2, route, y_rows, *final)


def _norm_kernel(x_ref, g_ref, o_ref):
    o_ref[...] = _rms_rows(x_ref[...], g_ref[...])


def _final_norm(x2, gain, tm):
    n = x2.shape[0]
    return pl.pallas_call(
        _norm_kernel,
        grid=(n // tm,),
        in_specs=[pl.BlockSpec((tm, D_MODEL), lambda i: (i, 0)),
                  pl.BlockSpec((1, D_MODEL), lambda i: (0, 0))],
        out_specs=pl.BlockSpec((tm, D_MODEL), lambda i: (i, 0)),
        out_shape=jax.ShapeDtypeStruct((n, D_MODEL), F32),
        compiler_params=_cparams("parallel"),
        name="final_norm",
    )(x2, gain)


ROWS_T = 8
LANE_TILE = 128


def _sattn_kernel(qd_ref, qb_ref, kd_all, vd_all, kdn_all, vdn_all, kb_all, vb_all, kbn_all, vbn_all, sink_ref,
                  od_ref, ob_ref, *rolled_refs, t_new, here):
    kd_ref, vd_ref, kdn_ref, vdn_ref, kb_ref, vb_ref, kbn_ref, vbn_ref = (
        r.at[here] for r in (kd_all, vd_all, kdn_all, vdn_all, kb_all, vb_all, kbn_all, vbn_all))
    bb = qd_ref.shape[0]
    first_new = LANE_TILE - t_new

    def distances(n_cache):
        row = lax.broadcasted_iota(jnp.int32, (ROWS_T, n_cache), 0)
        col = lax.broadcasted_iota(jnp.int32, (ROWS_T, n_cache), 1)
        rown = lax.broadcasted_iota(jnp.int32, (ROWS_T, LANE_TILE), 0)
        j = lax.broadcasted_iota(jnp.int32, (ROWS_T, LANE_TILE), 1) - first_new
        d_new = rown - j
        return n_cache + row - col, d_new, (j >= 0) & (d_new >= 0)

    def softmax_parts(sc, sn, ok_c, ok_n, sink):
        sc = jnp.where(ok_c, sc, NEG)
        sn = jnp.where(ok_n, sn, NEG)
        m = jnp.maximum(jnp.max(sc, axis=-1, keepdims=True), jnp.max(sn, axis=-1, keepdims=True))
        if sink is not None:
            m = jnp.maximum(m, sink)
        pc = jnp.exp(sc - m)
        pn = jnp.exp(sn - m)
        l = jnp.sum(pc, axis=-1, keepdims=True) + jnp.sum(pn, axis=-1, keepdims=True)
        if sink is not None:
            l = l + jnp.exp(sink - m)
        return pc, pn, l, m + jnp.log(l)

    sl = lambda h: slice(h * HEAD_DIM, (h + 1) * HEAD_DIM)
    units = [(i, h) for i in range(bb) for h in range(N_HEADS)]
    dc, dn, ok_new = distances(kd_ref.shape[-1])
    qs = [qd_ref[i][:, sl(h)] * SCALE for i, h in units]
    sc = [_mm(q, kd_ref[i, h]) for q, (i, h) in zip(qs, units)]
    sn = [_mm(q, kdn_ref[i, h]) for q, (i, h) in zip(qs, units)]
    pcs, pns, ls, lses = [], [], [], []
    for r in DILATIONS:
        ok_c = (dc <= WIN * r) & ((dc & (r - 1)) == 0)
        ok_n = ok_new & ((dn & (r - 1)) == 0)
        parts = [softmax_parts(a, b, ok_c, ok_n, None) for a, b in zip(sc, sn)]
        pcs.append([p[0] for p in parts])
        pns.append([p[1] for p in parts])
        ls.append([p[2] for p in parts])
        lses.append([p[3] for p in parts])
    nbr = len(DILATIONS)
    acc = [_mm_nt(jnp.concatenate([pcs[r][u] for r in range(nbr)], axis=0), vd_ref[i, h])
           + _mm_nt(jnp.concatenate([pns[r][u] for r in range(nbr)], axis=0), vdn_ref[i, h])
           for u, (i, h) in enumerate(units)]
    outs = []
    for u in range(len(units)):
        m = functools.reduce(jnp.maximum, [lses[r][u] for r in range(nbr)])
        es = [jnp.exp(lses[r][u] - m) for r in range(nbr)]
        num = sum(es[r] * (acc[u][r * ROWS_T:(r + 1) * ROWS_T] / ls[r][u]) for r in range(nbr))
        outs.append(num / sum(es))
    for i in range(bb):
        od_ref[i] = jnp.concatenate(outs[i * N_HEADS:(i + 1) * N_HEADS], axis=-1)
    group = N_HEADS // KV_B
    dc, dn, ok_new = distances(kb_ref.shape[-1])
    qs = [qb_ref[i][:, sl(h)] * SCALE for i, h in units]
    parts = [softmax_parts(_mm(q, kb_ref[i, h // group]), _mm(q, kbn_ref[i, h // group]), dc <= WIN, ok_new,
                           sink_ref[:, h:h + 1]) for q, (i, h) in zip(qs, units)]
    outs = [(_mm_nt(p[0], vb_ref[i, h // group]) + _mm_nt(p[1], vbn_ref[i, h // group])) / p[2]
            for p, (i, h) in zip(parts, units)]
    for i in range(bb):
        ob_ref[i] = jnp.concatenate(outs[i * N_HEADS:(i + 1) * N_HEADS], axis=-1)
    lane = lax.broadcasted_iota(jnp.int32, (HEAD_DIM, LANE_TILE), 1)
    if not rolled_refs:
        return
    for src, new, dst in zip((kd_all, vd_all, kb_all, vb_all), (kdn_all, vdn_all, kbn_all, vbn_all), rolled_refs):
        n = src.shape[-1]
        for layer in range(src.shape[0]):
            for i in range(bb):
                for h in range(src.shape[2]):
                    rolled = pltpu.roll(src[layer, i, h], n - t_new, axis=1)
                    if n > LANE_TILE:
                        dst[layer, i, h, :, 0:n - LANE_TILE] = rolled[:, 0:n - LANE_TILE]
                    dst[layer, i, h, :, n - LANE_TILE:n] = jnp.where(lane >= first_new, new[layer, i, h],
                                                                     rolled[:, n - LANE_TILE:n])


def _pad_rows(t, rows):
    return jnp.pad(t, ((0, 0), (0, rows - t.shape[1]), (0, 0)))


def _new_columns(t, n_heads):
    bs, t_new, _ = t.shape
    x = jnp.transpose(t.reshape(bs, t_new, n_heads, HEAD_DIM), (0, 2, 3, 1))
    return jnp.pad(x, ((0, 0), (0, 0), (0, 0), (LANE_TILE - t_new, 0)))


def _sample_attention(layer, pb_s, pd_s, caches_t, new_cols, sinks):
    bs, t_new, _ = pb_s.shape
    depth = caches_t[0].shape[0]
    last = layer == depth - 1
    bb = 1 if last else 2
    pb8, pd8 = _pad_rows(pb_s, ROWS_T), _pad_rows(pd_s, ROWS_T)
    sink_row = jnp.pad(sinks.astype(F32), (0, 128 - N_HEADS)).reshape(1, 128)
    cbk, cbv, cdk, cdv = caches_t
    if last:
        news = [jnp.stack([layer_cols[k] for layer_cols in new_cols], axis=0) for k in range(4)]
    else:
        news = [new_cols[layer][k][None] for k in range(4)]
    n_slab = depth if last else 1
    qblk = lambda: pl.BlockSpec((bb, ROWS_T, W_MIX), lambda i: (i, 0, 0))
    slab = lambda c: pl.BlockSpec((n_slab, bb) + c.shape[2:], lambda i: (0 if last else layer, i, 0, 0, 0))
    newblk = lambda c: pl.BlockSpec((n_slab, bb) + c.shape[2:], lambda i: (0, i, 0, 0, 0))
    rolled_specs = [slab(c) for c in (cdk, cdv, cbk, cbv)] if last else []
    rolled_shapes = [jax.ShapeDtypeStruct(c.shape, c.dtype) for c in (cdk, cdv, cbk, cbv)] if last else []
    res = pl.pallas_call(
        functools.partial(_sattn_kernel, t_new=t_new, here=layer if last else 0),
        grid=(bs // bb,),
        in_specs=[qblk(), qblk(), slab(cdk), slab(cdv), newblk(news[0]), newblk(news[1]),
                  slab(cbk), slab(cbv), newblk(news[2]), newblk(news[3]),
                  pl.BlockSpec((1, 128), lambda i: (0, 0))],
        out_specs=[qblk(), qblk()] + rolled_specs,
        out_shape=[jax.ShapeDtypeStruct((bs, ROWS_T, W_MIX), F32)] * 2 + rolled_shapes,
        compiler_params=_cparams("arbitrary"),
        name="sample_attn",
    )(pd8, pb8, cdk, cdv, news[0], news[1], cbk, cbv, news[2], news[3], sink_row)
    od, ob = res[0], res[1]
    return (ob[:, :t_new].reshape(bs * t_new, W_MIX), od[:, :t_new].reshape(bs * t_new, W_MIX), tuple(res[2:]))


def _srec_kernel(xq_ref, xk_ref, xv_ref, bq_ref, bk_ref, bv_ref, cwq_ref, cwk_ref, cwv_ref, ga_ref,
                 ba_ref, hp_ref, gna_ref, sa_ref, qc_ref, fc_ref, ic_ref, gcg_ref, lb_ref, gnc_ref, sc_ref,
                 oa_ref, sa_out, oc_ref, sc_out, q_scr, k_scr, d_scr, *, t_new):
    nb = sa_ref.shape[-1]
    zero = jnp.zeros((HEAD_DIM, nb), F32)

    def conv(x_ref, b_ref, cw_ref, t):
        y = None
        for tap in range(CONV_A):
            pos = t + tap
            src = b_ref[pos] if pos < CONV_A - 1 else x_ref[pos - (CONV_A - 1)]
            term = cw_ref[tap] * src
            y = term if y is None else y + term
        return _silu(y)

    def l2(x):
        return x * lax.rsqrt(jnp.sum(x * x, axis=0, keepdims=True) + EPS)

    def gated_norm(o, gate, gain):
        return o * lax.rsqrt(jnp.mean(o * o, axis=0, keepdims=True) + EPS) * gain * _silu(gate)

    sa_out[0] = sa_ref[0]
    for t in range(t_new):
        q_scr[...] = l2(conv(xq_ref, bq_ref, cwq_ref, t)) * SCALE
        k_scr[...] = l2(conv(xk_ref, bk_ref, cwk_ref, t))
        v = conv(xv_ref, bv_ref, cwv_ref, t)
        beta = _sigmoid(ba_ref[0, t:t + 1, :])
        dec = jnp.exp(hp_ref[0, 0:1, :] * _softplus(ba_ref[0, t_new + t:t_new + t + 1, :] + hp_ref[0, 1:2, :]))

        def decay_and_read(kk, acc):
            s = sa_out[0, kk] * dec
            sa_out[0, kk] = s
            return acc + k_scr[pl.ds(kk, 1), :] * s

        err = (v - lax.fori_loop(0, HEAD_DIM, decay_and_read, zero)) * beta

        def write_and_query(kk, acc):
            s = sa_out[0, kk] + k_scr[pl.ds(kk, 1), :] * err
            sa_out[0, kk] = s
            return acc + q_scr[pl.ds(kk, 1), :] * s

        o = lax.fori_loop(0, HEAD_DIM, write_and_query, zero)
        oa_ref[t] = gated_norm(o, ga_ref[t], gna_ref[...])

    sc_out[0] = sc_ref[0]
    for t in range(t_new):
        lb = lb_ref[...]
        f = lb + (1.0 - lb) * _sigmoid(fc_ref[t])
        q_scr[...] = qc_ref[t]
        k_scr[...] = 1.0 - f
        d_scr[...] = jnp.exp(jnp.log(f))
        v = ic_ref[t]

        def update(kk, acc):
            s = sc_out[0, kk] * d_scr[pl.ds(kk, 1), :] + k_scr[pl.ds(kk, 1), :] * v
            sc_out[0, kk] = s
            return acc + q_scr[pl.ds(kk, 1), :] * s

        o = lax.fori_loop(0, HEAD_DIM, update, zero)
        oc_ref[t] = gated_norm(o, gcg_ref[t], gnc_ref[...])


def _sample_recurrences(pa_s, pba_s, pc_s, conv_buf, s_a, s_c, conv_w, a_log, dt_bias, norm_a, lb, norm_c):
    bs, t_new, _ = pa_s.shape
    lanes_last = lambda t: jnp.transpose(t, (1, 2, 0))
    pa_t = lanes_last(pa_s)
    pc_t = lanes_last(pc_s)
    buf_t = lanes_last(conv_buf.astype(F32))
    ba = jnp.transpose(pba_s[:, :, 0:8], (2, 1, 0))
    ba = jnp.concatenate([ba[0:N_HEADS], ba[N_HEADS:2 * N_HEADS]], axis=1)
    hp = jnp.stack([-jnp.exp(a_log.astype(F32)), dt_bias.astype(F32)], axis=1)
    hp = jnp.broadcast_to(hp[:, :, None], (N_HEADS, 2, bs))
    cw = conv_w.astype(F32)[:, :, None]
    sa_t = jnp.transpose(s_a.astype(F32), (1, 2, 3, 0))
    sc_t = jnp.transpose(s_c.astype(F32), (1, 2, 3, 0))
    col = lambda v: v.astype(F32).reshape(-1, 1)
    hd = HEAD_DIM
    feat = lambda rows, off: pl.BlockSpec((rows, hd, bs), lambda h: (0, off + h, 0))
    cwspec = lambda off: pl.BlockSpec((CONV_A, hd, 1), lambda h: (0, off + h, 0))
    per_head = lambda rows: pl.BlockSpec((1, rows, bs), lambda h: (h, 0, 0))
    state = pl.BlockSpec((1, hd, hd, bs), lambda h: (h, 0, 0, 0))
    vec = pl.BlockSpec((hd, 1), lambda h: (0, 0))
    nh = N_HEADS
    oa, sa_n, oc, sc_n = pl.pallas_call(
        functools.partial(_srec_kernel, t_new=t_new),
        grid=(N_HEADS,),
        in_specs=[feat(t_new, 0), feat(t_new, nh), feat(t_new, 2 * nh),
                  feat(CONV_A - 1, 0), feat(CONV_A - 1, nh), feat(CONV_A - 1, 2 * nh),
                  cwspec(0), cwspec(nh), cwspec(2 * nh),
                  feat(t_new, 3 * nh), per_head(2 * t_new), per_head(2), vec, state,
                  feat(t_new, 0), feat(t_new, nh), feat(t_new, 2 * nh), feat(t_new, 3 * nh),
                  pl.BlockSpec((hd, 1), lambda h: (h, 0)), vec, state],
        out_specs=[feat(t_new, 0), state, feat(t_new, 0), state],
        out_shape=[jax.ShapeDtypeStruct((t_new, W_MIX, bs), F32),
                   jax.ShapeDtypeStruct((N_HEADS, hd, hd, bs), F32),
                   jax.ShapeDtypeStruct((t_new, W_MIX, bs), F32),
                   jax.ShapeDtypeStruct((N_HEADS, hd, hd, bs), F32)],
        scratch_shapes=[pltpu.VMEM((hd, bs), F32)] * 3,
        compiler_params=_cparams("parallel"),
        name="sample_recurrences",
    )(pa_t, pa_t, pa_t, buf_t, buf_t, buf_t, cw, cw, cw, pa_t, ba, hp, col(norm_a), sa_t,
      pc_t, pc_t, pc_t, pc_t, col(lb), col(norm_c), sc_t)
    rows_first = lambda t: jnp.transpose(t, (2, 0, 1)).reshape(bs * t_new, W_MIX)
    back = lambda t: jnp.transpose(t, (3, 0, 1, 2))
    return rows_first(oa), back(sa_n), rows_first(oc), back(sc_n)


def _heads(t, n):
    return t.reshape(t.shape[0], t.shape[1], n, HEAD_DIM)


def _prompt_mixers(pa, pb, pc, pd, pba, pd_strided, batch, seq, conv_w, head_params, norm_a, sinks, lb, norm_c):
    pa3 = pa.reshape(batch, seq, COLS_A)
    pb3 = pb.reshape(batch, seq, COLS_B)
    pc3 = pc.reshape(batch, seq, COLS_C)
    pd3 = pd.reshape(batch, seq, COLS_D)
    oa, s_a = _gdn_prompt(pa3, pba.reshape(batch, seq, COLS_BA), conv_w, head_params, norm_a)
    oc, s_c = _hgrn_prompt(pc3, lb, norm_c)
    sink_row = jnp.pad(sinks.astype(F32), (0, 128 - N_HEADS)).reshape(1, 128)
    (ob,) = _band_attention(pb3[:, None], 0, 256, 384, KV_B, sink_row, False)
    ods, lses = [], []
    for p4 in (pd3[:, None], *pd_strided):
        o, lse = _band_attention(p4, 0, 256, 512, N_HEADS, None, True)
        ods.append(o)
        lses.append(lse)
    nb, nd = min(CACHE_B, seq), min(CACHE_D, seq)
    state = (_heads(pb3[:, seq - nb:, 256:384], KV_B), _heads(pb3[:, seq - nb:, 384:512], KV_B),
             _heads(pd3[:, seq - nd:, 256:512], N_HEADS), _heads(pd3[:, seq - nd:, 512:768], N_HEADS),
             pa3[:, seq - (CONV_A - 1):, 0:3 * W_MIX], s_a, s_c)
    return oa, ob.reshape(batch * seq, W_MIX), oc, ods, lses, state


def _sample_mixers(layer, pa, pb, pc, pd, pba, bs, t_new, caches_t, new_cols, states, conv_w, a_log, dt_bias,
                   norm_a, sinks, lb, norm_c):
    conv_buf, s_a, s_c = states
    pa3 = pa.reshape(bs, t_new, COLS_A)
    pb3 = pb.reshape(bs, t_new, COLS_B)
    pc3 = pc.reshape(bs, t_new, COLS_C)
    pd3 = pd.reshape(bs, t_new, COLS_D)
    new_cols.append((_new_columns(pd3[:, :, 256:512], N_HEADS), _new_columns(pd3[:, :, 512:768], N_HEADS),
                     _new_columns(pb3[:, :, 256:384], KV_B), _new_columns(pb3[:, :, 384:512], KV_B)))
    ob, od, rolled = _sample_attention(layer, pb3, pd3, caches_t, new_cols, sinks)
    oa, s_a_new, oc, s_c_new = _sample_recurrences(
        pa3, pba.reshape(bs, t_new, COLS_BA), pc3, conv_buf, s_a, s_c, conv_w, a_log, dt_bias, norm_a, lb, norm_c)
    conv_all = jnp.concatenate([conv_buf.astype(F32), pa3[:, :, 0:3 * W_MIX]], axis=1)
    state = (conv_all[:, -(CONV_A - 1):].astype(conv_buf.dtype), s_a_new.astype(s_a.dtype),
             s_c_new.astype(s_c.dtype))
    return oa, ob, oc, od, rolled, state


def kernel(x_prompt, x_sample, cache_b_k, cache_b_v, cache_d_k, cache_d_v, state_a_conv, state_a_s, state_c_s, norm_mix, w_in, conv_a, a_log, dt_bias, norm_a, sinks_b, lb_logits, norm_c, w_out, norm_ffn, w_ffn_gu, w_ffn_down, w_router, b_router, w_moe_gu, w_moe_down, norm_final):
    depth = w_in.shape[0]
    batch, seq, _ = x_prompt.shape
    bs, t_new, _ = x_sample.shape
    lb_p = jax.nn.softmax(lb_logits.astype(F32), axis=0)
    lower_bounds = jnp.cumsum(lb_p, axis=0) - lb_p[0]
    xp = x_prompt.reshape(batch * seq, D_MODEL)
    xs = x_sample.reshape(bs * t_new, D_MODEL)
    tm_p, tm_s = 512, 256
    row = lambda v: v.astype(F32).reshape(1, -1)
    prompt_states, sample_states = [], []
    rows_last = lambda c: jnp.transpose(c, (0, 1, 3, 4, 2))
    caches_t = tuple(rows_last(c) for c in (cache_b_k, cache_b_v, cache_d_k, cache_d_v))
    new_cols = []
    for l in range(depth):
        w_in_l = _permute_w_in(w_in[l])
        w_out_l = w_out[l].astype(BF16)
        conv_w = conv_a[l].astype(F32)
        head_params = _gdn_head_params(a_log[l], dt_bias[l])
        lb = lower_bounds[l]
        mix = (row(norm_a[l]), sinks_b[l], row(lb), row(norm_c[l]))

        projs = _inproj(xp, row(norm_mix[l]), w_in_l, tm_p, strided_for=(batch, seq))
        oa, ob, oc, ods, lses, st_p = _prompt_mixers(*projs[0:5], projs[5:], batch, seq, conv_w, head_params, *mix)
        xp = _outproj(xp, oa, ob, oc, ods, lses, w_out_l, tm_p)
        prompt_states.append(st_p)

        projs = _inproj(xs, row(norm_mix[l]), w_in_l, tm_s)
        oa, ob, oc, od, rolled, st_s = _sample_mixers(
            l, *projs, bs, t_new, caches_t, new_cols, (state_a_conv[l], state_a_s[l], state_c_s[l]), conv_w,
            a_log[l], dt_bias[l], norm_a[l], sinks_b[l], lb, norm_c[l])
        xs = _outproj(xs, oa, ob, oc, [od], [], w_out_l, tm_s)
        sample_states.append(st_s)

        if l % 2 == 0:
            w_gu = w_ffn_gu[l // 2].astype(BF16)
            w_dn = w_ffn_down[l // 2].astype(BF16)
            xp = _ffn(xp, row(norm_ffn[l]), w_gu, w_dn, 512, FFN_FF_TILE)
            xs = _ffn(xs, row(norm_ffn[l]), w_gu, w_dn, 512, FFN_FF_TILE)
        else:
            w_gu = w_moe_gu[l // 2].astype(BF16)
            w_dn = w_moe_down[l // 2].astype(BF16)
            final = row(norm_final) if l == depth - 1 else None
            xp = _moe(xp, row(norm_ffn[l]), w_router[l // 2], b_router[l // 2], w_gu, w_dn, 512, final)
            xs = _moe(xs, row(norm_ffn[l]), w_router[l // 2], b_router[l // 2], w_gu, w_dn, 512, final)
    if depth % 2 == 1:
        xp = _final_norm(xp, row(norm_final), 1024)
        xs = _final_norm(xs, row(norm_final), 512)
    y_prompt = xp.reshape(batch, seq, D_MODEL)
    y_sample = xs.reshape(bs, t_new, D_MODEL)
    stack = lambda states: [jnp.stack(t, 0) for t in zip(*states)]
    rows_back = lambda c, ref: jnp.transpose(c, (0, 1, 4, 2, 3)).astype(ref.dtype)
    d_k, d_v, b_k, b_v = rolled
    sample_caches = [rows_back(b_k, cache_b_k), rows_back(b_v, cache_b_v), rows_back(d_k, cache_d_k),
                     rows_back(d_v, cache_d_v)]
    return (y_prompt, y_sample, *stack(prompt_states), *sample_caches, *stack(sample_states))
```

```python
import functools
import math

import numpy as np
import jax
import jax.numpy as jnp
from jax import lax
from jax.experimental import pallas as pl
from jax.experimental.pallas import tpu as pltpu

F32 = jnp.float32
BF16 = jnp.bfloat16
HIGHEST = lax.Precision.HIGHEST

D_MODEL = 1024
HEAD_DIM = 64
N_HEADS = 4
KV_B = 2
W_MIX = N_HEADS * HEAD_DIM
CONV_A = 4
WIN = 128
DILATIONS = (1, 4, 16)
CACHE_D = 2048
CACHE_B = 128
D_FF = 2816
N_EXPERTS = 8
TOP_K = 2
D_FF_E = 3584
EPS = 1e-6
SCALE = HEAD_DIM ** -0.5
NEG = -1e30

COLS_A = 1024
COLS_B = 512
COLS_C = 1024
COLS_D = 768
COLS_BA = 128
COLS_ALL = COLS_A + COLS_B + COLS_C + COLS_D + COLS_BA

CHUNK = 64
BATCH_BLOCK = 4
FFN_FF_TILE = 1408
MOE_ROWS = 512
MOE_FF_TILE = 1792
VMEM_LIMIT = 56 * 1024 * 1024


def _cparams(*sem):
    return pltpu.CompilerParams(dimension_semantics=sem, vmem_limit_bytes=VMEM_LIMIT)


def _mm(a, b):
    return jnp.dot(a.astype(BF16), b.astype(BF16), preferred_element_type=F32)


def _mm_nt(a, b):
    return lax.dot_general(a.astype(BF16), b.astype(BF16), (((1,), (1,)), ((), ())),
                           preferred_element_type=F32)


def _mm_tn(a, b):
    return lax.dot_general(a.astype(BF16), b.astype(BF16), (((0,), (0,)), ((), ())),
                           preferred_element_type=F32)


def _split_bf16(a):
    hi = a.astype(BF16)
    return hi, (a - hi.astype(F32)).astype(BF16)


def _mm_3pass(a, b):
    ah, al = _split_bf16(a)
    bh, bl = _split_bf16(b)
    dot = functools.partial(jnp.dot, preferred_element_type=F32)
    return dot(ah, bh) + (dot(al, bh) + dot(ah, bl))


def _mm_f32(a, b):
    return jnp.dot(a, b, precision=HIGHEST, preferred_element_type=F32)


def _split3_bf16(a):
    hi = a.astype(BF16)
    rest = a - hi.astype(F32)
    mid = rest.astype(BF16)
    return hi, mid, (rest - mid.astype(F32)).astype(BF16)


def _mm_exact_lhs(sel, b, dims=(((1,), (0,)), ((), ()))):
    sel = sel.astype(BF16)
    return sum(lax.dot_general(sel, t, dims, preferred_element_type=F32) for t in _split3_bf16(b))


def _mm_exact_rhs(a, sel, dims=(((1,), (0,)), ((), ()))):
    sel = sel.astype(BF16)
    return sum(lax.dot_general(t, sel, dims, preferred_element_type=F32) for t in _split3_bf16(a))


def _sigmoid(x):
    return 1.0 / (1.0 + jnp.exp(-x))


def _silu(x):
    return x * _sigmoid(x)


def _softplus(x):
    return jnp.maximum(x, 0.0) + jnp.log(1.0 + jnp.exp(-jnp.abs(x)))


def _rms_rows(x, gain):
    return x * lax.rsqrt(jnp.mean(x * x, axis=-1, keepdims=True) + EPS) * gain


def _inproj_kernel(x_ref, g_ref, w_ref, oa_ref, ob_ref, oc_ref, od_ref, oba_ref, *strided):
    h = _rms_rows(x_ref[...], g_ref[...]).astype(BF16)
    c = 0
    for o_ref in (oa_ref, ob_ref, oc_ref, od_ref, oba_ref):
        n = o_ref.shape[1]
        o_ref[...] = jnp.dot(h, w_ref[:, c:c + n], preferred_element_type=F32)
        c += n
    if strided:
        *strided, chunk_scr = strided
        n_chunks = chunk_scr.shape[0]
        for ch in range(n_chunks):
            chunk_scr[ch] = od_ref[:, ch * 128:(ch + 1) * 128]
        for o_ref in strided:
            r, rows = o_ref.shape[1], o_ref.shape[2]
            for rho in range(r):
                for ch in range(n_chunks):
                    o_ref[0, rho, :, ch * 128:(ch + 1) * 128] = chunk_scr[ch, pl.ds(rho, rows, stride=r), :]


def _inproj(x2, gain, w_perm, tm, strided_for=None):
    n = x2.shape[0]
    widths = (COLS_A, COLS_B, COLS_C, COLS_D, COLS_BA)
    out_specs = [pl.BlockSpec((tm, w), lambda i: (i, 0)) for w in widths]
    out_shape = [jax.ShapeDtypeStruct((n, w), F32) for w in widths]
    scratch = []
    if strided_for is not None:
        batch, seq = strided_for
        per_seq = seq // tm
        for r in DILATIONS[1:]:
            out_specs.append(pl.BlockSpec((1, r, tm // r, COLS_D), lambda i: (i // per_seq, 0, i % per_seq, 0)))
            out_shape.append(jax.ShapeDtypeStruct((batch, r, seq // r, COLS_D), F32))
        scratch = [pltpu.VMEM((COLS_D // 128, tm, 128), F32)]
    return pl.pallas_call(
        _inproj_kernel,
        grid=(n // tm,),
        in_specs=[pl.BlockSpec((tm, D_MODEL), lambda i: (i, 0)),
                  pl.BlockSpec((1, D_MODEL), lambda i: (0, 0)),
                  pl.BlockSpec((D_MODEL, COLS_ALL), lambda i: (0, 0))],
        out_specs=out_specs,
        out_shape=out_shape,
        scratch_shapes=scratch,
        compiler_params=_cparams("parallel"),
        name="inproj",
    )(x2, gain, w_perm)


def _permute_w_in(w):
    ba = jnp.pad(w[:, 1024:1032], ((0, 0), (0, COLS_BA - 8)))
    return jnp.concatenate([w[:, 0:1024], w[:, 1032:3336], ba], axis=1).astype(BF16)


def _band_kernel(q_ref, kp_ref, kc_ref, vp_ref, vc_ref, sink_ref, *out_refs, kv, with_sink, with_lse):
    o_ref = out_refs[0]
    n = pl.program_id(2)
    nb = q_ref.shape[0]
    row = lax.broadcasted_iota(jnp.int32, (WIN, 2 * WIN), 0)
    col = lax.broadcasted_iota(jnp.int32, (WIN, 2 * WIN), 1)
    dist = row + WIN - col
    valid = (dist >= 0) & (dist <= WIN) & ((col >= WIN) | (n > 0))
    group = N_HEADS // kv
    qs = [q_ref[b] * SCALE for b in range(nb)]
    kcat = [jnp.concatenate([kp_ref[b], kc_ref[b]], axis=0) for b in range(nb)]
    vcat = [jnp.concatenate([vp_ref[b], vc_ref[b]], axis=0) for b in range(nb)]
    units = [(b, h) for b in range(nb) for h in range(N_HEADS)]
    sl = lambda h: slice(h * HEAD_DIM, (h + 1) * HEAD_DIM)
    s = [jnp.where(valid, _mm_nt(qs[b][:, sl(h)], kcat[b][:, sl(h // group)]), NEG) for b, h in units]
    m = [jnp.max(t, axis=-1, keepdims=True) for t in s]
    if with_sink:
        sink = [sink_ref[:, h:h + 1] for _, h in units]
        m = [jnp.maximum(a, b) for a, b in zip(m, sink)]
    p = [jnp.exp(t - a) for t, a in zip(s, m)]
    l = [jnp.sum(t, axis=-1, keepdims=True) for t in p]
    if with_sink:
        l = [a + jnp.exp(b - c) for a, b, c in zip(l, sink, m)]
    o = [_mm(t, vcat[b][:, sl(h // group)]) / a for t, a, (b, h) in zip(p, l, units)]
    for b in range(nb):
        o_ref[b] = jnp.concatenate(o[b * N_HEADS:(b + 1) * N_HEADS], axis=-1)
        if with_lse:
            lse = [jnp.broadcast_to(m[i] + jnp.log(l[i]), (WIN, HEAD_DIM)) for i in range(b * N_HEADS, (b + 1) * N_HEADS)]
            out_refs[1][b] = jnp.concatenate(lse, axis=-1)


def _band_attention(p4, q_col, k_col, v_col, kv, sinks, with_lse):
    batch, r, ln, width = p4.shape
    bb = BATCH_BLOCK if batch % BATCH_BLOCK == 0 else 1
    wq, wk = W_MIX, kv * HEAD_DIM
    qb, kb, vb = q_col // wq, k_col // wk, v_col // wk
    cur = lambda off: (lambda b, rho, n: (b, rho, n, off))
    prev = lambda off: (lambda b, rho, n: (b, rho, jnp.maximum(n - 1, 0), off))
    out_spec = pl.BlockSpec((bb, None, WIN, W_MIX), lambda b, rho, n: (b, rho, n, 0))
    out_shape = jax.ShapeDtypeStruct((batch, r, ln, W_MIX), F32)
    n_out = 2 if with_lse else 1
    return pl.pallas_call(
        functools.partial(_band_kernel, kv=kv, with_sink=sinks is not None, with_lse=with_lse),
        grid=(batch // bb, r, ln // WIN),
        in_specs=[pl.BlockSpec((bb, None, WIN, wq), cur(qb)),
                  pl.BlockSpec((bb, None, WIN, wk), prev(kb)),
                  pl.BlockSpec((bb, None, WIN, wk), cur(kb)),
                  pl.BlockSpec((bb, None, WIN, wk), prev(vb)),
                  pl.BlockSpec((bb, None, WIN, wk), cur(vb)),
                  pl.BlockSpec((1, 128), lambda b, rho, n: (0, 0))],
        out_specs=[out_spec] * n_out,
        out_shape=[out_shape] * n_out,
        compiler_params=_cparams("parallel", "parallel", "arbitrary"),
        name=f"band_r{r}",
    )(p4, p4, p4, p4, p4, sinks if sinks is not None else jnp.zeros((1, 128), F32))


def _gated_norm_rows(o, gate, gain):
    return _rms_rows(o, gain) * _silu(gate)


def _hgrn_constants(c):
    halves = []
    h = c // 2
    while h >= 1:
        halves.append(h)
        h //= 2
    t = np.arange(c)[:, None]
    u = np.arange(c)[None, :]
    mats = [(u <= t).astype(np.float32)]
    level = np.full((c, c), -1, np.int32)
    level[np.arange(c), np.arange(c)] = 0
    for li, h in enumerate(halves, 1):
        mid = (t // (2 * h)) * (2 * h) + h
        second = (t % (2 * h)) >= h
        mats.append(np.where(second, (u > mid) & (u <= t), (u > t) & (u <= mid)).astype(np.float32))
        pair = (t // (2 * h) == u // (2 * h)) & ((t % (2 * h)) >= h) & ((u % (2 * h)) < h)
        level[pair] = li
    return np.concatenate(mats, axis=0), level, len(halves)


def _hgrn_kernel(pc_ref, lb_ref, gain_ref, mat_ref, lvl_ref, o_ref, st_ref, s_scr, *, c, n_levels):
    j = pl.program_id(1)
    nb = pc_ref.shape[0]

    @pl.when(j == 0)
    def _():
        s_scr[...] = jnp.zeros_like(s_scr)

    lb = lb_ref[...]
    lvl = lvl_ref[...]
    xs = [pc_ref[b] for b in range(nb)]
    fs = [lb + (1.0 - lb) * _sigmoid(x[:, 256:512]) for x in xs]
    sums = [_mm_exact_lhs(mat_ref[...], -jnp.log(f)) for f in fs]
    gcs = [-s[0:c] for s in sums]
    g_last = [gc[c - 1:c, :] for gc in gcs]
    q_dec = [x[:, 0:256] * jnp.exp(gc) for x, gc in zip(xs, gcs)]
    k_dec = [(1.0 - f) * jnp.exp(gl - gc) for f, gl, gc in zip(fs, g_last, gcs)]
    units = [(b, h) for b in range(nb) for h in range(N_HEADS)]
    sl = lambda h: slice(h * HEAD_DIM, (h + 1) * HEAD_DIM)
    qh = [xs[b][:, sl(h)] for b, h in units]
    kh = [1.0 - fs[b][:, sl(h)] for b, h in units]
    vh = [xs[b][:, 512 + h * HEAD_DIM:512 + (h + 1) * HEAD_DIM] for b, h in units]
    a = [jnp.where(lvl == 0, _mm_nt(q, k), 0.0) for q, k in zip(qh, kh)]
    for li in range(1, n_levels + 1):
        damp = [jnp.exp(-sums[b][li * c:(li + 1) * c, sl(h)]) for b, h in units]
        part = [_mm_nt(q * d, k * d) for q, k, d in zip(qh, kh, damp)]
        a = [acc + jnp.where(lvl == li, p, 0.0) for acc, p in zip(a, part)]
    st = [s_scr[b, h] for b, h in units]
    o_inter = [_mm_nt(q_dec[b][:, sl(h)], t) for (b, h), t in zip(units, st)]
    o = [oi + _mm(aa, v) for oi, aa, v in zip(o_inter, a, vh)]
    upd = [_mm_tn(v, k_dec[b][:, sl(h)]) for (b, h), v in zip(units, vh)]
    for i, (b, h) in enumerate(units):
        s_scr[b, h] = jnp.exp(g_last[b][:, sl(h)]) * st[i] + upd[i]
    for b in range(nb):
        outs = [_gated_norm_rows(o[b * N_HEADS + h], xs[b][:, 768 + h * HEAD_DIM:768 + (h + 1) * HEAD_DIM],
                                 gain_ref[...]) for h in range(N_HEADS)]
        o_ref[b] = jnp.concatenate(outs, axis=-1)

    @pl.when(j == pl.num_programs(1) - 1)
    def _():
        st_ref[...] = s_scr[...]


def _hgrn_prompt(pc3, lb, gain):
    batch, seq, _ = pc3.shape
    c = CHUNK
    bb = BATCH_BLOCK if batch % BATCH_BLOCK == 0 else 1
    mat, level, n_levels = _hgrn_constants(c)
    o, st = pl.pallas_call(
        functools.partial(_hgrn_kernel, c=c, n_levels=n_levels),
        grid=(batch // bb, seq // c),
        in_specs=[pl.BlockSpec((bb, c, COLS_C), lambda b, j: (b, j, 0)),
                  pl.BlockSpec((1, W_MIX), lambda b, j: (0, 0)),
                  pl.BlockSpec((1, HEAD_DIM), lambda b, j: (0, 0)),
                  pl.BlockSpec(mat.shape, lambda b, j: (0, 0)),
                  pl.BlockSpec(level.shape, lambda b, j: (0, 0))],
        out_specs=[pl.BlockSpec((bb, c, W_MIX), lambda b, j: (b, j, 0)),
                   pl.BlockSpec((bb, N_HEADS, HEAD_DIM, HEAD_DIM), lambda b, j: (b, 0, 0, 0))],
        out_shape=[jax.ShapeDtypeStruct((batch, seq, W_MIX), F32),
                   jax.ShapeDtypeStruct((batch, N_HEADS, HEAD_DIM, HEAD_DIM), F32)],
        scratch_shapes=[pltpu.VMEM((bb, N_HEADS, HEAD_DIM, HEAD_DIM), F32)],
        compiler_params=_cparams("parallel", "arbitrary"),
        name="hgrn_prompt",
    )(pc3, lb, gain, jnp.asarray(mat), jnp.asarray(level))
    return o.reshape(batch * seq, W_MIX), jnp.swapaxes(st, -1, -2)


def _unit_lower_solve(lows, rhss, c):
    xs = [rhs - _mm_3pass(low, rhs) for low, rhs in zip(lows, rhss)]
    ps = lows
    span = 2
    while span < c:
        mm = _mm_3pass if span == 2 else _mm
        ps = [mm(p, p) for p in ps]
        xs = [x + mm(p, x) for p, x in zip(ps, xs)]
        span *= 2
    return xs


def _gdn_kernel(pa_ref, pba_ref, cw_ref, hp_ref, gain_ref, tri_ref, ones_ref, o_ref, st_ref,
                s_scr, buf_scr, *, c):
    j = pl.program_id(1)
    pad = 8
    nb = pa_ref.shape[0]

    @pl.when(j == 0)
    def _():
        s_scr[...] = jnp.zeros_like(s_scr)
        buf_scr[:, 0:pad, :] = jnp.zeros((nb, pad, 3 * W_MIX), F32)

    ones = ones_ref[...]
    row = lax.broadcasted_iota(jnp.int32, (c, c), 0)
    col = lax.broadcasted_iota(jnp.int32, (c, c), 1)
    qs, ks, vs, gates, betas, gcs, gcts = [], [], [], [], [], [], []
    for b in range(nb):
        x = pa_ref[b, :, 0:3 * W_MIX]
        gates.append(pa_ref[b, :, 3 * W_MIX:4 * W_MIX])
        buf_scr[b, pad:pad + c, :] = x
        y = cw_ref[CONV_A - 1:CONV_A, :] * x
        for tap in range(CONV_A - 1):
            back = CONV_A - 1 - tap
            y = y + cw_ref[tap:tap + 1, :] * buf_scr[b, pad - back:pad - back + c, :]
        buf_scr[b, 0:pad, :] = buf_scr[b, c:c + pad, :]
        y = _silu(y)
        qs.append(y[:, 0:W_MIX])
        ks.append(y[:, W_MIX:2 * W_MIX])
        vs.append(y[:, 2 * W_MIX:3 * W_MIX])
        ba = pba_ref[b]
        betas.append(_sigmoid(ba))
        g = hp_ref[0:1, :] * _softplus(ba + hp_ref[1:2, :])
        gcs.append(_mm_exact_lhs(tri_ref[0], g))
        gcts.append(_mm_exact_rhs(g, tri_ref[1], (((0,), (0,)), ((), ()))))
    qs = [q * lax.rsqrt(_mm_exact_rhs(q * q, ones) + EPS) * SCALE for q in qs]
    ks = [k * lax.rsqrt(_mm_exact_rhs(k * k, ones) + EPS) for k in ks]
    units = [(b, h) for b in range(nb) for h in range(N_HEADS)]
    sl = lambda h: slice(h * HEAD_DIM, (h + 1) * HEAD_DIM)
    qh = [qs[b][:, sl(h)] for b, h in units]
    kh = [ks[b][:, sl(h)] for b, h in units]
    vh = [vs[b][:, sl(h)] for b, h in units]
    b_col = [betas[b][:, h:h + 1] for b, h in units]
    g_col = [gcs[b][:, 4 + h:5 + h] for b, h in units]
    g_row = [gcts[b][4 + h:5 + h, :] for b, h in units]
    decay = [jnp.where(row >= col, jnp.exp(jnp.minimum(gc_ - gr_, 0.0)), 0.0) for gc_, gr_ in zip(g_col, g_row)]
    kk = [_mm_nt(k, k) for k in kh]
    qk = [_mm_nt(q, k) for q, k in zip(qh, kh)]
    low = [jnp.where(row > col, bc * a * d, 0.0) for bc, a, d in zip(b_col, kk, decay)]
    eg = [jnp.exp(gc_) for gc_ in g_col]
    rhs = [jnp.concatenate([v * bc, k * (bc * e)], axis=-1) for v, k, bc, e in zip(vh, kh, b_col, eg)]
    sol = _unit_lower_solve(low, rhs, c)
    intra = [a * d for a, d in zip(qk, decay)]
    g_last = [gc_[c - 1:c, :] for gc_ in g_col]
    st = [s_scr[b, h] for b, h in units]
    u = [s[:, 0:HEAD_DIM] - _mm_nt(s[:, HEAD_DIM:2 * HEAD_DIM], t) for s, t in zip(sol, st)]
    o_inter = [_mm_nt(q * e, t) for q, e, t in zip(qh, eg, st)]
    o = [oi + _mm(a, uu) for oi, a, uu in zip(o_inter, intra, u)]
    k_dec = [k * jnp.exp(gl - gc_) for k, gl, gc_ in zip(kh, g_last, g_col)]
    upd = [_mm_tn(uu, kd) for uu, kd in zip(u, k_dec)]
    for i, (b, h) in enumerate(units):
        s_scr[b, h] = jnp.exp(g_last[i]) * st[i] + upd[i]
    for b in range(nb):
        outs = [_gated_norm_rows(o[b * N_HEADS + h], gates[b][:, sl(h)], gain_ref[...]) for h in range(N_HEADS)]
        o_ref[b] = jnp.concatenate(outs, axis=-1)

    @pl.when(j == pl.num_programs(1) - 1)
    def _():
        st_ref[...] = s_scr[...]


def _gdn_prompt(pa3, pba3, conv_w, head_params, gain):
    batch, seq, _ = pa3.shape
    c = CHUNK
    bb = BATCH_BLOCK if batch % BATCH_BLOCK == 0 else 1
    lower = np.tril(np.ones((c, c), np.float32))
    tri = np.stack([lower, lower.T], axis=0)
    ones = np.kron(np.eye(N_HEADS, dtype=np.float32), np.ones((HEAD_DIM, HEAD_DIM), np.float32))
    o, st = pl.pallas_call(
        functools.partial(_gdn_kernel, c=c),
        grid=(batch // bb, seq // c),
        in_specs=[pl.BlockSpec((bb, c, COLS_A), lambda b, j: (b, j, 0)),
                  pl.BlockSpec((bb, c, COLS_BA), lambda b, j: (b, j, 0)),
                  pl.BlockSpec((CONV_A, 3 * W_MIX), lambda b, j: (0, 0)),
                  pl.BlockSpec((2, COLS_BA), lambda b, j: (0, 0)),
                  pl.BlockSpec((1, HEAD_DIM), lambda b, j: (0, 0)),
                  pl.BlockSpec((2, c, c), lambda b, j: (0, 0, 0)),
                  pl.BlockSpec((W_MIX, W_MIX), lambda b, j: (0, 0))],
        out_specs=[pl.BlockSpec((bb, c, W_MIX), lambda b, j: (b, j, 0)),
                   pl.BlockSpec((bb, N_HEADS, HEAD_DIM, HEAD_DIM), lambda b, j: (b, 0, 0, 0))],
        out_shape=[jax.ShapeDtypeStruct((batch, seq, W_MIX), F32),
                   jax.ShapeDtypeStruct((batch, N_HEADS, HEAD_DIM, HEAD_DIM), F32)],
        scratch_shapes=[pltpu.VMEM((bb, N_HEADS, HEAD_DIM, HEAD_DIM), F32),
                        pltpu.VMEM((bb, c + 8, 3 * W_MIX), F32)],
        compiler_params=_cparams("parallel", "arbitrary"),
        name="gdn_prompt",
    )(pa3, pba3, conv_w, head_params, gain, jnp.asarray(tri), jnp.asarray(ones))
    return o.reshape(batch * seq, W_MIX), jnp.swapaxes(st, -1, -2)


def _gdn_head_params(a_log, dt_bias):
    neg_a = jnp.pad(-jnp.exp(a_log.astype(F32)), (4, COLS_BA - 8))
    dtb = jnp.pad(dt_bias.astype(F32), (4, COLS_BA - 8))
    return jnp.stack([neg_a, dtb], axis=0)


def _outproj_kernel(*refs, n_branch):
    x_ref, oa_ref, ob_ref, oc_ref = refs[0:4]
    tm = x_ref.shape[0]
    if n_branch == 1:
        od = refs[4][...]
        w_ref, o_ref = refs[5], refs[6]
    else:
        d_refs = refs[4:4 + n_branch]
        l_refs = refs[4 + n_branch:4 + 2 * n_branch]
        w_ref, o_ref = refs[4 + 2 * n_branch], refs[5 + 2 * n_branch]
        scratch = refs[6 + 2 * n_branch:]

        def token_order(ref, scr):
            r, rows = ref.shape[1], ref.shape[2]
            if r == 1:
                return ref[0, 0]
            n_chunks = scr.shape[0]
            for rho in range(r):
                for ch in range(n_chunks):
                    scr[ch, pl.ds(rho, rows, stride=r), :] = ref[0, rho, :, ch * 128:(ch + 1) * 128]
            return jnp.concatenate([scr[ch] for ch in range(n_chunks)], axis=-1)

        outs = [token_order(ref, scratch[2 * i]) for i, ref in enumerate(d_refs)]
        lses = [token_order(ref, scratch[2 * i + 1]) for i, ref in enumerate(l_refs)]
        m = functools.reduce(jnp.maximum, lses)
        es = [jnp.exp(l - m) for l in lses]
        od = sum(e * o for e, o in zip(es, outs)) / sum(es)
    acc = x_ref[...]
    for i, part in enumerate((oa_ref[...], ob_ref[...], oc_ref[...], od)):
        acc = acc + _mm(part, w_ref[i * W_MIX:(i + 1) * W_MIX, :])
    o_ref[...] = acc


def _outproj(x2, oa, ob, oc, ods, lses, w_bf, tm):
    n = x2.shape[0]
    n_branch = len(ods)
    row = lambda w: pl.BlockSpec((tm, w), lambda i: (i, 0))
    specs = [row(D_MODEL)] + [row(W_MIX)] * 3
    scratch = []
    if n_branch == 1:
        specs.append(row(W_MIX))
    else:
        per_seq = (ods[0].shape[1] * ods[0].shape[2]) // tm
        for t in (*ods, *lses):
            r = t.shape[1]
            specs.append(pl.BlockSpec((1, r, tm // r, W_MIX), lambda i: (i // per_seq, 0, i % per_seq, 0)))
        scratch = [pltpu.VMEM((W_MIX // 128, tm, 128), F32)] * (2 * n_branch)
    specs.append(pl.BlockSpec((4 * W_MIX, D_MODEL), lambda i: (0, 0)))
    return pl.pallas_call(
        functools.partial(_outproj_kernel, n_branch=n_branch),
        grid=(n // tm,),
        in_specs=specs,
        out_specs=row(D_MODEL),
        out_shape=jax.ShapeDtypeStruct((n, D_MODEL), F32),
        scratch_shapes=scratch,
        compiler_params=_cparams("parallel"),
        name="outproj",
    )(x2, oa, ob, oc, *ods, *lses, w_bf)


def _ffn_kernel(x_ref, g_ref, wg_ref, wu_ref, wd_ref, o_ref, h_scr, acc_scr):
    f = pl.program_id(1)

    @pl.when(f == 0)
    def _():
        h_scr[...] = _rms_rows(x_ref[...], g_ref[...]).astype(BF16)
        acc_scr[...] = jnp.zeros_like(acc_scr)

    h = h_scr[...]
    gate = jnp.dot(h, wg_ref[...], preferred_element_type=F32)
    up = jnp.dot(h, wu_ref[...], preferred_element_type=F32)
    acc_scr[...] += _mm(_silu(gate) * up, wd_ref[...])

    @pl.when(f == pl.num_programs(1) - 1)
    def _():
        o_ref[...] = x_ref[...] + acc_scr[...]


def _ffn(x2, gain, w_gu_bf, w_down_bf, tm, tf):
    n = x2.shape[0]
    nf = D_FF // tf
    return pl.pallas_call(
        _ffn_kernel,
        grid=(n // tm, nf),
        in_specs=[pl.BlockSpec((tm, D_MODEL), lambda i, f: (i, 0)),
                  pl.BlockSpec((1, D_MODEL), lambda i, f: (0, 0)),
                  pl.BlockSpec((D_MODEL, tf), lambda i, f: (0, f)),
                  pl.BlockSpec((D_MODEL, tf), lambda i, f: (0, nf + f)),
                  pl.BlockSpec((tf, D_MODEL), lambda i, f: (f, 0))],
        out_specs=pl.BlockSpec((tm, D_MODEL), lambda i, f: (i, 0)),
        out_shape=jax.ShapeDtypeStruct((n, D_MODEL), F32),
        scratch_shapes=[pltpu.VMEM((tm, D_MODEL), BF16), pltpu.VMEM((tm, D_MODEL), F32)],
        compiler_params=_cparams("parallel", "arbitrary"),
        name="ffn",
    )(x2, gain, w_gu_bf, w_gu_bf, w_down_bf)


ROW_TILE = (8, 128)


def _router_kernel(x_ref, g_ref, wr_ref, br_ref, h_ref, route_ref):
    i = pl.program_id(0)
    last = pl.num_programs(0) - 1
    tm = x_ref.shape[0]

    @pl.when(i < last)
    def _():
        h = _rms_rows(x_ref[...], g_ref[...])
        h_ref[...] = h.reshape(tm, *ROW_TILE)
        lane = lax.broadcasted_iota(jnp.int32, (tm, 128), 1).astype(F32)
        logits = _mm_f32(h, wr_ref[...]) + br_ref[...]
        m1 = jnp.max(logits, axis=-1, keepdims=True)
        i1 = jnp.min(jnp.where(logits == m1, lane, 128.0), axis=-1, keepdims=True)
        rest = jnp.where(lane == i1, NEG, logits)
        m2 = jnp.max(rest, axis=-1, keepdims=True)
        i2 = jnp.min(jnp.where(rest == m2, lane, 128.0), axis=-1, keepdims=True)
        e2 = jnp.exp(m2 - m1)
        route_ref[...] = (jnp.where(lane == 0.0, i1, 0.0) + jnp.where(lane == 1.0, i2, 0.0)
                          + jnp.where(lane == 2.0, 1.0 / (1.0 + e2), 0.0)
                          + jnp.where(lane == 3.0, e2 / (1.0 + e2), 0.0))

    @pl.when(i == last)
    def _():
        h_ref[...] = jnp.zeros(h_ref.shape, F32)
        route_ref[...] = jnp.zeros(route_ref.shape, F32)


def _expert_kernel(be_ref, tok_ref, h_hbm, wg_ref, wu_ref, wd_ref, o_ref, xbuf, xb_scr, acc_scr, sems):
    i = pl.program_id(0)
    f = pl.program_id(1)
    tm = xb_scr.shape[0]

    def gather(block, slot):
        base = block * tm

        def issue(r, carry):
            pltpu.async_copy(h_hbm.at[tok_ref[base + r]], xbuf.at[slot, r], sems.at[slot])
            return carry

        lax.fori_loop(0, tm, issue, 0, unroll=2)

    @pl.when(f == 0)
    def _():
        slot = i % 2

        @pl.when(i == 0)
        def _():
            gather(0, 0)

        pltpu.make_async_copy(h_hbm.at[pl.ds(0, tm)], xbuf.at[slot], sems.at[slot]).wait()
        xb_scr[...] = xbuf[slot].reshape(tm, D_MODEL).astype(BF16)
        acc_scr[...] = jnp.zeros_like(acc_scr)

        @pl.when(i + 1 < pl.num_programs(0))
        def _():
            gather(i + 1, 1 - slot)

    xb = xb_scr[...]
    gate = jnp.dot(xb, wg_ref[0], preferred_element_type=F32)
    up = jnp.dot(xb, wu_ref[0], preferred_element_type=F32)
    acc_scr[...] += _mm(_silu(gate) * up, wd_ref[0])

    @pl.when(f == pl.num_programs(1) - 1)
    def _():
        o_ref[...] = acc_scr[...].reshape(tm, *ROW_TILE)


def _combine_kernel(pos_ref, x_ref, route_ref, y_hbm, *rest, with_final_norm):
    final_ref = rest[0] if with_final_norm else None
    o_ref, first_scr, second_scr, sems = rest[int(with_final_norm):]
    tc = x_ref.shape[0]
    base = pl.program_id(0) * (2 * tc)

    def issue(r, carry):
        pltpu.async_copy(y_hbm.at[pos_ref[base + 2 * r]], first_scr.at[r], sems.at[0], priority=0)
        pltpu.async_copy(y_hbm.at[pos_ref[base + 2 * r + 1]], second_scr.at[r], sems.at[1], priority=1)
        return carry

    lax.fori_loop(0, tc, issue, 0)
    pltpu.make_async_copy(y_hbm.at[pl.ds(0, tc)], first_scr, sems.at[0]).wait()
    pltpu.make_async_copy(y_hbm.at[pl.ds(0, tc)], second_scr, sems.at[1]).wait()
    route = route_ref[...]
    y = (x_ref[...] + route[:, 2:3] * first_scr[...].reshape(tc, D_MODEL)
         + route[:, 3:4] * second_scr[...].reshape(tc, D_MODEL))
    o_ref[...] = y if final_ref is None else _rms_rows(y, final_ref[...])


def _moe_routing(route, n, tm_rows):
    e_flat = route[:n, 0:TOP_K].astype(jnp.int32).reshape(-1)
    n_assign = n * TOP_K
    onehot = (e_flat[:, None] == jnp.arange(N_EXPERTS, dtype=jnp.int32)[None, :]).astype(jnp.int32)
    csum = jnp.cumsum(onehot, axis=0)
    rank = jnp.sum(csum * onehot, axis=1) - 1
    counts = csum[-1]
    padded = (counts + tm_rows - 1) // tm_rows * tm_rows
    pad_end = jnp.cumsum(padded)
    pad_start = pad_end - padded
    start = jnp.cumsum(counts) - counts
    dest = pad_start[e_flat] + rank
    n_blocks = -(-(n_assign + N_EXPERTS * (tm_rows - 1)) // tm_rows)
    blk_e = jnp.minimum(jnp.sum(jnp.arange(n_blocks, dtype=jnp.int32)[:, None] * tm_rows >= pad_end[None, :], axis=1),
                        N_EXPERTS - 1).astype(jnp.int32)
    order = jnp.argsort(e_flat, stable=True).astype(jnp.int32)
    rows = jnp.arange(n_blocks * tm_rows, dtype=jnp.int32)
    row_e = jnp.repeat(blk_e, tm_rows)
    offset = rows - pad_start[row_e]
    valid = (offset < counts[row_e]) & (rows < pad_end[N_EXPERTS - 1])
    src = jnp.clip(start[row_e] + offset, 0, n_assign - 1)
    row_tok = jnp.where(valid, order[src] // TOP_K, n).astype(jnp.int32)
    return row_tok, dest.astype(jnp.int32), blk_e, n_blocks


def _moe(x2, gain, w_router, b_router, w_gu_bf, w_down_bf, tm, final_gain=None):
    n = x2.shape[0]
    rows = min(MOE_ROWS, max(128, n * TOP_K // N_EXPERTS))
    wr = jnp.pad(w_router.astype(F32), ((0, 0), (0, 128 - N_EXPERTS)))
    br = jnp.pad(b_router.astype(F32), (0, 128 - N_EXPERTS), constant_values=NEG).reshape(1, 128)
    nt = n // tm
    h3, route = pl.pallas_call(
        _router_kernel,
        grid=(nt + 1,),
        in_specs=[pl.BlockSpec((tm, D_MODEL), lambda i: (jnp.minimum(i, nt - 1), 0)),
                  pl.BlockSpec((1, D_MODEL), lambda i: (0, 0)),
                  pl.BlockSpec((D_MODEL, 128), lambda i: (0, 0)),
                  pl.BlockSpec((1, 128), lambda i: (0, 0))],
        out_specs=[pl.BlockSpec((tm, *ROW_TILE), lambda i: (i, 0, 0)),
                   pl.BlockSpec((tm, 128), lambda i: (i, 0))],
        out_shape=[jax.ShapeDtypeStruct((n + tm, *ROW_TILE), F32),
                   jax.ShapeDtypeStruct((n + tm, 128), F32)],
        compiler_params=_cparams("arbitrary"),
        name="moe_router",
    )(x2, gain, wr, br)

    row_tok, dest, blk_e, n_blocks = _moe_routing(route, n, rows)
    n_rows = n_blocks * rows

    nf = D_FF_E // MOE_FF_TILE
    y_rows = pl.pallas_call(
        _expert_kernel,
        grid_spec=pltpu.PrefetchScalarGridSpec(
            num_scalar_prefetch=2,
            grid=(n_blocks, nf),
            in_specs=[pl.BlockSpec(memory_space=pl.ANY),
                      pl.BlockSpec((1, D_MODEL, MOE_FF_TILE), lambda i, f, be, tok: (be[i], 0, f)),
                      pl.BlockSpec((1, D_MODEL, MOE_FF_TILE), lambda i, f, be, tok: (be[i], 0, nf + f)),
                      pl.BlockSpec((1, MOE_FF_TILE, D_MODEL), lambda i, f, be, tok: (be[i], f, 0))],
            out_specs=pl.BlockSpec((rows, *ROW_TILE), lambda i, f, be, tok: (i, 0, 0)),
            scratch_shapes=[pltpu.VMEM((2, rows, *ROW_TILE), F32), pltpu.VMEM((rows, D_MODEL), BF16),
                            pltpu.VMEM((rows, D_MODEL), F32), pltpu.SemaphoreType.DMA((2,))]),
        out_shape=jax.ShapeDtypeStruct((n_rows, *ROW_TILE), F32),
        compiler_params=_cparams("arbitrary", "arbitrary"),
        name="moe_experts",
    )(blk_e, row_tok, h3, w_gu_bf, w_gu_bf, w_down_bf)

    final = [] if final_gain is None else [final_gain]
    return pl.pallas_call(
        functools.partial(_combine_kernel, with_final_norm=bool(final)),
        grid_spec=pltpu.PrefetchScalarGridSpec(
            num_scalar_prefetch=1,
            grid=(nt,),
            in_specs=[pl.BlockSpec((tm, D_MODEL), lambda i, pos: (i, 0)),
                      pl.BlockSpec((tm, 128), lambda i, pos: (i, 0)),
                      pl.BlockSpec(memory_space=pl.ANY)]
            + [pl.BlockSpec((1, D_MODEL), lambda i, pos: (0, 0))] * len(final),
            out_specs=pl.BlockSpec((tm, D_MODEL), lambda i, pos: (i, 0)),
            scratch_shapes=[pltpu.VMEM((tm, *ROW_TILE), F32), pltpu.VMEM((tm, *ROW_TILE), F32),
                            pltpu.SemaphoreType.DMA((2,))]),
        out_shape=jax.ShapeDtypeStruct((n, D_MODEL), F32),
        compiler_params=_cparams("arbitrary"),
        name="moe_combine",
    )(dest, x2, route, y_rows, *final)


def _norm_kernel(x_ref, g_ref, o_ref):
    o_ref[...] = _rms_rows(x_ref[...], g_ref[...])


def _final_norm(x2, gain, tm):
    n = x2.shape[0]
    return pl.pallas_call(
        _norm_kernel,
        grid=(n // tm,),
        in_specs=[pl.BlockSpec((tm, D_MODEL), lambda i: (i, 0)),
                  pl.BlockSpec((1, D_MODEL), lambda i: (0, 0))],
        out_specs=pl.BlockSpec((tm, D_MODEL), lambda i: (i, 0)),
        out_shape=jax.ShapeDtypeStruct((n, D_MODEL), F32),
        compiler_params=_cparams("parallel"),
        name="final_norm",
    )(x2, gain)


ROWS_T = 8
LANE_TILE = 128


def _sattn_kernel(qd_ref, qb_ref, kd_all, vd_all, kdn_all, vdn_all, kb_all, vb_all, kbn_all, vbn_all, sink_ref,
                  od_ref, ob_ref, *rolled_refs, t_new, here):
    kd_ref, vd_ref, kdn_ref, vdn_ref, kb_ref, vb_ref, kbn_ref, vbn_ref = (
        r.at[here] for r in (kd_all, vd_all, kdn_all, vdn_all, kb_all, vb_all, kbn_all, vbn_all))
    bb = qd_ref.shape[0]
    first_new = LANE_TILE - t_new

    def distances(n_cache):
        row = lax.broadcasted_iota(jnp.int32, (ROWS_T, n_cache), 0)
        col = lax.broadcasted_iota(jnp.int32, (ROWS_T, n_cache), 1)
        rown = lax.broadcasted_iota(jnp.int32, (ROWS_T, LANE_TILE), 0)
        j = lax.broadcasted_iota(jnp.int32, (ROWS_T, LANE_TILE), 1) - first_new
        d_new = rown - j
        return n_cache + row - col, d_new, (j >= 0) & (d_new >= 0)

    def softmax_parts(sc, sn, ok_c, ok_n, sink):
        sc = jnp.where(ok_c, sc, NEG)
        sn = jnp.where(ok_n, sn, NEG)
        m = jnp.maximum(jnp.max(sc, axis=-1, keepdims=True), jnp.max(sn, axis=-1, keepdims=True))
        if sink is not None:
            m = jnp.maximum(m, sink)
        pc = jnp.exp(sc - m)
        pn = jnp.exp(sn - m)
        l = jnp.sum(pc, axis=-1, keepdims=True) + jnp.sum(pn, axis=-1, keepdims=True)
        if sink is not None:
            l = l + jnp.exp(sink - m)
        return pc, pn, l, m + jnp.log(l)

    sl = lambda h: slice(h * HEAD_DIM, (h + 1) * HEAD_DIM)
    units = [(i, h) for i in range(bb) for h in range(N_HEADS)]
    dc, dn, ok_new = distances(kd_ref.shape[-1])
    qs = [qd_ref[i][:, sl(h)] * SCALE for i, h in units]
    sc = [_mm(q, kd_ref[i, h]) for q, (i, h) in zip(qs, units)]
    sn = [_mm(q, kdn_ref[i, h]) for q, (i, h) in zip(qs, units)]
    pcs, pns, ls, lses = [], [], [], []
    for r in DILATIONS:
        ok_c = (dc <= WIN * r) & ((dc & (r - 1)) == 0)
        ok_n = ok_new & ((dn & (r - 1)) == 0)
        parts = [softmax_parts(a, b, ok_c, ok_n, None) for a, b in zip(sc, sn)]
        pcs.append([p[0] for p in parts])
        pns.append([p[1] for p in parts])
        ls.append([p[2] for p in parts])
        lses.append([p[3] for p in parts])
    nbr = len(DILATIONS)
    acc = [_mm_nt(jnp.concatenate([pcs[r][u] for r in range(nbr)], axis=0), vd_ref[i, h])
           + _mm_nt(jnp.concatenate([pns[r][u] for r in range(nbr)], axis=0), vdn_ref[i, h])
           for u, (i, h) in enumerate(units)]
    outs = []
    for u in range(len(units)):
        m = functools.reduce(jnp.maximum, [lses[r][u] for r in range(nbr)])
        es = [jnp.exp(lses[r][u] - m) for r in range(nbr)]
        num = sum(es[r] * (acc[u][r * ROWS_T:(r + 1) * ROWS_T] / ls[r][u]) for r in range(nbr))
        outs.append(num / sum(es))
    for i in range(bb):
        od_ref[i] = jnp.concatenate(outs[i * N_HEADS:(i + 1) * N_HEADS], axis=-1)
    group = N_HEADS // KV_B
    dc, dn, ok_new = distances(kb_ref.shape[-1])
    qs = [qb_ref[i][:, sl(h)] * SCALE for i, h in units]
    parts = [softmax_parts(_mm(q, kb_ref[i, h // group]), _mm(q, kbn_ref[i, h // group]), dc <= WIN, ok_new,
                           sink_ref[:, h:h + 1]) for q, (i, h) in zip(qs, units)]
    outs = [(_mm_nt(p[0], vb_ref[i, h // group]) + _mm_nt(p[1], vbn_ref[i, h // group])) / p[2]
            for p, (i, h) in zip(parts, units)]
    for i in range(bb):
        ob_ref[i] = jnp.concatenate(outs[i * N_HEADS:(i + 1) * N_HEADS], axis=-1)
    lane = lax.broadcasted_iota(jnp.int32, (HEAD_DIM, LANE_TILE), 1)
    if not rolled_refs:
        return
    for src, new, dst in zip((kd_all, vd_all, kb_all, vb_all), (kdn_all, vdn_all, kbn_all, vbn_all), rolled_refs):
        n = src.shape[-1]
        for layer in range(src.shape[0]):
            for i in range(bb):
                for h in range(src.shape[2]):
                    rolled = pltpu.roll(src[layer, i, h], n - t_new, axis=1)
                    if n > LANE_TILE:
                        dst[layer, i, h, :, 0:n - LANE_TILE] = rolled[:, 0:n - LANE_TILE]
                    dst[layer, i, h, :, n - LANE_TILE:n] = jnp.where(lane >= first_new, new[layer, i, h],
                                                                     rolled[:, n - LANE_TILE:n])


def _pad_rows(t, rows):
    return jnp.pad(t, ((0, 0), (0, rows - t.shape[1]), (0, 0)))


def _new_columns(t, n_heads):
    bs, t_new, _ = t.shape
    x = jnp.transpose(t.reshape(bs, t_new, n_heads, HEAD_DIM), (0, 2, 3, 1))
    return jnp.pad(x, ((0, 0), (0, 0), (0, 0), (LANE_TILE - t_new, 0)))


def _sample_attention(layer, pb_s, pd_s, caches_t, new_cols, sinks):
    bs, t_new, _ = pb_s.shape
    depth = caches_t[0].shape[0]
    last = layer == depth - 1
    bb = 1 if last else 2
    pb8, pd8 = _pad_rows(pb_s, ROWS_T), _pad_rows(pd_s, ROWS_T)
    sink_row = jnp.pad(sinks.astype(F32), (0, 128 - N_HEADS)).reshape(1, 128)
    cbk, cbv, cdk, cdv = caches_t
    if last:
        news = [jnp.stack([layer_cols[k] for layer_cols in new_cols], axis=0) for k in range(4)]
    else:
        news = [new_cols[layer][k][None] for k in range(4)]
    n_slab = depth if last else 1
    qblk = lambda: pl.BlockSpec((bb, ROWS_T, W_MIX), lambda i: (i, 0, 0))
    slab = lambda c: pl.BlockSpec((n_slab, bb) + c.shape[2:], lambda i: (0 if last else layer, i, 0, 0, 0))
    newblk = lambda c: pl.BlockSpec((n_slab, bb) + c.shape[2:], lambda i: (0, i, 0, 0, 0))
    rolled_specs = [slab(c) for c in (cdk, cdv, cbk, cbv)] if last else []
    rolled_shapes = [jax.ShapeDtypeStruct(c.shape, c.dtype) for c in (cdk, cdv, cbk, cbv)] if last else []
    res = pl.pallas_call(
        functools.partial(_sattn_kernel, t_new=t_new, here=layer if last else 0),
        grid=(bs // bb,),
        in_specs=[qblk(), qblk(), slab(cdk), slab(cdv), newblk(news[0]), newblk(news[1]),
                  slab(cbk), slab(cbv), newblk(news[2]), newblk(news[3]),
                  pl.BlockSpec((1, 128), lambda i: (0, 0))],
        out_specs=[qblk(), qblk()] + rolled_specs,
        out_shape=[jax.ShapeDtypeStruct((bs, ROWS_T, W_MIX), F32)] * 2 + rolled_shapes,
        compiler_params=_cparams("arbitrary"),
        name="sample_attn",
    )(pd8, pb8, cdk, cdv, news[0], news[1], cbk, cbv, news[2], news[3], sink_row)
    od, ob = res[0], res[1]
    return (ob[:, :t_new].reshape(bs * t_new, W_MIX), od[:, :t_new].reshape(bs * t_new, W_MIX), tuple(res[2:]))


def _srec_kernel(xq_ref, xk_ref, xv_ref, bq_ref, bk_ref, bv_ref, cwq_ref, cwk_ref, cwv_ref, ga_ref,
                 ba_ref, hp_ref, gna_ref, sa_ref, qc_ref, fc_ref, ic_ref, gcg_ref, lb_ref, gnc_ref, sc_ref,
                 oa_ref, sa_out, oc_ref, sc_out, q_scr, k_scr, d_scr, *, t_new):
    nb = sa_ref.shape[-1]
    zero = jnp.zeros((HEAD_DIM, nb), F32)

    def conv(x_ref, b_ref, cw_ref, t):
        y = None
        for tap in range(CONV_A):
            pos = t + tap
            src = b_ref[pos] if pos < CONV_A - 1 else x_ref[pos - (CONV_A - 1)]
            term = cw_ref[tap] * src
            y = term if y is None else y + term
        return _silu(y)

    def l2(x):
        return x * lax.rsqrt(jnp.sum(x * x, axis=0, keepdims=True) + EPS)

    def gated_norm(o, gate, gain):
        return o * lax.rsqrt(jnp.mean(o * o, axis=0, keepdims=True) + EPS) * gain * _silu(gate)

    sa_out[0] = sa_ref[0]
    for t in range(t_new):
        q_scr[...] = l2(conv(xq_ref, bq_ref, cwq_ref, t)) * SCALE
        k_scr[...] = l2(conv(xk_ref, bk_ref, cwk_ref, t))
        v = conv(xv_ref, bv_ref, cwv_ref, t)
        beta = _sigmoid(ba_ref[0, t:t + 1, :])
        dec = jnp.exp(hp_ref[0, 0:1, :] * _softplus(ba_ref[0, t_new + t:t_new + t + 1, :] + hp_ref[0, 1:2, :]))

        def decay_and_read(kk, acc):
            s = sa_out[0, kk] * dec
            sa_out[0, kk] = s
            return acc + k_scr[pl.ds(kk, 1), :] * s

        err = (v - lax.fori_loop(0, HEAD_DIM, decay_and_read, zero)) * beta

        def write_and_query(kk, acc):
            s = sa_out[0, kk] + k_scr[pl.ds(kk, 1), :] * err
            sa_out[0, kk] = s
            return acc + q_scr[pl.ds(kk, 1), :] * s

        o = lax.fori_loop(0, HEAD_DIM, write_and_query, zero)
        oa_ref[t] = gated_norm(o, ga_ref[t], gna_ref[...])

    sc_out[0] = sc_ref[0]
    for t in range(t_new):
        lb = lb_ref[...]
        f = lb + (1.0 - lb) * _sigmoid(fc_ref[t])
        q_scr[...] = qc_ref[t]
        k_scr[...] = 1.0 - f
        d_scr[...] = jnp.exp(jnp.log(f))
        v = ic_ref[t]

        def update(kk, acc):
            s = sc_out[0, kk] * d_scr[pl.ds(kk, 1), :] + k_scr[pl.ds(kk, 1), :] * v
            sc_out[0, kk] = s
            return acc + q_scr[pl.ds(kk, 1), :] * s

        o = lax.fori_loop(0, HEAD_DIM, update, zero)
        oc_ref[t] = gated_norm(o, gcg_ref[t], gnc_ref[...])


def _sample_recurrences(pa_s, pba_s, pc_s, conv_buf, s_a, s_c, conv_w, a_log, dt_bias, norm_a, lb, norm_c):
    bs, t_new, _ = pa_s.shape
    lanes_last = lambda t: jnp.transpose(t, (1, 2, 0))
    pa_t = lanes_last(pa_s)
    pc_t = lanes_last(pc_s)
    buf_t = lanes_last(conv_buf.astype(F32))
    ba = jnp.transpose(pba_s[:, :, 0:8], (2, 1, 0))
    ba = jnp.concatenate([ba[0:N_HEADS], ba[N_HEADS:2 * N_HEADS]], axis=1)
    hp = jnp.stack([-jnp.exp(a_log.astype(F32)), dt_bias.astype(F32)], axis=1)
    hp = jnp.broadcast_to(hp[:, :, None], (N_HEADS, 2, bs))
    cw = conv_w.astype(F32)[:, :, None]
    sa_t = jnp.transpose(s_a.astype(F32), (1, 2, 3, 0))
    sc_t = jnp.transpose(s_c.astype(F32), (1, 2, 3, 0))
    col = lambda v: v.astype(F32).reshape(-1, 1)
    hd = HEAD_DIM
    feat = lambda rows, off: pl.BlockSpec((rows, hd, bs), lambda h: (0, off + h, 0))
    cwspec = lambda off: pl.BlockSpec((CONV_A, hd, 1), lambda h: (0, off + h, 0))
    per_head = lambda rows: pl.BlockSpec((1, rows, bs), lambda h: (h, 0, 0))
    state = pl.BlockSpec((1, hd, hd, bs), lambda h: (h, 0, 0, 0))
    vec = pl.BlockSpec((hd, 1), lambda h: (0, 0))
    nh = N_HEADS
    oa, sa_n, oc, sc_n = pl.pallas_call(
        functools.partial(_srec_kernel, t_new=t_new),
        grid=(N_HEADS,),
        in_specs=[feat(t_new, 0), feat(t_new, nh), feat(t_new, 2 * nh),
                  feat(CONV_A - 1, 0), feat(CONV_A - 1, nh), feat(CONV_A - 1, 2 * nh),
                  cwspec(0), cwspec(nh), cwspec(2 * nh),
                  feat(t_new, 3 * nh), per_head(2 * t_new), per_head(2), vec, state,
                  feat(t_new, 0), feat(t_new, nh), feat(t_new, 2 * nh), feat(t_new, 3 * nh),
                  pl.BlockSpec((hd, 1), lambda h: (h, 0)), vec, state],
        out_specs=[feat(t_new, 0), state, feat(t_new, 0), state],
        out_shape=[jax.ShapeDtypeStruct((t_new, W_MIX, bs), F32),
                   jax.ShapeDtypeStruct((N_HEADS, hd, hd, bs), F32),
                   jax.ShapeDtypeStruct((t_new, W_MIX, bs), F32),
                   jax.ShapeDtypeStruct((N_HEADS, hd, hd, bs), F32)],
        scratch_shapes=[pltpu.VMEM((hd, bs), F32)] * 3,
        compiler_params=_cparams("parallel"),
        name="sample_recurrences",
    )(pa_t, pa_t, pa_t, buf_t, buf_t, buf_t, cw, cw, cw, pa_t, ba, hp, col(norm_a), sa_t,
      pc_t, pc_t, pc_t, pc_t, col(lb), col(norm_c), sc_t)
    rows_first = lambda t: jnp.transpose(t, (2, 0, 1)).reshape(bs * t_new, W_MIX)
    back = lambda t: jnp.transpose(t, (3, 0, 1, 2))
    return rows_first(oa), back(sa_n), rows_first(oc), back(sc_n)


def _heads(t, n):
    return t.reshape(t.shape[0], t.shape[1], n, HEAD_DIM)


def _prompt_mixers(pa, pb, pc, pd, pba, pd_strided, batch, seq, conv_w, head_params, norm_a, sinks, lb, norm_c):
    pa3 = pa.reshape(batch, seq, COLS_A)
    pb3 = pb.reshape(batch, seq, COLS_B)
    pc3 = pc.reshape(batch, seq, COLS_C)
    pd3 = pd.reshape(batch, seq, COLS_D)
    oa, s_a = _gdn_prompt(pa3, pba.reshape(batch, seq, COLS_BA), conv_w, head_params, norm_a)
    oc, s_c = _hgrn_prompt(pc3, lb, norm_c)
    sink_row = jnp.pad(sinks.astype(F32), (0, 128 - N_HEADS)).reshape(1, 128)
    (ob,) = _band_attention(pb3[:, None], 0, 256, 384, KV_B, sink_row, False)
    ods, lses = [], []
    for p4 in (pd3[:, None], *pd_strided):
        o, lse = _band_attention(p4, 0, 256, 512, N_HEADS, None, True)
        ods.append(o)
        lses.append(lse)
    nb, nd = min(CACHE_B, seq), min(CACHE_D, seq)
    state = (_heads(pb3[:, seq - nb:, 256:384], KV_B), _heads(pb3[:, seq - nb:, 384:512], KV_B),
             _heads(pd3[:, seq - nd:, 256:512], N_HEADS), _heads(pd3[:, seq - nd:, 512:768], N_HEADS),
             pa3[:, seq - (CONV_A - 1):, 0:3 * W_MIX], s_a, s_c)
    return oa, ob.reshape(batch * seq, W_MIX), oc, ods, lses, state


def _sample_mixers(layer, pa, pb, pc, pd, pba, bs, t_new, caches_t, new_cols, states, conv_w, a_log, dt_bias,
                   norm_a, sinks, lb, norm_c):
    conv_buf, s_a, s_c = states
    pa3 = pa.reshape(bs, t_new, COLS_A)
    pb3 = pb.reshape(bs, t_new, COLS_B)
    pc3 = pc.reshape(bs, t_new, COLS_C)
    pd3 = pd.reshape(bs, t_new, COLS_D)
    new_cols.append((_new_columns(pd3[:, :, 256:512], N_HEADS), _new_columns(pd3[:, :, 512:768], N_HEADS),
                     _new_columns(pb3[:, :, 256:384], KV_B), _new_columns(pb3[:, :, 384:512], KV_B)))
    ob, od, rolled = _sample_attention(layer, pb3, pd3, caches_t, new_cols, sinks)
    oa, s_a_new, oc, s_c_new = _sample_recurrences(
        pa3, pba.reshape(bs, t_new, COLS_BA), pc3, conv_buf, s_a, s_c, conv_w, a_log, dt_bias, norm_a, lb, norm_c)
    conv_all = jnp.concatenate([conv_buf.astype(F32), pa3[:, :, 0:3 * W_MIX]], axis=1)
    state = (conv_all[:, -(CONV_A - 1):].astype(conv_buf.dtype), s_a_new.astype(s_a.dtype),
             s_c_new.astype(s_c.dtype))
    return oa, ob, oc, od, rolled, state


def kernel(x_prompt, x_sample, cache_b_k, cache_b_v, cache_d_k, cache_d_v, state_a_conv, state_a_s, state_c_s, norm_mix, w_in, conv_a, a_log, dt_bias, norm_a, sinks_b, lb_logits, norm_c, w_out, norm_ffn, w_ffn_gu, w_ffn_down, w_router, b_router, w_moe_gu, w_moe_down, norm_final):
    depth = w_in.shape[0]
    batch, seq, _ = x_prompt.shape
    bs, t_new, _ = x_sample.shape
    lb_p = jax.nn.softmax(lb_logits.astype(F32), axis=0)
    lower_bounds = jnp.cumsum(lb_p, axis=0) - lb_p[0]
    xp = x_prompt.reshape(batch * seq, D_MODEL)
    xs = x_sample.reshape(bs * t_new, D_MODEL)
    tm_p, tm_s = 512, 256
    row = lambda v: v.astype(F32).reshape(1, -1)
    prompt_states, sample_states = [], []
    rows_last = lambda c: jnp.transpose(c, (0, 1, 3, 4, 2))
    caches_t = tuple(rows_last(c) for c in (cache_b_k, cache_b_v, cache_d_k, cache_d_v))
    new_cols = []
    for l in range(depth):
        w_in_l = _permute_w_in(w_in[l])
        w_out_l = w_out[l].astype(BF16)
        conv_w = conv_a[l].astype(F32)
        head_params = _gdn_head_params(a_log[l], dt_bias[l])
        lb = lower_bounds[l]
        mix = (row(norm_a[l]), sinks_b[l], row(lb), row(norm_c[l]))

        projs = _inproj(xp, row(norm_mix[l]), w_in_l, tm_p, strided_for=(batch, seq))
        oa, ob, oc, ods, lses, st_p = _prompt_mixers(*projs[0:5], projs[5:], batch, seq, conv_w, head_params, *mix)
        xp = _outproj(xp, oa, ob, oc, ods, lses, w_out_l, tm_p)
        prompt_states.append(st_p)

        projs = _inproj(xs, row(norm_mix[l]), w_in_l, tm_s)
        oa, ob, oc, od, rolled, st_s = _sample_mixers(
            l, *projs, bs, t_new, caches_t, new_cols, (state_a_conv[l], state_a_s[l], state_c_s[l]), conv_w,
            a_log[l], dt_bias[l], norm_a[l], sinks_b[l], lb, norm_c[l])
        xs = _outproj(xs, oa, ob, oc, [od], [], w_out_l, tm_s)
        sample_states.append(st_s)

        if l % 2 == 0:
            w_gu = w_ffn_gu[l // 2].astype(BF16)
            w_dn = w_ffn_down[l // 2].astype(BF16)
            xp = _ffn(xp, row(norm_ffn[l]), w_gu, w_dn, 512, FFN_FF_TILE)
            xs = _ffn(xs, row(norm_ffn[l]), w_gu, w_dn, 512, FFN_FF_TILE)
        else:
            w_gu = w_moe_gu[l // 2].astype(BF16)
            w_dn = w_moe_down[l // 2].astype(BF16)
            final = row(norm_final) if l == depth - 1 else None
            xp = _moe(xp, row(norm_ffn[l]), w_router[l // 2], b_router[l // 2], w_gu, w_dn, 512, final)
            xs = _moe(xs, row(norm_ffn[l]), w_router[l // 2], b_router[l // 2], w_gu, w_dn, 512, final)
    if depth % 2 == 1:
        xp = _final_norm(xp, row(norm_final), 1024)
        xs = _final_norm(xs, row(norm_final), 512)
    y_prompt = xp.reshape(batch, seq, D_MODEL)
    y_sample = xs.reshape(bs, t_new, D_MODEL)
    stack = lambda states: [jnp.stack(t, 0) for t in zip(*states)]
    rows_back = lambda c, ref: jnp.transpose(c, (0, 1, 4, 2, 3)).astype(ref.dtype)
    d_k, d_v, b_k, b_v = rolled
    sample_caches = [rows_back(b_k, cache_b_k), rows_back(b_v, cache_b_v), rows_back(d_k, cache_d_k),
                     rows_back(d_v, cache_d_v)]
    return (y_prompt, y_sample, *stack(prompt_states), *sample_caches, *stack(sample_states))
```

```python
import functools
import math

import numpy as np
import jax
import jax.numpy as jnp
from jax import lax
from jax.experimental import pallas as pl
from jax.experimental.pallas import tpu as pltpu

F32 = jnp.float32
BF16 = jnp.bfloat16
HIGHEST = lax.Precision.HIGHEST

D_MODEL = 1024
HEAD_DIM = 64
N_HEADS = 4
KV_B = 2
W_MIX = N_HEADS * HEAD_DIM
CONV_A = 4
WIN = 128
DILATIONS = (1, 4, 16)
CACHE_D = 2048
CACHE_B = 128
D_FF = 2816
N_EXPERTS = 8
TOP_K = 2
D_FF_E = 3584
EPS = 1e-6
SCALE = HEAD_DIM ** -0.5
NEG = -1e30

COLS_A = 1024
COLS_B = 512
COLS_C = 1024
COLS_D = 768
COLS_BA = 128
COLS_ALL = COLS_A + COLS_B + COLS_C + COLS_D + COLS_BA

CHUNK = 64
BATCH_BLOCK = 4
FFN_FF_TILE = 1408
MOE_ROWS = 512
MOE_FF_TILE = 1792
VMEM_LIMIT = 56 * 1024 * 1024


def _cparams(*sem):
    return pltpu.CompilerParams(dimension_semantics=sem, vmem_limit_bytes=VMEM_LIMIT)


def _mm(a, b):
    return jnp.dot(a.astype(BF16), b.astype(BF16), preferred_element_type=F32)


def _mm_nt(a, b):
    return lax.dot_general(a.astype(BF16), b.astype(BF16), (((1,), (1,)), ((), ())),
                           preferred_element_type=F32)


def _mm_tn(a, b):
    return lax.dot_general(a.astype(BF16), b.astype(BF16), (((0,), (0,)), ((), ())),
                           preferred_element_type=F32)


def _split_bf16(a):
    hi = a.astype(BF16)
    return hi, (a - hi.astype(F32)).astype(BF16)


def _mm_3pass(a, b):
    ah, al = _split_bf16(a)
    bh, bl = _split_bf16(b)
    dot = functools.partial(jnp.dot, preferred_element_type=F32)
    return dot(ah, bh) + (dot(al, bh) + dot(ah, bl))


def _mm_f32(a, b):
    return jnp.dot(a, b, precision=HIGHEST, preferred_element_type=F32)


def _split3_bf16(a):
    hi = a.astype(BF16)
    rest = a - hi.astype(F32)
    mid = rest.astype(BF16)
    return hi, mid, (rest - mid.astype(F32)).astype(BF16)


def _mm_exact_lhs(sel, b, dims=(((1,), (0,)), ((), ()))):
    sel = sel.astype(BF16)
    return sum(lax.dot_general(sel, t, dims, preferred_element_type=F32) for t in _split3_bf16(b))


def _mm_exact_rhs(a, sel, dims=(((1,), (0,)), ((), ()))):
    sel = sel.astype(BF16)
    return sum(lax.dot_general(t, sel, dims, preferred_element_type=F32) for t in _split3_bf16(a))


def _sigmoid(x):
    return 1.0 / (1.0 + jnp.exp(-x))


def _silu(x):
    return x * _sigmoid(x)


def _softplus(x):
    return jnp.maximum(x, 0.0) + jnp.log(1.0 + jnp.exp(-jnp.abs(x)))


def _rms_rows(x, gain):
    return x * lax.rsqrt(jnp.mean(x * x, axis=-1, keepdims=True) + EPS) * gain


def _inproj_kernel(x_ref, g_ref, w_ref, oa_ref, ob_ref, oc_ref, od_ref, oba_ref, *strided):
    h = _rms_rows(x_ref[...], g_ref[...]).astype(BF16)
    c = 0
    for o_ref in (oa_ref, ob_ref, oc_ref, od_ref, oba_ref):
        n = o_ref.shape[1]
        o_ref[...] = jnp.dot(h, w_ref[:, c:c + n], preferred_element_type=F32)
        c += n
    if strided:
        *strided, chunk_scr = strided
        n_chunks = chunk_scr.shape[0]
        for ch in range(n_chunks):
            chunk_scr[ch] = od_ref[:, ch * 128:(ch + 1) * 128]
        for o_ref in strided:
            r, rows = o_ref.shape[1], o_ref.shape[2]
            for rho in range(r):
                for ch in range(n_chunks):
                    o_ref[0, rho, :, ch * 128:(ch + 1) * 128] = chunk_scr[ch, pl.ds(rho, rows, stride=r), :]


def _inproj(x2, gain, w_perm, tm, strided_for=None):
    n = x2.shape[0]
    widths = (COLS_A, COLS_B, COLS_C, COLS_D, COLS_BA)
    out_specs = [pl.BlockSpec((tm, w), lambda i: (i, 0)) for w in widths]
    out_shape = [jax.ShapeDtypeStruct((n, w), F32) for w in widths]
    scratch = []
    if strided_for is not None:
        batch, seq = strided_for
        per_seq = seq // tm
        for r in DILATIONS[1:]:
            out_specs.append(pl.BlockSpec((1, r, tm // r, COLS_D), lambda i: (i // per_seq, 0, i % per_seq, 0)))
            out_shape.append(jax.ShapeDtypeStruct((batch, r, seq // r, COLS_D), F32))
        scratch = [pltpu.VMEM((COLS_D // 128, tm, 128), F32)]
    return pl.pallas_call(
        _inproj_kernel,
        grid=(n // tm,),
        in_specs=[pl.BlockSpec((tm, D_MODEL), lambda i: (i, 0)),
                  pl.BlockSpec((1, D_MODEL), lambda i: (0, 0)),
                  pl.BlockSpec((D_MODEL, COLS_ALL), lambda i: (0, 0))],
        out_specs=out_specs,
        out_shape=out_shape,
        scratch_shapes=scratch,
        compiler_params=_cparams("parallel"),
        name="inproj",
    )(x2, gain, w_perm)


def _permute_w_in(w):
    ba = jnp.pad(w[:, 1024:1032], ((0, 0), (0, COLS_BA - 8)))
    return jnp.concatenate([w[:, 0:1024], w[:, 1032:3336], ba], axis=1).astype(BF16)


def _band_kernel(q_ref, kp_ref, kc_ref, vp_ref, vc_ref, sink_ref, *out_refs, kv, with_sink, with_lse):
    o_ref = out_refs[0]
    n = pl.program_id(2)
    nb = q_ref.shape[0]
    row = lax.broadcasted_iota(jnp.int32, (WIN, 2 * WIN), 0)
    col = lax.broadcasted_iota(jnp.int32, (WIN, 2 * WIN), 1)
    dist = row + WIN - col
    valid = (dist >= 0) & (dist <= WIN) & ((col >= WIN) | (n > 0))
    group = N_HEADS // kv
    qs = [q_ref[b] * SCALE for b in range(nb)]
    kcat = [jnp.concatenate([kp_ref[b], kc_ref[b]], axis=0) for b in range(nb)]
    vcat = [jnp.concatenate([vp_ref[b], vc_ref[b]], axis=0) for b in range(nb)]
    units = [(b, h) for b in range(nb) for h in range(N_HEADS)]
    sl = lambda h: slice(h * HEAD_DIM, (h + 1) * HEAD_DIM)
    s = [jnp.where(valid, _mm_nt(qs[b][:, sl(h)], kcat[b][:, sl(h // group)]), NEG) for b, h in units]
    m = [jnp.max(t, axis=-1, keepdims=True) for t in s]
    if with_sink:
        sink = [sink_ref[:, h:h + 1] for _, h in units]
        m = [jnp.maximum(a, b) for a, b in zip(m, sink)]
    p = [jnp.exp(t - a) for t, a in zip(s, m)]
    l = [jnp.sum(t, axis=-1, keepdims=True) for t in p]
    if with_sink:
        l = [a + jnp.exp(b - c) for a, b, c in zip(l, sink, m)]
    o = [_mm(t, vcat[b][:, sl(h // group)]) / a for t, a, (b, h) in zip(p, l, units)]
    for b in range(nb):
        o_ref[b] = jnp.concatenate(o[b * N_HEADS:(b + 1) * N_HEADS], axis=-1)
        if with_lse:
            lse = [jnp.broadcast_to(m[i] + jnp.log(l[i]), (WIN, HEAD_DIM)) for i in range(b * N_HEADS, (b + 1) * N_HEADS)]
            out_refs[1][b] = jnp.concatenate(lse, axis=-1)


def _band_attention(p4, q_col, k_col, v_col, kv, sinks, with_lse):
    batch, r, ln, width = p4.shape
    bb = BATCH_BLOCK if batch % BATCH_BLOCK == 0 else 1
    wq, wk = W_MIX, kv * HEAD_DIM
    qb, kb, vb = q_col // wq, k_col // wk, v_col // wk
    cur = lambda off: (lambda b, rho, n: (b, rho, n, off))
    prev = lambda off: (lambda b, rho, n: (b, rho, jnp.maximum(n - 1, 0), off))
    out_spec = pl.BlockSpec((bb, None, WIN, W_MIX), lambda b, rho, n: (b, rho, n, 0))
    out_shape = jax.ShapeDtypeStruct((batch, r, ln, W_MIX), F32)
    n_out = 2 if with_lse else 1
    return pl.pallas_call(
        functools.partial(_band_kernel, kv=kv, with_sink=sinks is not None, with_lse=with_lse),
        grid=(batch // bb, r, ln // WIN),
        in_specs=[pl.BlockSpec((bb, None, WIN, wq), cur(qb)),
                  pl.BlockSpec((bb, None, WIN, wk), prev(kb)),
                  pl.BlockSpec((bb, None, WIN, wk), cur(kb)),
                  pl.BlockSpec((bb, None, WIN, wk), prev(vb)),
                  pl.BlockSpec((bb, None, WIN, wk), cur(vb)),
                  pl.BlockSpec((1, 128), lambda b, rho, n: (0, 0))],
        out_specs=[out_spec] * n_out,
        out_shape=[out_shape] * n_out,
        compiler_params=_cparams("parallel", "parallel", "arbitrary"),
        name=f"band_r{r}",
    )(p4, p4, p4, p4, p4, sinks if sinks is not None else jnp.zeros((1, 128), F32))


def _gated_norm_rows(o, gate, gain):
    return _rms_rows(o, gain) * _silu(gate)


def _hgrn_constants(c):
    halves = []
    h = c // 2
    while h >= 1:
        halves.append(h)
        h //= 2
    t = np.arange(c)[:, None]
    u = np.arange(c)[None, :]
    mats = [(u <= t).astype(np.float32)]
    level = np.full((c, c), -1, np.int32)
    level[np.arange(c), np.arange(c)] = 0
    for li, h in enumerate(halves, 1):
        mid = (t // (2 * h)) * (2 * h) + h
        second = (t % (2 * h)) >= h
        mats.append(np.where(second, (u > mid) & (u <= t), (u > t) & (u <= mid)).astype(np.float32))
        pair = (t // (2 * h) == u // (2 * h)) & ((t % (2 * h)) >= h) & ((u % (2 * h)) < h)
        level[pair] = li
    return np.concatenate(mats, axis=0), level, len(halves)


def _hgrn_kernel(pc_ref, lb_ref, gain_ref, mat_ref, lvl_ref, o_ref, st_ref, s_scr, *, c, n_levels):
    j = pl.program_id(1)
    nb = pc_ref.shape[0]

    @pl.when(j == 0)
    def _():
        s_scr[...] = jnp.zeros_like(s_scr)

    lb = lb_ref[...]
    lvl = lvl_ref[...]
    xs = [pc_ref[b] for b in range(nb)]
    fs = [lb + (1.0 - lb) * _sigmoid(x[:, 256:512]) for x in xs]
    sums = [_mm_exact_lhs(mat_ref[...], -jnp.log(f)) for f in fs]
    gcs = [-s[0:c] for s in sums]
    g_last = [gc[c - 1:c, :] for gc in gcs]
    q_dec = [x[:, 0:256] * jnp.exp(gc) for x, gc in zip(xs, gcs)]
    k_dec = [(1.0 - f) * jnp.exp(gl - gc) for f, gl, gc in zip(fs, g_last, gcs)]
    units = [(b, h) for b in range(nb) for h in range(N_HEADS)]
    sl = lambda h: slice(h * HEAD_DIM, (h + 1) * HEAD_DIM)
    qh = [xs[b][:, sl(h)] for b, h in units]
    kh = [1.0 - fs[b][:, sl(h)] for b, h in units]
    vh = [xs[b][:, 512 + h * HEAD_DIM:512 + (h + 1) * HEAD_DIM] for b, h in units]
    a = [jnp.where(lvl == 0, _mm_nt(q, k), 0.0) for q, k in zip(qh, kh)]
    for li in range(1, n_levels + 1):
        damp = [jnp.exp(-sums[b][li * c:(li + 1) * c, sl(h)]) for b, h in units]
        part = [_mm_nt(q * d, k * d) for q, k, d in zip(qh, kh, damp)]
        a = [acc + jnp.where(lvl == li, p, 0.0) for acc, p in zip(a, part)]
    st = [s_scr[b, h] for b, h in units]
    o_inter = [_mm_nt(q_dec[b][:, sl(h)], t) for (b, h), t in zip(units, st)]
    o = [oi + _mm(aa, v) for oi, aa, v in zip(o_inter, a, vh)]
    upd = [_mm_tn(v, k_dec[b][:, sl(h)]) for (b, h), v in zip(units, vh)]
    for i, (b, h) in enumerate(units):
        s_scr[b, h] = jnp.exp(g_last[b][:, sl(h)]) * st[i] + upd[i]
    for b in range(nb):
        outs = [_gated_norm_rows(o[b * N_HEADS + h], xs[b][:, 768 + h * HEAD_DIM:768 + (h + 1) * HEAD_DIM],
                                 gain_ref[...]) for h in range(N_HEADS)]
        o_ref[b] = jnp.concatenate(outs, axis=-1)

    @pl.when(j == pl.num_programs(1) - 1)
    def _():
        st_ref[...] = s_scr[...]


def _hgrn_prompt(pc3, lb, gain):
    batch, seq, _ = pc3.shape
    c = 2 * CHUNK
    bb = BATCH_BLOCK if batch % BATCH_BLOCK == 0 else 1
    mat, level, n_levels = _hgrn_constants(c)
    o, st = pl.pallas_call(
        functools.partial(_hgrn_kernel, c=c, n_levels=n_levels),
        grid=(batch // bb, seq // c),
        in_specs=[pl.BlockSpec((bb, c, COLS_C), lambda b, j: (b, j, 0)),
                  pl.BlockSpec((1, W_MIX), lambda b, j: (0, 0)),
                  pl.BlockSpec((1, HEAD_DIM), lambda b, j: (0, 0)),
                  pl.BlockSpec(mat.shape, lambda b, j: (0, 0)),
                  pl.BlockSpec(level.shape, lambda b, j: (0, 0))],
        out_specs=[pl.BlockSpec((bb, c, W_MIX), lambda b, j: (b, j, 0)),
                   pl.BlockSpec((bb, N_HEADS, HEAD_DIM, HEAD_DIM), lambda b, j: (b, 0, 0, 0))],
        out_shape=[jax.ShapeDtypeStruct((batch, seq, W_MIX), F32),
                   jax.ShapeDtypeStruct((batch, N_HEADS, HEAD_DIM, HEAD_DIM), F32)],
        scratch_shapes=[pltpu.VMEM((bb, N_HEADS, HEAD_DIM, HEAD_DIM), F32)],
        compiler_params=_cparams("parallel", "arbitrary"),
        name="hgrn_prompt",
    )(pc3, lb, gain, jnp.asarray(mat), jnp.asarray(level))
    return o.reshape(batch * seq, W_MIX), jnp.swapaxes(st, -1, -2)


def _unit_lower_solve(lows, rhss, c):
    xs = [rhs - _mm_3pass(low, rhs) for low, rhs in zip(lows, rhss)]
    ps = lows
    span = 2
    while span < c:
        mm = _mm_3pass if span == 2 else _mm
        ps = [mm(p, p) for p in ps]
        xs = [x + mm(p, x) for p, x in zip(ps, xs)]
        span *= 2
    return xs


def _gdn_kernel(pa_ref, pba_ref, cw_ref, hp_ref, gain_ref, tri_ref, ones_ref, o_ref, st_ref,
                s_scr, buf_scr, *, c):
    j = pl.program_id(1)
    pad = 8
    nb = pa_ref.shape[0]

    @pl.when(j == 0)
    def _():
        s_scr[...] = jnp.zeros_like(s_scr)
        buf_scr[:, 0:pad, :] = jnp.zeros((nb, pad, 3 * W_MIX), F32)

    ones = ones_ref[...]
    row = lax.broadcasted_iota(jnp.int32, (c, c), 0)
    col = lax.broadcasted_iota(jnp.int32, (c, c), 1)
    qs, ks, vs, gates, betas, gcs, gcts = [], [], [], [], [], [], []
    for b in range(nb):
        x = pa_ref[b, :, 0:3 * W_MIX]
        gates.append(pa_ref[b, :, 3 * W_MIX:4 * W_MIX])
        buf_scr[b, pad:pad + c, :] = x
        y = cw_ref[CONV_A - 1:CONV_A, :] * x
        for tap in range(CONV_A - 1):
            back = CONV_A - 1 - tap
            y = y + cw_ref[tap:tap + 1, :] * buf_scr[b, pad - back:pad - back + c, :]
        buf_scr[b, 0:pad, :] = buf_scr[b, c:c + pad, :]
        y = _silu(y)
        qs.append(y[:, 0:W_MIX])
        ks.append(y[:, W_MIX:2 * W_MIX])
        vs.append(y[:, 2 * W_MIX:3 * W_MIX])
        ba = pba_ref[b]
        betas.append(_sigmoid(ba))
        g = hp_ref[0:1, :] * _softplus(ba + hp_ref[1:2, :])
        gcs.append(_mm_exact_lhs(tri_ref[0], g))
        gcts.append(_mm_exact_rhs(g, tri_ref[1], (((0,), (0,)), ((), ()))))
    qs = [q * lax.rsqrt(_mm_exact_rhs(q * q, ones) + EPS) * SCALE for q in qs]
    ks = [k * lax.rsqrt(_mm_exact_rhs(k * k, ones) + EPS) for k in ks]
    units = [(b, h) for b in range(nb) for h in range(N_HEADS)]
    sl = lambda h: slice(h * HEAD_DIM, (h + 1) * HEAD_DIM)
    qh = [qs[b][:, sl(h)] for b, h in units]
    kh = [ks[b][:, sl(h)] for b, h in units]
    vh = [vs[b][:, sl(h)] for b, h in units]
    b_col = [betas[b][:, h:h + 1] for b, h in units]
    g_col = [gcs[b][:, 4 + h:5 + h] for b, h in units]
    g_row = [gcts[b][4 + h:5 + h, :] for b, h in units]
    decay = [jnp.where(row >= col, jnp.exp(jnp.minimum(gc_ - gr_, 0.0)), 0.0) for gc_, gr_ in zip(g_col, g_row)]
    kk = [_mm_nt(k, k) for k in kh]
    qk = [_mm_nt(q, k) for q, k in zip(qh, kh)]
    low = [jnp.where(row > col, bc * a * d, 0.0) for bc, a, d in zip(b_col, kk, decay)]
    eg = [jnp.exp(gc_) for gc_ in g_col]
    rhs = [jnp.concatenate([v * bc, k * (bc * e)], axis=-1) for v, k, bc, e in zip(vh, kh, b_col, eg)]
    sol = _unit_lower_solve(low, rhs, c)
    intra = [a * d for a, d in zip(qk, decay)]
    g_last = [gc_[c - 1:c, :] for gc_ in g_col]
    st = [s_scr[b, h] for b, h in units]
    u = [s[:, 0:HEAD_DIM] - _mm_nt(s[:, HEAD_DIM:2 * HEAD_DIM], t) for s, t in zip(sol, st)]
    o_inter = [_mm_nt(q * e, t) for q, e, t in zip(qh, eg, st)]
    o = [oi + _mm(a, uu) for oi, a, uu in zip(o_inter, intra, u)]
    k_dec = [k * jnp.exp(gl - gc_) for k, gl, gc_ in zip(kh, g_last, g_col)]
    upd = [_mm_tn(uu, kd) for uu, kd in zip(u, k_dec)]
    for i, (b, h) in enumerate(units):
        s_scr[b, h] = jnp.exp(g_last[i]) * st[i] + upd[i]
    for b in range(nb):
        outs = [_gated_norm_rows(o[b * N_HEADS + h], gates[b][:, sl(h)], gain_ref[...]) for h in range(N_HEADS)]
        o_ref[b] = jnp.concatenate(outs, axis=-1)

    @pl.when(j == pl.num_programs(1) - 1)
    def _():
        st_ref[...] = s_scr[...]


def _gdn_prompt(pa3, pba3, conv_w, head_params, gain):
    batch, seq, _ = pa3.shape
    c = CHUNK
    bb = BATCH_BLOCK if batch % BATCH_BLOCK == 0 else 1
    lower = np.tril(np.ones((c, c), np.float32))
    tri = np.stack([lower, lower.T], axis=0)
    ones = np.kron(np.eye(N_HEADS, dtype=np.float32), np.ones((HEAD_DIM, HEAD_DIM), np.float32))
    o, st = pl.pallas_call(
        functools.partial(_gdn_kernel, c=c),
        grid=(batch // bb, seq // c),
        in_specs=[pl.BlockSpec((bb, c, COLS_A), lambda b, j: (b, j, 0)),
                  pl.BlockSpec((bb, c, COLS_BA), lambda b, j: (b, j, 0)),
                  pl.BlockSpec((CONV_A, 3 * W_MIX), lambda b, j: (0, 0)),
                  pl.BlockSpec((2, COLS_BA), lambda b, j: (0, 0)),
                  pl.BlockSpec((1, HEAD_DIM), lambda b, j: (0, 0)),
                  pl.BlockSpec((2, c, c), lambda b, j: (0, 0, 0)),
                  pl.BlockSpec((W_MIX, W_MIX), lambda b, j: (0, 0))],
        out_specs=[pl.BlockSpec((bb, c, W_MIX), lambda b, j: (b, j, 0)),
                   pl.BlockSpec((bb, N_HEADS, HEAD_DIM, HEAD_DIM), lambda b, j: (b, 0, 0, 0))],
        out_shape=[jax.ShapeDtypeStruct((batch, seq, W_MIX), F32),
                   jax.ShapeDtypeStruct((batch, N_HEADS, HEAD_DIM, HEAD_DIM), F32)],
        scratch_shapes=[pltpu.VMEM((bb, N_HEADS, HEAD_DIM, HEAD_DIM), F32),
                        pltpu.VMEM((bb, c + 8, 3 * W_MIX), F32)],
        compiler_params=_cparams("parallel", "arbitrary"),
        name="gdn_prompt",
    )(pa3, pba3, conv_w, head_params, gain, jnp.asarray(tri), jnp.asarray(ones))
    return o.reshape(batch * seq, W_MIX), jnp.swapaxes(st, -1, -2)


def _gdn_head_params(a_log, dt_bias):
    neg_a = jnp.pad(-jnp.exp(a_log.astype(F32)), (4, COLS_BA - 8))
    dtb = jnp.pad(dt_bias.astype(F32), (4, COLS_BA - 8))
    return jnp.stack([neg_a, dtb], axis=0)


def _outproj_kernel(*refs, n_branch):
    x_ref, oa_ref, ob_ref, oc_ref = refs[0:4]
    tm = x_ref.shape[0]
    if n_branch == 1:
        od = refs[4][...]
        w_ref, o_ref = refs[5], refs[6]
    else:
        d_refs = refs[4:4 + n_branch]
        l_refs = refs[4 + n_branch:4 + 2 * n_branch]
        w_ref, o_ref = refs[4 + 2 * n_branch], refs[5 + 2 * n_branch]
        scratch = refs[6 + 2 * n_branch:]

        def token_order(ref, scr):
            r, rows = ref.shape[1], ref.shape[2]
            if r == 1:
                return ref[0, 0]
            n_chunks = scr.shape[0]
            for rho in range(r):
                for ch in range(n_chunks):
                    scr[ch, pl.ds(rho, rows, stride=r), :] = ref[0, rho, :, ch * 128:(ch + 1) * 128]
            return jnp.concatenate([scr[ch] for ch in range(n_chunks)], axis=-1)

        outs = [token_order(ref, scratch[2 * i]) for i, ref in enumerate(d_refs)]
        lses = [token_order(ref, scratch[2 * i + 1]) for i, ref in enumerate(l_refs)]
        m = functools.reduce(jnp.maximum, lses)
        es = [jnp.exp(l - m) for l in lses]
        od = sum(e * o for e, o in zip(es, outs)) / sum(es)
    acc = x_ref[...]
    for i, part in enumerate((oa_ref[...], ob_ref[...], oc_ref[...], od)):
        acc = acc + _mm(part, w_ref[i * W_MIX:(i + 1) * W_MIX, :])
    o_ref[...] = acc


def _outproj(x2, oa, ob, oc, ods, lses, w_bf, tm):
    n = x2.shape[0]
    n_branch = len(ods)
    row = lambda w: pl.BlockSpec((tm, w), lambda i: (i, 0))
    specs = [row(D_MODEL)] + [row(W_MIX)] * 3
    scratch = []
    if n_branch == 1:
        specs.append(row(W_MIX))
    else:
        per_seq = (ods[0].shape[1] * ods[0].shape[2]) // tm
        for t in (*ods, *lses):
            r = t.shape[1]
            specs.append(pl.BlockSpec((1, r, tm // r, W_MIX), lambda i: (i // per_seq, 0, i % per_seq, 0)))
        scratch = [pltpu.VMEM((W_MIX // 128, tm, 128), F32)] * (2 * n_branch)
    specs.append(pl.BlockSpec((4 * W_MIX, D_MODEL), lambda i: (0, 0)))
    return pl.pallas_call(
        functools.partial(_outproj_kernel, n_branch=n_branch),
        grid=(n // tm,),
        in_specs=specs,
        out_specs=row(D_MODEL),
        out_shape=jax.ShapeDtypeStruct((n, D_MODEL), F32),
        scratch_shapes=scratch,
        compiler_params=_cparams("parallel"),
        name="outproj",
    )(x2, oa, ob, oc, *ods, *lses, w_bf)


def _ffn_kernel(x_ref, g_ref, wg_ref, wu_ref, wd_ref, o_ref, h_scr, acc_scr):
    f = pl.program_id(1)

    @pl.when(f == 0)
    def _():
        h_scr[...] = _rms_rows(x_ref[...], g_ref[...]).astype(BF16)
        acc_scr[...] = jnp.zeros_like(acc_scr)

    h = h_scr[...]
    gate = jnp.dot(h, wg_ref[...], preferred_element_type=F32)
    up = jnp.dot(h, wu_ref[...], preferred_element_type=F32)
    acc_scr[...] += _mm(_silu(gate) * up, wd_ref[...])

    @pl.when(f == pl.num_programs(1) - 1)
    def _():
        o_ref[...] = x_ref[...] + acc_scr[...]


def _ffn(x2, gain, w_gu_bf, w_down_bf, tm, tf):
    n = x2.shape[0]
    nf = D_FF // tf
    return pl.pallas_call(
        _ffn_kernel,
        grid=(n // tm, nf),
        in_specs=[pl.BlockSpec((tm, D_MODEL), lambda i, f: (i, 0)),
                  pl.BlockSpec((1, D_MODEL), lambda i, f: (0, 0)),
                  pl.BlockSpec((D_MODEL, tf), lambda i, f: (0, f)),
                  pl.BlockSpec((D_MODEL, tf), lambda i, f: (0, nf + f)),
                  pl.BlockSpec((tf, D_MODEL), lambda i, f: (f, 0))],
        out_specs=pl.BlockSpec((tm, D_MODEL), lambda i, f: (i, 0)),
        out_shape=jax.ShapeDtypeStruct((n, D_MODEL), F32),
        scratch_shapes=[pltpu.VMEM((tm, D_MODEL), BF16), pltpu.VMEM((tm, D_MODEL), F32)],
        compiler_params=_cparams("parallel", "arbitrary"),
        name="ffn",
    )(x2, gain, w_gu_bf, w_gu_bf, w_down_bf)


ROW_TILE = (8, 128)


def _router_kernel(x_ref, g_ref, wr_ref, br_ref, h_ref, route_ref):
    i = pl.program_id(0)
    last = pl.num_programs(0) - 1
    tm = x_ref.shape[0]

    @pl.when(i < last)
    def _():
        h = _rms_rows(x_ref[...], g_ref[...])
        h_ref[...] = h.reshape(tm, *ROW_TILE)
        lane = lax.broadcasted_iota(jnp.int32, (tm, 128), 1).astype(F32)
        logits = _mm_f32(h, wr_ref[...]) + br_ref[...]
        m1 = jnp.max(logits, axis=-1, keepdims=True)
        i1 = jnp.min(jnp.where(logits == m1, lane, 128.0), axis=-1, keepdims=True)
        rest = jnp.where(lane == i1, NEG, logits)
        m2 = jnp.max(rest, axis=-1, keepdims=True)
        i2 = jnp.min(jnp.where(rest == m2, lane, 128.0), axis=-1, keepdims=True)
        e2 = jnp.exp(m2 - m1)
        route_ref[...] = (jnp.where(lane == 0.0, i1, 0.0) + jnp.where(lane == 1.0, i2, 0.0)
                          + jnp.where(lane == 2.0, 1.0 / (1.0 + e2), 0.0)
                          + jnp.where(lane == 3.0, e2 / (1.0 + e2), 0.0))

    @pl.when(i == last)
    def _():
        h_ref[...] = jnp.zeros(h_ref.shape, F32)
        route_ref[...] = jnp.zeros(route_ref.shape, F32)


def _expert_kernel(be_ref, tok_ref, h_hbm, wg_ref, wu_ref, wd_ref, o_ref, xbuf, xb_scr, acc_scr, sems):
    i = pl.program_id(0)
    f = pl.program_id(1)
    tm = xb_scr.shape[0]

    def gather(block, slot):
        base = block * tm

        def issue(r, carry):
            pltpu.async_copy(h_hbm.at[tok_ref[base + r]], xbuf.at[slot, r], sems.at[slot])
            return carry

        lax.fori_loop(0, tm, issue, 0, unroll=2)

    @pl.when(f == 0)
    def _():
        slot = i % 2

        @pl.when(i == 0)
        def _():
            gather(0, 0)

        pltpu.make_async_copy(h_hbm.at[pl.ds(0, tm)], xbuf.at[slot], sems.at[slot]).wait()
        xb_scr[...] = xbuf[slot].reshape(tm, D_MODEL).astype(BF16)
        acc_scr[...] = jnp.zeros_like(acc_scr)

        @pl.when(i + 1 < pl.num_programs(0))
        def _():
            gather(i + 1, 1 - slot)

    xb = xb_scr[...]
    gate = jnp.dot(xb, wg_ref[0], preferred_element_type=F32)
    up = jnp.dot(xb, wu_ref[0], preferred_element_type=F32)
    acc_scr[...] += _mm(_silu(gate) * up, wd_ref[0])

    @pl.when(f == pl.num_programs(1) - 1)
    def _():
        o_ref[...] = acc_scr[...].reshape(tm, *ROW_TILE)


def _combine_kernel(pos_ref, x_ref, route_ref, y_hbm, *rest, with_final_norm):
    final_ref = rest[0] if with_final_norm else None
    o_ref, first_scr, second_scr, sems = rest[int(with_final_norm):]
    tc = x_ref.shape[0]
    base = pl.program_id(0) * (2 * tc)

    def issue(r, carry):
        pltpu.async_copy(y_hbm.at[pos_ref[base + 2 * r]], first_scr.at[r], sems.at[0], priority=0)
        pltpu.async_copy(y_hbm.at[pos_ref[base + 2 * r + 1]], second_scr.at[r], sems.at[1], priority=1)
        return carry

    lax.fori_loop(0, tc, issue, 0)
    pltpu.make_async_copy(y_hbm.at[pl.ds(0, tc)], first_scr, sems.at[0]).wait()
    pltpu.make_async_copy(y_hbm.at[pl.ds(0, tc)], second_scr, sems.at[1]).wait()
    route = route_ref[...]
    y = (x_ref[...] + route[:, 2:3] * first_scr[...].reshape(tc, D_MODEL)
         + route[:, 3:4] * second_scr[...].reshape(tc, D_MODEL))
    o_ref[...] = y if final_ref is None else _rms_rows(y, final_ref[...])


def _moe_routing(route, n, tm_rows):
    e_flat = route[:n, 0:TOP_K].astype(jnp.int32).reshape(-1)
    n_assign = n * TOP_K
    onehot = (e_flat[:, None] == jnp.arange(N_EXPERTS, dtype=jnp.int32)[None, :]).astype(jnp.int32)
    csum = jnp.cumsum(onehot, axis=0)
    rank = jnp.sum(csum * onehot, axis=1) - 1
    counts = csum[-1]
    padded = (counts + tm_rows - 1) // tm_rows * tm_rows
    pad_end = jnp.cumsum(padded)
    pad_start = pad_end - padded
    start = jnp.cumsum(counts) - counts
    dest = pad_start[e_flat] + rank
    n_blocks = -(-(n_assign + N_EXPERTS * (tm_rows - 1)) // tm_rows)
    blk_e = jnp.minimum(jnp.sum(jnp.arange(n_blocks, dtype=jnp.int32)[:, None] * tm_rows >= pad_end[None, :], axis=1),
                        N_EXPERTS - 1).astype(jnp.int32)
    order = jnp.argsort(e_flat, stable=True).astype(jnp.int32)
    rows = jnp.arange(n_blocks * tm_rows, dtype=jnp.int32)
    row_e = jnp.repeat(blk_e, tm_rows)
    offset = rows - pad_start[row_e]
    valid = (offset < counts[row_e]) & (rows < pad_end[N_EXPERTS - 1])
    src = jnp.clip(start[row_e] + offset, 0, n_assign - 1)
    row_tok = jnp.where(valid, order[src] // TOP_K, n).astype(jnp.int32)
    return row_tok, dest.astype(jnp.int32), blk_e, n_blocks


def _moe(x2, gain, w_router, b_router, w_gu_bf, w_down_bf, tm, final_gain=None):
    n = x2.shape[0]
    rows = min(MOE_ROWS, max(128, n * TOP_K // N_EXPERTS))
    wr = jnp.pad(w_router.astype(F32), ((0, 0), (0, 128 - N_EXPERTS)))
    br = jnp.pad(b_router.astype(F32), (0, 128 - N_EXPERTS), constant_values=NEG).reshape(1, 128)
    nt = n // tm
    h3, route = pl.pallas_call(
        _router_kernel,
        grid=(nt + 1,),
        in_specs=[pl.BlockSpec((tm, D_MODEL), lambda i: (jnp.minimum(i, nt - 1), 0)),
                  pl.BlockSpec((1, D_MODEL), lambda i: (0, 0)),
                  pl.BlockSpec((D_MODEL, 128), lambda i: (0, 0)),
                  pl.BlockSpec((1, 128), lambda i: (0, 0))],
        out_specs=[pl.BlockSpec((tm, *ROW_TILE), lambda i: (i, 0, 0)),
                   pl.BlockSpec((tm, 128), lambda i: (i, 0))],
        out_shape=[jax.ShapeDtypeStruct((n + tm, *ROW_TILE), F32),
                   jax.ShapeDtypeStruct((n + tm, 128), F32)],
        compiler_params=_cparams("arbitrary"),
        name="moe_router",
    )(x2, gain, wr, br)

    row_tok, dest, blk_e, n_blocks = _moe_routing(route, n, rows)
    n_rows = n_blocks * rows

    nf = D_FF_E // MOE_FF_TILE
    y_rows = pl.pallas_call(
        _expert_kernel,
        grid_spec=pltpu.PrefetchScalarGridSpec(
            num_scalar_prefetch=2,
            grid=(n_blocks, nf),
            in_specs=[pl.BlockSpec(memory_space=pl.ANY),
                      pl.BlockSpec((1, D_MODEL, MOE_FF_TILE), lambda i, f, be, tok: (be[i], 0, f)),
                      pl.BlockSpec((1, D_MODEL, MOE_FF_TILE), lambda i, f, be, tok: (be[i], 0, nf + f)),
                      pl.BlockSpec((1, MOE_FF_TILE, D_MODEL), lambda i, f, be, tok: (be[i], f, 0))],
            out_specs=pl.BlockSpec((rows, *ROW_TILE), lambda i, f, be, tok: (i, 0, 0)),
            scratch_shapes=[pltpu.VMEM((2, rows, *ROW_TILE), F32), pltpu.VMEM((rows, D_MODEL), BF16),
                            pltpu.VMEM((rows, D_MODEL), F32), pltpu.SemaphoreType.DMA((2,))]),
        out_shape=jax.ShapeDtypeStruct((n_rows, *ROW_TILE), F32),
        compiler_params=_cparams("arbitrary", "arbitrary"),
        name="moe_experts",
    )(blk_e, row_tok, h3, w_gu_bf, w_gu_bf, w_down_bf)

    final = [] if final_gain is None else [final_gain]
    return pl.pallas_call(
        functools.partial(_combine_kernel, with_final_norm=bool(final)),
        grid_spec=pltpu.PrefetchScalarGridSpec(
            num_scalar_prefetch=1,
            grid=(nt,),
            in_specs=[pl.BlockSpec((tm, D_MODEL), lambda i, pos: (i, 0)),
                      pl.BlockSpec((tm, 128), lambda i, pos: (i, 0)),
                      pl.BlockSpec(memory_space=pl.ANY)]
            + [pl.BlockSpec((1, D_MODEL), lambda i, pos: (0, 0))] * len(final),
            out_specs=pl.BlockSpec((tm, D_MODEL), lambda i, pos: (i, 0)),
            scratch_shapes=[pltpu.VMEM((tm, *ROW_TILE), F32), pltpu.VMEM((tm, *ROW_TILE), F32),
                            pltpu.SemaphoreType.DMA((2,))]),
        out_shape=jax.ShapeDtypeStruct((n, D_MODEL), F32),
        compiler_params=_cparams("arbitrary"),
        name="moe_combine",
    )(dest, x2, route, y_rows, *final)


def _norm_kernel(x_ref, g_ref, o_ref):
    o_ref[...] = _rms_rows(x_ref[...], g_ref[...])


def _final_norm(x2, gain, tm):
    n = x2.shape[0]
    return pl.pallas_call(
        _norm_kernel,
        grid=(n // tm,),
        in_specs=[pl.BlockSpec((tm, D_MODEL), lambda i: (i, 0)),
                  pl.BlockSpec((1, D_MODEL), lambda i: (0, 0))],
        out_specs=pl.BlockSpec((tm, D_MODEL), lambda i: (i, 0)),
        out_shape=jax.ShapeDtypeStruct((n, D_MODEL), F32),
        compiler_params=_cparams("parallel"),
        name="final_norm",
    )(x2, gain)


ROWS_T = 8
LANE_TILE = 128


def _sattn_kernel(qd_ref, qb_ref, kd_all, vd_all, kdn_all, vdn_all, kb_all, vb_all, kbn_all, vbn_all, sink_ref,
                  od_ref, ob_ref, *rolled_refs, t_new, here):
    kd_ref, vd_ref, kdn_ref, vdn_ref, kb_ref, vb_ref, kbn_ref, vbn_ref = (
        r.at[here] for r in (kd_all, vd_all, kdn_all, vdn_all, kb_all, vb_all, kbn_all, vbn_all))
    bb = qd_ref.shape[0]
    first_new = LANE_TILE - t_new

    def distances(n_cache):
        row = lax.broadcasted_iota(jnp.int32, (ROWS_T, n_cache), 0)
        col = lax.broadcasted_iota(jnp.int32, (ROWS_T, n_cache), 1)
        rown = lax.broadcasted_iota(jnp.int32, (ROWS_T, LANE_TILE), 0)
        j = lax.broadcasted_iota(jnp.int32, (ROWS_T, LANE_TILE), 1) - first_new
        d_new = rown - j
        return n_cache + row - col, d_new, (j >= 0) & (d_new >= 0)

    def softmax_parts(sc, sn, ok_c, ok_n, sink):
        sc = jnp.where(ok_c, sc, NEG)
        sn = jnp.where(ok_n, sn, NEG)
        m = jnp.maximum(jnp.max(sc, axis=-1, keepdims=True), jnp.max(sn, axis=-1, keepdims=True))
        if sink is not None:
            m = jnp.maximum(m, sink)
        pc = jnp.exp(sc - m)
        pn = jnp.exp(sn - m)
        l = jnp.sum(pc, axis=-1, keepdims=True) + jnp.sum(pn, axis=-1, keepdims=True)
        if sink is not None:
            l = l + jnp.exp(sink - m)
        return pc, pn, l, m + jnp.log(l)

    sl = lambda h: slice(h * HEAD_DIM, (h + 1) * HEAD_DIM)
    units = [(i, h) for i in range(bb) for h in range(N_HEADS)]
    dc, dn, ok_new = distances(kd_ref.shape[-1])
    qs = [qd_ref[i][:, sl(h)] * SCALE for i, h in units]
    sc = [_mm(q, kd_ref[i, h]) for q, (i, h) in zip(qs, units)]
    sn = [_mm(q, kdn_ref[i, h]) for q, (i, h) in zip(qs, units)]
    pcs, pns, ls, lses = [], [], [], []
    for r in DILATIONS:
        ok_c = (dc <= WIN * r) & ((dc & (r - 1)) == 0)
        ok_n = ok_new & ((dn & (r - 1)) == 0)
        parts = [softmax_parts(a, b, ok_c, ok_n, None) for a, b in zip(sc, sn)]
        pcs.append([p[0] for p in parts])
        pns.append([p[1] for p in parts])
        ls.append([p[2] for p in parts])
        lses.append([p[3] for p in parts])
    nbr = len(DILATIONS)
    acc = [_mm_nt(jnp.concatenate([pcs[r][u] for r in range(nbr)], axis=0), vd_ref[i, h])
           + _mm_nt(jnp.concatenate([pns[r][u] for r in range(nbr)], axis=0), vdn_ref[i, h])
           for u, (i, h) in enumerate(units)]
    outs = []
    for u in range(len(units)):
        m = functools.reduce(jnp.maximum, [lses[r][u] for r in range(nbr)])
        es = [jnp.exp(lses[r][u] - m) for r in range(nbr)]
        num = sum(es[r] * (acc[u][r * ROWS_T:(r + 1) * ROWS_T] / ls[r][u]) for r in range(nbr))
        outs.append(num / sum(es))
    for i in range(bb):
        od_ref[i] = jnp.concatenate(outs[i * N_HEADS:(i + 1) * N_HEADS], axis=-1)
    group = N_HEADS // KV_B
    dc, dn, ok_new = distances(kb_ref.shape[-1])
    qs = [qb_ref[i][:, sl(h)] * SCALE for i, h in units]
    parts = [softmax_parts(_mm(q, kb_ref[i, h // group]), _mm(q, kbn_ref[i, h // group]), dc <= WIN, ok_new,
                           sink_ref[:, h:h + 1]) for q, (i, h) in zip(qs, units)]
    outs = [(_mm_nt(p[0], vb_ref[i, h // group]) + _mm_nt(p[1], vbn_ref[i, h // group])) / p[2]
            for p, (i, h) in zip(parts, units)]
    for i in range(bb):
        ob_ref[i] = jnp.concatenate(outs[i * N_HEADS:(i + 1) * N_HEADS], axis=-1)
    lane = lax.broadcasted_iota(jnp.int32, (HEAD_DIM, LANE_TILE), 1)
    if not rolled_refs:
        return
    for src, new, dst in zip((kd_all, vd_all, kb_all, vb_all), (kdn_all, vdn_all, kbn_all, vbn_all), rolled_refs):
        n = src.shape[-1]
        for layer in range(src.shape[0]):
            for i in range(bb):
                for h in range(src.shape[2]):
                    rolled = pltpu.roll(src[layer, i, h], n - t_new, axis=1)
                    if n > LANE_TILE:
                        dst[layer, i, h, :, 0:n - LANE_TILE] = rolled[:, 0:n - LANE_TILE]
                    dst[layer, i, h, :, n - LANE_TILE:n] = jnp.where(lane >= first_new, new[layer, i, h],
                                                                     rolled[:, n - LANE_TILE:n])


def _pad_rows(t, rows):
    return jnp.pad(t, ((0, 0), (0, rows - t.shape[1]), (0, 0)))


def _new_columns(t, n_heads):
    bs, t_new, _ = t.shape
    x = jnp.transpose(t.reshape(bs, t_new, n_heads, HEAD_DIM), (0, 2, 3, 1))
    return jnp.pad(x, ((0, 0), (0, 0), (0, 0), (LANE_TILE - t_new, 0)))


def _sample_attention(layer, pb_s, pd_s, caches_t, new_cols, sinks):
    bs, t_new, _ = pb_s.shape
    depth = caches_t[0].shape[0]
    last = layer == depth - 1
    bb = 1 if last else 2
    pb8, pd8 = _pad_rows(pb_s, ROWS_T), _pad_rows(pd_s, ROWS_T)
    sink_row = jnp.pad(sinks.astype(F32), (0, 128 - N_HEADS)).reshape(1, 128)
    cbk, cbv, cdk, cdv = caches_t
    if last:
        news = [jnp.stack([layer_cols[k] for layer_cols in new_cols], axis=0) for k in range(4)]
    else:
        news = [new_cols[layer][k][None] for k in range(4)]
    n_slab = depth if last else 1
    qblk = lambda: pl.BlockSpec((bb, ROWS_T, W_MIX), lambda i: (i, 0, 0))
    slab = lambda c: pl.BlockSpec((n_slab, bb) + c.shape[2:], lambda i: (0 if last else layer, i, 0, 0, 0))
    newblk = lambda c: pl.BlockSpec((n_slab, bb) + c.shape[2:], lambda i: (0, i, 0, 0, 0))
    rolled_specs = [slab(c) for c in (cdk, cdv, cbk, cbv)] if last else []
    rolled_shapes = [jax.ShapeDtypeStruct(c.shape, c.dtype) for c in (cdk, cdv, cbk, cbv)] if last else []
    res = pl.pallas_call(
        functools.partial(_sattn_kernel, t_new=t_new, here=layer if last else 0),
        grid=(bs // bb,),
        in_specs=[qblk(), qblk(), slab(cdk), slab(cdv), newblk(news[0]), newblk(news[1]),
                  slab(cbk), slab(cbv), newblk(news[2]), newblk(news[3]),
                  pl.BlockSpec((1, 128), lambda i: (0, 0))],
        out_specs=[qblk(), qblk()] + rolled_specs,
        out_shape=[jax.ShapeDtypeStruct((bs, ROWS_T, W_MIX), F32)] * 2 + rolled_shapes,
        compiler_params=_cparams("arbitrary"),
        name="sample_attn",
    )(pd8, pb8, cdk, cdv, news[0], news[1], cbk, cbv, news[2], news[3], sink_row)
    od, ob = res[0], res[1]
    return (ob[:, :t_new].reshape(bs * t_new, W_MIX), od[:, :t_new].reshape(bs * t_new, W_MIX), tuple(res[2:]))


def _srec_kernel(xq_ref, xk_ref, xv_ref, bq_ref, bk_ref, bv_ref, cwq_ref, cwk_ref, cwv_ref, ga_ref,
                 ba_ref, hp_ref, gna_ref, sa_ref, qc_ref, fc_ref, ic_ref, gcg_ref, lb_ref, gnc_ref, sc_ref,
                 oa_ref, sa_out, oc_ref, sc_out, q_scr, k_scr, d_scr, *, t_new):
    nb = sa_ref.shape[-1]
    zero = jnp.zeros((HEAD_DIM, nb), F32)

    def conv(x_ref, b_ref, cw_ref, t):
        y = None
        for tap in range(CONV_A):
            pos = t + tap
            src = b_ref[pos] if pos < CONV_A - 1 else x_ref[pos - (CONV_A - 1)]
            term = cw_ref[tap] * src
            y = term if y is None else y + term
        return _silu(y)

    def l2(x):
        return x * lax.rsqrt(jnp.sum(x * x, axis=0, keepdims=True) + EPS)

    def gated_norm(o, gate, gain):
        return o * lax.rsqrt(jnp.mean(o * o, axis=0, keepdims=True) + EPS) * gain * _silu(gate)

    sa_out[0] = sa_ref[0]
    for t in range(t_new):
        q_scr[...] = l2(conv(xq_ref, bq_ref, cwq_ref, t)) * SCALE
        k_scr[...] = l2(conv(xk_ref, bk_ref, cwk_ref, t))
        v = conv(xv_ref, bv_ref, cwv_ref, t)
        beta = _sigmoid(ba_ref[0, t:t + 1, :])
        dec = jnp.exp(hp_ref[0, 0:1, :] * _softplus(ba_ref[0, t_new + t:t_new + t + 1, :] + hp_ref[0, 1:2, :]))

        def decay_and_read(kk, acc):
            s = sa_out[0, kk] * dec
            sa_out[0, kk] = s
            return acc + k_scr[pl.ds(kk, 1), :] * s

        err = (v - lax.fori_loop(0, HEAD_DIM, decay_and_read, zero)) * beta

        def write_and_query(kk, acc):
            s = sa_out[0, kk] + k_scr[pl.ds(kk, 1), :] * err
            sa_out[0, kk] = s
            return acc + q_scr[pl.ds(kk, 1), :] * s

        o = lax.fori_loop(0, HEAD_DIM, write_and_query, zero)
        oa_ref[t] = gated_norm(o, ga_ref[t], gna_ref[...])

    sc_out[0] = sc_ref[0]
    for t in range(t_new):
        lb = lb_ref[...]
        f = lb + (1.0 - lb) * _sigmoid(fc_ref[t])
        q_scr[...] = qc_ref[t]
        k_scr[...] = 1.0 - f
        d_scr[...] = jnp.exp(jnp.log(f))
        v = ic_ref[t]

        def update(kk, acc):
            s = sc_out[0, kk] * d_scr[pl.ds(kk, 1), :] + k_scr[pl.ds(kk, 1), :] * v
            sc_out[0, kk] = s
            return acc + q_scr[pl.ds(kk, 1), :] * s

        o = lax.fori_loop(0, HEAD_DIM, update, zero)
        oc_ref[t] = gated_norm(o, gcg_ref[t], gnc_ref[...])


def _sample_recurrences(pa_s, pba_s, pc_s, conv_buf, s_a, s_c, conv_w, a_log, dt_bias, norm_a, lb, norm_c):
    bs, t_new, _ = pa_s.shape
    lanes_last = lambda t: jnp.transpose(t, (1, 2, 0))
    pa_t = lanes_last(pa_s)
    pc_t = lanes_last(pc_s)
    buf_t = lanes_last(conv_buf.astype(F32))
    ba = jnp.transpose(pba_s[:, :, 0:8], (2, 1, 0))
    ba = jnp.concatenate([ba[0:N_HEADS], ba[N_HEADS:2 * N_HEADS]], axis=1)
    hp = jnp.stack([-jnp.exp(a_log.astype(F32)), dt_bias.astype(F32)], axis=1)
    hp = jnp.broadcast_to(hp[:, :, None], (N_HEADS, 2, bs))
    cw = conv_w.astype(F32)[:, :, None]
    sa_t = jnp.transpose(s_a.astype(F32), (1, 2, 3, 0))
    sc_t = jnp.transpose(s_c.astype(F32), (1, 2, 3, 0))
    col = lambda v: v.astype(F32).reshape(-1, 1)
    hd = HEAD_DIM
    feat = lambda rows, off: pl.BlockSpec((rows, hd, bs), lambda h: (0, off + h, 0))
    cwspec = lambda off: pl.BlockSpec((CONV_A, hd, 1), lambda h: (0, off + h, 0))
    per_head = lambda rows: pl.BlockSpec((1, rows, bs), lambda h: (h, 0, 0))
    state = pl.BlockSpec((1, hd, hd, bs), lambda h: (h, 0, 0, 0))
    vec = pl.BlockSpec((hd, 1), lambda h: (0, 0))
    nh = N_HEADS
    oa, sa_n, oc, sc_n = pl.pallas_call(
        functools.partial(_srec_kernel, t_new=t_new),
        grid=(N_HEADS,),
        in_specs=[feat(t_new, 0), feat(t_new, nh), feat(t_new, 2 * nh),
                  feat(CONV_A - 1, 0), feat(CONV_A - 1, nh), feat(CONV_A - 1, 2 * nh),
                  cwspec(0), cwspec(nh), cwspec(2 * nh),
                  feat(t_new, 3 * nh), per_head(2 * t_new), per_head(2), vec, state,
                  feat(t_new, 0), feat(t_new, nh), feat(t_new, 2 * nh), feat(t_new, 3 * nh),
                  pl.BlockSpec((hd, 1), lambda h: (h, 0)), vec, state],
        out_specs=[feat(t_new, 0), state, feat(t_new, 0), state],
        out_shape=[jax.ShapeDtypeStruct((t_new, W_MIX, bs), F32),
                   jax.ShapeDtypeStruct((N_HEADS, hd, hd, bs), F32),
                   jax.ShapeDtypeStruct((t_new, W_MIX, bs), F32),
                   jax.ShapeDtypeStruct((N_HEADS, hd, hd, bs), F32)],
        scratch_shapes=[pltpu.VMEM((hd, bs), F32)] * 3,
        compiler_params=_cparams("parallel"),
        name="sample_recurrences",
    )(pa_t, pa_t, pa_t, buf_t, buf_t, buf_t, cw, cw, cw, pa_t, ba, hp, col(norm_a), sa_t,
      pc_t, pc_t, pc_t, pc_t, col(lb), col(norm_c), sc_t)
    rows_first = lambda t: jnp.transpose(t, (2, 0, 1)).reshape(bs * t_new, W_MIX)
    back = lambda t: jnp.transpose(t, (3, 0, 1, 2))
    return rows_first(oa), back(sa_n), rows_first(oc), back(sc_n)


def _heads(t, n):
    return t.reshape(t.shape[0], t.shape[1], n, HEAD_DIM)


def _prompt_mixers(pa, pb, pc, pd, pba, pd_strided, batch, seq, conv_w, head_params, norm_a, sinks, lb, norm_c):
    pa3 = pa.reshape(batch, seq, COLS_A)
    pb3 = pb.reshape(batch, seq, COLS_B)
    pc3 = pc.reshape(batch, seq, COLS_C)
    pd3 = pd.reshape(batch, seq, COLS_D)
    oa, s_a = _gdn_prompt(pa3, pba.reshape(batch, seq, COLS_BA), conv_w, head_params, norm_a)
    oc, s_c = _hgrn_prompt(pc3, lb, norm_c)
    sink_row = jnp.pad(sinks.astype(F32), (0, 128 - N_HEADS)).reshape(1, 128)
    (ob,) = _band_attention(pb3[:, None], 0, 256, 384, KV_B, sink_row, False)
    ods, lses = [], []
    for p4 in (pd3[:, None], *pd_strided):
        o, lse = _band_attention(p4, 0, 256, 512, N_HEADS, None, True)
        ods.append(o)
        lses.append(lse)
    nb, nd = min(CACHE_B, seq), min(CACHE_D, seq)
    state = (_heads(pb3[:, seq - nb:, 256:384], KV_B), _heads(pb3[:, seq - nb:, 384:512], KV_B),
             _heads(pd3[:, seq - nd:, 256:512], N_HEADS), _heads(pd3[:, seq - nd:, 512:768], N_HEADS),
             pa3[:, seq - (CONV_A - 1):, 0:3 * W_MIX], s_a, s_c)
    return oa, ob.reshape(batch * seq, W_MIX), oc, ods, lses, state


def _sample_mixers(layer, pa, pb, pc, pd, pba, bs, t_new, caches_t, new_cols, states, conv_w, a_log, dt_bias,
                   norm_a, sinks, lb, norm_c):
    conv_buf, s_a, s_c = states
    pa3 = pa.reshape(bs, t_new, COLS_A)
    pb3 = pb.reshape(bs, t_new, COLS_B)
    pc3 = pc.reshape(bs, t_new, COLS_C)
    pd3 = pd.reshape(bs, t_new, COLS_D)
    new_cols.append((_new_columns(pd3[:, :, 256:512], N_HEADS), _new_columns(pd3[:, :, 512:768], N_HEADS),
                     _new_columns(pb3[:, :, 256:384], KV_B), _new_columns(pb3[:, :, 384:512], KV_B)))
    ob, od, rolled = _sample_attention(layer, pb3, pd3, caches_t, new_cols, sinks)
    oa, s_a_new, oc, s_c_new = _sample_recurrences(
        pa3, pba.reshape(bs, t_new, COLS_BA), pc3, conv_buf, s_a, s_c, conv_w, a_log, dt_bias, norm_a, lb, norm_c)
    conv_all = jnp.concatenate([conv_buf.astype(F32), pa3[:, :, 0:3 * W_MIX]], axis=1)
    state = (conv_all[:, -(CONV_A - 1):].astype(conv_buf.dtype), s_a_new.astype(s_a.dtype),
             s_c_new.astype(s_c.dtype))
    return oa, ob, oc, od, rolled, state


def kernel(x_prompt, x_sample, cache_b_k, cache_b_v, cache_d_k, cache_d_v, state_a_conv, state_a_s, state_c_s, norm_mix, w_in, conv_a, a_log, dt_bias, norm_a, sinks_b, lb_logits, norm_c, w_out, norm_ffn, w_ffn_gu, w_ffn_down, w_router, b_router, w_moe_gu, w_moe_down, norm_final):
    depth = w_in.shape[0]
    batch, seq, _ = x_prompt.shape
    bs, t_new, _ = x_sample.shape
    lb_p = jax.nn.softmax(lb_logits.astype(F32), axis=0)
    lower_bounds = jnp.cumsum(lb_p, axis=0) - lb_p[0]
    xp = x_prompt.reshape(batch * seq, D_MODEL)
    xs = x_sample.reshape(bs * t_new, D_MODEL)
    tm_p, tm_s = 512, 256
    row = lambda v: v.astype(F32).reshape(1, -1)
    prompt_states, sample_states = [], []
    rows_last = lambda c: jnp.transpose(c, (0, 1, 3, 4, 2))
    caches_t = tuple(rows_last(c) for c in (cache_b_k, cache_b_v, cache_d_k, cache_d_v))
    new_cols = []
    for l in range(depth):
        w_in_l = _permute_w_in(w_in[l])
        w_out_l = w_out[l].astype(BF16)
        conv_w = conv_a[l].astype(F32)
        head_params = _gdn_head_params(a_log[l], dt_bias[l])
        lb = lower_bounds[l]
        mix = (row(norm_a[l]), sinks_b[l], row(lb), row(norm_c[l]))

        projs = _inproj(xp, row(norm_mix[l]), w_in_l, tm_p, strided_for=(batch, seq))
        oa, ob, oc, ods, lses, st_p = _prompt_mixers(*projs[0:5], projs[5:], batch, seq, conv_w, head_params, *mix)
        xp = _outproj(xp, oa, ob, oc, ods, lses, w_out_l, tm_p)
        prompt_states.append(st_p)

        projs = _inproj(xs, row(norm_mix[l]), w_in_l, tm_s)
        oa, ob, oc, od, rolled, st_s = _sample_mixers(
            l, *projs, bs, t_new, caches_t, new_cols, (state_a_conv[l], state_a_s[l], state_c_s[l]), conv_w,
            a_log[l], dt_bias[l], norm_a[l], sinks_b[l], lb, norm_c[l])
        xs = _outproj(xs, oa, ob, oc, [od], [], w_out_l, tm_s)
        sample_states.append(st_s)

        if l % 2 == 0:
            w_gu = w_ffn_gu[l // 2].astype(BF16)
            w_dn = w_ffn_down[l // 2].astype(BF16)
            xp = _ffn(xp, row(norm_ffn[l]), w_gu, w_dn, 512, FFN_FF_TILE)
            xs = _ffn(xs, row(norm_ffn[l]), w_gu, w_dn, 512, FFN_FF_TILE)
        else:
            w_gu = w_moe_gu[l // 2].astype(BF16)
            w_dn = w_moe_down[l // 2].astype(BF16)
            final = row(norm_final) if l == depth - 1 else None
            xp = _moe(xp, row(norm_ffn[l]), w_router[l // 2], b_router[l // 2], w_gu, w_dn, 512, final)
            xs = _moe(xs, row(norm_ffn[l]), w_router[l // 2], b_router[l // 2], w_gu, w_dn, 512, final)
    if depth % 2 == 1:
        xp = _final_norm(xp, row(norm_final), 1024)
        xs = _final_norm(xs, row(norm_final), 512)
    y_prompt = xp.reshape(batch, seq, D_MODEL)
    y_sample = xs.reshape(bs, t_new, D_MODEL)
    stack = lambda states: [jnp.stack(t, 0) for t in zip(*states)]
    rows_back = lambda c, ref: jnp.transpose(c, (0, 1, 4, 2, 3)).astype(ref.dtype)
    d_k, d_v, b_k, b_v = rolled
    sample_caches = [rows_back(b_k, cache_b_k), rows_back(b_v, cache_b_v), rows_back(d_k, cache_d_k),
                     rows_back(d_v, cache_d_v)]
    return (y_prompt, y_sample, *stack(prompt_states), *sample_caches, *stack(sample_states))
```
